```python
import math
import jax
import jax.numpy as jnp
from jax import lax
import numpy as np

D_MODEL = 1024
BATCH = 8
SEQ = 4096
DEPTH = 2

N_MIXERS = 2
POOL_WINDOWS = (2, 4, 8, 16)
N_POOL_GROUPS = len(POOL_WINDOWS)
POOL_GROUP_DIM = D_MODEL // N_POOL_GROUPS
HEAD_DIM = 64
N_HEADS = D_MODEL // HEAD_DIM
DIL_CONFIGS = ((128, 1), (512, 4), (2048, 16))
N_DIL_GROUPS = len(DIL_CONFIGS)
ATTN_WIDTH = N_HEADS * HEAD_DIM
QKV_WIDTH = N_DIL_GROUPS * 3 * ATTN_WIDTH
D_FF = 2816
MACARON_WEIGHT = 0.5
ALPHA = (2.0 * DEPTH) ** 0.25
BETA = (8.0 * DEPTH) ** -0.25
LN_EPS = 1e-5
MASK_VALUE = -1e30

kernel_name = "hybrid_pool_dilated_attn_macaron_deepnorm"


def _alibi_slopes():
    n = N_DIL_GROUPS * N_HEADS
    s = 2.0 ** (-8.0 * np.arange(1, n + 1) / n)
    return s.reshape(N_DIL_GROUPS, N_HEADS).astype(np.float32)


def _layer_norm(x, g, b):
    x32 = x.astype(jnp.float32)
    mu = jnp.mean(x32, axis=-1, keepdims=True)
    var = jnp.mean(jnp.square(x32 - mu), axis=-1, keepdims=True)
    y = (x32 - mu) * lax.rsqrt(var + LN_EPS)
    return (y * g.astype(jnp.float32) + b.astype(jnp.float32)).astype(x.dtype)


def _swiglu(x, w_gate, w_up, w_down):
    return (jax.nn.silu(x @ w_gate) * (x @ w_up)) @ w_down


def _pool_mixer(x, w_in, w_group, scale, w_out):
    B, S, _ = x.shape
    u = (x @ w_in).reshape(B, S, N_POOL_GROUPS, POOL_GROUP_DIM).astype(jnp.float32)
    csum = jnp.concatenate([jnp.zeros_like(u[:, :1]), jnp.cumsum(u, axis=1)], axis=1)
    half = jnp.asarray([w // 2 for w in POOL_WINDOWS], dtype=jnp.int32)
    t = jnp.arange(S, dtype=jnp.int32)[:, None]
    lo = jnp.clip(t - half[None, :], 0, S)
    hi = jnp.clip(t + half[None, :], 0, S)
    gidx = jnp.arange(N_POOL_GROUPS)[None, :]
    win_sum = csum[:, hi, gidx] - csum[:, lo, gidx]
    mean = win_sum / (hi - lo).astype(jnp.float32)[None, :, :, None]
    mixed = (mean - u).astype(x.dtype)
    y = jnp.einsum('bsgc,gce->bsge', mixed, w_group).reshape(B, S, D_MODEL) * scale
    return y @ w_out


def _dilated_group(q, k, v, window, dilation, slopes):
    B, S, H, E = q.shape
    d = dilation
    L = S // d
    R = window // (2 * d)
    W = R
    nb = -(-L // W)
    Lp = nb * W

    def to_sub(a):
        return a.reshape(B, L, d, H, E).transpose(0, 2, 3, 1, 4)

    qs = jnp.pad(to_sub(q), ((0, 0),) * 3 + ((0, Lp - L), (0, 0))).reshape(B, d, H, nb, W, E)

    def windows(a):
        ap = jnp.pad(to_sub(a), ((0, 0),) * 3 + ((W, W + Lp - L), (0, 0)))
        ap = ap.reshape(B, d, H, nb + 2, W, E)
        return jnp.concatenate([ap[:, :, :, :-2], ap[:, :, :, 1:-1], ap[:, :, :, 2:]], axis=4)

    kw = windows(k)
    vw = windows(v)
    a_idx = jnp.arange(W)
    c_idx = jnp.arange(3 * W)
    n_idx = jnp.arange(nb)
    rel = c_idx[None, :] - W - a_idx[:, None]
    j = (n_idx[:, None, None] - 1) * W + c_idx[None, None, :]
    valid = (jnp.abs(rel)[None] <= R) & (j >= 0) & (j < L)
    dist = (d * jnp.abs(rel)).astype(jnp.float32)
    bias = -slopes[:, None, None] * dist[None]
    scores = jnp.einsum('bdhnqe,bdhnke->bdhnqk', qs.astype(jnp.float32),
                        kw.astype(jnp.float32)) * (E ** -0.5) + bias[:, None]
    scores = jnp.where(valid, scores, MASK_VALUE)
    lse = jax.nn.logsumexp(scores, axis=-1)
    p = jnp.exp(scores - lse[..., None])
    o = jnp.einsum('bdhnqk,bdhnke->bdhnqe', p, vw.astype(jnp.float32))
    o = o.reshape(B, d, H, Lp, E)[:, :, :, :L].transpose(0, 3, 1, 2, 4).reshape(B, S, H, E)
    lse = lse.reshape(B, d, H, Lp)[..., :L].transpose(0, 3, 1, 2).reshape(B, S, H)
    return o, lse


def _dilated_attention_mixer(x, w_qkv, w_out):
    B, S, _ = x.shape
    qkv = (x @ w_qkv).reshape(B, S, N_DIL_GROUPS, 3, N_HEADS, HEAD_DIM)
    slopes = jnp.asarray(_alibi_slopes())
    outs, lses = [], []
    for g, (window, dil) in enumerate(DIL_CONFIGS):
        o, l = _dilated_group(qkv[:, :, g, 0], qkv[:, :, g, 1], qkv[:, :, g, 2],
                              window, dil, slopes[g])
        outs.append(o)
        lses.append(l)
    wts = jax.nn.softmax(jnp.stack(lses), axis=0)
    o = jnp.sum(wts[..., None] * jnp.stack(outs), axis=0)
    return o.reshape(B, S, ATTN_WIDTH).astype(x.dtype) @ w_out


def _fwd_setup_inputs(seed: int = 0) -> dict:
    key = jax.random.key(seed)
    ks = jax.random.split(key, 16)
    n_pool = (DEPTH + 1) // 2
    n_attn = DEPTH // 2
    f32 = jnp.float32

    def nrm(k, shape, scale):
        return jax.random.normal(k, shape, f32) * scale

    return {
        "x": jax.random.normal(ks[0], (BATCH, SEQ, D_MODEL), f32),
        "ffn1_w_gate": nrm(ks[1], (DEPTH, D_MODEL, D_FF), D_MODEL ** -0.5),
        "ffn1_w_up": nrm(ks[2], (DEPTH, D_MODEL, D_FF), D_MODEL ** -0.5),
        "ffn1_w_down": nrm(ks[3], (DEPTH, D_FF, D_MODEL), BETA * D_FF ** -0.5),
        "ffn2_w_gate": nrm(ks[4], (DEPTH, D_MODEL, D_FF), D_MODEL ** -0.5),
        "ffn2_w_up": nrm(ks[5], (DEPTH, D_MODEL, D_FF), D_MODEL ** -0.5),
        "ffn2_w_down": nrm(ks[6], (DEPTH, D_FF, D_MODEL), BETA * D_FF ** -0.5),
        "ln_gain": 1.0 + nrm(ks[7], (DEPTH, 3, D_MODEL), 0.02),
        "ln_bias": nrm(ks[8], (DEPTH, 3, D_MODEL), 0.02),
        "pool_w_in": nrm(ks[9], (n_pool, D_MODEL, D_MODEL), D_MODEL ** -0.5),
        "pool_w_group": nrm(ks[10], (n_pool, N_POOL_GROUPS, POOL_GROUP_DIM, POOL_GROUP_DIM),
                            POOL_GROUP_DIM ** -0.5),
        "pool_scale": 1.0 + nrm(ks[11], (n_pool, D_MODEL), 0.1),
        "pool_w_out": nrm(ks[12], (n_pool, D_MODEL, D_MODEL), BETA * D_MODEL ** -0.5),
        "attn_w_qkv": nrm(ks[13], (n_attn, D_MODEL, QKV_WIDTH), D_MODEL ** -0.5),
        "attn_w_out": nrm(ks[14], (n_attn, ATTN_WIDTH, D_MODEL), BETA * ATTN_WIDTH ** -0.5),
    }


def _fwd_reference(x, ffn1_w_gate, ffn1_w_up, ffn1_w_down, ffn2_w_gate, ffn2_w_up, ffn2_w_down,
              ln_gain, ln_bias, pool_w_in, pool_w_group, pool_scale, pool_w_out,
              attn_w_qkv, attn_w_out):
    for i in range(DEPTH):
        h = _swiglu(x, ffn1_w_gate[i], ffn1_w_up[i], ffn1_w_down[i])
        x = _layer_norm(ALPHA * x + MACARON_WEIGHT * h, ln_gain[i, 0], ln_bias[i, 0])
        if i % N_MIXERS == 0:
            p = i // N_MIXERS
            m = _pool_mixer(x, pool_w_in[p], pool_w_group[p], pool_scale[p], pool_w_out[p])
        else:
            a = i // N_MIXERS
            m = _dilated_attention_mixer(x, attn_w_qkv[a], attn_w_out[a])
        x = _layer_norm(ALPHA * x + m, ln_gain[i, 1], ln_bias[i, 1])
        h = _swiglu(x, ffn2_w_gate[i], ffn2_w_up[i], ffn2_w_down[i])
        x = _layer_norm(ALPHA * x + MACARON_WEIGHT * h, ln_gain[i, 2], ln_bias[i, 2])
    return x


import jax as _jax
import jax.numpy as _jnp

TWIN_FORMAT = 'train_step'
FWD_PARAMS = ['x', 'ffn1_w_gate', 'ffn1_w_up', 'ffn1_w_down', 'ffn2_w_gate', 'ffn2_w_up', 'ffn2_w_down', 'ln_gain', 'ln_bias', 'pool_w_in', 'pool_w_group', 'pool_scale', 'pool_w_out', 'attn_w_qkv', 'attn_w_out']
TWIN_WEIGHTS = ['ffn1_w_gate', 'ffn1_w_up', 'ffn1_w_down', 'ffn2_w_gate', 'ffn2_w_up', 'ffn2_w_down', 'ln_gain', 'ln_bias', 'pool_w_in', 'pool_w_group', 'pool_scale', 'pool_w_out', 'attn_w_qkv', 'attn_w_out']
TWIN_DIFF_INPUT = 'x'
TWIN_INPUTS = ['x', 'ffn1_w_gate', 'ffn1_w_up', 'ffn1_w_down', 'ffn2_w_gate', 'ffn2_w_up', 'ffn2_w_down', 'ln_gain', 'ln_bias', 'pool_w_in', 'pool_w_group', 'pool_scale', 'pool_w_out', 'attn_w_qkv', 'attn_w_out', 'loss_target', 'm_ffn1_w_gate', 'm_ffn1_w_up', 'm_ffn1_w_down', 'm_ffn2_w_gate', 'm_ffn2_w_up', 'm_ffn2_w_down', 'm_ln_gain', 'm_ln_bias', 'm_pool_w_in', 'm_pool_w_group', 'm_pool_scale', 'm_pool_w_out', 'm_attn_w_qkv', 'm_attn_w_out', 'v_ffn1_w_gate', 'v_ffn1_w_up', 'v_ffn1_w_down', 'v_ffn2_w_gate', 'v_ffn2_w_up', 'v_ffn2_w_down', 'v_ln_gain', 'v_ln_bias', 'v_pool_w_in', 'v_pool_w_group', 'v_pool_scale', 'v_pool_w_out', 'v_attn_w_qkv', 'v_attn_w_out']
TWIN_OUTPUTS = ['loss', 'grad_x', 'grad_ffn1_w_gate', 'grad_ffn1_w_up', 'grad_ffn1_w_down', 'grad_ffn2_w_gate', 'grad_ffn2_w_up', 'grad_ffn2_w_down', 'grad_ln_gain', 'grad_ln_bias', 'grad_pool_w_in', 'grad_pool_w_group', 'grad_pool_scale', 'grad_pool_w_out', 'grad_attn_w_qkv', 'grad_attn_w_out', 'delta_ffn1_w_gate', 'delta_ffn1_w_up', 'delta_ffn1_w_down', 'delta_ffn2_w_gate', 'delta_ffn2_w_up', 'delta_ffn2_w_down', 'delta_ln_gain', 'delta_ln_bias', 'delta_pool_w_in', 'delta_pool_w_group', 'delta_pool_scale', 'delta_pool_w_out', 'delta_attn_w_qkv', 'delta_attn_w_out', 'new_m_ffn1_w_gate', 'new_m_ffn1_w_up', 'new_m_ffn1_w_down', 'new_m_ffn2_w_gate', 'new_m_ffn2_w_up', 'new_m_ffn2_w_down', 'new_m_ln_gain', 'new_m_ln_bias', 'new_m_pool_w_in', 'new_m_pool_w_group', 'new_m_pool_scale', 'new_m_pool_w_out', 'new_m_attn_w_qkv', 'new_m_attn_w_out', 'new_v_ffn1_w_gate', 'new_v_ffn1_w_up', 'new_v_ffn1_w_down', 'new_v_ffn2_w_gate', 'new_v_ffn2_w_up', 'new_v_ffn2_w_down', 'new_v_ln_gain', 'new_v_ln_bias', 'new_v_pool_w_in', 'new_v_pool_w_group', 'new_v_pool_scale', 'new_v_pool_w_out', 'new_v_attn_w_qkv', 'new_v_attn_w_out']
TWIN_LEAF_KINDS = {'loss': 'loss', 'grad_x': 'grad_x', 'grad_ffn1_w_gate': 'grad_w', 'grad_ffn1_w_up': 'grad_w', 'grad_ffn1_w_down': 'grad_w', 'grad_ffn2_w_gate': 'grad_w', 'grad_ffn2_w_up': 'grad_w', 'grad_ffn2_w_down': 'grad_w', 'grad_ln_gain': 'grad_w', 'grad_ln_bias': 'grad_w', 'grad_pool_w_in': 'grad_w', 'grad_pool_w_group': 'grad_w', 'grad_pool_scale': 'grad_w', 'grad_pool_w_out': 'grad_w', 'grad_attn_w_qkv': 'grad_w', 'grad_attn_w_out': 'grad_w', 'delta_ffn1_w_gate': 'delta_w', 'delta_ffn1_w_up': 'delta_w', 'delta_ffn1_w_down': 'delta_w', 'delta_ffn2_w_gate': 'delta_w', 'delta_ffn2_w_up': 'delta_w', 'delta_ffn2_w_down': 'delta_w', 'delta_ln_gain': 'delta_w', 'delta_ln_bias': 'delta_w', 'delta_pool_w_in': 'delta_w', 'delta_pool_w_group': 'delta_w', 'delta_pool_scale': 'delta_w', 'delta_pool_w_out': 'delta_w', 'delta_attn_w_qkv': 'delta_w', 'delta_attn_w_out': 'delta_w', 'new_m_ffn1_w_gate': 'new_m', 'new_m_ffn1_w_up': 'new_m', 'new_m_ffn1_w_down': 'new_m', 'new_m_ffn2_w_gate': 'new_m', 'new_m_ffn2_w_up': 'new_m', 'new_m_ffn2_w_down': 'new_m', 'new_m_ln_gain': 'new_m', 'new_m_ln_bias': 'new_m', 'new_m_pool_w_in': 'new_m', 'new_m_pool_w_group': 'new_m', 'new_m_pool_scale': 'new_m', 'new_m_pool_w_out': 'new_m', 'new_m_attn_w_qkv': 'new_m', 'new_m_attn_w_out': 'new_m', 'new_v_ffn1_w_gate': 'new_v', 'new_v_ffn1_w_up': 'new_v', 'new_v_ffn1_w_down': 'new_v', 'new_v_ffn2_w_gate': 'new_v', 'new_v_ffn2_w_up': 'new_v', 'new_v_ffn2_w_down': 'new_v', 'new_v_ln_gain': 'new_v', 'new_v_ln_bias': 'new_v', 'new_v_pool_w_in': 'new_v', 'new_v_pool_w_group': 'new_v', 'new_v_pool_scale': 'new_v', 'new_v_pool_w_out': 'new_v', 'new_v_attn_w_qkv': 'new_v', 'new_v_attn_w_out': 'new_v'}


def _forward(args):
    return _fwd_reference(*[args[k] for k in FWD_PARAMS])


def _output_shape():
    def fwd():
        inp = _fwd_setup_inputs(0)
        return _fwd_reference(*[inp[k] for k in FWD_PARAMS])
    out = _jax.eval_shape(fwd)
    return out.shape, out.dtype

N_MICROBATCH = 1
ADAM_LR = 0.001
ADAM_B1 = 0.9
ADAM_B2 = 0.999
ADAM_EPS = 1e-08
ADAM_WD = 0.01
ADAM_STEP = 10
PER_EXAMPLE_BATCH_AXIS = {'x': 0, 'loss_target': 0}
SHARED_INPUTS = []
_WEIGHT_DTYPES = {'ffn1_w_gate': _jnp.float32, 'ffn1_w_up': _jnp.float32, 'ffn1_w_down': _jnp.float32, 'ffn2_w_gate': _jnp.float32, 'ffn2_w_up': _jnp.float32, 'ffn2_w_down': _jnp.float32, 'ln_gain': _jnp.float32, 'ln_bias': _jnp.float32, 'pool_w_in': _jnp.float32, 'pool_w_group': _jnp.float32, 'pool_scale': _jnp.float32, 'pool_w_out': _jnp.float32, 'attn_w_qkv': _jnp.float32, 'attn_w_out': _jnp.float32}
MOMENT_SCALE = {'ffn1_w_gate': 1.177804e-02, 'ffn1_w_up': 1.140523e-02, 'ffn1_w_down': 3.782114e-02, 'ffn2_w_gate': 1.167342e-02, 'ffn2_w_up': 1.131826e-02, 'ffn2_w_down': 3.751595e-02, 'ln_gain': 1.315254e+01, 'ln_bias': 7.811373e-01, 'pool_w_in': 5.404979e-02, 'pool_w_group': 5.436804e-02, 'pool_scale': 5.468567e-02, 'pool_w_out': 1.090586e-01, 'attn_w_qkv': 8.606495e-03, 'attn_w_out': 3.223584e-02}


def _to_microbatches(a, axis):
    t = _jnp.moveaxis(a, axis, 0)
    t = t.reshape((N_MICROBATCH, t.shape[0] // N_MICROBATCH) + t.shape[1:])
    return _jnp.moveaxis(t, 1, axis + 1)


def setup_inputs(seed: int = 0) -> dict:
    inp = _fwd_setup_inputs(seed)
    key = _jax.random.fold_in(_jax.random.key(seed), 7919)
    shape, _ = _output_shape()
    out = dict(inp)
    out["loss_target"] = _jax.random.normal(_jax.random.fold_in(key, 0), shape, _jnp.float32)
    for i, name in enumerate(TWIN_WEIGHTS):
        w = inp[name].astype(_jnp.float32)
        if MOMENT_SCALE is None:
            s = _jnp.sqrt(_jnp.mean(_jnp.square(w)) + 1e-30)
        else:
            s = MOMENT_SCALE[name]
        km, kv = _jax.random.split(_jax.random.fold_in(key, i + 1))
        out[name] = w
        out["m_" + name] = s * _jax.random.normal(km, w.shape, _jnp.float32)
        out["v_" + name] = (s * s) * _jax.random.uniform(kv, w.shape, _jnp.float32, 0.5, 1.5)
    if N_MICROBATCH > 1:
        for name, axis in PER_EXAMPLE_BATCH_AXIS.items():
            out[name] = _to_microbatches(out[name], axis)
    return {'x': out['x'], 'ffn1_w_gate': out['ffn1_w_gate'], 'ffn1_w_up': out['ffn1_w_up'], 'ffn1_w_down': out['ffn1_w_down'], 'ffn2_w_gate': out['ffn2_w_gate'], 'ffn2_w_up': out['ffn2_w_up'], 'ffn2_w_down': out['ffn2_w_down'], 'ln_gain': out['ln_gain'], 'ln_bias': out['ln_bias'], 'pool_w_in': out['pool_w_in'], 'pool_w_group': out['pool_w_group'], 'pool_scale': out['pool_scale'], 'pool_w_out': out['pool_w_out'], 'attn_w_qkv': out['attn_w_qkv'], 'attn_w_out': out['attn_w_out'], 'loss_target': out['loss_target'], 'm_ffn1_w_gate': out['m_ffn1_w_gate'], 'm_ffn1_w_up': out['m_ffn1_w_up'], 'm_ffn1_w_down': out['m_ffn1_w_down'], 'm_ffn2_w_gate': out['m_ffn2_w_gate'], 'm_ffn2_w_up': out['m_ffn2_w_up'], 'm_ffn2_w_down': out['m_ffn2_w_down'], 'm_ln_gain': out['m_ln_gain'], 'm_ln_bias': out['m_ln_bias'], 'm_pool_w_in': out['m_pool_w_in'], 'm_pool_w_group': out['m_pool_w_group'], 'm_pool_scale': out['m_pool_scale'], 'm_pool_w_out': out['m_pool_w_out'], 'm_attn_w_qkv': out['m_attn_w_qkv'], 'm_attn_w_out': out['m_attn_w_out'], 'v_ffn1_w_gate': out['v_ffn1_w_gate'], 'v_ffn1_w_up': out['v_ffn1_w_up'], 'v_ffn1_w_down': out['v_ffn1_w_down'], 'v_ffn2_w_gate': out['v_ffn2_w_gate'], 'v_ffn2_w_up': out['v_ffn2_w_up'], 'v_ffn2_w_down': out['v_ffn2_w_down'], 'v_ln_gain': out['v_ln_gain'], 'v_ln_bias': out['v_ln_bias'], 'v_pool_w_in': out['v_pool_w_in'], 'v_pool_w_group': out['v_pool_w_group'], 'v_pool_scale': out['v_pool_scale'], 'v_pool_w_out': out['v_pool_w_out'], 'v_attn_w_qkv': out['v_attn_w_qkv'], 'v_attn_w_out': out['v_attn_w_out']}


def _loss(weights, diff, rest, loss_target):
    with _jax.named_scope("forward"):
        args = {**rest, TWIN_DIFF_INPUT: diff, **{k: w.astype(_WEIGHT_DTYPES[k]) for k, w in weights.items()}}
        y = _forward(args)
    with _jax.named_scope("loss_head"):
        err = _jnp.square(y.astype(_jnp.float32) - loss_target)
        return 0.5 * _jnp.sum(_jnp.mean(err, axis=-1)) if err.ndim else 0.5 * err


def _adamw(w, g, m, v):
    m = ADAM_B1 * m + (1.0 - ADAM_B1) * g
    v = ADAM_B2 * v + (1.0 - ADAM_B2) * _jnp.square(g)
    m_hat = m / (1.0 - ADAM_B1 ** ADAM_STEP)
    v_hat = v / (1.0 - ADAM_B2 ** ADAM_STEP)
    delta = -ADAM_LR * (m_hat / (_jnp.sqrt(v_hat) + ADAM_EPS) + ADAM_WD * w)
    return delta, m, v


def reference(x, ffn1_w_gate, ffn1_w_up, ffn1_w_down, ffn2_w_gate, ffn2_w_up, ffn2_w_down, ln_gain, ln_bias, pool_w_in, pool_w_group, pool_scale, pool_w_out, attn_w_qkv, attn_w_out, loss_target, m_ffn1_w_gate, m_ffn1_w_up, m_ffn1_w_down, m_ffn2_w_gate, m_ffn2_w_up, m_ffn2_w_down, m_ln_gain, m_ln_bias, m_pool_w_in, m_pool_w_group, m_pool_scale, m_pool_w_out, m_attn_w_qkv, m_attn_w_out, v_ffn1_w_gate, v_ffn1_w_up, v_ffn1_w_down, v_ffn2_w_gate, v_ffn2_w_up, v_ffn2_w_down, v_ln_gain, v_ln_bias, v_pool_w_in, v_pool_w_group, v_pool_scale, v_pool_w_out, v_attn_w_qkv, v_attn_w_out):
    given = dict(x=x, ffn1_w_gate=ffn1_w_gate, ffn1_w_up=ffn1_w_up, ffn1_w_down=ffn1_w_down, ffn2_w_gate=ffn2_w_gate, ffn2_w_up=ffn2_w_up, ffn2_w_down=ffn2_w_down, ln_gain=ln_gain, ln_bias=ln_bias, pool_w_in=pool_w_in, pool_w_group=pool_w_group, pool_scale=pool_scale, pool_w_out=pool_w_out, attn_w_qkv=attn_w_qkv, attn_w_out=attn_w_out, loss_target=loss_target, m_ffn1_w_gate=m_ffn1_w_gate, m_ffn1_w_up=m_ffn1_w_up, m_ffn1_w_down=m_ffn1_w_down, m_ffn2_w_gate=m_ffn2_w_gate, m_ffn2_w_up=m_ffn2_w_up, m_ffn2_w_down=m_ffn2_w_down, m_ln_gain=m_ln_gain, m_ln_bias=m_ln_bias, m_pool_w_in=m_pool_w_in, m_pool_w_group=m_pool_w_group, m_pool_scale=m_pool_scale, m_pool_w_out=m_pool_w_out, m_attn_w_qkv=m_attn_w_qkv, m_attn_w_out=m_attn_w_out, v_ffn1_w_gate=v_ffn1_w_gate, v_ffn1_w_up=v_ffn1_w_up, v_ffn1_w_down=v_ffn1_w_down, v_ffn2_w_gate=v_ffn2_w_gate, v_ffn2_w_up=v_ffn2_w_up, v_ffn2_w_down=v_ffn2_w_down, v_ln_gain=v_ln_gain, v_ln_bias=v_ln_bias, v_pool_w_in=v_pool_w_in, v_pool_w_group=v_pool_w_group, v_pool_scale=v_pool_scale, v_pool_w_out=v_pool_w_out, v_attn_w_qkv=v_attn_w_qkv, v_attn_w_out=v_attn_w_out)
    weights = {n: given[n] for n in TWIN_WEIGHTS}
    shared = {n: given[n] for n in SHARED_INPUTS}
    per_example = {n: given[n] for n in ['x']}
    grad_fn = _jax.value_and_grad(_loss, argnums=(0, 1))

    def one_microbatch(ex, loss_target):
        ex = dict(ex)
        diff = ex.pop(TWIN_DIFF_INPUT)
        return grad_fn(weights, diff, {**shared, **ex}, loss_target)

    if N_MICROBATCH == 1:
        loss, (grad_w, grad_x) = one_microbatch(per_example, given["loss_target"])
    else:
        def body(carry, xs):
            loss_sum, grad_sum = carry
            l_k, (gw_k, gx_k) = one_microbatch(xs[0], xs[1])
            with _jax.named_scope("update"):
                return (loss_sum + l_k, _jax.tree.map(_jnp.add, grad_sum, gw_k)), gx_k

        init = (_jnp.zeros((), _jnp.float32), _jax.tree.map(_jnp.zeros_like, weights))
        (loss, grad_w), grad_x = _jax.lax.scan(body, init, (per_example, given["loss_target"]))
    with _jax.named_scope("update"):
        delta_w, new_m, new_v = {}, {}, {}
        for n in TWIN_WEIGHTS:
            delta_w[n], new_m[n], new_v[n] = _adamw(weights[n], grad_w[n], given["m_" + n], given["v_" + n])
    return (loss, grad_x, *[grad_w[n] for n in TWIN_WEIGHTS], *[delta_w[n] for n in TWIN_WEIGHTS],
            *[new_m[n] for n in TWIN_WEIGHTS], *[new_v[n] for n in TWIN_WEIGHTS])
```

```python
import functools

import numpy as np
import jax
import jax.numpy as jnp
from jax import lax
from jax.experimental import pallas as pl
from jax.experimental.pallas import tpu as pltpu

F32 = jnp.float32
BF16 = jnp.bfloat16

D = 1024
N_DEV = 8
N_HEADS = 16
HEAD_DIM = 64
N_POOL_GROUPS = 4
POOL_GROUP_DIM = 256
POOL_HALF = (1, 2, 4, 8)
DIL_CONFIGS = ((128, 1), (512, 4), (2048, 16))
ATTN_HALO = 64
ATTN_BLOCK = 128
QKV_SHARD = 3 * 3 * D // N_DEV
DEPTH = 2
ALPHA = (2.0 * DEPTH) ** 0.25
MACARON = 0.5
LN_EPS = 1e-5
MASK_VALUE = -1e30
ADAM_LR = 0.001
ADAM_B1 = 0.9
ADAM_B2 = 0.999
ADAM_EPS = 1e-08
ADAM_WD = 0.01
ADAM_STEP = 10
LANE = 128
VMEM_LIMIT = 56 * 1024 * 1024
MESH = pl.DeviceIdType.MESH
AXES = ("x", "y", "c")


def _round_up(n, m):
    return (n + m - 1) // m * m


def _pcall(body, **kw):
    return pl.pallas_call(body, **kw)


def _params(*sem):
    return pltpu.CompilerParams(dimension_semantics=sem, vmem_limit_bytes=VMEM_LIMIT)


def _alibi_slopes():
    n = len(DIL_CONFIGS) * N_HEADS
    s = 2.0 ** (-8.0 * np.arange(1, n + 1) / n)
    return s.reshape(len(DIL_CONFIGS), N_HEADS).astype(np.float32)


def _my_slot():
    return 4 * lax.axis_index("c") + 2 * lax.axis_index("x") + lax.axis_index("y")


def _mm(a, b, *, grid, a_spec, b_spec, out_shape, out_spec, nt=False, name, alias=None):
    nk = grid[2]
    dn = (((1,), (1,)), ((), ())) if nt else (((1,), (0,)), ((), ()))
    blk = tuple(s for s in out_spec.block_shape if s is not None)

    def body(*refs):
        a_ref, b_ref = refs[0], refs[1]
        o_ref = refs[3] if alias is not None else refs[2]
        p = lax.dot_general(a_ref[...], b_ref[...], dn, preferred_element_type=F32)
        if nk == 1:
            o_ref[...] = p.astype(o_ref.dtype)
        else:
            acc = refs[-1]
            k = pl.program_id(2)

            @pl.when(k == 0)
            def _():
                acc[...] = p

            @pl.when(k > 0)
            def _():
                acc[...] += p

            @pl.when(k == nk - 1)
            def _():
                o_ref[...] = acc[...].astype(o_ref.dtype)

    in_specs = [a_spec, b_spec]
    args = [a, b]
    aliases = {}
    if alias is not None:
        in_specs.append(pl.BlockSpec(memory_space=pl.ANY))
        args.append(alias)
        aliases = {2: 0}
    return _pcall(
        body, name=name, grid=grid, in_specs=in_specs, out_specs=out_spec, out_shape=out_shape,
        scratch_shapes=[] if nk == 1 else [pltpu.VMEM(blk, F32)],
        input_output_aliases=aliases,
        compiler_params=_params("parallel", "parallel", "arbitrary"),
    )(*args)


def _transpose_cast(x, name):
    T = x.shape[0]
    tm = min(512, T)

    def body(x_ref, xb_ref, xt_ref):
        v = x_ref[...]
        xb_ref[...] = v.astype(BF16)
        xt_ref[...] = v.T.astype(BF16)

    return _pcall(
        body, name=name, grid=(T // tm,),
        in_specs=[pl.BlockSpec((tm, D), lambda t: (t, 0))],
        out_specs=[pl.BlockSpec((tm, D), lambda t: (t, 0)), pl.BlockSpec((D, tm), lambda t: (0, t))],
        out_shape=[jax.ShapeDtypeStruct((T, D), BF16), jax.ShapeDtypeStruct((D, T), BF16)],
        compiler_params=_params("parallel"),
    )(x)


def _mm_ln(a, b, xres, gain, bias, hscale, name):
    T, K = a.shape
    tm = min(512, T)

    def body(a_ref, b_ref, x_ref, g_ref, bt_ref, y_ref, yb_ref, yt_ref, xh_ref, rs_ref):
        h = jnp.dot(a_ref[...], b_ref[...], preferred_element_type=F32)
        z = ALPHA * x_ref[...] + hscale * h
        mu = jnp.mean(z, axis=-1, keepdims=True)
        zc = z - mu
        var = jnp.mean(zc * zc, axis=-1, keepdims=True)
        rstd = lax.rsqrt(var + LN_EPS)
        xh = zc * rstd
        y = xh * g_ref[...] + bt_ref[...]
        y_ref[...] = y
        yb_ref[...] = y.astype(BF16)
        yt_ref[...] = y.T.astype(BF16)
        xh_ref[...] = xh
        rs_ref[...] = rstd

    row = pl.BlockSpec((tm, D), lambda t: (t, 0))
    vec = pl.BlockSpec((1, D), lambda t: (0, 0))
    return _pcall(
        body, name=name, grid=(T // tm,),
        in_specs=[pl.BlockSpec((tm, K), lambda t: (t, 0)), pl.BlockSpec((K, D), lambda t: (0, 0)), row, vec, vec],
        out_specs=[row, row, pl.BlockSpec((D, tm), lambda t: (0, t)), row, pl.BlockSpec((tm, 1), lambda t: (t, 0))],
        out_shape=[jax.ShapeDtypeStruct((T, D), F32), jax.ShapeDtypeStruct((T, D), BF16),
                   jax.ShapeDtypeStruct((D, T), BF16), jax.ShapeDtypeStruct((T, D), F32),
                   jax.ShapeDtypeStruct((T, 1), F32)],
        compiler_params=_params("parallel"),
    )(a, b, xres, gain, bias)


def _ln_bwd(dys, xhat, rstd, gain, hscale, name):
    T = xhat.shape[0]
    tm = min(512, T)
    n = len(dys)

    def body(*refs):
        dy_refs = refs[:n]
        xh_ref, rs_ref, g_ref, dx_ref, dh_ref, dht_ref, dg_ref, db_ref = refs[n:]
        dy = dy_refs[0][...]
        for r in dy_refs[1:]:
            dy = dy + r[...]
        xh = xh_ref[...]
        dxh = dy * g_ref[...]
        m1 = jnp.mean(dxh, axis=-1, keepdims=True)
        m2 = jnp.mean(dxh * xh, axis=-1, keepdims=True)
        dz = rs_ref[...] * (dxh - m1 - xh * m2)
        dx_ref[...] = ALPHA * dz
        dh = hscale * dz
        dh_ref[...] = dh.astype(BF16)
        dht_ref[...] = dh.T.astype(BF16)
        dg = jnp.sum(dy * xh, axis=0, keepdims=True)
        db = jnp.sum(dy, axis=0, keepdims=True)

        @pl.when(pl.program_id(0) == 0)
        def _():
            dg_ref[...] = dg
            db_ref[...] = db

        @pl.when(pl.program_id(0) > 0)
        def _():
            dg_ref[...] += dg
            db_ref[...] += db

    row = pl.BlockSpec((tm, D), lambda t: (t, 0))
    vec = pl.BlockSpec((1, D), lambda t: (0, 0))
    return _pcall(
        body, name=name, grid=(T // tm,),
        in_specs=[row] * n + [row, pl.BlockSpec((tm, 1), lambda t: (t, 0)), vec],
        out_specs=[row, row, pl.BlockSpec((D, tm), lambda t: (0, t)), vec, vec],
        out_shape=[jax.ShapeDtypeStruct((T, D), F32), jax.ShapeDtypeStruct((T, D), BF16),
                   jax.ShapeDtypeStruct((D, T), BF16), jax.ShapeDtypeStruct((1, D), F32),
                   jax.ShapeDtypeStruct((1, D), F32)],
        compiler_params=_params("arbitrary"),
    )(*dys, xhat, rstd, gain)


def _add2(a, b, name):
    T = a.shape[0]
    tm = min(512, T)

    def body(a_ref, b_ref, o_ref):
        o_ref[...] = a_ref[...] + b_ref[...]

    row = pl.BlockSpec((tm, D), lambda t: (t, 0))
    return _pcall(body, name=name, grid=(T // tm,), in_specs=[row, row], out_specs=row,
                  out_shape=jax.ShapeDtypeStruct((T, D), F32), compiler_params=_params("parallel"))(a, b)


def _loss_head(y, tgt):
    T = y.shape[0]
    tm = min(512, T)

    def body(y_ref, t_ref, dy_ref, l_ref):
        e = y_ref[...] - t_ref[...]
        dy_ref[...] = e * (1.0 / D)
        part = jnp.sum(jnp.sum(e * e, axis=1, keepdims=True), axis=0, keepdims=True) * (0.5 / D)

        @pl.when(pl.program_id(0) == 0)
        def _():
            l_ref[...] = jnp.zeros_like(l_ref)

        l_ref[...] += part

    row = pl.BlockSpec((tm, D), lambda t: (t, 0))
    return _pcall(
        body, name="loss_head", grid=(T // tm,),
        in_specs=[row, row],
        out_specs=[row, pl.BlockSpec((8, LANE), lambda t: (0, 0))],
        out_shape=[jax.ShapeDtypeStruct((T, D), F32), jax.ShapeDtypeStruct((8, LANE), F32)],
        compiler_params=_params("arbitrary"),
    )(y, tgt)


def _ffn_up(xb, wgu, f, fp, name):
    T = xb.shape[0]
    tm = min(1024, T)

    def body(x_ref, w_ref, g_ref, u_ref, a_ref):
        p = jnp.dot(x_ref[...], w_ref[...], preferred_element_type=F32)
        g = p[:, :fp]
        u = p[:, fp:]
        a = g * jax.nn.sigmoid(g) * u
        g_ref[...] = g.astype(BF16)
        u_ref[...] = u.astype(BF16)
        a_ref[...] = a.astype(BF16)

    out = pl.BlockSpec((tm, fp), lambda j, t: (t, j))
    shp = jax.ShapeDtypeStruct((T, N_DEV * fp), BF16)
    return _pcall(
        body, name=name, grid=(N_DEV, T // tm),
        in_specs=[pl.BlockSpec((tm, D), lambda j, t: (t, 0)),
                  pl.BlockSpec((None, D, 2 * fp), lambda j, t: (j, f, 0))],
        out_specs=[out, out, out], out_shape=[shp, shp, shp],
        compiler_params=_params("parallel", "parallel"),
    )(xb, wgu)


def _ffn_bwd_act(dhb, wd, g, u, fp, name):
    T = dhb.shape[0]
    tm = min(1024, T)

    def body(dh_ref, w_ref, g_ref, u_ref, o_ref):
        da = lax.dot_general(dh_ref[...], w_ref[...], (((1,), (1,)), ((), ())), preferred_element_type=F32)
        gv = g_ref[...].astype(F32)
        uv = u_ref[...].astype(F32)
        sig = jax.nn.sigmoid(gv)
        dgate = da * uv * (sig * (1.0 + gv * (1.0 - sig)))
        dup = da * (gv * sig)
        o_ref[:, :fp] = dgate.astype(BF16)
        o_ref[:, fp:] = dup.astype(BF16)

    blk = pl.BlockSpec((tm, fp), lambda j, t: (t, j))
    return _pcall(
        body, name=name, grid=(N_DEV, T // tm),
        in_specs=[pl.BlockSpec((tm, D), lambda j, t: (t, 0)), pl.BlockSpec((fp, D), lambda j, t: (j, 0)), blk, blk],
        out_specs=pl.BlockSpec((tm, 2 * fp), lambda j, t: (t, j)),
        out_shape=jax.ShapeDtypeStruct((T, N_DEV * 2 * fp), BF16),
        compiler_params=_params("parallel", "parallel"),
    )(dhb, wd, g, u)


POOL_PAD = 16
POOL_CHUNK = 512


def _pool_window(v, transpose, name):
    T = v.shape[0]
    ch = min(POOL_CHUNK, T)
    ext = ch + 2 * POOL_PAD
    gd = POOL_GROUP_DIM

    def body(v_ref, o_ref, ot_ref, pad_ref):
        pad_ref[0:POOL_PAD, :] = jnp.zeros((POOL_PAD, gd), F32)
        pad_ref[POOL_PAD + T:POOL_PAD + T + POOL_PAD, :] = jnp.zeros((POOL_PAD, gd), F32)
        for gi, hw in enumerate(POOL_HALF):
            @pl.when(pl.program_id(0) == gi)
            def _(hw=hw):
                def count(t):
                    return (jnp.minimum(t + hw, T) - jnp.maximum(t - hw, 0)).astype(F32)

                if transpose:
                    t_all = lax.broadcasted_iota(jnp.int32, (T, gd), 0)
                    pad_ref[POOL_PAD:POOL_PAD + T, :] = v_ref[...] / count(t_all)
                else:
                    pad_ref[POOL_PAD:POOL_PAD + T, :] = v_ref[...]
                shift = hw if transpose else hw - 1
                for c in range(T // ch):
                    e = pad_ref[c * ch:c * ch + ext, :]
                    step = 1
                    while step < 2 * hw:
                        e = e + pltpu.roll(e, step, 0)
                        step *= 2
                    if shift:
                        e = pltpu.roll(e, ext - shift, 0)
                    s = e[POOL_PAD:POOL_PAD + ch, :]
                    center = v_ref[c * ch:(c + 1) * ch, :]
                    if transpose:
                        res = s - center
                    else:
                        t_idx = c * ch + lax.broadcasted_iota(jnp.int32, (ch, gd), 0)
                        res = s / count(t_idx) - center
                    o_ref[c * ch:(c + 1) * ch, :] = res.astype(BF16)
                    ot_ref[:, c * ch:(c + 1) * ch] = res.T.astype(BF16)

    return _pcall(
        body, name=name, grid=(N_POOL_GROUPS,),
        in_specs=[pl.BlockSpec((T, gd), lambda g: (0, g))],
        out_specs=[pl.BlockSpec((T, gd), lambda g: (0, g)), pl.BlockSpec((gd, T), lambda g: (g, 0))],
        out_shape=[jax.ShapeDtypeStruct((T, D), BF16), jax.ShapeDtypeStruct((D, T), BF16)],
        scratch_shapes=[pltpu.VMEM((T + 2 * POOL_PAD, gd), F32)],
        compiler_params=_params("arbitrary"),
    )(v)


def _pool_group(mixedb, wgroup, scale):
    T = mixedb.shape[0]
    tm = min(1024, T)
    gd = POOL_GROUP_DIM

    def body(a_ref, w_ref, s_ref, y_ref, ys_ref, yst_ref):
        y = jnp.dot(a_ref[...], w_ref[...], preferred_element_type=F32)
        ys = y * s_ref[...]
        y_ref[...] = y
        ys_ref[...] = ys.astype(BF16)
        yst_ref[...] = ys.T.astype(BF16)

    blk = pl.BlockSpec((tm, gd), lambda g, t: (t, g))
    return _pcall(
        body, name="pool_group", grid=(N_POOL_GROUPS, T // tm),
        in_specs=[blk, pl.BlockSpec((None, gd, gd), lambda g, t: (g, 0, 0)), pl.BlockSpec((1, gd), lambda g, t: (0, g))],
        out_specs=[blk, blk, pl.BlockSpec((gd, tm), lambda g, t: (g, t))],
        out_shape=[jax.ShapeDtypeStruct((T, D), F32), jax.ShapeDtypeStruct((T, D), BF16),
                   jax.ShapeDtypeStruct((D, T), BF16)],
        compiler_params=_params("parallel", "parallel"),
    )(mixedb, wgroup, scale)


def _pool_bwd_out(dmb, w_out, y, scale):
    T = dmb.shape[0]
    tm = min(512, T)

    def body(a_ref, w_ref, y_ref, s_ref, dy_ref, ds_ref):
        dys = lax.dot_general(a_ref[...], w_ref[...], (((1,), (1,)), ((), ())), preferred_element_type=F32)
        dy_ref[...] = (dys * s_ref[...]).astype(BF16)
        part = jnp.sum(dys * y_ref[...], axis=0, keepdims=True)

        @pl.when(pl.program_id(0) == 0)
        def _():
            ds_ref[...] = part

        @pl.when(pl.program_id(0) > 0)
        def _():
            ds_ref[...] += part

    row = pl.BlockSpec((tm, D), lambda t: (t, 0))
    vec = pl.BlockSpec((1, D), lambda t: (0, 0))
    return _pcall(
        body, name="pool_bwd_out", grid=(T // tm,),
        in_specs=[row, pl.BlockSpec((D, D), lambda t: (0, 0)), row, vec],
        out_specs=[row, vec],
        out_shape=[jax.ShapeDtypeStruct((T, D), BF16), jax.ShapeDtypeStruct((1, D), F32)],
        compiler_params=_params("arbitrary"),
    )(dmb, w_out, y, scale)


def _attn_masks(n, L, d):
    w = ATTN_BLOCK + 2 * ATTN_HALO
    a = lax.broadcasted_iota(jnp.int32, (ATTN_BLOCK, w), 0)
    c = lax.broadcasted_iota(jnp.int32, (ATTN_BLOCK, w), 1)
    rel = c - ATTN_HALO - a
    j = n * ATTN_BLOCK - ATTN_HALO + c
    valid = (jnp.abs(rel) <= ATTN_HALO) & (j >= 0) & (j < L)
    dist = (d * jnp.abs(rel)).astype(F32)
    return valid, dist


def _lane_col(st, idx):
    lane = lax.broadcasted_iota(jnp.int32, st.shape, 1)
    return jnp.sum(jnp.where(lane == idx, st, 0.0), axis=1, keepdims=True)


def _window_specs(nb, d, col, width):
    last = 2 * d * nb - 1

    def prev(r, n):
        return (jnp.maximum(2 * (r * nb + n) - 1, 0), col)

    def cur(r, n):
        return (r * nb + n, col)

    def nxt(r, n):
        return (jnp.minimum(2 * (r * nb + n) + 2, last), col)

    return [pl.BlockSpec((ATTN_HALO, width), prev), pl.BlockSpec((ATTN_BLOCK, width), cur),
            pl.BlockSpec((ATTN_HALO, width), nxt)]


def _attn_fwd(qkv_g, d, slopes, name):
    T = qkv_g.shape[0]
    L = T // d
    nb = L // ATTN_BLOCK

    def body(q_ref, kp_ref, kc_ref, kn_ref, vp_ref, vc_ref, vn_ref, o_ref, lse_ref):
        valid, dist = _attn_masks(pl.program_id(1), L, d)
        lane = lax.broadcasted_iota(jnp.int32, (ATTN_BLOCK, LANE), 1)
        first = lane < HEAD_DIM
        head_mask = [jnp.where(first, 1.0, 0.0).astype(BF16), jnp.where(first, 0.0, 1.0).astype(BF16)]
        lse_acc = jnp.zeros((ATTN_BLOCK, LANE), F32)
        for hp in range(N_HEADS // 2):
            cs = slice(hp * LANE, (hp + 1) * LANE)
            q2 = q_ref[:, cs]
            k2 = jnp.concatenate([kp_ref[:, cs], kc_ref[:, cs], kn_ref[:, cs]], axis=0)
            v2 = jnp.concatenate([vp_ref[:, cs], vc_ref[:, cs], vn_ref[:, cs]], axis=0)
            outs = []
            for hh in range(2):
                h = 2 * hp + hh
                qh = q2 * head_mask[hh]
                s = lax.dot_general(qh, k2, (((1,), (1,)), ((), ())), preferred_element_type=F32)
                s = s * (HEAD_DIM ** -0.5) - float(slopes[h]) * dist
                s = jnp.where(valid, s, MASK_VALUE)
                m = jnp.max(s, axis=1, keepdims=True)
                p = jnp.exp(s - m)
                l = jnp.sum(p, axis=1, keepdims=True)
                o = jnp.dot(p.astype(BF16), v2, preferred_element_type=F32) / l
                outs.append(o)
                lse_acc = jnp.where(lane == h, m + jnp.log(l), lse_acc)
            o_ref[:, cs] = jnp.where(first, outs[0], outs[1])
        lse_ref[...] = lse_acc

    specs = ([pl.BlockSpec((ATTN_BLOCK, D), lambda r, n: (r * nb + n, 0))]
             + _window_specs(nb, d, 1, D) + _window_specs(nb, d, 2, D))
    row = lambda w: pl.BlockSpec((ATTN_BLOCK, w), lambda r, n: (r * nb + n, 0))
    return _pcall(
        body, name=name, grid=(d, nb), in_specs=specs,
        out_specs=[row(D), row(LANE)],
        out_shape=[jax.ShapeDtypeStruct((T, D), F32), jax.ShapeDtypeStruct((T, LANE), F32)],
        compiler_params=_params("parallel", "parallel"),
    )(*([qkv_g] * 7))


def _attn_combine(os_, lses):
    T = os_[0].shape[0]
    tm = min(256, T)
    ng = len(os_)

    def body(*refs):
        o_refs = refs[:ng]
        l_refs = refs[ng:2 * ng]
        o32_ref, ob_ref, ot_ref, lt_ref = refs[2 * ng:]
        ls = [r[...] for r in l_refs]
        m = ls[0]
        for l in ls[1:]:
            m = jnp.maximum(m, l)
        tot = jnp.exp(ls[0] - m)
        for l in ls[1:]:
            tot = tot + jnp.exp(l - m)
        lt = m + jnp.log(tot)
        lt_ref[...] = lt
        ws = [jnp.exp(l - lt) for l in ls]
        lane = lax.broadcasted_iota(jnp.int32, (tm, LANE), 1)
        first = lane < HEAD_DIM
        for hp in range(N_HEADS // 2):
            cs = slice(hp * LANE, (hp + 1) * LANE)
            acc = jnp.zeros((tm, LANE), F32)
            for g in range(ng):
                wt = jnp.where(first, _lane_col(ws[g], 2 * hp), _lane_col(ws[g], 2 * hp + 1))
                acc = acc + wt * o_refs[g][:, cs]
            o32_ref[:, cs] = acc
            ob_ref[:, cs] = acc.astype(BF16)
        ot_ref[...] = o32_ref[...].T.astype(BF16)

    row = pl.BlockSpec((tm, D), lambda t: (t, 0))
    st = pl.BlockSpec((tm, LANE), lambda t: (t, 0))
    return _pcall(
        body, name="attn_combine", grid=(T // tm,),
        in_specs=[row] * ng + [st] * ng,
        out_specs=[row, row, pl.BlockSpec((D, tm), lambda t: (0, t)), st],
        out_shape=[jax.ShapeDtypeStruct((T, D), F32), jax.ShapeDtypeStruct((T, D), BF16),
                   jax.ShapeDtypeStruct((D, T), BF16), jax.ShapeDtypeStruct((T, LANE), F32)],
        compiler_params=_params("parallel"),
    )(*os_, *lses)


def _attn_bwd_prep(dmb, w_out, o32, lse_tot):
    T = dmb.shape[0]
    tm = min(512, T)

    def body(a_ref, w_ref, o_ref, l_ref, do_ref, st_ref):
        do = lax.dot_general(a_ref[...], w_ref[...], (((1,), (1,)), ((), ())), preferred_element_type=F32)
        do_ref[...] = do.astype(BF16)
        prod = do * o_ref[...]
        lane = lax.broadcasted_iota(jnp.int32, (tm, LANE), 1)
        first = lane < HEAD_DIM
        st = jnp.where(lane < N_HEADS, l_ref[...], 0.0)
        for hp in range(N_HEADS // 2):
            pr = prod[:, hp * LANE:(hp + 1) * LANE]
            d0 = jnp.sum(jnp.where(first, pr, 0.0), axis=1, keepdims=True)
            d1 = jnp.sum(jnp.where(first, 0.0, pr), axis=1, keepdims=True)
            st = jnp.where(lane == N_HEADS + 2 * hp, d0, st)
            st = jnp.where(lane == N_HEADS + 2 * hp + 1, d1, st)
        st_ref[...] = st

    row = pl.BlockSpec((tm, D), lambda t: (t, 0))
    stb = pl.BlockSpec((tm, LANE), lambda t: (t, 0))
    return _pcall(
        body, name="attn_bwd_prep", grid=(T // tm,),
        in_specs=[row, pl.BlockSpec((D, D), lambda t: (0, 0)), row, stb],
        out_specs=[row, stb],
        out_shape=[jax.ShapeDtypeStruct((T, D), BF16), jax.ShapeDtypeStruct((T, LANE), F32)],
        compiler_params=_params("parallel"),
    )(dmb, w_out, o32, lse_tot)


def _attn_bwd(qkv_g, do_g, st_g, d, slopes, name):
    T = qkv_g.shape[0]
    L = T // d
    nb = L // ATTN_BLOCK
    scale = HEAD_DIM ** -0.5
    nt = (((1,), (1,)), ((), ()))

    def body(qp_ref, qc_ref, qn_ref, kp_ref, kc_ref, kn_ref, vp_ref, vc_ref, vn_ref,
             dp_ref, dc_ref, dn_ref, sp_ref, sc_ref, sn_ref, o_ref):
        valid, dist = _attn_masks(pl.program_id(1), L, d)
        lane = lax.broadcasted_iota(jnp.int32, (ATTN_BLOCK, LANE), 1)
        first = lane < HEAD_DIM
        head_mask = [jnp.where(first, 1.0, 0.0).astype(BF16), jnp.where(first, 0.0, 1.0).astype(BF16)]
        stc = sc_ref[...]
        stw_t = jnp.concatenate([sp_ref[...], stc, sn_ref[...]], axis=0).T
        for hp in range(N_HEADS // 2):
            cs = slice(hp * LANE, (hp + 1) * LANE)
            cat = lambda a, b, c: jnp.concatenate([a[:, cs], b[:, cs], c[:, cs]], axis=0)
            q2, k2, v2, do2 = qc_ref[:, cs], kc_ref[:, cs], vc_ref[:, cs], dc_ref[:, cs]
            qw, kw, vw, dow = cat(qp_ref, qc_ref, qn_ref), cat(kp_ref, kc_ref, kn_ref), cat(vp_ref, vc_ref, vn_ref), cat(dp_ref, dc_ref, dn_ref)
            dqs, dks, dvs = [], [], []
            for hh in range(2):
                h = 2 * hp + hh
                pick = lambda t, hh=hh: t * head_mask[hh]
                bias = float(slopes[h]) * dist
                s = lax.dot_general(pick(q2), kw, nt, preferred_element_type=F32) * scale - bias
                s = jnp.where(valid, s, MASK_VALUE)
                p = jnp.exp(s - _lane_col(stc, h))
                dp = lax.dot_general(pick(do2), vw, nt, preferred_element_type=F32)
                ds = p * (dp - _lane_col(stc, N_HEADS + h)) * scale
                dqs.append(jnp.dot(ds.astype(BF16), kw, preferred_element_type=F32))
                st_ = lax.dot_general(pick(k2), qw, nt, preferred_element_type=F32) * scale - bias
                st_ = jnp.where(valid, st_, MASK_VALUE)
                pt = jnp.exp(st_ - stw_t[h:h + 1, :])
                dvs.append(jnp.dot(pt.astype(BF16), dow, preferred_element_type=F32))
                dpt = lax.dot_general(pick(v2), dow, nt, preferred_element_type=F32)
                dst = pt * (dpt - stw_t[N_HEADS + h:N_HEADS + h + 1, :]) * scale
                dks.append(jnp.dot(dst.astype(BF16), qw, preferred_element_type=F32))
            o_ref[:, hp * LANE:(hp + 1) * LANE] = jnp.where(first, dqs[0], dqs[1]).astype(BF16)
            o_ref[:, D + hp * LANE:D + (hp + 1) * LANE] = jnp.where(first, dks[0], dks[1]).astype(BF16)
            o_ref[:, 2 * D + hp * LANE:2 * D + (hp + 1) * LANE] = jnp.where(first, dvs[0], dvs[1]).astype(BF16)

    specs = (_window_specs(nb, d, 0, D) + _window_specs(nb, d, 1, D) + _window_specs(nb, d, 2, D)
             + _window_specs(nb, d, 0, D) + _window_specs(nb, d, 0, LANE))
    return _pcall(
        body, name=name, grid=(d, nb), in_specs=specs,
        out_specs=pl.BlockSpec((ATTN_BLOCK, 3 * D), lambda r, n: (r * nb + n, 0)),
        out_shape=jax.ShapeDtypeStruct((T, 3 * D), BF16),
        compiler_params=_params("parallel", "parallel"),
    )(*([qkv_g] * 9), *([do_g] * 3), *([st_g] * 3))


def _to_sub(a, d):
    if d == 1:
        return a
    T, C = a.shape
    return a.reshape(T // d, d, C).transpose(1, 0, 2).reshape(T, C)


def _from_sub(a, d):
    if d == 1:
        return a
    T, C = a.shape
    return a.reshape(d, T // d, C).transpose(1, 0, 2).reshape(T, C)


def _all_gather(arrs, name):
    n = len(arrs)

    def body(*refs):
        ins, outs = refs[:n], refs[n:2 * n]
        send_sems, recv_sems, local_sems = refs[2 * n:]
        x, y, c = lax.axis_index("x"), lax.axis_index("y"), lax.axis_index("c")
        me, sibling = (x, y, c), (x, y, 1 - c)
        chips = [(1 - x, y), (x, 1 - y), (1 - x, 1 - y)]

        def slot(px, py, pc):
            return 4 * pc + 2 * px + py

        def copy(i, k, block, to, src=None):
            dst = outs[i].at[slot(*block)]
            return pltpu.make_async_remote_copy(
                src_ref=dst if src is None else src, dst_ref=dst,
                send_sem=send_sems.at[i, k], recv_sem=recv_sems.at[i, k],
                device_id=to, device_id_type=MESH)

        mine = [pltpu.make_async_copy(ins[i], outs[i].at[slot(*me)], local_sems.at[i]) for i in range(n)]
        for cp in mine:
            cp.start()
        first = []
        for i in range(n):
            first.append(copy(i, 0, me, sibling, src=ins[i]))
            first += [copy(i, 1 + j, me, (*chip, c), src=ins[i]) for j, chip in enumerate(chips)]
        for cp in first:
            cp.start()
        passed = []
        for j, chip in enumerate(chips):
            for i in range(n):
                copy(i, 1 + j, (*chip, c), me).wait_recv()
                fwd = copy(i, 4 + j, (*chip, c), sibling)
                fwd.start()
                passed.append(fwd)
        for i in range(n):
            copy(i, 0, sibling, me).wait_recv()
            for j, chip in enumerate(chips):
                copy(i, 4 + j, (*chip, 1 - c), me).wait_recv()
        for cp in first + passed:
            cp.wait_send()
        for cp in mine:
            cp.wait()

    hbm = pl.BlockSpec(memory_space=pl.ANY)
    return _pcall(
        body, name=name,
        in_specs=[hbm] * n, out_specs=[hbm] * n,
        out_shape=[jax.ShapeDtypeStruct((N_DEV,) + a.shape, a.dtype) for a in arrs],
        scratch_shapes=[pltpu.SemaphoreType.DMA((n, 7)), pltpu.SemaphoreType.DMA((n, 7)),
                        pltpu.SemaphoreType.DMA((n,))],
    )(*arrs)


def _small_all_gather(v, name):
    R, C = v.shape

    def body(x_ref, out_ref, sum_ref, send_sems, recv_sems, local_sem):
        x, y, c = lax.axis_index("x"), lax.axis_index("y"), lax.axis_index("c")
        me, sibling = (x, y, c), (x, y, 1 - c)
        chips = [(1 - x, y), (x, 1 - y), (1 - x, 1 - y)]

        def rows(px, py, pc):
            return out_ref.at[4 * pc + 2 * px + py]

        def copy(k, block, to, src=None):
            return pltpu.make_async_remote_copy(
                src_ref=rows(*block) if src is None else src, dst_ref=rows(*block),
                send_sem=send_sems.at[k], recv_sem=recv_sems.at[k],
                device_id=to, device_id_type=MESH)

        mine = pltpu.make_async_copy(x_ref, rows(*me), local_sem)
        mine.start()
        first = [copy(0, me, sibling, src=x_ref)]
        first += [copy(1 + j, me, (*chip, c), src=x_ref) for j, chip in enumerate(chips)]
        for cp in first:
            cp.start()
        passed = [copy(4 + j, (*chip, c), sibling) for j, chip in enumerate(chips)]
        for j, chip in enumerate(chips):
            copy(1 + j, (*chip, c), me).wait_recv()
            passed[j].start()
        copy(0, sibling, me).wait_recv()
        for j, chip in enumerate(chips):
            copy(4 + j, (*chip, 1 - c), me).wait_recv()
        for cp in first + passed:
            cp.wait_send()
        mine.wait()
        acc = out_ref[0]
        for s in range(1, N_DEV):
            acc = acc + out_ref[s]
        sum_ref[...] = acc

    vm = pl.BlockSpec(memory_space=pltpu.VMEM)
    return _pcall(
        body, name=name, in_specs=[vm], out_specs=[vm, vm],
        out_shape=[jax.ShapeDtypeStruct((N_DEV, R, C), v.dtype), jax.ShapeDtypeStruct((R, C), v.dtype)],
        scratch_shapes=[pltpu.SemaphoreType.DMA((7,)), pltpu.SemaphoreType.DMA((7,)), pltpu.SemaphoreType.DMA],
    )(v)


def _rs_sibling(arrs, name):
    n = len(arrs)

    def body(*refs):
        ins, outs = refs[:n], refs[n:2 * n]
        send_sems, recv_sems = refs[2 * n:]
        x, y, c = lax.axis_index("x"), lax.axis_index("y"), lax.axis_index("c")
        cps = [pltpu.make_async_remote_copy(
            src_ref=ins[i].at[pl.ds(4 * (1 - c), 4)], dst_ref=outs[i],
            send_sem=send_sems.at[i], recv_sem=recv_sems.at[i],
            device_id=(x, y, 1 - c), device_id_type=MESH) for i in range(n)]
        for cp in cps:
            cp.start()
        for cp in cps:
            cp.wait()

    hbm = pl.BlockSpec(memory_space=pl.ANY)
    return _pcall(
        body, name=name, in_specs=[hbm] * n, out_specs=[hbm] * n,
        out_shape=[jax.ShapeDtypeStruct((4,) + a.shape[1:], a.dtype) for a in arrs],
        scratch_shapes=[pltpu.SemaphoreType.DMA((n,)), pltpu.SemaphoreType.DMA((n,))],
    )(*arrs)


def _rs_chips(arrs, name):
    n = len(arrs)

    def body(*refs):
        ins, outs = refs[:n], refs[n:2 * n]
        send_sems, recv_sems = refs[2 * n:]
        x, y, c = lax.axis_index("x"), lax.axis_index("y"), lax.axis_index("c")
        chips = [(1 - x, y), (x, 1 - y), (1 - x, 1 - y)]
        cps = []
        for i in range(n):
            for j, (px, py) in enumerate(chips):
                cps.append(pltpu.make_async_remote_copy(
                    src_ref=ins[i].at[2 * px + py], dst_ref=outs[i].at[j],
                    send_sem=send_sems.at[i, j], recv_sem=recv_sems.at[i, j],
                    device_id=(px, py, c), device_id_type=MESH))
        for cp in cps:
            cp.start()
        for cp in cps:
            cp.wait()

    hbm = pl.BlockSpec(memory_space=pl.ANY)
    return _pcall(
        body, name=name, in_specs=[hbm] * n, out_specs=[hbm] * n,
        out_shape=[jax.ShapeDtypeStruct((3,) + a.shape[1:], a.dtype) for a in arrs],
        scratch_shapes=[pltpu.SemaphoreType.DMA((n, 3)), pltpu.SemaphoreType.DMA((n, 3))],
    )(*arrs)


def _row_tile(R, C, itemsize=4, budget=2 * 1024 * 1024):
    best = None
    for t in range(16, R + 1, 16):
        if R % t == 0 and t * C * itemsize <= budget:
            best = t
    return best if best is not None else R


def _add_half(arr, recv, c_idx, name):
    _, R, C = arr.shape
    tr = _row_tile(R, C)

    def body(c_ref, a_ref, r_ref, o_ref):
        o_ref[...] = (a_ref[...].astype(F32) + r_ref[...].astype(F32)).astype(o_ref.dtype)

    gs = pltpu.PrefetchScalarGridSpec(
        num_scalar_prefetch=1, grid=(4, R // tr),
        in_specs=[pl.BlockSpec((None, tr, C), lambda q, i, c_ref: (4 * c_ref[0] + q, i, 0)),
                  pl.BlockSpec((None, tr, C), lambda q, i, c_ref: (q, i, 0))],
        out_specs=pl.BlockSpec((None, tr, C), lambda q, i, c_ref: (q, i, 0)))
    return _pcall(body, name=name, grid_spec=gs, out_shape=jax.ShapeDtypeStruct((4, R, C), arr.dtype),
                  compiler_params=_params("parallel", "parallel"))(c_idx, arr, recv)


def _sum_chips(chipsum, recv, q_idx, name):
    _, R, C = chipsum.shape
    tr = _row_tile(R, C)

    def body(q_ref, a_ref, r_ref, o_ref):
        acc = a_ref[...].astype(F32)
        for j in range(3):
            acc = acc + r_ref[j].astype(F32)
        o_ref[...] = acc

    gs = pltpu.PrefetchScalarGridSpec(
        num_scalar_prefetch=1, grid=(R // tr,),
        in_specs=[pl.BlockSpec((None, tr, C), lambda i, q_ref: (q_ref[0], i, 0)),
                  pl.BlockSpec((3, tr, C), lambda i, q_ref: (0, i, 0))],
        out_specs=pl.BlockSpec((tr, C), lambda i, q_ref: (i, 0)))
    return _pcall(body, name=name, grid_spec=gs, out_shape=jax.ShapeDtypeStruct((R, C), F32),
                  compiler_params=_params("parallel"))(q_idx, chipsum, recv)


def _adamw(w, g, m, v, name):
    shape = w.shape
    C = shape[-1]
    R = int(np.prod(shape[:-1]))
    tr = _row_tile(R, C, budget=1024 * 1024)

    def body(w_ref, g_ref, m_ref, v_ref, d_ref, nm_ref, nv_ref):
        gv = g_ref[...]
        mv = ADAM_B1 * m_ref[...] + (1.0 - ADAM_B1) * gv
        vv = ADAM_B2 * v_ref[...] + (1.0 - ADAM_B2) * jnp.square(gv)
        m_hat = mv / (1.0 - ADAM_B1 ** ADAM_STEP)
        v_hat = vv / (1.0 - ADAM_B2 ** ADAM_STEP)
        d_ref[...] = -ADAM_LR * (m_hat / (jnp.sqrt(v_hat) + ADAM_EPS) + ADAM_WD * w_ref[...])
        nm_ref[...] = mv
        nv_ref[...] = vv

    blk = pl.BlockSpec((tr, C), lambda i: (i, 0))
    shp = jax.ShapeDtypeStruct((R, C), F32)
    outs = _pcall(body, name=name, grid=(R // tr,), in_specs=[blk] * 4, out_specs=[blk] * 3,
                  out_shape=[shp] * 3, compiler_params=_params("parallel"))(
        w.reshape(R, C), g.reshape(R, C), m.reshape(R, C), v.reshape(R, C))
    return tuple(o.reshape(shape) for o in outs)


def _pad_cols(w, width):
    return jnp.pad(w, ((0, 0), (0, width - w.shape[1])))


def _pad_rows(w, height):
    return jnp.pad(w, ((0, height - w.shape[0]), (0, 0)))


def _slot_to_device_order(a):
    s = a.shape
    return a.reshape((2, 4) + s[1:]).swapaxes(0, 1).reshape(s)


def _device_to_slot_order(a):
    s = a.shape
    return a.reshape((4, 2) + s[1:]).swapaxes(0, 1).reshape(s)


def kernel(x, ffn1_w_gate, ffn1_w_up, ffn1_w_down, ffn2_w_gate, ffn2_w_up, ffn2_w_down, ln_gain, ln_bias, pool_w_in, pool_w_group, pool_scale, pool_w_out, attn_w_qkv, attn_w_out, loss_target, m_ffn1_w_gate, m_ffn1_w_up, m_ffn1_w_down, m_ffn2_w_gate, m_ffn2_w_up, m_ffn2_w_down, m_ln_gain, m_ln_bias, m_pool_w_in, m_pool_w_group, m_pool_scale, m_pool_w_out, m_attn_w_qkv, m_attn_w_out, v_ffn1_w_gate, v_ffn1_w_up, v_ffn1_w_down, v_ffn2_w_gate, v_ffn2_w_up, v_ffn2_w_down, v_ln_gain, v_ln_bias, v_pool_w_in, v_pool_w_group, v_pool_scale, v_pool_w_out, v_attn_w_qkv, v_attn_w_out):
    T = x.shape[1]
    fs = ffn1_w_gate.shape[2]
    fp = _round_up(fs, LANE)
    rs = D // N_DEV
    x0 = x[0]
    tgt = loss_target[0]
    slopes = _alibi_slopes()
    c_idx = lax.axis_index("c").astype(jnp.int32).reshape(1)
    q_idx = (2 * lax.axis_index("x") + lax.axis_index("y")).astype(jnp.int32).reshape(1)

    gates = (ffn1_w_gate, ffn2_w_gate)
    ups = (ffn1_w_up, ffn2_w_up)
    downs = (ffn1_w_down, ffn2_w_down)
    ffns = [(i, k) for i in range(DEPTH) for k in range(2)]
    wgu_sh = jnp.concatenate(
        [jnp.concatenate([_pad_cols(gates[k][i], fp), _pad_cols(ups[k][i], fp)], axis=1) for i, k in ffns],
        axis=0).astype(BF16)
    sq_sh = jnp.concatenate(
        [_pad_rows(downs[k][i], fp) for i, k in ffns]
        + [pool_w_in[0], pool_w_out[0], attn_w_out[0], pool_w_group[0].reshape(rs // 4, D)],
        axis=0).astype(BF16)
    qkv_sh = attn_w_qkv[0].astype(BF16)
    wgu_all, sq_all, wqkv_all = _all_gather([wgu_sh, sq_sh, qkv_sh], "ag_weights")

    ln_sh = jnp.concatenate([ln_gain.reshape(DEPTH * 3, rs), ln_bias.reshape(DEPTH * 3, rs),
                             jnp.zeros((4, rs), F32)], axis=0)
    ln_all, _ = _small_all_gather(ln_sh, "ag_ln")
    ln_all = _slot_to_device_order(ln_all).transpose(1, 0, 2).reshape(16, D)
    gain = lambda i, s: ln_all[3 * i + s][None]
    bias = lambda i, s: ln_all[DEPTH * 3 + 3 * i + s][None]

    wd_full = [sq_all[:, f * fp:(f + 1) * fp, :].reshape(N_DEV * fp, D) for f in range(4)]
    base = 4 * fp

    def square(idx):
        blk = sq_all[:, base + idx * rs:base + (idx + 1) * rs, :]
        return _slot_to_device_order(blk).reshape(D, D)

    w_pin, w_pout, w_aout = square(0), square(1), square(2)
    grp = _slot_to_device_order(sq_all[:, base + 3 * rs:base + 3 * rs + rs // 4, :])
    w_grp = grp.reshape(N_DEV, N_POOL_GROUPS, rs // 4, POOL_GROUP_DIM).transpose(1, 0, 2, 3).reshape(
        N_POOL_GROUPS, POOL_GROUP_DIM, POOL_GROUP_DIM)

    def ffn_fwd(xf, xb, f, i, s):
        g, u, act = _ffn_up(xb, wgu_all, f, fp, f"ffn_up{f}")
        y, yb, yt, xh, rstd = _mm_ln(act, wd_full[f], xf, gain(i, s), bias(i, s), MACARON, f"ffn_down_ln{f}")
        return (y, yb, yt), dict(g=g, u=u, act=act, xh=xh, rstd=rstd)

    x0b, x0t = _transpose_cast(x0, "x_cast")
    (a1, a1b, a1t), s_f0 = ffn_fwd(x0, x0b, 0, 0, 0)
    tm = min(512, T)
    row_spec = pl.BlockSpec((tm, D), lambda i, j, k: (i, 0))
    full_w = pl.BlockSpec((D, D), lambda i, j, k: (0, 0))
    u_pool = _mm(a1b, w_pin, grid=(T // tm, 1, 1), a_spec=row_spec, b_spec=full_w,
                 out_shape=jax.ShapeDtypeStruct((T, D), F32), out_spec=row_spec, name="pool_in")
    mixedb, mixedt = _pool_window(u_pool, False, "pool_window")
    y_pool, ysb, yst = _pool_group(mixedb, w_grp, pool_scale)
    a2, a2b, a2t, xh_p, rstd_p = _mm_ln(ysb, w_pout, a1, gain(0, 1), bias(0, 1), 1.0, "pool_out_ln")
    (a3, a3b, a3t), s_f1 = ffn_fwd(a2, a2b, 1, 0, 2)
    (b1, b1b, b1t), s_f2 = ffn_fwd(a3, a3b, 2, 1, 0)
    tq = min(1024, T)
    qkv = _mm(b1b, wqkv_all, grid=(N_DEV, T // tq, 1),
              a_spec=pl.BlockSpec((tq, D), lambda j, t, k: (t, 0)),
              b_spec=pl.BlockSpec((None, D, QKV_SHARD), lambda j, t, k: (4 * (j % 2) + j // 2, 0, 0)),
              out_shape=jax.ShapeDtypeStruct((T, N_DEV * QKV_SHARD), BF16),
              out_spec=pl.BlockSpec((tq, QKV_SHARD), lambda j, t, k: (t, j)), name="attn_qkv")
    qkv_gs, o_gs, lse_gs = [], [], []
    for gi, (_, d) in enumerate(DIL_CONFIGS):
        qkv_g = _to_sub(qkv[:, gi * 3 * D:(gi + 1) * 3 * D], d)
        o_g, lse_g = _attn_fwd(qkv_g, d, slopes[gi], f"attn_fwd{gi}")
        qkv_gs.append(qkv_g)
        o_gs.append(_from_sub(o_g, d))
        lse_gs.append(_from_sub(lse_g, d))
    o32, ob, ot, lse_tot = _attn_combine(o_gs, lse_gs)
    b2, b2b, b2t, xh_a, rstd_a = _mm_ln(ob, w_aout, b1, gain(1, 1), bias(1, 1), 1.0, "attn_out_ln")
    (b3, _, _), s_f3 = ffn_fwd(b2, b2b, 3, 1, 2)

    dy, loss_tile = _loss_head(b3, tgt)
    loss = lax.psum(loss_tile[0, 0], AXES)

    bm = min(512, D)
    dgains, dbiases = {}, {}
    g_gu, g_dt = [None] * 4, [None] * 4

    def ffn_bwd(dys, f, i, s, st, xt):
        dxres, dhb, dht, dg, db = _ln_bwd(dys, st["xh"], st["rstd"], gain(i, s), MACARON, f"ffn_ln_bwd{f}")
        dgains[(i, s)], dbiases[(i, s)] = dg, db
        dgu = _ffn_bwd_act(dhb, wd_full[f], st["g"], st["u"], fp, f"ffn_bwd_act{f}")
        g_dt[f] = _mm(dht, st["act"], grid=(D // bm, N_DEV, 1),
                      a_spec=pl.BlockSpec((bm, T), lambda r, j, k: (r, 0)),
                      b_spec=pl.BlockSpec((T, fp), lambda r, j, k: (0, j)),
                      out_shape=jax.ShapeDtypeStruct((N_DEV, D, fp), BF16),
                      out_spec=pl.BlockSpec((None, bm, fp), lambda r, j, k: (j, r, 0)), name=f"ffn_dwd{f}")
        g_gu[f] = _mm(xt, dgu, grid=(D // bm, N_DEV, 1),
                      a_spec=pl.BlockSpec((bm, T), lambda r, j, k: (r, 0)),
                      b_spec=pl.BlockSpec((T, 2 * fp), lambda r, j, k: (0, j)),
                      out_shape=jax.ShapeDtypeStruct((N_DEV, D, 2 * fp), BF16),
                      out_spec=pl.BlockSpec((None, bm, 2 * fp), lambda r, j, k: (j, r, 0)), name=f"ffn_dwgu{f}")
        tmx = min(1024, T)
        dx = _mm(dgu, wgu_all, grid=(T // tmx, 1, N_DEV), nt=True,
                 a_spec=pl.BlockSpec((tmx, 2 * fp), lambda t, j, k: (t, k)),
                 b_spec=pl.BlockSpec((None, D, 2 * fp), lambda t, j, k: (k, f, 0)),
                 out_shape=jax.ShapeDtypeStruct((T, D), F32),
                 out_spec=pl.BlockSpec((tmx, D), lambda t, j, k: (t, 0)), name=f"ffn_dx{f}")
        return [dxres, dx]

    def dw_square(at, bmat, name):
        return _mm(at, bmat, grid=(D // bm, 1, 1),
                   a_spec=pl.BlockSpec((bm, T), lambda r, j, k: (r, 0)),
                   b_spec=pl.BlockSpec((T, D), lambda r, j, k: (0, 0)),
                   out_shape=jax.ShapeDtypeStruct((D, D), BF16),
                   out_spec=pl.BlockSpec((bm, D), lambda r, j, k: (r, 0)), name=name)

    def dx_square(a, w, name):
        return _mm(a, w, grid=(T // tm, 1, 1), nt=True, a_spec=row_spec, b_spec=full_w,
                   out_shape=jax.ShapeDtypeStruct((T, D), F32), out_spec=row_spec, name=name)

    d_b2 = ffn_bwd([dy], 3, 1, 2, s_f3, b2t)
    dxres, dmb, dmt, dg, db = _ln_bwd(d_b2, xh_a, rstd_a, gain(1, 1), 1.0, "attn_ln_bwd")
    dgains[(1, 1)], dbiases[(1, 1)] = dg, db
    g_aout = dw_square(ot, dmb, "attn_dwout")
    dob, stats = _attn_bwd_prep(dmb, w_aout, o32, lse_tot)
    dqkv_parts = []
    for gi, (_, d) in enumerate(DIL_CONFIGS):
        dqkv_g = _attn_bwd(qkv_gs[gi], _to_sub(dob, d), _to_sub(stats, d), d, slopes[gi], f"attn_bwd{gi}")
        dqkv_parts.append(_from_sub(dqkv_g, d))
    dqkv = jnp.concatenate(dqkv_parts, axis=1)
    g_qkv = _mm(b1t, dqkv, grid=(D // bm, N_DEV, 1),
                a_spec=pl.BlockSpec((bm, T), lambda r, j, k: (r, 0)),
                b_spec=pl.BlockSpec((T, QKV_SHARD), lambda r, j, k: (0, j)),
                out_shape=jax.ShapeDtypeStruct((N_DEV, D, QKV_SHARD), BF16),
                out_spec=pl.BlockSpec((None, bm, QKV_SHARD), lambda r, j, k: (4 * (j % 2) + j // 2, r, 0)),
                name="attn_dwqkv")
    dx_attn = _mm(dqkv, wqkv_all, grid=(T // tq, 1, N_DEV), nt=True,
                  a_spec=pl.BlockSpec((tq, QKV_SHARD), lambda t, j, k: (t, k)),
                  b_spec=pl.BlockSpec((None, D, QKV_SHARD), lambda t, j, k: (4 * (k % 2) + k // 2, 0, 0)),
                  out_shape=jax.ShapeDtypeStruct((T, D), F32),
                  out_spec=pl.BlockSpec((tq, D), lambda t, j, k: (t, 0)), name="attn_dx")
    d_a3 = ffn_bwd([dxres, dx_attn], 2, 1, 0, s_f2, a3t)
    d_a2 = ffn_bwd(d_a3, 1, 0, 2, s_f1, a2t)
    dxres, dmb, dmt, dg, db = _ln_bwd(d_a2, xh_p, rstd_p, gain(0, 1), 1.0, "pool_ln_bwd")
    dgains[(0, 1)], dbiases[(0, 1)] = dg, db
    g_pout = dw_square(yst, dmb, "pool_dwout")
    dyb, dscale = _pool_bwd_out(dmb, w_pout, y_pool, pool_scale)
    gd = POOL_GROUP_DIM
    g_grp = _mm(mixedt, dyb, grid=(N_POOL_GROUPS, 1, 1),
                a_spec=pl.BlockSpec((gd, T), lambda g, j, k: (g, 0)),
                b_spec=pl.BlockSpec((T, gd), lambda g, j, k: (0, g)),
                out_shape=jax.ShapeDtypeStruct((N_POOL_GROUPS, gd, gd), BF16),
                out_spec=pl.BlockSpec((None, gd, gd), lambda g, j, k: (g, 0, 0)), name="pool_dwgroup")
    tg = min(1024, T)
    dmixed = _mm(dyb, w_grp, grid=(N_POOL_GROUPS, T // tg, 1), nt=True,
                 a_spec=pl.BlockSpec((tg, gd), lambda g, t, k: (t, g)),
                 b_spec=pl.BlockSpec((None, gd, gd), lambda g, t, k: (g, 0, 0)),
                 out_shape=jax.ShapeDtypeStruct((T, D), F32),
                 out_spec=pl.BlockSpec((tg, gd), lambda g, t, k: (t, g)), name="pool_dmixed")
    dub, _ = _pool_window(dmixed, True, "pool_window_bwd")
    g_pin = dw_square(a1t, dub, "pool_dwin")
    dx_pool = dx_square(dub, w_pin, "pool_dx")
    d_x0 = ffn_bwd([dxres, dx_pool], 0, 0, 0, s_f0, x0t)
    grad_x = _add2(d_x0[0], d_x0[1], "grad_x_add")[None]

    to_slots = lambda g2d: _device_to_slot_order(g2d.reshape(N_DEV, rs, D))
    g_grp_slots = _device_to_slot_order(
        g_grp.reshape(N_POOL_GROUPS, N_DEV, rs // 4, gd).transpose(1, 0, 2, 3).reshape(N_DEV, rs // 4, D))
    g_grp_slots = jnp.pad(g_grp_slots, ((0, 0), (0, 16 - rs // 4), (0, 0))) if rs // 4 < 16 else g_grp_slots
    bufs = g_gu + g_dt + [to_slots(g_pin), to_slots(g_pout), to_slots(g_aout), g_grp_slots, g_qkv]
    recv1 = _rs_sibling(bufs, "rs_sibling")
    chipsums = [_add_half(a, r, c_idx, f"rs_add{i}") for i, (a, r) in enumerate(zip(bufs, recv1))]
    recv2 = _rs_chips(chipsums, "rs_chips")
    gsum = [_sum_chips(a, r, q_idx, f"rs_sum{i}") for i, (a, r) in enumerate(zip(chipsums, recv2))]
    gw_gu, gw_dt = gsum[0:4], gsum[4:8]
    gw_pin, gw_pout, gw_aout, gw_grp, gw_qkv = gsum[8:13]

    small = jnp.concatenate([dgains[(i, s)] for i in range(DEPTH) for s in range(3)]
                            + [dbiases[(i, s)] for i in range(DEPTH) for s in range(3)]
                            + [dscale, jnp.zeros((3, D), F32)], axis=0)
    _, small_sum = _small_all_gather(small, "ag_small_grads")
    dev = 4 * lax.axis_index("x") + 2 * lax.axis_index("y") + lax.axis_index("c")
    mine = lax.dynamic_slice_in_dim(small_sum, dev * rs, rs, axis=1)
    grads = {
        "ffn1_w_gate": jnp.stack([gw_gu[2 * i][:, :fs] for i in range(DEPTH)]),
        "ffn1_w_up": jnp.stack([gw_gu[2 * i][:, fp:fp + fs] for i in range(DEPTH)]),
        "ffn1_w_down": jnp.stack([gw_dt[2 * i].T[:fs] for i in range(DEPTH)]),
        "ffn2_w_gate": jnp.stack([gw_gu[2 * i + 1][:, :fs] for i in range(DEPTH)]),
        "ffn2_w_up": jnp.stack([gw_gu[2 * i + 1][:, fp:fp + fs] for i in range(DEPTH)]),
        "ffn2_w_down": jnp.stack([gw_dt[2 * i + 1].T[:fs] for i in range(DEPTH)]),
        "ln_gain": mine[0:DEPTH * 3].reshape(DEPTH, 3, rs),
        "ln_bias": mine[DEPTH * 3:2 * DEPTH * 3].reshape(DEPTH, 3, rs),
        "pool_w_in": gw_pin[None],
        "pool_w_group": gw_grp[:rs // 4].reshape(N_POOL_GROUPS, rs // 4, gd)[None],
        "pool_scale": small_sum[2 * DEPTH * 3][None],
        "pool_w_out": gw_pout[None],
        "attn_w_qkv": gw_qkv[None],
        "attn_w_out": gw_aout[None],
    }
    weights = dict(ffn1_w_gate=ffn1_w_gate, ffn1_w_up=ffn1_w_up, ffn1_w_down=ffn1_w_down,
                   ffn2_w_gate=ffn2_w_gate, ffn2_w_up=ffn2_w_up, ffn2_w_down=ffn2_w_down,
                   ln_gain=ln_gain, ln_bias=ln_bias, pool_w_in=pool_w_in, pool_w_group=pool_w_group,
                   pool_scale=pool_scale, pool_w_out=pool_w_out, attn_w_qkv=attn_w_qkv, attn_w_out=attn_w_out)
    ms = dict(ffn1_w_gate=m_ffn1_w_gate, ffn1_w_up=m_ffn1_w_up, ffn1_w_down=m_ffn1_w_down,
              ffn2_w_gate=m_ffn2_w_gate, ffn2_w_up=m_ffn2_w_up, ffn2_w_down=m_ffn2_w_down,
              ln_gain=m_ln_gain, ln_bias=m_ln_bias, pool_w_in=m_pool_w_in, pool_w_group=m_pool_w_group,
              pool_scale=m_pool_scale, pool_w_out=m_pool_w_out, attn_w_qkv=m_attn_w_qkv, attn_w_out=m_attn_w_out)
    vs = dict(ffn1_w_gate=v_ffn1_w_gate, ffn1_w_up=v_ffn1_w_up, ffn1_w_down=v_ffn1_w_down,
              ffn2_w_gate=v_ffn2_w_gate, ffn2_w_up=v_ffn2_w_up, ffn2_w_down=v_ffn2_w_down,
              ln_gain=v_ln_gain, ln_bias=v_ln_bias, pool_w_in=v_pool_w_in, pool_w_group=v_pool_w_group,
              pool_scale=v_pool_scale, pool_w_out=v_pool_w_out, attn_w_qkv=v_attn_w_qkv, attn_w_out=v_attn_w_out)
    names = list(weights)
    deltas, new_m, new_v = {}, {}, {}
    for nme in names:
        deltas[nme], new_m[nme], new_v[nme] = _adamw(weights[nme], grads[nme], ms[nme], vs[nme], f"adamw_{nme}")
    return (loss, grad_x, *[grads[k] for k in names], *[deltas[k] for k in names],
            *[new_m[k] for k in names], *[new_v[k] for k in names])
```

```python
import functools

import numpy as np
import jax
import jax.numpy as jnp
from jax import lax
from jax.experimental import pallas as pl
from jax.experimental.pallas import tpu as pltpu

F32 = jnp.float32
BF16 = jnp.bfloat16

D = 1024
N_DEV = 8
N_HEADS = 16
HEAD_DIM = 64
N_POOL_GROUPS = 4
POOL_GROUP_DIM = 256
POOL_HALF = (1, 2, 4, 8)
DIL_CONFIGS = ((128, 1), (512, 4), (2048, 16))
ATTN_HALO = 64
ATTN_BLOCK = 128
QKV_SHARD = 3 * 3 * D // N_DEV
DEPTH = 2
ALPHA = (2.0 * DEPTH) ** 0.25
MACARON = 0.5
LN_EPS = 1e-5
MASK_VALUE = -1e30
ADAM_LR = 0.001
ADAM_B1 = 0.9
ADAM_B2 = 0.999
ADAM_EPS = 1e-08
ADAM_WD = 0.01
ADAM_STEP = 10
LANE = 128
VMEM_LIMIT = 56 * 1024 * 1024
MESH = pl.DeviceIdType.MESH
AXES = ("x", "y", "c")


def _round_up(n, m):
    return (n + m - 1) // m * m


def _pcall(body, deps=(), **kw):
    if not deps:
        return pl.pallas_call(body, **kw)
    n_in, n_dep = len(kw["in_specs"]), len(deps)

    def wrapped(*refs):
        return body(*refs[:n_in], *refs[n_in + n_dep:])

    kw["in_specs"] = list(kw["in_specs"]) + [pl.BlockSpec(memory_space=pl.ANY)] * n_dep
    call = pl.pallas_call(wrapped, **kw)
    return lambda *args: call(*args, *deps)


def _params(*sem):
    return pltpu.CompilerParams(dimension_semantics=sem, vmem_limit_bytes=VMEM_LIMIT)


def _alibi_slopes():
    n = len(DIL_CONFIGS) * N_HEADS
    s = 2.0 ** (-8.0 * np.arange(1, n + 1) / n)
    return s.reshape(len(DIL_CONFIGS), N_HEADS).astype(np.float32)


def _my_slot():
    return 4 * lax.axis_index("c") + 2 * lax.axis_index("x") + lax.axis_index("y")


def _mm(a, b, *, grid, a_spec, b_spec, out_shape, out_spec, nt=False, name, alias=None, deps=()):
    nk = grid[2]
    dn = (((1,), (1,)), ((), ())) if nt else (((1,), (0,)), ((), ()))
    blk = tuple(s for s in out_spec.block_shape if s is not None)

    def body(*refs):
        a_ref, b_ref = refs[0], refs[1]
        o_ref = refs[3] if alias is not None else refs[2]
        p = lax.dot_general(a_ref[...], b_ref[...], dn, preferred_element_type=F32)
        if nk == 1:
            o_ref[...] = p.astype(o_ref.dtype)
        else:
            acc = refs[-1]
            k = pl.program_id(2)

            @pl.when(k == 0)
            def _():
                acc[...] = p

            @pl.when(k > 0)
            def _():
                acc[...] += p

            @pl.when(k == nk - 1)
            def _():
                o_ref[...] = acc[...].astype(o_ref.dtype)

    in_specs = [a_spec, b_spec]
    args = [a, b]
    aliases = {}
    if alias is not None:
        in_specs.append(pl.BlockSpec(memory_space=pl.ANY))
        args.append(alias)
        aliases = {2: 0}
    return _pcall(
        body, deps=deps, name=name, grid=grid, in_specs=in_specs, out_specs=out_spec, out_shape=out_shape,
        scratch_shapes=[] if nk == 1 else [pltpu.VMEM(blk, F32)],
        input_output_aliases=aliases,
        compiler_params=_params("parallel", "parallel", "arbitrary"),
    )(*args)


def _transpose_cast(x, name, deps=()):
    T = x.shape[0]
    tm = min(512, T)

    def body(x_ref, xb_ref, xt_ref):
        v = x_ref[...]
        xb_ref[...] = v.astype(BF16)
        xt_ref[...] = v.T.astype(BF16)

    return _pcall(
        body, deps=deps, name=name, grid=(T // tm,),
        in_specs=[pl.BlockSpec((tm, D), lambda t: (t, 0))],
        out_specs=[pl.BlockSpec((tm, D), lambda t: (t, 0)), pl.BlockSpec((D, tm), lambda t: (0, t))],
        out_shape=[jax.ShapeDtypeStruct((T, D), BF16), jax.ShapeDtypeStruct((D, T), BF16)],
        compiler_params=_params("parallel"),
    )(x)


def _mm_ln(a, b, xres, gain, bias, hscale, name, deps=()):
    T, K = a.shape
    tm = min(512, T)

    def body(a_ref, b_ref, x_ref, g_ref, bt_ref, y_ref, yb_ref, yt_ref, xh_ref, rs_ref):
        h = jnp.dot(a_ref[...], b_ref[...], preferred_element_type=F32)
        z = ALPHA * x_ref[...] + hscale * h
        mu = jnp.mean(z, axis=-1, keepdims=True)
        zc = z - mu
        var = jnp.mean(zc * zc, axis=-1, keepdims=True)
        rstd = lax.rsqrt(var + LN_EPS)
        xh = zc * rstd
        y = xh * g_ref[...] + bt_ref[...]
        y_ref[...] = y
        yb_ref[...] = y.astype(BF16)
        yt_ref[...] = y.T.astype(BF16)
        xh_ref[...] = xh
        rs_ref[...] = rstd

    row = pl.BlockSpec((tm, D), lambda t: (t, 0))
    vec = pl.BlockSpec((1, D), lambda t: (0, 0))
    return _pcall(
        body, deps=deps, name=name, grid=(T // tm,),
        in_specs=[pl.BlockSpec((tm, K), lambda t: (t, 0)), pl.BlockSpec((K, D), lambda t: (0, 0)), row, vec, vec],
        out_specs=[row, row, pl.BlockSpec((D, tm), lambda t: (0, t)), row, pl.BlockSpec((tm, 1), lambda t: (t, 0))],
        out_shape=[jax.ShapeDtypeStruct((T, D), F32), jax.ShapeDtypeStruct((T, D), BF16),
                   jax.ShapeDtypeStruct((D, T), BF16), jax.ShapeDtypeStruct((T, D), F32),
                   jax.ShapeDtypeStruct((T, 1), F32)],
        compiler_params=_params("parallel"),
    )(a, b, xres, gain, bias)


def _ln_bwd(dys, xhat, rstd, gain, hscale, name):
    T = xhat.shape[0]
    tm = min(512, T)
    n = len(dys)

    def body(*refs):
        dy_refs = refs[:n]
        xh_ref, rs_ref, g_ref, dx_ref, dh_ref, dht_ref, dg_ref, db_ref = refs[n:]
        dy = dy_refs[0][...]
        for r in dy_refs[1:]:
            dy = dy + r[...]
        xh = xh_ref[...]
        dxh = dy * g_ref[...]
        m1 = jnp.mean(dxh, axis=-1, keepdims=True)
        m2 = jnp.mean(dxh * xh, axis=-1, keepdims=True)
        dz = rs_ref[...] * (dxh - m1 - xh * m2)
        dx_ref[...] = ALPHA * dz
        dh = hscale * dz
        dh_ref[...] = dh.astype(BF16)
        dht_ref[...] = dh.T.astype(BF16)
        dg = jnp.sum(dy * xh, axis=0, keepdims=True)
        db = jnp.sum(dy, axis=0, keepdims=True)

        @pl.when(pl.program_id(0) == 0)
        def _():
            dg_ref[...] = dg
            db_ref[...] = db

        @pl.when(pl.program_id(0) > 0)
        def _():
            dg_ref[...] += dg
            db_ref[...] += db

    row = pl.BlockSpec((tm, D), lambda t: (t, 0))
    vec = pl.BlockSpec((1, D), lambda t: (0, 0))
    return _pcall(
        body, name=name, grid=(T // tm,),
        in_specs=[row] * n + [row, pl.BlockSpec((tm, 1), lambda t: (t, 0)), vec],
        out_specs=[row, row, pl.BlockSpec((D, tm), lambda t: (0, t)), vec, vec],
        out_shape=[jax.ShapeDtypeStruct((T, D), F32), jax.ShapeDtypeStruct((T, D), BF16),
                   jax.ShapeDtypeStruct((D, T), BF16), jax.ShapeDtypeStruct((1, D), F32),
                   jax.ShapeDtypeStruct((1, D), F32)],
        compiler_params=_params("arbitrary"),
    )(*dys, xhat, rstd, gain)


def _add2(a, b, name):
    T = a.shape[0]
    tm = min(512, T)

    def body(a_ref, b_ref, o_ref):
        o_ref[...] = a_ref[...] + b_ref[...]

    row = pl.BlockSpec((tm, D), lambda t: (t, 0))
    return _pcall(body, name=name, grid=(T // tm,), in_specs=[row, row], out_specs=row,
                  out_shape=jax.ShapeDtypeStruct((T, D), F32), compiler_params=_params("parallel"))(a, b)


def _loss_head(y, tgt):
    T = y.shape[0]
    tm = min(512, T)

    def body(y_ref, t_ref, dy_ref, l_ref):
        e = y_ref[...] - t_ref[...]
        dy_ref[...] = e * (1.0 / D)
        part = jnp.sum(jnp.sum(e * e, axis=1, keepdims=True), axis=0, keepdims=True) * (0.5 / D)

        @pl.when(pl.program_id(0) == 0)
        def _():
            l_ref[...] = jnp.zeros_like(l_ref)

        l_ref[...] += part

    row = pl.BlockSpec((tm, D), lambda t: (t, 0))
    return _pcall(
        body, name="loss_head", grid=(T // tm,),
        in_specs=[row, row],
        out_specs=[row, pl.BlockSpec((8, LANE), lambda t: (0, 0))],
        out_shape=[jax.ShapeDtypeStruct((T, D), F32), jax.ShapeDtypeStruct((8, LANE), F32)],
        compiler_params=_params("arbitrary"),
    )(y, tgt)


def _ffn_up(xb, wgu, fp, name, deps=()):
    T = xb.shape[0]
    tm = min(1024, T)

    def body(x_ref, w_ref, g_ref, u_ref, a_ref):
        p = jnp.dot(x_ref[...], w_ref[...], preferred_element_type=F32)
        g = p[:, :fp]
        u = p[:, fp:]
        a = g * jax.nn.sigmoid(g) * u
        g_ref[...] = g.astype(BF16)
        u_ref[...] = u.astype(BF16)
        a_ref[...] = a.astype(BF16)

    out = pl.BlockSpec((tm, fp), lambda j, t: (t, j))
    shp = jax.ShapeDtypeStruct((T, N_DEV * fp), BF16)
    return _pcall(
        body, deps=deps, name=name, grid=(N_DEV, T // tm),
        in_specs=[pl.BlockSpec((tm, D), lambda j, t: (t, 0)),
                  pl.BlockSpec((None, D, 2 * fp), lambda j, t: (j, 0, 0))],
        out_specs=[out, out, out], out_shape=[shp, shp, shp],
        compiler_params=_params("parallel", "parallel"),
    )(xb, wgu)


def _ffn_bwd_act(dhb, wd, g, u, fp, name):
    T = dhb.shape[0]
    tm = min(1024, T)

    def body(dh_ref, w_ref, g_ref, u_ref, o_ref):
        da = lax.dot_general(dh_ref[...], w_ref[...], (((1,), (1,)), ((), ())), preferred_element_type=F32)
        gv = g_ref[...].astype(F32)
        uv = u_ref[...].astype(F32)
        sig = jax.nn.sigmoid(gv)
        dgate = da * uv * (sig * (1.0 + gv * (1.0 - sig)))
        dup = da * (gv * sig)
        o_ref[:, :fp] = dgate.astype(BF16)
        o_ref[:, fp:] = dup.astype(BF16)

    blk = pl.BlockSpec((tm, fp), lambda j, t: (t, j))
    return _pcall(
        body, name=name, grid=(N_DEV, T // tm),
        in_specs=[pl.BlockSpec((tm, D), lambda j, t: (t, 0)), pl.BlockSpec((fp, D), lambda j, t: (j, 0)), blk, blk],
        out_specs=pl.BlockSpec((tm, 2 * fp), lambda j, t: (t, j)),
        out_shape=jax.ShapeDtypeStruct((T, N_DEV * 2 * fp), BF16),
        compiler_params=_params("parallel", "parallel"),
    )(dhb, wd, g, u)


POOL_PAD = 16
POOL_CHUNK = 512


def _pool_window(v, transpose, name):
    T = v.shape[0]
    ch = min(POOL_CHUNK, T)
    ext = ch + 2 * POOL_PAD
    gd = POOL_GROUP_DIM

    def body(v_ref, o_ref, ot_ref, pad_ref):
        pad_ref[0:POOL_PAD, :] = jnp.zeros((POOL_PAD, gd), F32)
        pad_ref[POOL_PAD + T:POOL_PAD + T + POOL_PAD, :] = jnp.zeros((POOL_PAD, gd), F32)
        for gi, hw in enumerate(POOL_HALF):
            @pl.when(pl.program_id(0) == gi)
            def _(hw=hw):
                def count(t):
                    return (jnp.minimum(t + hw, T) - jnp.maximum(t - hw, 0)).astype(F32)

                if transpose:
                    t_all = lax.broadcasted_iota(jnp.int32, (T, gd), 0)
                    pad_ref[POOL_PAD:POOL_PAD + T, :] = v_ref[...] / count(t_all)
                else:
                    pad_ref[POOL_PAD:POOL_PAD + T, :] = v_ref[...]
                shift = hw if transpose else hw - 1
                for c in range(T // ch):
                    e = pad_ref[c * ch:c * ch + ext, :]
                    step = 1
                    while step < 2 * hw:
                        e = e + pltpu.roll(e, step, 0)
                        step *= 2
                    if shift:
                        e = pltpu.roll(e, ext - shift, 0)
                    s = e[POOL_PAD:POOL_PAD + ch, :]
                    center = v_ref[c * ch:(c + 1) * ch, :]
                    if transpose:
                        res = s - center
                    else:
                        t_idx = c * ch + lax.broadcasted_iota(jnp.int32, (ch, gd), 0)
                        res = s / count(t_idx) - center
                    o_ref[c * ch:(c + 1) * ch, :] = res.astype(BF16)
                    ot_ref[:, c * ch:(c + 1) * ch] = res.T.astype(BF16)

    return _pcall(
        body, name=name, grid=(N_POOL_GROUPS,),
        in_specs=[pl.BlockSpec((T, gd), lambda g: (0, g))],
        out_specs=[pl.BlockSpec((T, gd), lambda g: (0, g)), pl.BlockSpec((gd, T), lambda g: (g, 0))],
        out_shape=[jax.ShapeDtypeStruct((T, D), BF16), jax.ShapeDtypeStruct((D, T), BF16)],
        scratch_shapes=[pltpu.VMEM((T + 2 * POOL_PAD, gd), F32)],
        compiler_params=_params("arbitrary"),
    )(v)


def _pool_group(mixedb, wgroup, scale):
    T = mixedb.shape[0]
    tm = min(1024, T)
    gd = POOL_GROUP_DIM

    def body(a_ref, w_ref, s_ref, y_ref, ys_ref, yst_ref):
        y = jnp.dot(a_ref[...], w_ref[...], preferred_element_type=F32)
        ys = y * s_ref[...]
        y_ref[...] = y
        ys_ref[...] = ys.astype(BF16)
        yst_ref[...] = ys.T.astype(BF16)

    blk = pl.BlockSpec((tm, gd), lambda g, t: (t, g))
    return _pcall(
        body, name="pool_group", grid=(N_POOL_GROUPS, T // tm),
        in_specs=[blk, pl.BlockSpec((None, gd, gd), lambda g, t: (g, 0, 0)), pl.BlockSpec((1, gd), lambda g, t: (0, g))],
        out_specs=[blk, blk, pl.BlockSpec((gd, tm), lambda g, t: (g, t))],
        out_shape=[jax.ShapeDtypeStruct((T, D), F32), jax.ShapeDtypeStruct((T, D), BF16),
                   jax.ShapeDtypeStruct((D, T), BF16)],
        compiler_params=_params("parallel", "parallel"),
    )(mixedb, wgroup, scale)


def _pool_bwd_out(dmb, w_out, y, scale):
    T = dmb.shape[0]
    tm = min(512, T)

    def body(a_ref, w_ref, y_ref, s_ref, dy_ref, ds_ref):
        dys = lax.dot_general(a_ref[...], w_ref[...], (((1,), (1,)), ((), ())), preferred_element_type=F32)
        dy_ref[...] = (dys * s_ref[...]).astype(BF16)
        part = jnp.sum(dys * y_ref[...], axis=0, keepdims=True)

        @pl.when(pl.program_id(0) == 0)
        def _():
            ds_ref[...] = part

        @pl.when(pl.program_id(0) > 0)
        def _():
            ds_ref[...] += part

    row = pl.BlockSpec((tm, D), lambda t: (t, 0))
    vec = pl.BlockSpec((1, D), lambda t: (0, 0))
    return _pcall(
        body, name="pool_bwd_out", grid=(T // tm,),
        in_specs=[row, pl.BlockSpec((D, D), lambda t: (0, 0)), row, vec],
        out_specs=[row, vec],
        out_shape=[jax.ShapeDtypeStruct((T, D), BF16), jax.ShapeDtypeStruct((1, D), F32)],
        compiler_params=_params("arbitrary"),
    )(dmb, w_out, y, scale)


def _attn_masks(n, L, d):
    w = ATTN_BLOCK + 2 * ATTN_HALO
    a = lax.broadcasted_iota(jnp.int32, (ATTN_BLOCK, w), 0)
    c = lax.broadcasted_iota(jnp.int32, (ATTN_BLOCK, w), 1)
    rel = c - ATTN_HALO - a
    j = n * ATTN_BLOCK - ATTN_HALO + c
    valid = (jnp.abs(rel) <= ATTN_HALO) & (j >= 0) & (j < L)
    dist = (d * jnp.abs(rel)).astype(F32)
    return valid, dist


def _lane_col(st, idx):
    lane = lax.broadcasted_iota(jnp.int32, st.shape, 1)
    return jnp.sum(jnp.where(lane == idx, st, 0.0), axis=1, keepdims=True)


def _window_specs(nb, d, col, width):
    last = 2 * d * nb - 1

    def prev(r, n):
        return (jnp.maximum(2 * (r * nb + n) - 1, 0), col)

    def cur(r, n):
        return (r * nb + n, col)

    def nxt(r, n):
        return (jnp.minimum(2 * (r * nb + n) + 2, last), col)

    return [pl.BlockSpec((ATTN_HALO, width), prev), pl.BlockSpec((ATTN_BLOCK, width), cur),
            pl.BlockSpec((ATTN_HALO, width), nxt)]


def _attn_fwd(qkv_g, d, slopes, name):
    T = qkv_g.shape[0]
    L = T // d
    nb = L // ATTN_BLOCK

    def body(q_ref, kp_ref, kc_ref, kn_ref, vp_ref, vc_ref, vn_ref, o_ref, lse_ref):
        valid, dist = _attn_masks(pl.program_id(1), L, d)
        lane = lax.broadcasted_iota(jnp.int32, (ATTN_BLOCK, LANE), 1)
        first = lane < HEAD_DIM
        head_mask = [jnp.where(first, 1.0, 0.0).astype(BF16), jnp.where(first, 0.0, 1.0).astype(BF16)]
        lse_acc = jnp.zeros((ATTN_BLOCK, LANE), F32)
        for hp in range(N_HEADS // 2):
            cs = slice(hp * LANE, (hp + 1) * LANE)
            q2 = q_ref[:, cs]
            k2 = jnp.concatenate([kp_ref[:, cs], kc_ref[:, cs], kn_ref[:, cs]], axis=0)
            v2 = jnp.concatenate([vp_ref[:, cs], vc_ref[:, cs], vn_ref[:, cs]], axis=0)
            outs = []
            for hh in range(2):
                h = 2 * hp + hh
                qh = q2 * head_mask[hh]
                s = lax.dot_general(qh, k2, (((1,), (1,)), ((), ())), preferred_element_type=F32)
                s = s * (HEAD_DIM ** -0.5) - float(slopes[h]) * dist
                s = jnp.where(valid, s, MASK_VALUE)
                m = jnp.max(s, axis=1, keepdims=True)
                p = jnp.exp(s - m)
                l = jnp.sum(p, axis=1, keepdims=True)
                o = jnp.dot(p.astype(BF16), v2, preferred_element_type=F32) / l
                outs.append(o)
                lse_acc = jnp.where(lane == h, m + jnp.log(l), lse_acc)
            o_ref[:, cs] = jnp.where(first, outs[0], outs[1])
        lse_ref[...] = lse_acc

    specs = ([pl.BlockSpec((ATTN_BLOCK, D), lambda r, n: (r * nb + n, 0))]
             + _window_specs(nb, d, 1, D) + _window_specs(nb, d, 2, D))
    row = lambda w: pl.BlockSpec((ATTN_BLOCK, w), lambda r, n: (r * nb + n, 0))
    return _pcall(
        body, name=name, grid=(d, nb), in_specs=specs,
        out_specs=[row(D), row(LANE)],
        out_shape=[jax.ShapeDtypeStruct((T, D), F32), jax.ShapeDtypeStruct((T, LANE), F32)],
        compiler_params=_params("parallel", "parallel"),
    )(*([qkv_g] * 7))


def _attn_combine(os_, lses):
    T = os_[0].shape[0]
    tm = min(256, T)
    ng = len(os_)

    def body(*refs):
        o_refs = refs[:ng]
        l_refs = refs[ng:2 * ng]
        o32_ref, ob_ref, ot_ref, lt_ref = refs[2 * ng:]
        ls = [r[...] for r in l_refs]
        m = ls[0]
        for l in ls[1:]:
            m = jnp.maximum(m, l)
        tot = jnp.exp(ls[0] - m)
        for l in ls[1:]:
            tot = tot + jnp.exp(l - m)
        lt = m + jnp.log(tot)
        lt_ref[...] = lt
        ws = [jnp.exp(l - lt) for l in ls]
        lane = lax.broadcasted_iota(jnp.int32, (tm, LANE), 1)
        first = lane < HEAD_DIM
        for hp in range(N_HEADS // 2):
            cs = slice(hp * LANE, (hp + 1) * LANE)
            acc = jnp.zeros((tm, LANE), F32)
            for g in range(ng):
                wt = jnp.where(first, _lane_col(ws[g], 2 * hp), _lane_col(ws[g], 2 * hp + 1))
                acc = acc + wt * o_refs[g][:, cs]
            o32_ref[:, cs] = acc
            ob_ref[:, cs] = acc.astype(BF16)
        ot_ref[...] = o32_ref[...].T.astype(BF16)

    row = pl.BlockSpec((tm, D), lambda t: (t, 0))
    st = pl.BlockSpec((tm, LANE), lambda t: (t, 0))
    return _pcall(
        body, name="attn_combine", grid=(T // tm,),
        in_specs=[row] * ng + [st] * ng,
        out_specs=[row, row, pl.BlockSpec((D, tm), lambda t: (0, t)), st],
        out_shape=[jax.ShapeDtypeStruct((T, D), F32), jax.ShapeDtypeStruct((T, D), BF16),
                   jax.ShapeDtypeStruct((D, T), BF16), jax.ShapeDtypeStruct((T, LANE), F32)],
        compiler_params=_params("parallel"),
    )(*os_, *lses)


def _attn_bwd_prep(dmb, w_out, o32, lse_tot):
    T = dmb.shape[0]
    tm = min(512, T)

    def body(a_ref, w_ref, o_ref, l_ref, do_ref, st_ref):
        do = lax.dot_general(a_ref[...], w_ref[...], (((1,), (1,)), ((), ())), preferred_element_type=F32)
        do_ref[...] = do.astype(BF16)
        prod = do * o_ref[...]
        lane = lax.broadcasted_iota(jnp.int32, (tm, LANE), 1)
        first = lane < HEAD_DIM
        st = jnp.where(lane < N_HEADS, l_ref[...], 0.0)
        for hp in range(N_HEADS // 2):
            pr = prod[:, hp * LANE:(hp + 1) * LANE]
            d0 = jnp.sum(jnp.where(first, pr, 0.0), axis=1, keepdims=True)
            d1 = jnp.sum(jnp.where(first, 0.0, pr), axis=1, keepdims=True)
            st = jnp.where(lane == N_HEADS + 2 * hp, d0, st)
            st = jnp.where(lane == N_HEADS + 2 * hp + 1, d1, st)
        st_ref[...] = st

    row = pl.BlockSpec((tm, D), lambda t: (t, 0))
    stb = pl.BlockSpec((tm, LANE), lambda t: (t, 0))
    return _pcall(
        body, name="attn_bwd_prep", grid=(T // tm,),
        in_specs=[row, pl.BlockSpec((D, D), lambda t: (0, 0)), row, stb],
        out_specs=[row, stb],
        out_shape=[jax.ShapeDtypeStruct((T, D), BF16), jax.ShapeDtypeStruct((T, LANE), F32)],
        compiler_params=_params("parallel"),
    )(dmb, w_out, o32, lse_tot)


def _attn_bwd(qkv_g, do_g, st_g, d, slopes, name):
    T = qkv_g.shape[0]
    L = T // d
    nb = L // ATTN_BLOCK
    scale = HEAD_DIM ** -0.5
    nt = (((1,), (1,)), ((), ()))

    def body(qp_ref, qc_ref, qn_ref, kp_ref, kc_ref, kn_ref, vp_ref, vc_ref, vn_ref,
             dp_ref, dc_ref, dn_ref, sp_ref, sc_ref, sn_ref, o_ref):
        valid, dist = _attn_masks(pl.program_id(1), L, d)
        lane = lax.broadcasted_iota(jnp.int32, (ATTN_BLOCK, LANE), 1)
        first = lane < HEAD_DIM
        head_mask = [jnp.where(first, 1.0, 0.0).astype(BF16), jnp.where(first, 0.0, 1.0).astype(BF16)]
        stc = sc_ref[...]
        stw_t = jnp.concatenate([sp_ref[...], stc, sn_ref[...]], axis=0).T
        for hp in range(N_HEADS // 2):
            cs = slice(hp * LANE, (hp + 1) * LANE)
            cat = lambda a, b, c: jnp.concatenate([a[:, cs], b[:, cs], c[:, cs]], axis=0)
            q2, k2, v2, do2 = qc_ref[:, cs], kc_ref[:, cs], vc_ref[:, cs], dc_ref[:, cs]
            qw, kw, vw, dow = cat(qp_ref, qc_ref, qn_ref), cat(kp_ref, kc_ref, kn_ref), cat(vp_ref, vc_ref, vn_ref), cat(dp_ref, dc_ref, dn_ref)
            dqs, dks, dvs = [], [], []
            for hh in range(2):
                h = 2 * hp + hh
                pick = lambda t, hh=hh: t * head_mask[hh]
                bias = float(slopes[h]) * dist
                s = lax.dot_general(pick(q2), kw, nt, preferred_element_type=F32) * scale - bias
                s = jnp.where(valid, s, MASK_VALUE)
                p = jnp.exp(s - _lane_col(stc, h))
                dp = lax.dot_general(pick(do2), vw, nt, preferred_element_type=F32)
                ds = p * (dp - _lane_col(stc, N_HEADS + h)) * scale
                dqs.append(jnp.dot(ds.astype(BF16), kw, preferred_element_type=F32))
                st_ = lax.dot_general(pick(k2), qw, nt, preferred_element_type=F32) * scale - bias
                st_ = jnp.where(valid, st_, MASK_VALUE)
                pt = jnp.exp(st_ - stw_t[h:h + 1, :])
                dvs.append(jnp.dot(pt.astype(BF16), dow, preferred_element_type=F32))
                dpt = lax.dot_general(pick(v2), dow, nt, preferred_element_type=F32)
                dst = pt * (dpt - stw_t[N_HEADS + h:N_HEADS + h + 1, :]) * scale
                dks.append(jnp.dot(dst.astype(BF16), qw, preferred_element_type=F32))
            o_ref[:, hp * LANE:(hp + 1) * LANE] = jnp.where(first, dqs[0], dqs[1]).astype(BF16)
            o_ref[:, D + hp * LANE:D + (hp + 1) * LANE] = jnp.where(first, dks[0], dks[1]).astype(BF16)
            o_ref[:, 2 * D + hp * LANE:2 * D + (hp + 1) * LANE] = jnp.where(first, dvs[0], dvs[1]).astype(BF16)

    specs = (_window_specs(nb, d, 0, D) + _window_specs(nb, d, 1, D) + _window_specs(nb, d, 2, D)
             + _window_specs(nb, d, 0, D) + _window_specs(nb, d, 0, LANE))
    return _pcall(
        body, name=name, grid=(d, nb), in_specs=specs,
        out_specs=pl.BlockSpec((ATTN_BLOCK, 3 * D), lambda r, n: (r * nb + n, 0)),
        out_shape=jax.ShapeDtypeStruct((T, 3 * D), BF16),
        compiler_params=_params("parallel", "parallel"),
    )(*([qkv_g] * 9), *([do_g] * 3), *([st_g] * 3))


def _to_sub(a, d):
    if d == 1:
        return a
    T, C = a.shape
    return a.reshape(T // d, d, C).transpose(1, 0, 2).reshape(T, C)


def _from_sub(a, d):
    if d == 1:
        return a
    T, C = a.shape
    return a.reshape(d, T // d, C).transpose(1, 0, 2).reshape(T, C)


HBM_SPEC = pl.BlockSpec(memory_space=pltpu.HBM)
SEM_SPEC = pl.BlockSpec(memory_space=pltpu.SEMAPHORE)
ANY_SPEC = pl.BlockSpec(memory_space=pl.ANY)
DATAFLOW = pltpu.SideEffectType.DATAFLOW_SIDE_EFFECTING


def _me_and_peers():
    x, y, c = lax.axis_index("x"), lax.axis_index("y"), lax.axis_index("c")
    return (x, y, c), [(x, y, 1 - c), (1 - x, y, c), (x, 1 - y, c), (1 - x, 1 - y, c)]


def _slot(px, py, pc):
    return 4 * pc + 2 * px + py


def _split_start(srcs, lands, after, start_copies, n_sem, name):
    n = len(srcs)
    n_after = len(after)

    def body(*refs):
        src_refs, land_refs = refs[:n], refs[n:2 * n]
        send_sems, recv_sems = refs[2 * n + n_after], refs[2 * n + n_after + 1]
        token = refs[-1]
        start_copies(src_refs, land_refs, send_sems, recv_sems)
        token[...] = jnp.zeros_like(token)

    outs = _pcall(
        body, name=name,
        in_specs=[HBM_SPEC] * (2 * n) + [ANY_SPEC] * n_after,
        out_shape=(pltpu.SemaphoreType.DMA(n_sem), pltpu.SemaphoreType.DMA(n_sem),
                   *[pltpu.HBM(a.shape, a.dtype) for a in srcs], *[pltpu.HBM(a.shape, a.dtype) for a in lands],
                   jax.ShapeDtypeStruct((8, LANE), F32)),
        out_specs=(SEM_SPEC, SEM_SPEC, *[HBM_SPEC] * (2 * n), pl.BlockSpec(memory_space=pltpu.VMEM)),
        input_output_aliases={i: 2 + i for i in range(2 * n)},
        compiler_params=pltpu.CompilerParams(has_side_effects=DATAFLOW),
    )(*[pltpu.with_memory_space_constraint(a, pltpu.HBM) for a in srcs],
      *[pltpu.with_memory_space_constraint(a, pltpu.HBM) for a in lands], *after)
    return outs[0], outs[1], list(outs[2:2 + n]), list(outs[2 + n:2 + 2 * n]), outs[-1]


def _split_wait(handle, after, wait_copies, name):
    send_sems, recv_sems, srcs, lands, _ = handle
    n = len(srcs)

    def body(*refs):
        src_refs, land_refs = refs[:n], refs[n:2 * n]
        wait_copies(src_refs, land_refs, refs[2 * n], refs[2 * n + 1])

    outs = _pcall(
        body, name=name,
        in_specs=[HBM_SPEC] * (2 * n) + [SEM_SPEC, SEM_SPEC] + [ANY_SPEC] * len(after),
        out_shape=tuple(pltpu.HBM(a.shape, a.dtype) for a in srcs + lands),
        out_specs=tuple([HBM_SPEC] * (2 * n)),
        input_output_aliases={i: i for i in range(2 * n)},
        compiler_params=pltpu.CompilerParams(has_side_effects=DATAFLOW),
    )(*srcs, *lands, send_sems, recv_sems, *after)
    return list(outs[:n]), list(outs[n:])


def _ag_copies(src_refs, land_refs, send_sems, recv_sems, received):
    me, peers = _me_and_peers()
    cps = []
    for i in range(len(src_refs)):
        for k, to in enumerate(peers):
            cps.append(pltpu.make_async_remote_copy(
                src_ref=src_refs[i], dst_ref=land_refs[i].at[_slot(*(to if received else me))],
                send_sem=send_sems.at[4 * i + k], recv_sem=recv_sems.at[4 * i + k], device_id=to,
                device_id_type=MESH))
    return cps


def _ag_start(shards, after, name):
    lands = [lax.empty((N_DEV,) + a.shape, a.dtype) for a in shards]

    def start(src_refs, land_refs, send_sems, recv_sems):
        for cp in _ag_copies(src_refs, land_refs, send_sems, recv_sems, False):
            cp.start()

    return _split_start(shards, lands, after, start, (4 * len(shards),), name)


def _ag_finish(handle, after, name):
    def wait(src_refs, land_refs, send_sems, recv_sems):
        for cp in _ag_copies(src_refs, land_refs, send_sems, recv_sems, True):
            cp.wait_send()
            cp.wait_recv()

    shards, lands = _split_wait(handle, after, wait, name + "_wait")
    n = len(shards)

    def body(*refs):
        src_refs, out_refs = refs[:n], refs[2 * n:3 * n]
        send_sems, recv_sems, local_sems = refs[3 * n:]
        me, peers = _me_and_peers()
        mine = [pltpu.make_async_copy(src_refs[i], out_refs[i].at[_slot(*me)], local_sems.at[i]) for i in range(n)]
        for cp in mine:
            cp.start()
        cps = []
        for i in range(n):
            for j, chip in enumerate(peers[1:]):
                blk = out_refs[i].at[_slot(*chip)]
                cps.append(pltpu.make_async_remote_copy(
                    src_ref=blk, dst_ref=blk, send_sem=send_sems.at[i, j], recv_sem=recv_sems.at[i, j],
                    device_id=peers[0], device_id_type=MESH))
        for cp in cps:
            cp.start()
        for cp in cps:
            cp.wait()
        for cp in mine:
            cp.wait()

    outs = _pcall(
        body, name=name + "_pass",
        in_specs=[ANY_SPEC] * (2 * n), out_specs=[ANY_SPEC] * n,
        out_shape=[jax.ShapeDtypeStruct(a.shape, a.dtype) for a in lands],
        input_output_aliases={n + i: i for i in range(n)},
        scratch_shapes=[pltpu.SemaphoreType.DMA((n, 3)), pltpu.SemaphoreType.DMA((n, 3)),
                        pltpu.SemaphoreType.DMA((n,))],
    )(*shards, *lands)
    return list(outs)


def _small_all_gather(v, name):
    R, C = v.shape

    def body(x_ref, out_ref, sum_ref, send_sems, recv_sems, local_sem):
        x, y, c = lax.axis_index("x"), lax.axis_index("y"), lax.axis_index("c")
        me, sibling = (x, y, c), (x, y, 1 - c)
        chips = [(1 - x, y), (x, 1 - y), (1 - x, 1 - y)]

        def rows(px, py, pc):
            return out_ref.at[4 * pc + 2 * px + py]

        def copy(k, block, to, src=None):
            return pltpu.make_async_remote_copy(
                src_ref=rows(*block) if src is None else src, dst_ref=rows(*block),
                send_sem=send_sems.at[k], recv_sem=recv_sems.at[k],
                device_id=to, device_id_type=MESH)

        mine = pltpu.make_async_copy(x_ref, rows(*me), local_sem)
        mine.start()
        first = [copy(0, me, sibling, src=x_ref)]
        first += [copy(1 + j, me, (*chip, c), src=x_ref) for j, chip in enumerate(chips)]
        for cp in first:
            cp.start()
        passed = [copy(4 + j, (*chip, c), sibling) for j, chip in enumerate(chips)]
        for j, chip in enumerate(chips):
            copy(1 + j, (*chip, c), me).wait_recv()
            passed[j].start()
        copy(0, sibling, me).wait_recv()
        for j, chip in enumerate(chips):
            copy(4 + j, (*chip, 1 - c), me).wait_recv()
        for cp in first + passed:
            cp.wait_send()
        mine.wait()
        acc = out_ref[0]
        for s in range(1, N_DEV):
            acc = acc + out_ref[s]
        sum_ref[...] = acc

    vm = pl.BlockSpec(memory_space=pltpu.VMEM)
    return _pcall(
        body, name=name, in_specs=[vm], out_specs=[vm, vm],
        out_shape=[jax.ShapeDtypeStruct((N_DEV, R, C), v.dtype), jax.ShapeDtypeStruct((R, C), v.dtype)],
        scratch_shapes=[pltpu.SemaphoreType.DMA((7,)), pltpu.SemaphoreType.DMA((7,)), pltpu.SemaphoreType.DMA],
    )(v)


def _rs_sibling(arrs, name):
    n = len(arrs)

    def body(*refs):
        ins, outs = refs[:n], refs[n:2 * n]
        send_sems, recv_sems = refs[2 * n:]
        x, y, c = lax.axis_index("x"), lax.axis_index("y"), lax.axis_index("c")
        cps = [pltpu.make_async_remote_copy(
            src_ref=ins[i].at[pl.ds(4 * (1 - c), 4)], dst_ref=outs[i],
            send_sem=send_sems.at[i], recv_sem=recv_sems.at[i],
            device_id=(x, y, 1 - c), device_id_type=MESH) for i in range(n)]
        for cp in cps:
            cp.start()
        for cp in cps:
            cp.wait()

    hbm = pl.BlockSpec(memory_space=pl.ANY)
    return _pcall(
        body, name=name, in_specs=[hbm] * n, out_specs=[hbm] * n,
        out_shape=[jax.ShapeDtypeStruct((4,) + a.shape[1:], a.dtype) for a in arrs],
        scratch_shapes=[pltpu.SemaphoreType.DMA((n,)), pltpu.SemaphoreType.DMA((n,))],
    )(*arrs)


def _rs_copies(src_refs, land_refs, send_sems, recv_sems):
    _, peers = _me_and_peers()
    cps = []
    for i in range(len(src_refs)):
        for j, (px, py, pc) in enumerate(peers[1:]):
            cps.append(pltpu.make_async_remote_copy(
                src_ref=src_refs[i].at[2 * px + py], dst_ref=land_refs[i].at[j],
                send_sem=send_sems.at[3 * i + j], recv_sem=recv_sems.at[3 * i + j],
                device_id=(px, py, pc), device_id_type=MESH))
    return cps


def _rs_start(chipsums, after, name):
    lands = [lax.empty((3,) + a.shape[1:], a.dtype) for a in chipsums]

    def start(src_refs, land_refs, send_sems, recv_sems):
        for cp in _rs_copies(src_refs, land_refs, send_sems, recv_sems):
            cp.start()

    return _split_start(chipsums, lands, after, start, (3 * len(chipsums),), name)


def _rs_wait(handle, after, name):
    def wait(src_refs, land_refs, send_sems, recv_sems):
        for cp in _rs_copies(src_refs, land_refs, send_sems, recv_sems):
            cp.wait_send()
            cp.wait_recv()

    return _split_wait(handle, after, wait, name)


def _row_tile(R, C, itemsize=4, budget=2 * 1024 * 1024):
    best = None
    for t in range(16, R + 1, 16):
        if R % t == 0 and t * C * itemsize <= budget:
            best = t
    return best if best is not None else R


def _add_half(arr, recv, c_idx, name):
    _, R, C = arr.shape
    tr = _row_tile(R, C)

    def body(c_ref, a_ref, r_ref, o_ref):
        o_ref[...] = (a_ref[...].astype(F32) + r_ref[...].astype(F32)).astype(o_ref.dtype)

    gs = pltpu.PrefetchScalarGridSpec(
        num_scalar_prefetch=1, grid=(4, R // tr),
        in_specs=[pl.BlockSpec((None, tr, C), lambda q, i, c_ref: (4 * c_ref[0] + q, i, 0)),
                  pl.BlockSpec((None, tr, C), lambda q, i, c_ref: (q, i, 0))],
        out_specs=pl.BlockSpec((None, tr, C), lambda q, i, c_ref: (q, i, 0)))
    return _pcall(body, name=name, grid_spec=gs, out_shape=jax.ShapeDtypeStruct((4, R, C), arr.dtype),
                  compiler_params=_params("parallel", "parallel"))(c_idx, arr, recv)


def _sum_chips(chipsum, recv, q_idx, name):
    _, R, C = chipsum.shape
    tr = _row_tile(R, C)

    def body(q_ref, a_ref, r_ref, o_ref):
        acc = a_ref[...].astype(F32)
        for j in range(3):
            acc = acc + r_ref[j].astype(F32)
        o_ref[...] = acc

    gs = pltpu.PrefetchScalarGridSpec(
        num_scalar_prefetch=1, grid=(R // tr,),
        in_specs=[pl.BlockSpec((None, tr, C), lambda i, q_ref: (q_ref[0], i, 0)),
                  pl.BlockSpec((3, tr, C), lambda i, q_ref: (0, i, 0))],
        out_specs=pl.BlockSpec((tr, C), lambda i, q_ref: (i, 0)))
    return _pcall(body, name=name, grid_spec=gs, out_shape=jax.ShapeDtypeStruct((R, C), F32),
                  compiler_params=_params("parallel"))(q_idx, chipsum, recv)


def _adamw(w, g, m, v, name):
    shape = w.shape
    C = shape[-1]
    R = int(np.prod(shape[:-1]))
    tr = _row_tile(R, C, budget=1024 * 1024)

    def body(w_ref, g_ref, m_ref, v_ref, d_ref, nm_ref, nv_ref):
        gv = g_ref[...]
        mv = ADAM_B1 * m_ref[...] + (1.0 - ADAM_B1) * gv
        vv = ADAM_B2 * v_ref[...] + (1.0 - ADAM_B2) * jnp.square(gv)
        m_hat = mv / (1.0 - ADAM_B1 ** ADAM_STEP)
        v_hat = vv / (1.0 - ADAM_B2 ** ADAM_STEP)
        d_ref[...] = -ADAM_LR * (m_hat / (jnp.sqrt(v_hat) + ADAM_EPS) + ADAM_WD * w_ref[...])
        nm_ref[...] = mv
        nv_ref[...] = vv

    blk = pl.BlockSpec((tr, C), lambda i: (i, 0))
    shp = jax.ShapeDtypeStruct((R, C), F32)
    outs = _pcall(body, name=name, grid=(R // tr,), in_specs=[blk] * 4, out_specs=[blk] * 3,
                  out_shape=[shp] * 3, compiler_params=_params("parallel"))(
        w.reshape(R, C), g.reshape(R, C), m.reshape(R, C), v.reshape(R, C))
    return tuple(o.reshape(shape) for o in outs)


def _pad_cols(w, width):
    return jnp.pad(w, ((0, 0), (0, width - w.shape[1])))


def _pad_rows(w, height):
    return jnp.pad(w, ((0, height - w.shape[0]), (0, 0)))


def _slot_to_device_order(a):
    s = a.shape
    return a.reshape((2, 4) + s[1:]).swapaxes(0, 1).reshape(s)


def _device_to_slot_order(a):
    s = a.shape
    return a.reshape((4, 2) + s[1:]).swapaxes(0, 1).reshape(s)


def kernel(x, ffn1_w_gate, ffn1_w_up, ffn1_w_down, ffn2_w_gate, ffn2_w_up, ffn2_w_down, ln_gain, ln_bias, pool_w_in, pool_w_group, pool_scale, pool_w_out, attn_w_qkv, attn_w_out, loss_target, m_ffn1_w_gate, m_ffn1_w_up, m_ffn1_w_down, m_ffn2_w_gate, m_ffn2_w_up, m_ffn2_w_down, m_ln_gain, m_ln_bias, m_pool_w_in, m_pool_w_group, m_pool_scale, m_pool_w_out, m_attn_w_qkv, m_attn_w_out, v_ffn1_w_gate, v_ffn1_w_up, v_ffn1_w_down, v_ffn2_w_gate, v_ffn2_w_up, v_ffn2_w_down, v_ln_gain, v_ln_bias, v_pool_w_in, v_pool_w_group, v_pool_scale, v_pool_w_out, v_attn_w_qkv, v_attn_w_out):
    T = x.shape[1]
    fs = ffn1_w_gate.shape[2]
    fp = _round_up(fs, LANE)
    rs = D // N_DEV
    x0 = x[0]
    tgt = loss_target[0]
    slopes = _alibi_slopes()
    c_idx = lax.axis_index("c").astype(jnp.int32).reshape(1)
    q_idx = (2 * lax.axis_index("x") + lax.axis_index("y")).astype(jnp.int32).reshape(1)

    gates = (ffn1_w_gate, ffn2_w_gate)
    ups = (ffn1_w_up, ffn2_w_up)
    downs = (ffn1_w_down, ffn2_w_down)
    ffns = [(i, k) for i in range(DEPTH) for k in range(2)]
    wgu_sh = [jnp.concatenate([_pad_cols(gates[k][i], fp), _pad_cols(ups[k][i], fp)], axis=1).astype(BF16)
              for i, k in ffns]
    wd_sh = [_pad_rows(downs[k][i], fp).astype(BF16) for i, k in ffns]
    sq_sh = jnp.concatenate([wd_sh[0], pool_w_in[0].astype(BF16), pool_w_out[0].astype(BF16),
                             pool_w_group[0].reshape(rs // 4, D).astype(BF16)], axis=0)
    qkv_sh = attn_w_qkv[0].astype(BF16)
    aout_sh = attn_w_out[0].astype(BF16)
    ln_sh = jnp.concatenate([ln_gain.reshape(DEPTH * 3, rs), ln_bias.reshape(DEPTH * 3, rs),
                             jnp.zeros((4, rs), F32)], axis=0)

    h0 = _ag_start([wgu_sh[0], ln_sh], (), "ag0")
    h1 = _ag_start([sq_sh], (h0[4],), "ag1")
    x0b, x0t = _transpose_cast(x0, "x_cast", deps=(h1[4],))
    wgu0, ln_all = _ag_finish(h0, (x0b,), "ag0")
    h2 = _ag_start([wgu_sh[1], wd_sh[1]], (wgu0,), "ag2")
    ln_all = _slot_to_device_order(ln_all).transpose(1, 0, 2).reshape(16, D)
    gain = lambda i, s: ln_all[3 * i + s][None]
    bias = lambda i, s: ln_all[DEPTH * 3 + 3 * i + s][None]

    def ffn_fwd(xf, xb, wgu, wd, f, i, s, dep_up=(), dep_down=()):
        g, u, act = _ffn_up(xb, wgu, fp, f"ffn_up{f}", deps=dep_up)
        wd = wd(act) if callable(wd) else wd
        y, yb, yt, xh, rstd = _mm_ln(act, wd, xf, gain(i, s), bias(i, s), MACARON, f"ffn_down_ln{f}",
                                     deps=dep_down() if callable(dep_down) else dep_down)
        return (y, yb, yt), dict(g=g, u=u, act=act, xh=xh, rstd=rstd, wgu=wgu, wd=wd)

    pool_w = {}

    def wd0_after(act):
        (sq_all,) = _ag_finish(h1, (act,), "ag1")
        pool_w["h3"] = _ag_start([wgu_sh[2], wd_sh[2]], (sq_all,), "ag3")
        pool_w["pin"] = _slot_to_device_order(sq_all[:, fp:fp + rs, :]).reshape(D, D)
        pool_w["pout"] = _slot_to_device_order(sq_all[:, fp + rs:fp + 2 * rs, :]).reshape(D, D)
        grp = _slot_to_device_order(sq_all[:, fp + 2 * rs:, :])
        pool_w["grp"] = grp.reshape(N_DEV, N_POOL_GROUPS, rs // 4, POOL_GROUP_DIM).transpose(1, 0, 2, 3).reshape(
            N_POOL_GROUPS, POOL_GROUP_DIM, POOL_GROUP_DIM)
        return sq_all[:, :fp, :].reshape(N_DEV * fp, D)

    (a1, a1b, a1t), s_f0 = ffn_fwd(x0, x0b, wgu0, wd0_after, 0, 0, 0, dep_up=(h2[4],),
                                   dep_down=lambda: (pool_w["h3"][4],))
    h3 = pool_w["h3"]
    w_pin, w_pout, w_grp = pool_w["pin"], pool_w["pout"], pool_w["grp"]
    tm = min(512, T)
    row_spec = pl.BlockSpec((tm, D), lambda i, j, k: (i, 0))
    full_w = pl.BlockSpec((D, D), lambda i, j, k: (0, 0))
    u_pool = _mm(a1b, w_pin, grid=(T // tm, 1, 1), a_spec=row_spec, b_spec=full_w,
                 out_shape=jax.ShapeDtypeStruct((T, D), F32), out_spec=row_spec, name="pool_in", deps=(h3[4],))
    mixedb, mixedt = _pool_window(u_pool, False, "pool_window")
    y_pool, ysb, yst = _pool_group(mixedb, w_grp, pool_scale)
    a2, a2b, a2t, xh_p, rstd_p = _mm_ln(ysb, w_pout, a1, gain(0, 1), bias(0, 1), 1.0, "pool_out_ln")
    wgu1, wd1 = _ag_finish(h2, (a2,), "ag2")
    h4 = _ag_start([qkv_sh, aout_sh], (wgu1,), "ag4")
    (a3, a3b, a3t), s_f1 = ffn_fwd(a2, a2b, wgu1, wd1.reshape(N_DEV * fp, D), 1, 0, 2, dep_up=(h4[4],))
    wgu2, wd2 = _ag_finish(h3, (a3,), "ag3")
    h5 = _ag_start([wgu_sh[3], wd_sh[3]], (wgu2,), "ag5")
    (b1, b1b, b1t), s_f2 = ffn_fwd(a3, a3b, wgu2, wd2.reshape(N_DEV * fp, D), 2, 1, 0, dep_up=(h5[4],))
    wqkv_all, aout_all = _ag_finish(h4, (b1,), "ag4")
    w_aout = _slot_to_device_order(aout_all).reshape(D, D)
    tq = min(1024, T)
    qkv = _mm(b1b, wqkv_all, grid=(N_DEV, T // tq, 1),
              a_spec=pl.BlockSpec((tq, D), lambda j, t, k: (t, 0)),
              b_spec=pl.BlockSpec((None, D, QKV_SHARD), lambda j, t, k: (4 * (j % 2) + j // 2, 0, 0)),
              out_shape=jax.ShapeDtypeStruct((T, N_DEV * QKV_SHARD), BF16),
              out_spec=pl.BlockSpec((tq, QKV_SHARD), lambda j, t, k: (t, j)), name="attn_qkv")
    qkv_gs, o_gs, lse_gs = [], [], []
    for gi, (_, d) in enumerate(DIL_CONFIGS):
        qkv_g = _to_sub(qkv[:, gi * 3 * D:(gi + 1) * 3 * D], d)
        o_g, lse_g = _attn_fwd(qkv_g, d, slopes[gi], f"attn_fwd{gi}")
        qkv_gs.append(qkv_g)
        o_gs.append(_from_sub(o_g, d))
        lse_gs.append(_from_sub(lse_g, d))
    o32, ob, ot, lse_tot = _attn_combine(o_gs, lse_gs)
    b2, b2b, b2t, xh_a, rstd_a = _mm_ln(ob, w_aout, b1, gain(1, 1), bias(1, 1), 1.0, "attn_out_ln")
    wgu3, wd3 = _ag_finish(h5, (b2,), "ag5")
    (b3, _, _), s_f3 = ffn_fwd(b2, b2b, wgu3, wd3.reshape(N_DEV * fp, D), 3, 1, 2)

    dy, loss_tile = _loss_head(b3, tgt)
    loss = lax.psum(loss_tile[0, 0], AXES)

    bm = min(512, D)
    dgains, dbiases = {}, {}
    rs_pending = []
    gsums = {}

    def rs_finish(after):
        h, tag = rs_pending.pop()
        chips, lands = _rs_wait(h, after, f"rs_{tag}_wait")
        gsums[tag] = [_sum_chips(a, r, q_idx, f"rs_sum_{tag}{i}") for i, (a, r) in enumerate(zip(chips, lands))]

    def rs_stage(bufs, tag):
        if rs_pending:
            rs_finish((bufs[-1],))
        recv = _rs_sibling(bufs, f"rs_sib_{tag}")
        chips = [_add_half(a, r, c_idx, f"rs_add_{tag}{i}") for i, (a, r) in enumerate(zip(bufs, recv))]
        h = _rs_start(chips, (), f"rs_{tag}")
        rs_pending.append((h, tag))
        return h[4]

    def ffn_bwd(dys, f, i, s, st, xt):
        dxres, dhb, dht, dg, db = _ln_bwd(dys, st["xh"], st["rstd"], gain(i, s), MACARON, f"ffn_ln_bwd{f}")
        dgains[(i, s)], dbiases[(i, s)] = dg, db
        dgu = _ffn_bwd_act(dhb, st["wd"], st["g"], st["u"], fp, f"ffn_bwd_act{f}")
        g_dt = _mm(dht, st["act"], grid=(D // bm, N_DEV, 1),
                   a_spec=pl.BlockSpec((bm, T), lambda r, j, k: (r, 0)),
                   b_spec=pl.BlockSpec((T, fp), lambda r, j, k: (0, j)),
                   out_shape=jax.ShapeDtypeStruct((N_DEV, D, fp), BF16),
                   out_spec=pl.BlockSpec((None, bm, fp), lambda r, j, k: (j, r, 0)), name=f"ffn_dwd{f}")
        g_gu = _mm(xt, dgu, grid=(D // bm, N_DEV, 1),
                   a_spec=pl.BlockSpec((bm, T), lambda r, j, k: (r, 0)),
                   b_spec=pl.BlockSpec((T, 2 * fp), lambda r, j, k: (0, j)),
                   out_shape=jax.ShapeDtypeStruct((N_DEV, D, 2 * fp), BF16),
                   out_spec=pl.BlockSpec((None, bm, 2 * fp), lambda r, j, k: (j, r, 0)), name=f"ffn_dwgu{f}")
        token = rs_stage([g_dt, g_gu], f"f{f}")
        tmx = min(1024, T)
        dx = _mm(dgu, st["wgu"], grid=(T // tmx, 1, N_DEV), nt=True,
                 a_spec=pl.BlockSpec((tmx, 2 * fp), lambda t, j, k: (t, k)),
                 b_spec=pl.BlockSpec((None, D, 2 * fp), lambda t, j, k: (k, 0, 0)),
                 out_shape=jax.ShapeDtypeStruct((T, D), F32),
                 out_spec=pl.BlockSpec((tmx, D), lambda t, j, k: (t, 0)), name=f"ffn_dx{f}", deps=(token,))
        return [dxres, dx]

    def dw_square(at, bmat, name):
        return _mm(at, bmat, grid=(D // bm, 1, 1),
                   a_spec=pl.BlockSpec((bm, T), lambda r, j, k: (r, 0)),
                   b_spec=pl.BlockSpec((T, D), lambda r, j, k: (0, 0)),
                   out_shape=jax.ShapeDtypeStruct((D, D), BF16),
                   out_spec=pl.BlockSpec((bm, D), lambda r, j, k: (r, 0)), name=name)

    def dx_square(a, w, name, deps=()):
        return _mm(a, w, grid=(T // tm, 1, 1), nt=True, a_spec=row_spec, b_spec=full_w,
                   out_shape=jax.ShapeDtypeStruct((T, D), F32), out_spec=row_spec, name=name, deps=deps)

    to_slots = lambda g2d: _device_to_slot_order(g2d.reshape(N_DEV, rs, D))

    d_b2 = ffn_bwd([dy], 3, 1, 2, s_f3, b2t)
    dxres, dmb, dmt, dg, db = _ln_bwd(d_b2, xh_a, rstd_a, gain(1, 1), 1.0, "attn_ln_bwd")
    dgains[(1, 1)], dbiases[(1, 1)] = dg, db
    g_aout = dw_square(ot, dmb, "attn_dwout")
    dob, stats = _attn_bwd_prep(dmb, w_aout, o32, lse_tot)
    dqkv_parts = []
    for gi, (_, d) in enumerate(DIL_CONFIGS):
        dqkv_g = _attn_bwd(qkv_gs[gi], _to_sub(dob, d), _to_sub(stats, d), d, slopes[gi], f"attn_bwd{gi}")
        dqkv_parts.append(_from_sub(dqkv_g, d))
    dqkv = jnp.concatenate(dqkv_parts, axis=1)
    g_qkv = _mm(b1t, dqkv, grid=(D // bm, N_DEV, 1),
                a_spec=pl.BlockSpec((bm, T), lambda r, j, k: (r, 0)),
                b_spec=pl.BlockSpec((T, QKV_SHARD), lambda r, j, k: (0, j)),
                out_shape=jax.ShapeDtypeStruct((N_DEV, D, QKV_SHARD), BF16),
                out_spec=pl.BlockSpec((None, bm, QKV_SHARD), lambda r, j, k: (4 * (j % 2) + j // 2, r, 0)),
                name="attn_dwqkv")
    token = rs_stage([to_slots(g_aout), g_qkv], "attn")
    dx_attn = _mm(dqkv, wqkv_all, grid=(T // tq, 1, N_DEV), nt=True, deps=(token,),
                  a_spec=pl.BlockSpec((tq, QKV_SHARD), lambda t, j, k: (t, k)),
                  b_spec=pl.BlockSpec((None, D, QKV_SHARD), lambda t, j, k: (4 * (k % 2) + k // 2, 0, 0)),
                  out_shape=jax.ShapeDtypeStruct((T, D), F32),
                  out_spec=pl.BlockSpec((tq, D), lambda t, j, k: (t, 0)), name="attn_dx")
    d_a3 = ffn_bwd([dxres, dx_attn], 2, 1, 0, s_f2, a3t)
    d_a2 = ffn_bwd(d_a3, 1, 0, 2, s_f1, a2t)
    dxres, dmb, dmt, dg, db = _ln_bwd(d_a2, xh_p, rstd_p, gain(0, 1), 1.0, "pool_ln_bwd")
    dgains[(0, 1)], dbiases[(0, 1)] = dg, db
    g_pout = dw_square(yst, dmb, "pool_dwout")
    dyb, dscale = _pool_bwd_out(dmb, w_pout, y_pool, pool_scale)
    gd = POOL_GROUP_DIM
    g_grp = _mm(mixedt, dyb, grid=(N_POOL_GROUPS, 1, 1),
                a_spec=pl.BlockSpec((gd, T), lambda g, j, k: (g, 0)),
                b_spec=pl.BlockSpec((T, gd), lambda g, j, k: (0, g)),
                out_shape=jax.ShapeDtypeStruct((N_POOL_GROUPS, gd, gd), BF16),
                out_spec=pl.BlockSpec((None, gd, gd), lambda g, j, k: (g, 0, 0)), name="pool_dwgroup")
    tg = min(1024, T)
    dmixed = _mm(dyb, w_grp, grid=(N_POOL_GROUPS, T // tg, 1), nt=True,
                 a_spec=pl.BlockSpec((tg, gd), lambda g, t, k: (t, g)),
                 b_spec=pl.BlockSpec((None, gd, gd), lambda g, t, k: (g, 0, 0)),
                 out_shape=jax.ShapeDtypeStruct((T, D), F32),
                 out_spec=pl.BlockSpec((tg, gd), lambda g, t, k: (t, g)), name="pool_dmixed")
    dub, _ = _pool_window(dmixed, True, "pool_window_bwd")
    g_pin = dw_square(a1t, dub, "pool_dwin")
    g_grp_slots = _device_to_slot_order(
        g_grp.reshape(N_POOL_GROUPS, N_DEV, rs // 4, gd).transpose(1, 0, 2, 3).reshape(N_DEV, rs // 4, D))
    token = rs_stage([to_slots(g_pout), g_grp_slots, to_slots(g_pin)], "pool")
    dx_pool = dx_square(dub, w_pin, "pool_dx", deps=(token,))
    d_x0 = ffn_bwd([dxres, dx_pool], 0, 0, 0, s_f0, x0t)
    grad_x = _add2(d_x0[0], d_x0[1], "grad_x_add")
    rs_finish((grad_x,))
    grad_x = grad_x[None]
    gw_dt = [gsums[f"f{f}"][0] for f in range(4)]
    gw_gu = [gsums[f"f{f}"][1] for f in range(4)]
    gw_aout, gw_qkv = gsums["attn"]
    gw_pout, gw_grp, gw_pin = gsums["pool"]

    small = jnp.concatenate([dgains[(i, s)] for i in range(DEPTH) for s in range(3)]
                            + [dbiases[(i, s)] for i in range(DEPTH) for s in range(3)]
                            + [dscale, jnp.zeros((3, D), F32)], axis=0)
    _, small_sum = _small_all_gather(small, "ag_small_grads")
    dev = 4 * lax.axis_index("x") + 2 * lax.axis_index("y") + lax.axis_index("c")
    mine = lax.dynamic_slice_in_dim(small_sum, dev * rs, rs, axis=1)
    grads = {
        "ffn1_w_gate": jnp.stack([gw_gu[2 * i][:, :fs] for i in range(DEPTH)]),
        "ffn1_w_up": jnp.stack([gw_gu[2 * i][:, fp:fp + fs] for i in range(DEPTH)]),
        "ffn1_w_down": jnp.stack([gw_dt[2 * i].T[:fs] for i in range(DEPTH)]),
        "ffn2_w_gate": jnp.stack([gw_gu[2 * i + 1][:, :fs] for i in range(DEPTH)]),
        "ffn2_w_up": jnp.stack([gw_gu[2 * i + 1][:, fp:fp + fs] for i in range(DEPTH)]),
        "ffn2_w_down": jnp.stack([gw_dt[2 * i + 1].T[:fs] for i in range(DEPTH)]),
        "ln_gain": mine[0:DEPTH * 3].reshape(DEPTH, 3, rs),
        "ln_bias": mine[DEPTH * 3:2 * DEPTH * 3].reshape(DEPTH, 3, rs),
        "pool_w_in": gw_pin[None],
        "pool_w_group": gw_grp[:rs // 4].reshape(N_POOL_GROUPS, rs // 4, gd)[None],
        "pool_scale": small_sum[2 * DEPTH * 3][None],
        "pool_w_out": gw_pout[None],
        "attn_w_qkv": gw_qkv[None],
        "attn_w_out": gw_aout[None],
    }
    weights = dict(ffn1_w_gate=ffn1_w_gate, ffn1_w_up=ffn1_w_up, ffn1_w_down=ffn1_w_down,
                   ffn2_w_gate=ffn2_w_gate, ffn2_w_up=ffn2_w_up, ffn2_w_down=ffn2_w_down,
                   ln_gain=ln_gain, ln_bias=ln_bias, pool_w_in=pool_w_in, pool_w_group=pool_w_group,
                   pool_scale=pool_scale, pool_w_out=pool_w_out, attn_w_qkv=attn_w_qkv, attn_w_out=attn_w_out)
    ms = dict(ffn1_w_gate=m_ffn1_w_gate, ffn1_w_up=m_ffn1_w_up, ffn1_w_down=m_ffn1_w_down,
              ffn2_w_gate=m_ffn2_w_gate, ffn2_w_up=m_ffn2_w_up, ffn2_w_down=m_ffn2_w_down,
              ln_gain=m_ln_gain, ln_bias=m_ln_bias, pool_w_in=m_pool_w_in, pool_w_group=m_pool_w_group,
              pool_scale=m_pool_scale, pool_w_out=m_pool_w_out, attn_w_qkv=m_attn_w_qkv, attn_w_out=m_attn_w_out)
    vs = dict(ffn1_w_gate=v_ffn1_w_gate, ffn1_w_up=v_ffn1_w_up, ffn1_w_down=v_ffn1_w_down,
              ffn2_w_gate=v_ffn2_w_gate, ffn2_w_up=v_ffn2_w_up, ffn2_w_down=v_ffn2_w_down,
              ln_gain=v_ln_gain, ln_bias=v_ln_bias, pool_w_in=v_pool_w_in, pool_w_group=v_pool_w_group,
              pool_scale=v_pool_scale, pool_w_out=v_pool_w_out, attn_w_qkv=v_attn_w_qkv, attn_w_out=v_attn_w_out)
    names = list(weights)
    deltas, new_m, new_v = {}, {}, {}
    for nme in names:
        deltas[nme], new_m[nme], new_v[nme] = _adamw(weights[nme], grads[nme], ms[nme], vs[nme], f"adamw_{nme}")
    return (loss, grad_x, *[grads[k] for k in names], *[deltas[k] for k in names],
            *[new_m[k] for k in names], *[new_v[k] for k in names])
```

```python
import functools

import numpy as np
import jax
import jax.numpy as jnp
from jax import lax
from jax.experimental import pallas as pl
from jax.experimental.pallas import tpu as pltpu

F32 = jnp.float32
BF16 = jnp.bfloat16

D = 1024
N_DEV = 8
N_HEADS = 16
HEAD_DIM = 64
N_POOL_GROUPS = 4
POOL_GROUP_DIM = 256
POOL_HALF = (1, 2, 4, 8)
DIL_CONFIGS = ((128, 1), (512, 4), (2048, 16))
ATTN_HALO = 64
ATTN_BLOCK = 128
QKV_SHARD = 3 * 3 * D // N_DEV
DEPTH = 2
ALPHA = (2.0 * DEPTH) ** 0.25
MACARON = 0.5
LN_EPS = 1e-5
MASK_VALUE = -1e30
ADAM_LR = 0.001
ADAM_B1 = 0.9
ADAM_B2 = 0.999
ADAM_EPS = 1e-08
ADAM_WD = 0.01
ADAM_STEP = 10
LANE = 128
VMEM_LIMIT = 56 * 1024 * 1024
MESH = pl.DeviceIdType.MESH
AXES = ("x", "y", "c")


def _round_up(n, m):
    return (n + m - 1) // m * m


def _pcall(body, deps=(), **kw):
    if not deps:
        return pl.pallas_call(body, **kw)
    n_in, n_dep = len(kw["in_specs"]), len(deps)

    def wrapped(*refs):
        return body(*refs[:n_in], *refs[n_in + n_dep:])

    kw["in_specs"] = list(kw["in_specs"]) + [pl.BlockSpec(memory_space=pl.ANY)] * n_dep
    call = pl.pallas_call(wrapped, **kw)
    return lambda *args: call(*args, *deps)


def _params(*sem):
    return pltpu.CompilerParams(dimension_semantics=sem, vmem_limit_bytes=VMEM_LIMIT)


def _alibi_slopes():
    n = len(DIL_CONFIGS) * N_HEADS
    s = 2.0 ** (-8.0 * np.arange(1, n + 1) / n)
    return s.reshape(len(DIL_CONFIGS), N_HEADS).astype(np.float32)


def _my_slot():
    return 4 * lax.axis_index("c") + 2 * lax.axis_index("x") + lax.axis_index("y")


def _mm(a, b, *, grid, a_spec, b_spec, out_shape, out_spec, nt=False, name, alias=None, deps=()):
    nk = grid[2]
    dn = (((1,), (1,)), ((), ())) if nt else (((1,), (0,)), ((), ()))
    blk = tuple(s for s in out_spec.block_shape if s is not None)

    def body(*refs):
        a_ref, b_ref = refs[0], refs[1]
        o_ref = refs[3] if alias is not None else refs[2]
        p = lax.dot_general(a_ref[...], b_ref[...], dn, preferred_element_type=F32)
        if nk == 1:
            o_ref[...] = p.astype(o_ref.dtype)
        else:
            acc = refs[-1]
            k = pl.program_id(2)

            @pl.when(k == 0)
            def _():
                acc[...] = p

            @pl.when(k > 0)
            def _():
                acc[...] += p

            @pl.when(k == nk - 1)
            def _():
                o_ref[...] = acc[...].astype(o_ref.dtype)

    in_specs = [a_spec, b_spec]
    args = [a, b]
    aliases = {}
    if alias is not None:
        in_specs.append(pl.BlockSpec(memory_space=pl.ANY))
        args.append(alias)
        aliases = {2: 0}
    return _pcall(
        body, deps=deps, name=name, grid=grid, in_specs=in_specs, out_specs=out_spec, out_shape=out_shape,
        scratch_shapes=[] if nk == 1 else [pltpu.VMEM(blk, F32)],
        input_output_aliases=aliases,
        compiler_params=_params("parallel", "parallel", "arbitrary"),
    )(*args)


def _transpose_cast(x, name, deps=()):
    T = x.shape[0]
    tm = min(512, T)

    def body(x_ref, xb_ref, xt_ref):
        v = x_ref[...]
        xb_ref[...] = v.astype(BF16)
        xt_ref[...] = v.T.astype(BF16)

    return _pcall(
        body, deps=deps, name=name, grid=(T // tm,),
        in_specs=[pl.BlockSpec((tm, D), lambda t: (t, 0))],
        out_specs=[pl.BlockSpec((tm, D), lambda t: (t, 0)), pl.BlockSpec((D, tm), lambda t: (0, t))],
        out_shape=[jax.ShapeDtypeStruct((T, D), BF16), jax.ShapeDtypeStruct((D, T), BF16)],
        compiler_params=_params("parallel"),
    )(x)


def _mm_ln(a, b, xres, gain, bias, hscale, name, deps=()):
    T, K = a.shape
    tm = min(512, T)

    def body(a_ref, b_ref, x_ref, g_ref, bt_ref, y_ref, yb_ref, yt_ref, xh_ref, rs_ref):
        h = jnp.dot(a_ref[...], b_ref[...], preferred_element_type=F32)
        z = ALPHA * x_ref[...] + hscale * h
        mu = jnp.mean(z, axis=-1, keepdims=True)
        zc = z - mu
        var = jnp.mean(zc * zc, axis=-1, keepdims=True)
        rstd = lax.rsqrt(var + LN_EPS)
        xh = zc * rstd
        y = xh * g_ref[...] + bt_ref[...]
        y_ref[...] = y
        yb_ref[...] = y.astype(BF16)
        yt_ref[...] = y.T.astype(BF16)
        xh_ref[...] = xh
        rs_ref[...] = rstd

    row = pl.BlockSpec((tm, D), lambda t: (t, 0))
    vec = pl.BlockSpec((1, D), lambda t: (0, 0))
    return _pcall(
        body, deps=deps, name=name, grid=(T // tm,),
        in_specs=[pl.BlockSpec((tm, K), lambda t: (t, 0)), pl.BlockSpec((K, D), lambda t: (0, 0)), row, vec, vec],
        out_specs=[row, row, pl.BlockSpec((D, tm), lambda t: (0, t)), row, pl.BlockSpec((tm, 1), lambda t: (t, 0))],
        out_shape=[jax.ShapeDtypeStruct((T, D), F32), jax.ShapeDtypeStruct((T, D), BF16),
                   jax.ShapeDtypeStruct((D, T), BF16), jax.ShapeDtypeStruct((T, D), F32),
                   jax.ShapeDtypeStruct((T, 1), F32)],
        compiler_params=_params("parallel"),
    )(a, b, xres, gain, bias)


def _ln_bwd(dys, xhat, rstd, gain, hscale, name):
    T = xhat.shape[0]
    tm = min(512, T)
    n = len(dys)

    def body(*refs):
        dy_refs = refs[:n]
        xh_ref, rs_ref, g_ref, dx_ref, dh_ref, dht_ref, dg_ref, db_ref = refs[n:]
        dy = dy_refs[0][...]
        for r in dy_refs[1:]:
            dy = dy + r[...]
        xh = xh_ref[...]
        dxh = dy * g_ref[...]
        m1 = jnp.mean(dxh, axis=-1, keepdims=True)
        m2 = jnp.mean(dxh * xh, axis=-1, keepdims=True)
        dz = rs_ref[...] * (dxh - m1 - xh * m2)
        dx_ref[...] = ALPHA * dz
        dh = hscale * dz
        dh_ref[...] = dh.astype(BF16)
        dht_ref[...] = dh.T.astype(BF16)
        dg = jnp.sum(dy * xh, axis=0, keepdims=True)
        db = jnp.sum(dy, axis=0, keepdims=True)

        @pl.when(pl.program_id(0) == 0)
        def _():
            dg_ref[...] = dg
            db_ref[...] = db

        @pl.when(pl.program_id(0) > 0)
        def _():
            dg_ref[...] += dg
            db_ref[...] += db

    row = pl.BlockSpec((tm, D), lambda t: (t, 0))
    vec = pl.BlockSpec((1, D), lambda t: (0, 0))
    return _pcall(
        body, name=name, grid=(T // tm,),
        in_specs=[row] * n + [row, pl.BlockSpec((tm, 1), lambda t: (t, 0)), vec],
        out_specs=[row, row, pl.BlockSpec((D, tm), lambda t: (0, t)), vec, vec],
        out_shape=[jax.ShapeDtypeStruct((T, D), F32), jax.ShapeDtypeStruct((T, D), BF16),
                   jax.ShapeDtypeStruct((D, T), BF16), jax.ShapeDtypeStruct((1, D), F32),
                   jax.ShapeDtypeStruct((1, D), F32)],
        compiler_params=_params("arbitrary"),
    )(*dys, xhat, rstd, gain)


def _add2(a, b, name):
    T = a.shape[0]
    tm = min(512, T)

    def body(a_ref, b_ref, o_ref):
        o_ref[...] = a_ref[...] + b_ref[...]

    row = pl.BlockSpec((tm, D), lambda t: (t, 0))
    return _pcall(body, name=name, grid=(T // tm,), in_specs=[row, row], out_specs=row,
                  out_shape=jax.ShapeDtypeStruct((T, D), F32), compiler_params=_params("parallel"))(a, b)


def _loss_head(y, tgt):
    T = y.shape[0]
    tm = min(512, T)

    def body(y_ref, t_ref, dy_ref, l_ref):
        e = y_ref[...] - t_ref[...]
        dy_ref[...] = e * (1.0 / D)
        part = jnp.sum(jnp.sum(e * e, axis=1, keepdims=True), axis=0, keepdims=True) * (0.5 / D)

        @pl.when(pl.program_id(0) == 0)
        def _():
            l_ref[...] = jnp.zeros_like(l_ref)

        l_ref[...] += part

    row = pl.BlockSpec((tm, D), lambda t: (t, 0))
    return _pcall(
        body, name="loss_head", grid=(T // tm,),
        in_specs=[row, row],
        out_specs=[row, pl.BlockSpec((8, LANE), lambda t: (0, 0))],
        out_shape=[jax.ShapeDtypeStruct((T, D), F32), jax.ShapeDtypeStruct((8, LANE), F32)],
        compiler_params=_params("arbitrary"),
    )(y, tgt)


def _ffn_up(xb, wgu, fp, name, deps=()):
    T = xb.shape[0]
    tm = min(1024, T)

    def body(x_ref, w_ref, g_ref, u_ref, a_ref):
        p = jnp.dot(x_ref[...], w_ref[...], preferred_element_type=F32)
        g = p[:, :fp]
        u = p[:, fp:]
        a = g * jax.nn.sigmoid(g) * u
        g_ref[...] = g.astype(BF16)
        u_ref[...] = u.astype(BF16)
        a_ref[...] = a.astype(BF16)

    out = pl.BlockSpec((tm, fp), lambda j, t: (t, j))
    shp = jax.ShapeDtypeStruct((T, N_DEV * fp), BF16)
    return _pcall(
        body, deps=deps, name=name, grid=(N_DEV, T // tm),
        in_specs=[pl.BlockSpec((tm, D), lambda j, t: (t, 0)),
                  pl.BlockSpec((None, D, 2 * fp), lambda j, t: (j, 0, 0))],
        out_specs=[out, out, out], out_shape=[shp, shp, shp],
        compiler_params=_params("parallel", "parallel"),
    )(xb, wgu)


def _ffn_bwd_act(dhb, wd, g, u, fp, name):
    T = dhb.shape[0]
    tm = min(1024, T)

    def body(dh_ref, w_ref, g_ref, u_ref, o_ref):
        da = lax.dot_general(dh_ref[...], w_ref[...], (((1,), (1,)), ((), ())), preferred_element_type=F32)
        gv = g_ref[...].astype(F32)
        uv = u_ref[...].astype(F32)
        sig = jax.nn.sigmoid(gv)
        dgate = da * uv * (sig * (1.0 + gv * (1.0 - sig)))
        dup = da * (gv * sig)
        o_ref[:, :fp] = dgate.astype(BF16)
        o_ref[:, fp:] = dup.astype(BF16)

    blk = pl.BlockSpec((tm, fp), lambda j, t: (t, j))
    return _pcall(
        body, name=name, grid=(N_DEV, T // tm),
        in_specs=[pl.BlockSpec((tm, D), lambda j, t: (t, 0)), pl.BlockSpec((fp, D), lambda j, t: (j, 0)), blk, blk],
        out_specs=pl.BlockSpec((tm, 2 * fp), lambda j, t: (t, j)),
        out_shape=jax.ShapeDtypeStruct((T, N_DEV * 2 * fp), BF16),
        compiler_params=_params("parallel", "parallel"),
    )(dhb, wd, g, u)


POOL_PAD = 16
POOL_CHUNK = 512


def _pool_window(v, transpose, name):
    T = v.shape[0]
    ch = min(POOL_CHUNK, T)
    ext = ch + 2 * POOL_PAD
    gd = POOL_GROUP_DIM

    def body(v_ref, o_ref, ot_ref, pad_ref):
        pad_ref[0:POOL_PAD, :] = jnp.zeros((POOL_PAD, gd), F32)
        pad_ref[POOL_PAD + T:POOL_PAD + T + POOL_PAD, :] = jnp.zeros((POOL_PAD, gd), F32)
        for gi, hw in enumerate(POOL_HALF):
            @pl.when(pl.program_id(0) == gi)
            def _(hw=hw):
                def count(t):
                    return (jnp.minimum(t + hw, T) - jnp.maximum(t - hw, 0)).astype(F32)

                if transpose:
                    t_all = lax.broadcasted_iota(jnp.int32, (T, gd), 0)
                    pad_ref[POOL_PAD:POOL_PAD + T, :] = v_ref[...] / count(t_all)
                else:
                    pad_ref[POOL_PAD:POOL_PAD + T, :] = v_ref[...]
                shift = hw if transpose else hw - 1
                for c in range(T // ch):
                    e = pad_ref[c * ch:c * ch + ext, :]
                    step = 1
                    while step < 2 * hw:
                        e = e + pltpu.roll(e, step, 0)
                        step *= 2
                    if shift:
                        e = pltpu.roll(e, ext - shift, 0)
                    s = e[POOL_PAD:POOL_PAD + ch, :]
                    center = v_ref[c * ch:(c + 1) * ch, :]
                    if transpose:
                        res = s - center
                    else:
                        t_idx = c * ch + lax.broadcasted_iota(jnp.int32, (ch, gd), 0)
                        res = s / count(t_idx) - center
                    o_ref[c * ch:(c + 1) * ch, :] = res.astype(BF16)
                    ot_ref[:, c * ch:(c + 1) * ch] = res.T.astype(BF16)

    return _pcall(
        body, name=name, grid=(N_POOL_GROUPS,),
        in_specs=[pl.BlockSpec((T, gd), lambda g: (0, g))],
        out_specs=[pl.BlockSpec((T, gd), lambda g: (0, g)), pl.BlockSpec((gd, T), lambda g: (g, 0))],
        out_shape=[jax.ShapeDtypeStruct((T, D), BF16), jax.ShapeDtypeStruct((D, T), BF16)],
        scratch_shapes=[pltpu.VMEM((T + 2 * POOL_PAD, gd), F32)],
        compiler_params=_params("arbitrary"),
    )(v)


def _pool_group(mixedb, wgroup, scale):
    T = mixedb.shape[0]
    tm = min(1024, T)
    gd = POOL_GROUP_DIM

    def body(a_ref, w_ref, s_ref, y_ref, ys_ref, yst_ref):
        y = jnp.dot(a_ref[...], w_ref[...], preferred_element_type=F32)
        ys = y * s_ref[...]
        y_ref[...] = y
        ys_ref[...] = ys.astype(BF16)
        yst_ref[...] = ys.T.astype(BF16)

    blk = pl.BlockSpec((tm, gd), lambda g, t: (t, g))
    return _pcall(
        body, name="pool_group", grid=(N_POOL_GROUPS, T // tm),
        in_specs=[blk, pl.BlockSpec((None, gd, gd), lambda g, t: (g, 0, 0)), pl.BlockSpec((1, gd), lambda g, t: (0, g))],
        out_specs=[blk, blk, pl.BlockSpec((gd, tm), lambda g, t: (g, t))],
        out_shape=[jax.ShapeDtypeStruct((T, D), F32), jax.ShapeDtypeStruct((T, D), BF16),
                   jax.ShapeDtypeStruct((D, T), BF16)],
        compiler_params=_params("parallel", "parallel"),
    )(mixedb, wgroup, scale)


def _pool_bwd_out(dmb, w_out, y, scale):
    T = dmb.shape[0]
    tm = min(512, T)

    def body(a_ref, w_ref, y_ref, s_ref, dy_ref, ds_ref):
        dys = lax.dot_general(a_ref[...], w_ref[...], (((1,), (1,)), ((), ())), preferred_element_type=F32)
        dy_ref[...] = (dys * s_ref[...]).astype(BF16)
        part = jnp.sum(dys * y_ref[...], axis=0, keepdims=True)

        @pl.when(pl.program_id(0) == 0)
        def _():
            ds_ref[...] = part

        @pl.when(pl.program_id(0) > 0)
        def _():
            ds_ref[...] += part

    row = pl.BlockSpec((tm, D), lambda t: (t, 0))
    vec = pl.BlockSpec((1, D), lambda t: (0, 0))
    return _pcall(
        body, name="pool_bwd_out", grid=(T // tm,),
        in_specs=[row, pl.BlockSpec((D, D), lambda t: (0, 0)), row, vec],
        out_specs=[row, vec],
        out_shape=[jax.ShapeDtypeStruct((T, D), BF16), jax.ShapeDtypeStruct((1, D), F32)],
        compiler_params=_params("arbitrary"),
    )(dmb, w_out, y, scale)


def _attn_masks(n, L, d):
    w = ATTN_BLOCK + 2 * ATTN_HALO
    a = lax.broadcasted_iota(jnp.int32, (ATTN_BLOCK, w), 0)
    c = lax.broadcasted_iota(jnp.int32, (ATTN_BLOCK, w), 1)
    rel = c - ATTN_HALO - a
    j = n * ATTN_BLOCK - ATTN_HALO + c
    valid = (jnp.abs(rel) <= ATTN_HALO) & (j >= 0) & (j < L)
    dist = (d * jnp.abs(rel)).astype(F32)
    return valid, dist


def _lane_col(st, idx):
    lane = lax.broadcasted_iota(jnp.int32, st.shape, 1)
    return jnp.sum(jnp.where(lane == idx, st, 0.0), axis=1, keepdims=True)


def _window_specs(nb, d, col, width):
    last = 2 * d * nb - 1

    def prev(r, n):
        return (jnp.maximum(2 * (r * nb + n) - 1, 0), col)

    def cur(r, n):
        return (r * nb + n, col)

    def nxt(r, n):
        return (jnp.minimum(2 * (r * nb + n) + 2, last), col)

    return [pl.BlockSpec((ATTN_HALO, width), prev), pl.BlockSpec((ATTN_BLOCK, width), cur),
            pl.BlockSpec((ATTN_HALO, width), nxt)]


def _attn_fwd(qkv_g, d, slopes, name):
    T = qkv_g.shape[0]
    L = T // d
    nb = L // ATTN_BLOCK

    def body(q_ref, kp_ref, kc_ref, kn_ref, vp_ref, vc_ref, vn_ref, o_ref, lse_ref):
        valid, dist = _attn_masks(pl.program_id(1), L, d)
        lane = lax.broadcasted_iota(jnp.int32, (ATTN_BLOCK, LANE), 1)
        first = lane < HEAD_DIM
        head_mask = [jnp.where(first, 1.0, 0.0).astype(BF16), jnp.where(first, 0.0, 1.0).astype(BF16)]
        lse_acc = jnp.zeros((ATTN_BLOCK, LANE), F32)
        for hp in range(N_HEADS // 2):
            cs = slice(hp * LANE, (hp + 1) * LANE)
            q2 = q_ref[:, cs]
            k2 = jnp.concatenate([kp_ref[:, cs], kc_ref[:, cs], kn_ref[:, cs]], axis=0)
            v2 = jnp.concatenate([vp_ref[:, cs], vc_ref[:, cs], vn_ref[:, cs]], axis=0)
            outs = []
            for hh in range(2):
                h = 2 * hp + hh
                qh = q2 * head_mask[hh]
                s = lax.dot_general(qh, k2, (((1,), (1,)), ((), ())), preferred_element_type=F32)
                s = s * (HEAD_DIM ** -0.5) - float(slopes[h]) * dist
                s = jnp.where(valid, s, MASK_VALUE)
                m = jnp.max(s, axis=1, keepdims=True)
                p = jnp.exp(s - m)
                l = jnp.sum(p, axis=1, keepdims=True)
                o = jnp.dot(p.astype(BF16), v2, preferred_element_type=F32) / l
                outs.append(o)
                lse_acc = jnp.where(lane == h, m + jnp.log(l), lse_acc)
            o_ref[:, cs] = jnp.where(first, outs[0], outs[1])
        lse_ref[...] = lse_acc

    specs = ([pl.BlockSpec((ATTN_BLOCK, D), lambda r, n: (r * nb + n, 0))]
             + _window_specs(nb, d, 1, D) + _window_specs(nb, d, 2, D))
    row = lambda w: pl.BlockSpec((ATTN_BLOCK, w), lambda r, n: (r * nb + n, 0))
    return _pcall(
        body, name=name, grid=(d, nb), in_specs=specs,
        out_specs=[row(D), row(LANE)],
        out_shape=[jax.ShapeDtypeStruct((T, D), F32), jax.ShapeDtypeStruct((T, LANE), F32)],
        compiler_params=_params("parallel", "parallel"),
    )(*([qkv_g] * 7))


def _attn_combine(os_, lses):
    T = os_[0].shape[0]
    tm = min(256, T)
    ng = len(os_)

    def body(*refs):
        o_refs = refs[:ng]
        l_refs = refs[ng:2 * ng]
        o32_ref, ob_ref, ot_ref, lt_ref = refs[2 * ng:]
        ls = [r[...] for r in l_refs]
        m = ls[0]
        for l in ls[1:]:
            m = jnp.maximum(m, l)
        tot = jnp.exp(ls[0] - m)
        for l in ls[1:]:
            tot = tot + jnp.exp(l - m)
        lt = m + jnp.log(tot)
        lt_ref[...] = lt
        ws = [jnp.exp(l - lt) for l in ls]
        lane = lax.broadcasted_iota(jnp.int32, (tm, LANE), 1)
        first = lane < HEAD_DIM
        for hp in range(N_HEADS // 2):
            cs = slice(hp * LANE, (hp + 1) * LANE)
            acc = jnp.zeros((tm, LANE), F32)
            for g in range(ng):
                wt = jnp.where(first, _lane_col(ws[g], 2 * hp), _lane_col(ws[g], 2 * hp + 1))
                acc = acc + wt * o_refs[g][:, cs]
            o32_ref[:, cs] = acc
            ob_ref[:, cs] = acc.astype(BF16)
        ot_ref[...] = o32_ref[...].T.astype(BF16)

    row = pl.BlockSpec((tm, D), lambda t: (t, 0))
    st = pl.BlockSpec((tm, LANE), lambda t: (t, 0))
    return _pcall(
        body, name="attn_combine", grid=(T // tm,),
        in_specs=[row] * ng + [st] * ng,
        out_specs=[row, row, pl.BlockSpec((D, tm), lambda t: (0, t)), st],
        out_shape=[jax.ShapeDtypeStruct((T, D), F32), jax.ShapeDtypeStruct((T, D), BF16),
                   jax.ShapeDtypeStruct((D, T), BF16), jax.ShapeDtypeStruct((T, LANE), F32)],
        compiler_params=_params("parallel"),
    )(*os_, *lses)


def _attn_bwd_prep(dmb, w_out, o32, lse_tot):
    T = dmb.shape[0]
    tm = min(512, T)

    def body(a_ref, w_ref, o_ref, l_ref, do_ref, st_ref):
        do = lax.dot_general(a_ref[...], w_ref[...], (((1,), (1,)), ((), ())), preferred_element_type=F32)
        do_ref[...] = do.astype(BF16)
        prod = do * o_ref[...]
        lane = lax.broadcasted_iota(jnp.int32, (tm, LANE), 1)
        first = lane < HEAD_DIM
        st = jnp.where(lane < N_HEADS, l_ref[...], 0.0)
        for hp in range(N_HEADS // 2):
            pr = prod[:, hp * LANE:(hp + 1) * LANE]
            d0 = jnp.sum(jnp.where(first, pr, 0.0), axis=1, keepdims=True)
            d1 = jnp.sum(jnp.where(first, 0.0, pr), axis=1, keepdims=True)
            st = jnp.where(lane == N_HEADS + 2 * hp, d0, st)
            st = jnp.where(lane == N_HEADS + 2 * hp + 1, d1, st)
        st_ref[...] = st

    row = pl.BlockSpec((tm, D), lambda t: (t, 0))
    stb = pl.BlockSpec((tm, LANE), lambda t: (t, 0))
    return _pcall(
        body, name="attn_bwd_prep", grid=(T // tm,),
        in_specs=[row, pl.BlockSpec((D, D), lambda t: (0, 0)), row, stb],
        out_specs=[row, stb],
        out_shape=[jax.ShapeDtypeStruct((T, D), BF16), jax.ShapeDtypeStruct((T, LANE), F32)],
        compiler_params=_params("parallel"),
    )(dmb, w_out, o32, lse_tot)


def _attn_bwd(qkv_g, do_g, st_g, d, slopes, name):
    T = qkv_g.shape[0]
    L = T // d
    nb = L // ATTN_BLOCK
    scale = HEAD_DIM ** -0.5
    nt = (((1,), (1,)), ((), ()))

    def body(qp_ref, qc_ref, qn_ref, kp_ref, kc_ref, kn_ref, vp_ref, vc_ref, vn_ref,
             dp_ref, dc_ref, dn_ref, sp_ref, sc_ref, sn_ref, o_ref):
        valid, dist = _attn_masks(pl.program_id(1), L, d)
        lane = lax.broadcasted_iota(jnp.int32, (ATTN_BLOCK, LANE), 1)
        first = lane < HEAD_DIM
        head_mask = [jnp.where(first, 1.0, 0.0).astype(BF16), jnp.where(first, 0.0, 1.0).astype(BF16)]
        stc = sc_ref[...]
        stw_t = jnp.concatenate([sp_ref[...], stc, sn_ref[...]], axis=0).T
        for hp in range(N_HEADS // 2):
            cs = slice(hp * LANE, (hp + 1) * LANE)
            cat = lambda a, b, c: jnp.concatenate([a[:, cs], b[:, cs], c[:, cs]], axis=0)
            q2, k2, v2, do2 = qc_ref[:, cs], kc_ref[:, cs], vc_ref[:, cs], dc_ref[:, cs]
            qw, kw, vw, dow = cat(qp_ref, qc_ref, qn_ref), cat(kp_ref, kc_ref, kn_ref), cat(vp_ref, vc_ref, vn_ref), cat(dp_ref, dc_ref, dn_ref)
            dqs, dks, dvs = [], [], []
            for hh in range(2):
                h = 2 * hp + hh
                pick = lambda t, hh=hh: t * head_mask[hh]
                bias = float(slopes[h]) * dist
                s = lax.dot_general(pick(q2), kw, nt, preferred_element_type=F32) * scale - bias
                s = jnp.where(valid, s, MASK_VALUE)
                p = jnp.exp(s - _lane_col(stc, h))
                dp = lax.dot_general(pick(do2), vw, nt, preferred_element_type=F32)
                ds = p * (dp - _lane_col(stc, N_HEADS + h)) * scale
                dqs.append(jnp.dot(ds.astype(BF16), kw, preferred_element_type=F32))
                st_ = lax.dot_general(pick(k2), qw, nt, preferred_element_type=F32) * scale - bias
                st_ = jnp.where(valid, st_, MASK_VALUE)
                pt = jnp.exp(st_ - stw_t[h:h + 1, :])
                dvs.append(jnp.dot(pt.astype(BF16), dow, preferred_element_type=F32))
                dpt = lax.dot_general(pick(v2), dow, nt, preferred_element_type=F32)
                dst = pt * (dpt - stw_t[N_HEADS + h:N_HEADS + h + 1, :]) * scale
                dks.append(jnp.dot(dst.astype(BF16), qw, preferred_element_type=F32))
            o_ref[:, hp * LANE:(hp + 1) * LANE] = jnp.where(first, dqs[0], dqs[1]).astype(BF16)
            o_ref[:, D + hp * LANE:D + (hp + 1) * LANE] = jnp.where(first, dks[0], dks[1]).astype(BF16)
            o_ref[:, 2 * D + hp * LANE:2 * D + (hp + 1) * LANE] = jnp.where(first, dvs[0], dvs[1]).astype(BF16)

    specs = (_window_specs(nb, d, 0, D) + _window_specs(nb, d, 1, D) + _window_specs(nb, d, 2, D)
             + _window_specs(nb, d, 0, D) + _window_specs(nb, d, 0, LANE))
    return _pcall(
        body, name=name, grid=(d, nb), in_specs=specs,
        out_specs=pl.BlockSpec((ATTN_BLOCK, 3 * D), lambda r, n: (r * nb + n, 0)),
        out_shape=jax.ShapeDtypeStruct((T, 3 * D), BF16),
        compiler_params=_params("parallel", "parallel"),
    )(*([qkv_g] * 9), *([do_g] * 3), *([st_g] * 3))


def _to_sub(a, d):
    if d == 1:
        return a
    T, C = a.shape
    return a.reshape(T // d, d, C).transpose(1, 0, 2).reshape(T, C)


def _from_sub(a, d):
    if d == 1:
        return a
    T, C = a.shape
    return a.reshape(d, T // d, C).transpose(1, 0, 2).reshape(T, C)


HBM_SPEC = pl.BlockSpec(memory_space=pltpu.HBM)
SEM_SPEC = pl.BlockSpec(memory_space=pltpu.SEMAPHORE)
ANY_SPEC = pl.BlockSpec(memory_space=pl.ANY)
DATAFLOW = pltpu.SideEffectType.DATAFLOW_SIDE_EFFECTING


def _me_and_peers():
    x, y, c = lax.axis_index("x"), lax.axis_index("y"), lax.axis_index("c")
    return (x, y, c), [(x, y, 1 - c), (1 - x, y, c), (x, 1 - y, c), (1 - x, 1 - y, c)]


def _slot(px, py, pc):
    return 4 * pc + 2 * px + py


def _split_start(srcs, lands, after, start_copies, n_sem, name):
    n = len(srcs)
    n_after = len(after)

    def body(*refs):
        src_refs, land_refs = refs[:n], refs[n:2 * n]
        send_sems, recv_sems = refs[2 * n + n_after], refs[2 * n + n_after + 1]
        token = refs[-1]
        start_copies(src_refs, land_refs, send_sems, recv_sems)
        token[...] = jnp.zeros_like(token)

    outs = _pcall(
        body, name=name,
        in_specs=[HBM_SPEC] * (2 * n) + [ANY_SPEC] * n_after,
        out_shape=(pltpu.SemaphoreType.DMA(n_sem), pltpu.SemaphoreType.DMA(n_sem),
                   *[pltpu.HBM(a.shape, a.dtype) for a in srcs], *[pltpu.HBM(a.shape, a.dtype) for a in lands],
                   jax.ShapeDtypeStruct((8, LANE), F32)),
        out_specs=(SEM_SPEC, SEM_SPEC, *[HBM_SPEC] * (2 * n), pl.BlockSpec(memory_space=pltpu.VMEM)),
        input_output_aliases={i: 2 + i for i in range(2 * n)},
        compiler_params=pltpu.CompilerParams(has_side_effects=DATAFLOW),
    )(*[pltpu.with_memory_space_constraint(a, pltpu.HBM) for a in srcs],
      *[pltpu.with_memory_space_constraint(a, pltpu.HBM) for a in lands], *after)
    return outs[0], outs[1], list(outs[2:2 + n]), list(outs[2 + n:2 + 2 * n]), outs[-1]


def _split_wait(handle, after, wait_copies, name):
    send_sems, recv_sems, srcs, lands, _ = handle
    n = len(srcs)

    def body(*refs):
        src_refs, land_refs = refs[:n], refs[n:2 * n]
        wait_copies(src_refs, land_refs, refs[2 * n], refs[2 * n + 1])

    outs = _pcall(
        body, name=name,
        in_specs=[HBM_SPEC] * (2 * n) + [SEM_SPEC, SEM_SPEC] + [ANY_SPEC] * len(after),
        out_shape=tuple(pltpu.HBM(a.shape, a.dtype) for a in srcs + lands),
        out_specs=tuple([HBM_SPEC] * (2 * n)),
        input_output_aliases={i: i for i in range(2 * n)},
        compiler_params=pltpu.CompilerParams(has_side_effects=DATAFLOW),
    )(*srcs, *lands, send_sems, recv_sems, *after)
    return list(outs[:n]), list(outs[n:])


def _ag_copies(src_refs, land_refs, send_sems, recv_sems, received):
    me, peers = _me_and_peers()
    cps = []
    for i in range(len(src_refs)):
        for k, to in enumerate(peers):
            cps.append(pltpu.make_async_remote_copy(
                src_ref=src_refs[i], dst_ref=land_refs[i].at[_slot(*(to if received else me))],
                send_sem=send_sems.at[4 * i + k], recv_sem=recv_sems.at[4 * i + k], device_id=to,
                device_id_type=MESH))
    return cps


def _ag_start(shards, after, name):
    lands = [lax.empty((N_DEV,) + a.shape, a.dtype) for a in shards]

    def start(src_refs, land_refs, send_sems, recv_sems):
        for cp in _ag_copies(src_refs, land_refs, send_sems, recv_sems, False):
            cp.start()

    return _split_start(shards, lands, after, start, (4 * len(shards),), name)


def _ag_finish(handle, after, name):
    def wait(src_refs, land_refs, send_sems, recv_sems):
        for cp in _ag_copies(src_refs, land_refs, send_sems, recv_sems, True):
            cp.wait_send()
            cp.wait_recv()

    shards, lands = _split_wait(handle, after, wait, name + "_wait")
    n = len(shards)

    def body(*refs):
        src_refs, out_refs = refs[:n], refs[2 * n:3 * n]
        send_sems, recv_sems, local_sems = refs[3 * n:]
        me, peers = _me_and_peers()
        mine = [pltpu.make_async_copy(src_refs[i], out_refs[i].at[_slot(*me)], local_sems.at[i]) for i in range(n)]
        for cp in mine:
            cp.start()
        cps = []
        for i in range(n):
            for j, chip in enumerate(peers[1:]):
                blk = out_refs[i].at[_slot(*chip)]
                cps.append(pltpu.make_async_remote_copy(
                    src_ref=blk, dst_ref=blk, send_sem=send_sems.at[i, j], recv_sem=recv_sems.at[i, j],
                    device_id=peers[0], device_id_type=MESH))
        for cp in cps:
            cp.start()
        for cp in cps:
            cp.wait()
        for cp in mine:
            cp.wait()

    outs = _pcall(
        body, name=name + "_pass",
        in_specs=[ANY_SPEC] * (2 * n), out_specs=[ANY_SPEC] * n,
        out_shape=[jax.ShapeDtypeStruct(a.shape, a.dtype) for a in lands],
        input_output_aliases={n + i: i for i in range(n)},
        scratch_shapes=[pltpu.SemaphoreType.DMA((n, 3)), pltpu.SemaphoreType.DMA((n, 3)),
                        pltpu.SemaphoreType.DMA((n,))],
    )(*shards, *lands)
    return list(outs)


def _small_all_gather(v, name):
    R, C = v.shape

    def body(x_ref, out_ref, sum_ref, send_sems, recv_sems, local_sem):
        x, y, c = lax.axis_index("x"), lax.axis_index("y"), lax.axis_index("c")
        me, sibling = (x, y, c), (x, y, 1 - c)
        chips = [(1 - x, y), (x, 1 - y), (1 - x, 1 - y)]

        def rows(px, py, pc):
            return out_ref.at[4 * pc + 2 * px + py]

        def copy(k, block, to, src=None):
            return pltpu.make_async_remote_copy(
                src_ref=rows(*block) if src is None else src, dst_ref=rows(*block),
                send_sem=send_sems.at[k], recv_sem=recv_sems.at[k],
                device_id=to, device_id_type=MESH)

        mine = pltpu.make_async_copy(x_ref, rows(*me), local_sem)
        mine.start()
        first = [copy(0, me, sibling, src=x_ref)]
        first += [copy(1 + j, me, (*chip, c), src=x_ref) for j, chip in enumerate(chips)]
        for cp in first:
            cp.start()
        passed = [copy(4 + j, (*chip, c), sibling) for j, chip in enumerate(chips)]
        for j, chip in enumerate(chips):
            copy(1 + j, (*chip, c), me).wait_recv()
            passed[j].start()
        copy(0, sibling, me).wait_recv()
        for j, chip in enumerate(chips):
            copy(4 + j, (*chip, 1 - c), me).wait_recv()
        for cp in first + passed:
            cp.wait_send()
        mine.wait()
        acc = out_ref[0]
        for s in range(1, N_DEV):
            acc = acc + out_ref[s]
        sum_ref[...] = acc

    vm = pl.BlockSpec(memory_space=pltpu.VMEM)
    return _pcall(
        body, name=name, in_specs=[vm], out_specs=[vm, vm],
        out_shape=[jax.ShapeDtypeStruct((N_DEV, R, C), v.dtype), jax.ShapeDtypeStruct((R, C), v.dtype)],
        scratch_shapes=[pltpu.SemaphoreType.DMA((7,)), pltpu.SemaphoreType.DMA((7,)), pltpu.SemaphoreType.DMA],
    )(v)


def _rs_sibling(arrs, name):
    n = len(arrs)

    def body(*refs):
        ins, outs = refs[:n], refs[n:2 * n]
        send_sems, recv_sems = refs[2 * n:]
        x, y, c = lax.axis_index("x"), lax.axis_index("y"), lax.axis_index("c")
        cps = [pltpu.make_async_remote_copy(
            src_ref=ins[i].at[pl.ds(4 * (1 - c), 4)], dst_ref=outs[i],
            send_sem=send_sems.at[i], recv_sem=recv_sems.at[i],
            device_id=(x, y, 1 - c), device_id_type=MESH) for i in range(n)]
        for cp in cps:
            cp.start()
        for cp in cps:
            cp.wait()

    hbm = pl.BlockSpec(memory_space=pl.ANY)
    return _pcall(
        body, name=name, in_specs=[hbm] * n, out_specs=[hbm] * n,
        out_shape=[jax.ShapeDtypeStruct((4,) + a.shape[1:], a.dtype) for a in arrs],
        scratch_shapes=[pltpu.SemaphoreType.DMA((n,)), pltpu.SemaphoreType.DMA((n,))],
    )(*arrs)


def _rs_copies(src_refs, land_refs, send_sems, recv_sems):
    _, peers = _me_and_peers()
    cps = []
    for i in range(len(src_refs)):
        for j, (px, py, pc) in enumerate(peers[1:]):
            cps.append(pltpu.make_async_remote_copy(
                src_ref=src_refs[i].at[2 * px + py], dst_ref=land_refs[i].at[j],
                send_sem=send_sems.at[3 * i + j], recv_sem=recv_sems.at[3 * i + j],
                device_id=(px, py, pc), device_id_type=MESH))
    return cps


def _rs_start(chipsums, after, name):
    lands = [lax.empty((3,) + a.shape[1:], a.dtype) for a in chipsums]

    def start(src_refs, land_refs, send_sems, recv_sems):
        for cp in _rs_copies(src_refs, land_refs, send_sems, recv_sems):
            cp.start()

    return _split_start(chipsums, lands, after, start, (3 * len(chipsums),), name)


def _rs_wait(handle, after, name):
    def wait(src_refs, land_refs, send_sems, recv_sems):
        for cp in _rs_copies(src_refs, land_refs, send_sems, recv_sems):
            cp.wait_send()
            cp.wait_recv()

    return _split_wait(handle, after, wait, name)


def _row_tile(R, C, itemsize=4, budget=2 * 1024 * 1024):
    best = None
    for t in range(16, R + 1, 16):
        if R % t == 0 and t * C * itemsize <= budget:
            best = t
    return best if best is not None else R


def _add_half(arr, recv, c_idx, name):
    _, R, C = arr.shape
    tr = _row_tile(R, C)

    def body(c_ref, a_ref, r_ref, o_ref):
        o_ref[...] = (a_ref[...].astype(F32) + r_ref[...].astype(F32)).astype(o_ref.dtype)

    gs = pltpu.PrefetchScalarGridSpec(
        num_scalar_prefetch=1, grid=(4, R // tr),
        in_specs=[pl.BlockSpec((None, tr, C), lambda q, i, c_ref: (4 * c_ref[0] + q, i, 0)),
                  pl.BlockSpec((None, tr, C), lambda q, i, c_ref: (q, i, 0))],
        out_specs=pl.BlockSpec((None, tr, C), lambda q, i, c_ref: (q, i, 0)))
    return _pcall(body, name=name, grid_spec=gs, out_shape=jax.ShapeDtypeStruct((4, R, C), arr.dtype),
                  compiler_params=_params("parallel", "parallel"))(c_idx, arr, recv)


def _sum_chips(chipsum, recv, q_idx, name):
    _, R, C = chipsum.shape
    tr = _row_tile(R, C)

    def body(q_ref, a_ref, r_ref, o_ref):
        acc = a_ref[...].astype(F32)
        for j in range(3):
            acc = acc + r_ref[j].astype(F32)
        o_ref[...] = acc

    gs = pltpu.PrefetchScalarGridSpec(
        num_scalar_prefetch=1, grid=(R // tr,),
        in_specs=[pl.BlockSpec((None, tr, C), lambda i, q_ref: (q_ref[0], i, 0)),
                  pl.BlockSpec((3, tr, C), lambda i, q_ref: (0, i, 0))],
        out_specs=pl.BlockSpec((tr, C), lambda i, q_ref: (i, 0)))
    return _pcall(body, name=name, grid_spec=gs, out_shape=jax.ShapeDtypeStruct((R, C), F32),
                  compiler_params=_params("parallel"))(q_idx, chipsum, recv)


def _adamw(w, g, m, v, name):
    shape = w.shape
    C = shape[-1]
    R = int(np.prod(shape[:-1]))
    tr = _row_tile(R, C, budget=1024 * 1024)

    def body(w_ref, g_ref, m_ref, v_ref, d_ref, nm_ref, nv_ref):
        gv = g_ref[...]
        mv = ADAM_B1 * m_ref[...] + (1.0 - ADAM_B1) * gv
        vv = ADAM_B2 * v_ref[...] + (1.0 - ADAM_B2) * jnp.square(gv)
        m_hat = mv / (1.0 - ADAM_B1 ** ADAM_STEP)
        v_hat = vv / (1.0 - ADAM_B2 ** ADAM_STEP)
        d_ref[...] = -ADAM_LR * (m_hat / (jnp.sqrt(v_hat) + ADAM_EPS) + ADAM_WD * w_ref[...])
        nm_ref[...] = mv
        nv_ref[...] = vv

    blk = pl.BlockSpec((tr, C), lambda i: (i, 0))
    shp = jax.ShapeDtypeStruct((R, C), F32)
    outs = _pcall(body, name=name, grid=(R // tr,), in_specs=[blk] * 4, out_specs=[blk] * 3,
                  out_shape=[shp] * 3, compiler_params=_params("parallel"))(
        w.reshape(R, C), g.reshape(R, C), m.reshape(R, C), v.reshape(R, C))
    return tuple(o.reshape(shape) for o in outs)


def _pad_cols(w, width):
    return jnp.pad(w, ((0, 0), (0, width - w.shape[1])))


def _pad_rows(w, height):
    return jnp.pad(w, ((0, height - w.shape[0]), (0, 0)))


def _slot_to_device_order(a):
    s = a.shape
    return a.reshape((2, 4) + s[1:]).swapaxes(0, 1).reshape(s)


def _device_to_slot_order(a):
    s = a.shape
    return a.reshape((4, 2) + s[1:]).swapaxes(0, 1).reshape(s)


def kernel(x, ffn1_w_gate, ffn1_w_up, ffn1_w_down, ffn2_w_gate, ffn2_w_up, ffn2_w_down, ln_gain, ln_bias, pool_w_in, pool_w_group, pool_scale, pool_w_out, attn_w_qkv, attn_w_out, loss_target, m_ffn1_w_gate, m_ffn1_w_up, m_ffn1_w_down, m_ffn2_w_gate, m_ffn2_w_up, m_ffn2_w_down, m_ln_gain, m_ln_bias, m_pool_w_in, m_pool_w_group, m_pool_scale, m_pool_w_out, m_attn_w_qkv, m_attn_w_out, v_ffn1_w_gate, v_ffn1_w_up, v_ffn1_w_down, v_ffn2_w_gate, v_ffn2_w_up, v_ffn2_w_down, v_ln_gain, v_ln_bias, v_pool_w_in, v_pool_w_group, v_pool_scale, v_pool_w_out, v_attn_w_qkv, v_attn_w_out):
    T = x.shape[1]
    fs = ffn1_w_gate.shape[2]
    fp = _round_up(fs, LANE)
    rs = D // N_DEV
    x0 = x[0]
    tgt = loss_target[0]
    slopes = _alibi_slopes()
    c_idx = lax.axis_index("c").astype(jnp.int32).reshape(1)
    q_idx = (2 * lax.axis_index("x") + lax.axis_index("y")).astype(jnp.int32).reshape(1)

    gates = (ffn1_w_gate, ffn2_w_gate)
    ups = (ffn1_w_up, ffn2_w_up)
    downs = (ffn1_w_down, ffn2_w_down)
    ffns = [(i, k) for i in range(DEPTH) for k in range(2)]
    wgu_sh = [jnp.concatenate([_pad_cols(gates[k][i], fp), _pad_cols(ups[k][i], fp)], axis=1).astype(BF16)
              for i, k in ffns]
    wd_sh = [_pad_rows(downs[k][i], fp).astype(BF16) for i, k in ffns]
    sq_sh = jnp.concatenate([wd_sh[0], pool_w_in[0].astype(BF16), pool_w_out[0].astype(BF16),
                             pool_w_group[0].reshape(rs // 4, D).astype(BF16)], axis=0)
    qkv_sh = attn_w_qkv[0].astype(BF16)
    aout_sh = attn_w_out[0].astype(BF16)
    ln_sh = jnp.concatenate([ln_gain.reshape(DEPTH * 3, rs), ln_bias.reshape(DEPTH * 3, rs),
                             jnp.zeros((4, rs), F32)], axis=0)

    h0 = _ag_start([wgu_sh[0], ln_sh], (), "ag0")
    x0b, x0t = _transpose_cast(x0, "x_cast", deps=(h0[4],))
    wgu0, ln_all = _ag_finish(h0, (x0b,), "ag0")
    h1 = _ag_start([sq_sh], (wgu0,), "ag1")
    ln_all = _slot_to_device_order(ln_all).transpose(1, 0, 2).reshape(16, D)
    gain = lambda i, s: ln_all[3 * i + s][None]
    bias = lambda i, s: ln_all[DEPTH * 3 + 3 * i + s][None]

    def ffn_fwd(xf, xb, wgu, wd, f, i, s, dep_up=(), dep_down=()):
        g, u, act = _ffn_up(xb, wgu, fp, f"ffn_up{f}", deps=dep_up)
        wd = wd(act) if callable(wd) else wd
        y, yb, yt, xh, rstd = _mm_ln(act, wd, xf, gain(i, s), bias(i, s), MACARON, f"ffn_down_ln{f}",
                                     deps=dep_down() if callable(dep_down) else dep_down)
        return (y, yb, yt), dict(g=g, u=u, act=act, xh=xh, rstd=rstd, wgu=wgu, wd=wd)

    pool_w = {}

    def wd0_after(act):
        (sq_all,) = _ag_finish(h1, (act,), "ag1")
        pool_w["h2"] = _ag_start([wgu_sh[1], wd_sh[1]], (sq_all,), "ag2")
        pool_w["pin"] = _slot_to_device_order(sq_all[:, fp:fp + rs, :]).reshape(D, D)
        pool_w["pout"] = _slot_to_device_order(sq_all[:, fp + rs:fp + 2 * rs, :]).reshape(D, D)
        grp = _slot_to_device_order(sq_all[:, fp + 2 * rs:, :])
        pool_w["grp"] = grp.reshape(N_DEV, N_POOL_GROUPS, rs // 4, POOL_GROUP_DIM).transpose(1, 0, 2, 3).reshape(
            N_POOL_GROUPS, POOL_GROUP_DIM, POOL_GROUP_DIM)
        return sq_all[:, :fp, :].reshape(N_DEV * fp, D)

    (a1, a1b, a1t), s_f0 = ffn_fwd(x0, x0b, wgu0, wd0_after, 0, 0, 0, dep_up=(h1[4],),
                                   dep_down=lambda: (pool_w["h2"][4],))
    h2 = pool_w["h2"]
    w_pin, w_pout, w_grp = pool_w["pin"], pool_w["pout"], pool_w["grp"]
    tm = min(512, T)
    row_spec = pl.BlockSpec((tm, D), lambda i, j, k: (i, 0))
    full_w = pl.BlockSpec((D, D), lambda i, j, k: (0, 0))
    u_pool = _mm(a1b, w_pin, grid=(T // tm, 1, 1), a_spec=row_spec, b_spec=full_w,
                 out_shape=jax.ShapeDtypeStruct((T, D), F32), out_spec=row_spec, name="pool_in")
    mixedb, mixedt = _pool_window(u_pool, False, "pool_window")
    y_pool, ysb, yst = _pool_group(mixedb, w_grp, pool_scale)
    a2, a2b, a2t, xh_p, rstd_p = _mm_ln(ysb, w_pout, a1, gain(0, 1), bias(0, 1), 1.0, "pool_out_ln")
    wgu1, wd1 = _ag_finish(h2, (a2,), "ag2")
    h3 = _ag_start([wgu_sh[2], wd_sh[2]], (wgu1,), "ag3")
    (a3, a3b, a3t), s_f1 = ffn_fwd(a2, a2b, wgu1, wd1.reshape(N_DEV * fp, D), 1, 0, 2, dep_up=(h3[4],))
    wgu2, wd2 = _ag_finish(h3, (a3,), "ag3")
    h4 = _ag_start([qkv_sh, aout_sh], (wgu2,), "ag4")
    (b1, b1b, b1t), s_f2 = ffn_fwd(a3, a3b, wgu2, wd2.reshape(N_DEV * fp, D), 2, 1, 0, dep_up=(h4[4],))
    wqkv_all, aout_all = _ag_finish(h4, (b1,), "ag4")
    h5 = _ag_start([wgu_sh[3], wd_sh[3]], (wqkv_all,), "ag5")
    w_aout = _slot_to_device_order(aout_all).reshape(D, D)
    tq = min(1024, T)
    qkv = _mm(b1b, wqkv_all, grid=(N_DEV, T // tq, 1),
              a_spec=pl.BlockSpec((tq, D), lambda j, t, k: (t, 0)),
              b_spec=pl.BlockSpec((None, D, QKV_SHARD), lambda j, t, k: (4 * (j % 2) + j // 2, 0, 0)),
              out_shape=jax.ShapeDtypeStruct((T, N_DEV * QKV_SHARD), BF16),
              out_spec=pl.BlockSpec((tq, QKV_SHARD), lambda j, t, k: (t, j)), name="attn_qkv", deps=(h5[4],))
    qkv_gs, o_gs, lse_gs = [], [], []
    for gi, (_, d) in enumerate(DIL_CONFIGS):
        qkv_g = _to_sub(qkv[:, gi * 3 * D:(gi + 1) * 3 * D], d)
        o_g, lse_g = _attn_fwd(qkv_g, d, slopes[gi], f"attn_fwd{gi}")
        qkv_gs.append(qkv_g)
        o_gs.append(_from_sub(o_g, d))
        lse_gs.append(_from_sub(lse_g, d))
    o32, ob, ot, lse_tot = _attn_combine(o_gs, lse_gs)
    b2, b2b, b2t, xh_a, rstd_a = _mm_ln(ob, w_aout, b1, gain(1, 1), bias(1, 1), 1.0, "attn_out_ln")
    wgu3, wd3 = _ag_finish(h5, (b2,), "ag5")
    (b3, _, _), s_f3 = ffn_fwd(b2, b2b, wgu3, wd3.reshape(N_DEV * fp, D), 3, 1, 2)

    dy, loss_tile = _loss_head(b3, tgt)
    loss = lax.psum(loss_tile[0, 0], AXES)

    bm = min(512, D)
    dgains, dbiases = {}, {}
    rs_pending = []
    gsums = {}

    def rs_finish(after):
        h, tag = rs_pending.pop()
        chips, lands = _rs_wait(h, after, f"rs_{tag}_wait")
        gsums[tag] = [_sum_chips(a, r, q_idx, f"rs_sum_{tag}{i}") for i, (a, r) in enumerate(zip(chips, lands))]

    def rs_stage(bufs, tag):
        if rs_pending:
            rs_finish((bufs[-1],))
        recv = _rs_sibling(bufs, f"rs_sib_{tag}")
        chips = [_add_half(a, r, c_idx, f"rs_add_{tag}{i}") for i, (a, r) in enumerate(zip(bufs, recv))]
        h = _rs_start(chips, (), f"rs_{tag}")
        rs_pending.append((h, tag))
        return h[4]

    def ffn_bwd(dys, f, i, s, st, xt):
        dxres, dhb, dht, dg, db = _ln_bwd(dys, st["xh"], st["rstd"], gain(i, s), MACARON, f"ffn_ln_bwd{f}")
        dgains[(i, s)], dbiases[(i, s)] = dg, db
        dgu = _ffn_bwd_act(dhb, st["wd"], st["g"], st["u"], fp, f"ffn_bwd_act{f}")
        g_dt = _mm(dht, st["act"], grid=(D // bm, N_DEV, 1),
                   a_spec=pl.BlockSpec((bm, T), lambda r, j, k: (r, 0)),
                   b_spec=pl.BlockSpec((T, fp), lambda r, j, k: (0, j)),
                   out_shape=jax.ShapeDtypeStruct((N_DEV, D, fp), BF16),
                   out_spec=pl.BlockSpec((None, bm, fp), lambda r, j, k: (j, r, 0)), name=f"ffn_dwd{f}")
        g_gu = _mm(xt, dgu, grid=(D // bm, N_DEV, 1),
                   a_spec=pl.BlockSpec((bm, T), lambda r, j, k: (r, 0)),
                   b_spec=pl.BlockSpec((T, 2 * fp), lambda r, j, k: (0, j)),
                   out_shape=jax.ShapeDtypeStruct((N_DEV, D, 2 * fp), BF16),
                   out_spec=pl.BlockSpec((None, bm, 2 * fp), lambda r, j, k: (j, r, 0)), name=f"ffn_dwgu{f}")
        token = rs_stage([g_dt, g_gu], f"f{f}")
        tmx = min(1024, T)
        dx = _mm(dgu, st["wgu"], grid=(T // tmx, 1, N_DEV), nt=True,
                 a_spec=pl.BlockSpec((tmx, 2 * fp), lambda t, j, k: (t, k)),
                 b_spec=pl.BlockSpec((None, D, 2 * fp), lambda t, j, k: (k, 0, 0)),
                 out_shape=jax.ShapeDtypeStruct((T, D), F32),
                 out_spec=pl.BlockSpec((tmx, D), lambda t, j, k: (t, 0)), name=f"ffn_dx{f}", deps=(token,))
        return [dxres, dx]

    def dw_square(at, bmat, name):
        return _mm(at, bmat, grid=(D // bm, 1, 1),
                   a_spec=pl.BlockSpec((bm, T), lambda r, j, k: (r, 0)),
                   b_spec=pl.BlockSpec((T, D), lambda r, j, k: (0, 0)),
                   out_shape=jax.ShapeDtypeStruct((D, D), BF16),
                   out_spec=pl.BlockSpec((bm, D), lambda r, j, k: (r, 0)), name=name)

    def dx_square(a, w, name, deps=()):
        return _mm(a, w, grid=(T // tm, 1, 1), nt=True, a_spec=row_spec, b_spec=full_w,
                   out_shape=jax.ShapeDtypeStruct((T, D), F32), out_spec=row_spec, name=name, deps=deps)

    to_slots = lambda g2d: _device_to_slot_order(g2d.reshape(N_DEV, rs, D))

    d_b2 = ffn_bwd([dy], 3, 1, 2, s_f3, b2t)
    dxres, dmb, dmt, dg, db = _ln_bwd(d_b2, xh_a, rstd_a, gain(1, 1), 1.0, "attn_ln_bwd")
    dgains[(1, 1)], dbiases[(1, 1)] = dg, db
    g_aout = dw_square(ot, dmb, "attn_dwout")
    dob, stats = _attn_bwd_prep(dmb, w_aout, o32, lse_tot)
    dqkv_parts = []
    for gi, (_, d) in enumerate(DIL_CONFIGS):
        dqkv_g = _attn_bwd(qkv_gs[gi], _to_sub(dob, d), _to_sub(stats, d), d, slopes[gi], f"attn_bwd{gi}")
        dqkv_parts.append(_from_sub(dqkv_g, d))
    dqkv = jnp.concatenate(dqkv_parts, axis=1)
    g_qkv = _mm(b1t, dqkv, grid=(D // bm, N_DEV, 1),
                a_spec=pl.BlockSpec((bm, T), lambda r, j, k: (r, 0)),
                b_spec=pl.BlockSpec((T, QKV_SHARD), lambda r, j, k: (0, j)),
                out_shape=jax.ShapeDtypeStruct((N_DEV, D, QKV_SHARD), BF16),
                out_spec=pl.BlockSpec((None, bm, QKV_SHARD), lambda r, j, k: (4 * (j % 2) + j // 2, r, 0)),
                name="attn_dwqkv")
    token = rs_stage([to_slots(g_aout), g_qkv], "attn")
    dx_attn = _mm(dqkv, wqkv_all, grid=(T // tq, 1, N_DEV), nt=True, deps=(token,),
                  a_spec=pl.BlockSpec((tq, QKV_SHARD), lambda t, j, k: (t, k)),
                  b_spec=pl.BlockSpec((None, D, QKV_SHARD), lambda t, j, k: (4 * (k % 2) + k // 2, 0, 0)),
                  out_shape=jax.ShapeDtypeStruct((T, D), F32),
                  out_spec=pl.BlockSpec((tq, D), lambda t, j, k: (t, 0)), name="attn_dx")
    d_a3 = ffn_bwd([dxres, dx_attn], 2, 1, 0, s_f2, a3t)
    d_a2 = ffn_bwd(d_a3, 1, 0, 2, s_f1, a2t)
    dxres, dmb, dmt, dg, db = _ln_bwd(d_a2, xh_p, rstd_p, gain(0, 1), 1.0, "pool_ln_bwd")
    dgains[(0, 1)], dbiases[(0, 1)] = dg, db
    g_pout = dw_square(yst, dmb, "pool_dwout")
    dyb, dscale = _pool_bwd_out(dmb, w_pout, y_pool, pool_scale)
    gd = POOL_GROUP_DIM
    g_grp = _mm(mixedt, dyb, grid=(N_POOL_GROUPS, 1, 1),
                a_spec=pl.BlockSpec((gd, T), lambda g, j, k: (g, 0)),
                b_spec=pl.BlockSpec((T, gd), lambda g, j, k: (0, g)),
                out_shape=jax.ShapeDtypeStruct((N_POOL_GROUPS, gd, gd), BF16),
                out_spec=pl.BlockSpec((None, gd, gd), lambda g, j, k: (g, 0, 0)), name="pool_dwgroup")
    tg = min(1024, T)
    dmixed = _mm(dyb, w_grp, grid=(N_POOL_GROUPS, T // tg, 1), nt=True,
                 a_spec=pl.BlockSpec((tg, gd), lambda g, t, k: (t, g)),
                 b_spec=pl.BlockSpec((None, gd, gd), lambda g, t, k: (g, 0, 0)),
                 out_shape=jax.ShapeDtypeStruct((T, D), F32),
                 out_spec=pl.BlockSpec((tg, gd), lambda g, t, k: (t, g)), name="pool_dmixed")
    dub, _ = _pool_window(dmixed, True, "pool_window_bwd")
    g_pin = dw_square(a1t, dub, "pool_dwin")
    g_grp_slots = _device_to_slot_order(
        g_grp.reshape(N_POOL_GROUPS, N_DEV, rs // 4, gd).transpose(1, 0, 2, 3).reshape(N_DEV, rs // 4, D))
    token = rs_stage([to_slots(g_pout), g_grp_slots, to_slots(g_pin)], "pool")
    dx_pool = dx_square(dub, w_pin, "pool_dx", deps=(token,))
    d_x0 = ffn_bwd([dxres, dx_pool], 0, 0, 0, s_f0, x0t)
    grad_x = _add2(d_x0[0], d_x0[1], "grad_x_add")
    rs_finish((grad_x,))
    grad_x = grad_x[None]
    gw_dt = [gsums[f"f{f}"][0] for f in range(4)]
    gw_gu = [gsums[f"f{f}"][1] for f in range(4)]
    gw_aout, gw_qkv = gsums["attn"]
    gw_pout, gw_grp, gw_pin = gsums["pool"]

    small = jnp.concatenate([dgains[(i, s)] for i in range(DEPTH) for s in range(3)]
                            + [dbiases[(i, s)] for i in range(DEPTH) for s in range(3)]
                            + [dscale, jnp.zeros((3, D), F32)], axis=0)
    _, small_sum = _small_all_gather(small, "ag_small_grads")
    dev = 4 * lax.axis_index("x") + 2 * lax.axis_index("y") + lax.axis_index("c")
    mine = lax.dynamic_slice_in_dim(small_sum, dev * rs, rs, axis=1)
    grads = {
        "ffn1_w_gate": jnp.stack([gw_gu[2 * i][:, :fs] for i in range(DEPTH)]),
        "ffn1_w_up": jnp.stack([gw_gu[2 * i][:, fp:fp + fs] for i in range(DEPTH)]),
        "ffn1_w_down": jnp.stack([gw_dt[2 * i].T[:fs] for i in range(DEPTH)]),
        "ffn2_w_gate": jnp.stack([gw_gu[2 * i + 1][:, :fs] for i in range(DEPTH)]),
        "ffn2_w_up": jnp.stack([gw_gu[2 * i + 1][:, fp:fp + fs] for i in range(DEPTH)]),
        "ffn2_w_down": jnp.stack([gw_dt[2 * i + 1].T[:fs] for i in range(DEPTH)]),
        "ln_gain": mine[0:DEPTH * 3].reshape(DEPTH, 3, rs),
        "ln_bias": mine[DEPTH * 3:2 * DEPTH * 3].reshape(DEPTH, 3, rs),
        "pool_w_in": gw_pin[None],
        "pool_w_group": gw_grp[:rs // 4].reshape(N_POOL_GROUPS, rs // 4, gd)[None],
        "pool_scale": small_sum[2 * DEPTH * 3][None],
        "pool_w_out": gw_pout[None],
        "attn_w_qkv": gw_qkv[None],
        "attn_w_out": gw_aout[None],
    }
    weights = dict(ffn1_w_gate=ffn1_w_gate, ffn1_w_up=ffn1_w_up, ffn1_w_down=ffn1_w_down,
                   ffn2_w_gate=ffn2_w_gate, ffn2_w_up=ffn2_w_up, ffn2_w_down=ffn2_w_down,
                   ln_gain=ln_gain, ln_bias=ln_bias, pool_w_in=pool_w_in, pool_w_group=pool_w_group,
                   pool_scale=pool_scale, pool_w_out=pool_w_out, attn_w_qkv=attn_w_qkv, attn_w_out=attn_w_out)
    ms = dict(ffn1_w_gate=m_ffn1_w_gate, ffn1_w_up=m_ffn1_w_up, ffn1_w_down=m_ffn1_w_down,
              ffn2_w_gate=m_ffn2_w_gate, ffn2_w_up=m_ffn2_w_up, ffn2_w_down=m_ffn2_w_down,
              ln_gain=m_ln_gain, ln_bias=m_ln_bias, pool_w_in=m_pool_w_in, pool_w_group=m_pool_w_group,
              pool_scale=m_pool_scale, pool_w_out=m_pool_w_out, attn_w_qkv=m_attn_w_qkv, attn_w_out=m_attn_w_out)
    vs = dict(ffn1_w_gate=v_ffn1_w_gate, ffn1_w_up=v_ffn1_w_up, ffn1_w_down=v_ffn1_w_down,
              ffn2_w_gate=v_ffn2_w_gate, ffn2_w_up=v_ffn2_w_up, ffn2_w_down=v_ffn2_w_down,
              ln_gain=v_ln_gain, ln_bias=v_ln_bias, pool_w_in=v_pool_w_in, pool_w_group=v_pool_w_group,
              pool_scale=v_pool_scale, pool_w_out=v_pool_w_out, attn_w_qkv=v_attn_w_qkv, attn_w_out=v_attn_w_out)
    names = list(weights)
    deltas, new_m, new_v = {}, {}, {}
    for nme in names:
        deltas[nme], new_m[nme], new_v[nme] = _adamw(weights[nme], grads[nme], ms[nme], vs[nme], f"adamw_{nme}")
    return (loss, grad_x, *[grads[k] for k in names], *[deltas[k] for k in names],
            *[new_m[k] for k in names], *[new_v[k] for k in names])
```

```python
import functools

import numpy as np
import jax
import jax.numpy as jnp
from jax import lax
from jax.experimental import pallas as pl
from jax.experimental.pallas import tpu as pltpu

F32 = jnp.float32
BF16 = jnp.bfloat16

D = 1024
N_DEV = 8
N_HEADS = 16
HEAD_DIM = 64
N_POOL_GROUPS = 4
POOL_GROUP_DIM = 256
POOL_HALF = (1, 2, 4, 8)
DIL_CONFIGS = ((128, 1), (512, 4), (2048, 16))
ATTN_HALO = 64
ATTN_BLOCK = 128
QKV_SHARD = 3 * 3 * D // N_DEV
DEPTH = 2
ALPHA = (2.0 * DEPTH) ** 0.25
MACARON = 0.5
LN_EPS = 1e-5
MASK_VALUE = -1e30
ADAM_LR = 0.001
ADAM_B1 = 0.9
ADAM_B2 = 0.999
ADAM_EPS = 1e-08
ADAM_WD = 0.01
ADAM_STEP = 10
LANE = 128
VMEM_LIMIT = 56 * 1024 * 1024
MESH = pl.DeviceIdType.MESH
AXES = ("x", "y", "c")


def _round_up(n, m):
    return (n + m - 1) // m * m


def _pcall(body, deps=(), **kw):
    if not deps:
        return pl.pallas_call(body, **kw)
    n_in, n_dep = len(kw["in_specs"]), len(deps)

    def wrapped(*refs):
        return body(*refs[:n_in], *refs[n_in + n_dep:])

    kw["in_specs"] = list(kw["in_specs"]) + [pl.BlockSpec(memory_space=pl.ANY)] * n_dep
    call = pl.pallas_call(wrapped, **kw)
    return lambda *args: call(*args, *deps)


def _params(*sem):
    return pltpu.CompilerParams(dimension_semantics=sem, vmem_limit_bytes=VMEM_LIMIT)


def _alibi_slopes():
    n = len(DIL_CONFIGS) * N_HEADS
    s = 2.0 ** (-8.0 * np.arange(1, n + 1) / n)
    return s.reshape(len(DIL_CONFIGS), N_HEADS).astype(np.float32)


def _my_slot():
    return 4 * lax.axis_index("c") + 2 * lax.axis_index("x") + lax.axis_index("y")


def _mm(a, b, *, grid, a_spec, b_spec, out_shape, out_spec, nt=False, name, alias=None, deps=()):
    nk = grid[2]
    dn = (((1,), (1,)), ((), ())) if nt else (((1,), (0,)), ((), ()))
    blk = tuple(s for s in out_spec.block_shape if s is not None)

    def body(*refs):
        a_ref, b_ref = refs[0], refs[1]
        o_ref = refs[3] if alias is not None else refs[2]
        p = lax.dot_general(a_ref[...], b_ref[...], dn, preferred_element_type=F32)
        if nk == 1:
            o_ref[...] = p.astype(o_ref.dtype)
        else:
            acc = refs[-1]
            k = pl.program_id(2)

            @pl.when(k == 0)
            def _():
                acc[...] = p

            @pl.when(k > 0)
            def _():
                acc[...] += p

            @pl.when(k == nk - 1)
            def _():
                o_ref[...] = acc[...].astype(o_ref.dtype)

    in_specs = [a_spec, b_spec]
    args = [a, b]
    aliases = {}
    if alias is not None:
        in_specs.append(pl.BlockSpec(memory_space=pl.ANY))
        args.append(alias)
        aliases = {2: 0}
    return _pcall(
        body, deps=deps, name=name, grid=grid, in_specs=in_specs, out_specs=out_spec, out_shape=out_shape,
        scratch_shapes=[] if nk == 1 else [pltpu.VMEM(blk, F32)],
        input_output_aliases=aliases,
        compiler_params=_params("parallel", "parallel", "arbitrary"),
    )(*args)


def _transpose_cast(x, name, deps=()):
    T = x.shape[0]
    tm = min(512, T)

    def body(x_ref, xb_ref, xt_ref):
        v = x_ref[...]
        xb_ref[...] = v.astype(BF16)
        xt_ref[...] = v.T.astype(BF16)

    return _pcall(
        body, deps=deps, name=name, grid=(T // tm,),
        in_specs=[pl.BlockSpec((tm, D), lambda t: (t, 0))],
        out_specs=[pl.BlockSpec((tm, D), lambda t: (t, 0)), pl.BlockSpec((D, tm), lambda t: (0, t))],
        out_shape=[jax.ShapeDtypeStruct((T, D), BF16), jax.ShapeDtypeStruct((D, T), BF16)],
        compiler_params=_params("parallel"),
    )(x)


def _mm_ln(a, b, xres, gain, bias, hscale, name, deps=()):
    T, K = a.shape
    tm = min(512, T)

    def body(a_ref, b_ref, x_ref, g_ref, bt_ref, y_ref, yb_ref, yt_ref, xh_ref, rs_ref):
        h = jnp.dot(a_ref[...], b_ref[...], preferred_element_type=F32)
        z = ALPHA * x_ref[...] + hscale * h
        mu = jnp.mean(z, axis=-1, keepdims=True)
        zc = z - mu
        var = jnp.mean(zc * zc, axis=-1, keepdims=True)
        rstd = lax.rsqrt(var + LN_EPS)
        xh = zc * rstd
        y = xh * g_ref[...] + bt_ref[...]
        y_ref[...] = y
        yb_ref[...] = y.astype(BF16)
        yt_ref[...] = y.T.astype(BF16)
        xh_ref[...] = xh
        rs_ref[...] = rstd

    row = pl.BlockSpec((tm, D), lambda t: (t, 0))
    vec = pl.BlockSpec((1, D), lambda t: (0, 0))
    return _pcall(
        body, deps=deps, name=name, grid=(T // tm,),
        in_specs=[pl.BlockSpec((tm, K), lambda t: (t, 0)), pl.BlockSpec((K, D), lambda t: (0, 0)), row, vec, vec],
        out_specs=[row, row, pl.BlockSpec((D, tm), lambda t: (0, t)), row, pl.BlockSpec((tm, 1), lambda t: (t, 0))],
        out_shape=[jax.ShapeDtypeStruct((T, D), F32), jax.ShapeDtypeStruct((T, D), BF16),
                   jax.ShapeDtypeStruct((D, T), BF16), jax.ShapeDtypeStruct((T, D), F32),
                   jax.ShapeDtypeStruct((T, 1), F32)],
        compiler_params=_params("parallel"),
    )(a, b, xres, gain, bias)


def _ln_bwd(dys, xhat, rstd, gain, hscale, name):
    T = xhat.shape[0]
    tm = min(512, T)
    n = len(dys)

    def body(*refs):
        dy_refs = refs[:n]
        xh_ref, rs_ref, g_ref, dx_ref, dh_ref, dht_ref, dg_ref, db_ref = refs[n:]
        dy = dy_refs[0][...]
        for r in dy_refs[1:]:
            dy = dy + r[...]
        xh = xh_ref[...]
        dxh = dy * g_ref[...]
        m1 = jnp.mean(dxh, axis=-1, keepdims=True)
        m2 = jnp.mean(dxh * xh, axis=-1, keepdims=True)
        dz = rs_ref[...] * (dxh - m1 - xh * m2)
        dx_ref[...] = ALPHA * dz
        dh = hscale * dz
        dh_ref[...] = dh.astype(BF16)
        dht_ref[...] = dh.T.astype(BF16)
        dg = jnp.sum(dy * xh, axis=0, keepdims=True)
        db = jnp.sum(dy, axis=0, keepdims=True)

        @pl.when(pl.program_id(0) == 0)
        def _():
            dg_ref[...] = dg
            db_ref[...] = db

        @pl.when(pl.program_id(0) > 0)
        def _():
            dg_ref[...] += dg
            db_ref[...] += db

    row = pl.BlockSpec((tm, D), lambda t: (t, 0))
    vec = pl.BlockSpec((1, D), lambda t: (0, 0))
    return _pcall(
        body, name=name, grid=(T // tm,),
        in_specs=[row] * n + [row, pl.BlockSpec((tm, 1), lambda t: (t, 0)), vec],
        out_specs=[row, row, pl.BlockSpec((D, tm), lambda t: (0, t)), vec, vec],
        out_shape=[jax.ShapeDtypeStruct((T, D), F32), jax.ShapeDtypeStruct((T, D), BF16),
                   jax.ShapeDtypeStruct((D, T), BF16), jax.ShapeDtypeStruct((1, D), F32),
                   jax.ShapeDtypeStruct((1, D), F32)],
        compiler_params=_params("arbitrary"),
    )(*dys, xhat, rstd, gain)


def _add2(a, b, name):
    T = a.shape[0]
    tm = min(512, T)

    def body(a_ref, b_ref, o_ref):
        o_ref[...] = a_ref[...] + b_ref[...]

    row = pl.BlockSpec((tm, D), lambda t: (t, 0))
    return _pcall(body, name=name, grid=(T // tm,), in_specs=[row, row], out_specs=row,
                  out_shape=jax.ShapeDtypeStruct((T, D), F32), compiler_params=_params("parallel"))(a, b)


def _loss_head(y, tgt):
    T = y.shape[0]
    tm = min(512, T)

    def body(y_ref, t_ref, dy_ref, l_ref):
        e = y_ref[...] - t_ref[...]
        dy_ref[...] = e * (1.0 / D)
        part = jnp.sum(jnp.sum(e * e, axis=1, keepdims=True), axis=0, keepdims=True) * (0.5 / D)

        @pl.when(pl.program_id(0) == 0)
        def _():
            l_ref[...] = jnp.zeros_like(l_ref)

        l_ref[...] += part

    row = pl.BlockSpec((tm, D), lambda t: (t, 0))
    return _pcall(
        body, name="loss_head", grid=(T // tm,),
        in_specs=[row, row],
        out_specs=[row, pl.BlockSpec((8, LANE), lambda t: (0, 0))],
        out_shape=[jax.ShapeDtypeStruct((T, D), F32), jax.ShapeDtypeStruct((8, LANE), F32)],
        compiler_params=_params("arbitrary"),
    )(y, tgt)


def _ffn_up(xb, wgu, fp, name, deps=()):
    T = xb.shape[0]
    tm = min(1024, T)

    def body(x_ref, w_ref, g_ref, u_ref, a_ref):
        p = jnp.dot(x_ref[...], w_ref[...], preferred_element_type=F32)
        g = p[:, :fp]
        u = p[:, fp:]
        a = g * jax.nn.sigmoid(g) * u
        g_ref[...] = g.astype(BF16)
        u_ref[...] = u.astype(BF16)
        a_ref[...] = a.astype(BF16)

    out = pl.BlockSpec((tm, fp), lambda j, t: (t, j))
    shp = jax.ShapeDtypeStruct((T, N_DEV * fp), BF16)
    return _pcall(
        body, deps=deps, name=name, grid=(N_DEV, T // tm),
        in_specs=[pl.BlockSpec((tm, D), lambda j, t: (t, 0)),
                  pl.BlockSpec((None, D, 2 * fp), lambda j, t: (j, 0, 0))],
        out_specs=[out, out, out], out_shape=[shp, shp, shp],
        compiler_params=_params("parallel", "parallel"),
    )(xb, wgu)


def _ffn_bwd_act(dhb, wd, g, u, fp, name):
    T = dhb.shape[0]
    tm = min(1024, T)

    def body(dh_ref, w_ref, g_ref, u_ref, o_ref):
        da = lax.dot_general(dh_ref[...], w_ref[...], (((1,), (1,)), ((), ())), preferred_element_type=F32)
        gv = g_ref[...].astype(F32)
        uv = u_ref[...].astype(F32)
        sig = jax.nn.sigmoid(gv)
        dgate = da * uv * (sig * (1.0 + gv * (1.0 - sig)))
        dup = da * (gv * sig)
        o_ref[:, :fp] = dgate.astype(BF16)
        o_ref[:, fp:] = dup.astype(BF16)

    blk = pl.BlockSpec((tm, fp), lambda j, t: (t, j))
    return _pcall(
        body, name=name, grid=(N_DEV, T // tm),
        in_specs=[pl.BlockSpec((tm, D), lambda j, t: (t, 0)), pl.BlockSpec((fp, D), lambda j, t: (j, 0)), blk, blk],
        out_specs=pl.BlockSpec((tm, 2 * fp), lambda j, t: (t, j)),
        out_shape=jax.ShapeDtypeStruct((T, N_DEV * 2 * fp), BF16),
        compiler_params=_params("parallel", "parallel"),
    )(dhb, wd, g, u)


POOL_PAD = 16
POOL_CHUNK = 512


def _pool_window(v, transpose, name):
    T = v.shape[0]
    ch = min(POOL_CHUNK, T)
    ext = ch + 2 * POOL_PAD
    gd = POOL_GROUP_DIM

    def body(v_ref, o_ref, ot_ref, pad_ref):
        pad_ref[0:POOL_PAD, :] = jnp.zeros((POOL_PAD, gd), F32)
        pad_ref[POOL_PAD + T:POOL_PAD + T + POOL_PAD, :] = jnp.zeros((POOL_PAD, gd), F32)
        for gi, hw in enumerate(POOL_HALF):
            @pl.when(pl.program_id(0) == gi)
            def _(hw=hw):
                def count(t):
                    return (jnp.minimum(t + hw, T) - jnp.maximum(t - hw, 0)).astype(F32)

                if transpose:
                    t_all = lax.broadcasted_iota(jnp.int32, (T, gd), 0)
                    pad_ref[POOL_PAD:POOL_PAD + T, :] = v_ref[...] / count(t_all)
                else:
                    pad_ref[POOL_PAD:POOL_PAD + T, :] = v_ref[...]
                shift = hw if transpose else hw - 1
                for c in range(T // ch):
                    e = pad_ref[c * ch:c * ch + ext, :]
                    step = 1
                    while step < 2 * hw:
                        e = e + pltpu.roll(e, step, 0)
                        step *= 2
                    if shift:
                        e = pltpu.roll(e, ext - shift, 0)
                    s = e[POOL_PAD:POOL_PAD + ch, :]
                    center = v_ref[c * ch:(c + 1) * ch, :]
                    if transpose:
                        res = s - center
                    else:
                        t_idx = c * ch + lax.broadcasted_iota(jnp.int32, (ch, gd), 0)
                        res = s / count(t_idx) - center
                    o_ref[c * ch:(c + 1) * ch, :] = res.astype(BF16)
                    ot_ref[:, c * ch:(c + 1) * ch] = res.T.astype(BF16)

    return _pcall(
        body, name=name, grid=(N_POOL_GROUPS,),
        in_specs=[pl.BlockSpec((T, gd), lambda g: (0, g))],
        out_specs=[pl.BlockSpec((T, gd), lambda g: (0, g)), pl.BlockSpec((gd, T), lambda g: (g, 0))],
        out_shape=[jax.ShapeDtypeStruct((T, D), BF16), jax.ShapeDtypeStruct((D, T), BF16)],
        scratch_shapes=[pltpu.VMEM((T + 2 * POOL_PAD, gd), F32)],
        compiler_params=_params("arbitrary"),
    )(v)


def _pool_group(mixedb, wgroup, scale):
    T = mixedb.shape[0]
    tm = min(1024, T)
    gd = POOL_GROUP_DIM

    def body(a_ref, w_ref, s_ref, y_ref, ys_ref, yst_ref):
        y = jnp.dot(a_ref[...], w_ref[...], preferred_element_type=F32)
        ys = y * s_ref[...]
        y_ref[...] = y
        ys_ref[...] = ys.astype(BF16)
        yst_ref[...] = ys.T.astype(BF16)

    blk = pl.BlockSpec((tm, gd), lambda g, t: (t, g))
    return _pcall(
        body, name="pool_group", grid=(N_POOL_GROUPS, T // tm),
        in_specs=[blk, pl.BlockSpec((None, gd, gd), lambda g, t: (g, 0, 0)), pl.BlockSpec((1, gd), lambda g, t: (0, g))],
        out_specs=[blk, blk, pl.BlockSpec((gd, tm), lambda g, t: (g, t))],
        out_shape=[jax.ShapeDtypeStruct((T, D), F32), jax.ShapeDtypeStruct((T, D), BF16),
                   jax.ShapeDtypeStruct((D, T), BF16)],
        compiler_params=_params("parallel", "parallel"),
    )(mixedb, wgroup, scale)


def _pool_bwd_out(dmb, w_out, y, scale):
    T = dmb.shape[0]
    tm = min(512, T)

    def body(a_ref, w_ref, y_ref, s_ref, dy_ref, ds_ref):
        dys = lax.dot_general(a_ref[...], w_ref[...], (((1,), (1,)), ((), ())), preferred_element_type=F32)
        dy_ref[...] = (dys * s_ref[...]).astype(BF16)
        part = jnp.sum(dys * y_ref[...], axis=0, keepdims=True)

        @pl.when(pl.program_id(0) == 0)
        def _():
            ds_ref[...] = part

        @pl.when(pl.program_id(0) > 0)
        def _():
            ds_ref[...] += part

    row = pl.BlockSpec((tm, D), lambda t: (t, 0))
    vec = pl.BlockSpec((1, D), lambda t: (0, 0))
    return _pcall(
        body, name="pool_bwd_out", grid=(T // tm,),
        in_specs=[row, pl.BlockSpec((D, D), lambda t: (0, 0)), row, vec],
        out_specs=[row, vec],
        out_shape=[jax.ShapeDtypeStruct((T, D), BF16), jax.ShapeDtypeStruct((1, D), F32)],
        compiler_params=_params("arbitrary"),
    )(dmb, w_out, y, scale)


def _attn_masks(n, L, d):
    w = ATTN_BLOCK + 2 * ATTN_HALO
    a = lax.broadcasted_iota(jnp.int32, (ATTN_BLOCK, w), 0)
    c = lax.broadcasted_iota(jnp.int32, (ATTN_BLOCK, w), 1)
    rel = c - ATTN_HALO - a
    j = n * ATTN_BLOCK - ATTN_HALO + c
    valid = (jnp.abs(rel) <= ATTN_HALO) & (j >= 0) & (j < L)
    dist = (d * jnp.abs(rel)).astype(F32)
    return valid, dist


def _lane_col(st, idx):
    lane = lax.broadcasted_iota(jnp.int32, st.shape, 1)
    return jnp.sum(jnp.where(lane == idx, st, 0.0), axis=1, keepdims=True)


def _window_specs(nb, d, col, width):
    last = 2 * d * nb - 1

    def prev(r, n):
        return (jnp.maximum(2 * (r * nb + n) - 1, 0), col)

    def cur(r, n):
        return (r * nb + n, col)

    def nxt(r, n):
        return (jnp.minimum(2 * (r * nb + n) + 2, last), col)

    return [pl.BlockSpec((ATTN_HALO, width), prev), pl.BlockSpec((ATTN_BLOCK, width), cur),
            pl.BlockSpec((ATTN_HALO, width), nxt)]


def _attn_fwd(qkv_g, d, slopes, name):
    T = qkv_g.shape[0]
    L = T // d
    nb = L // ATTN_BLOCK

    def body(q_ref, kp_ref, kc_ref, kn_ref, vp_ref, vc_ref, vn_ref, o_ref, lse_ref):
        valid, dist = _attn_masks(pl.program_id(1), L, d)
        lane = lax.broadcasted_iota(jnp.int32, (ATTN_BLOCK, LANE), 1)
        first = lane < HEAD_DIM
        head_mask = [jnp.where(first, 1.0, 0.0).astype(BF16), jnp.where(first, 0.0, 1.0).astype(BF16)]
        lse_acc = jnp.zeros((ATTN_BLOCK, LANE), F32)
        for hp in range(N_HEADS // 2):
            cs = slice(hp * LANE, (hp + 1) * LANE)
            q2 = q_ref[:, cs]
            k2 = jnp.concatenate([kp_ref[:, cs], kc_ref[:, cs], kn_ref[:, cs]], axis=0)
            v2 = jnp.concatenate([vp_ref[:, cs], vc_ref[:, cs], vn_ref[:, cs]], axis=0)
            outs = []
            for hh in range(2):
                h = 2 * hp + hh
                qh = q2 * head_mask[hh]
                s = lax.dot_general(qh, k2, (((1,), (1,)), ((), ())), preferred_element_type=F32)
                s = s * (HEAD_DIM ** -0.5) - float(slopes[h]) * dist
                s = jnp.where(valid, s, MASK_VALUE)
                m = jnp.max(s, axis=1, keepdims=True)
                p = jnp.exp(s - m)
                l = jnp.sum(p, axis=1, keepdims=True)
                o = jnp.dot(p.astype(BF16), v2, preferred_element_type=F32) / l
                outs.append(o)
                lse_acc = jnp.where(lane == h, m + jnp.log(l), lse_acc)
            o_ref[:, cs] = jnp.where(first, outs[0], outs[1])
        lse_ref[...] = lse_acc

    specs = ([pl.BlockSpec((ATTN_BLOCK, D), lambda r, n: (r * nb + n, 0))]
             + _window_specs(nb, d, 1, D) + _window_specs(nb, d, 2, D))
    row = lambda w: pl.BlockSpec((ATTN_BLOCK, w), lambda r, n: (r * nb + n, 0))
    return _pcall(
        body, name=name, grid=(d, nb), in_specs=specs,
        out_specs=[row(D), row(LANE)],
        out_shape=[jax.ShapeDtypeStruct((T, D), F32), jax.ShapeDtypeStruct((T, LANE), F32)],
        compiler_params=_params("parallel", "parallel"),
    )(*([qkv_g] * 7))


def _stage(scr3, val):
    for c in range(val.shape[1] // LANE):
        scr3[c] = val[:, c * LANE:(c + 1) * LANE]


def _unstage(scr3):
    return jnp.concatenate([scr3[c] for c in range(scr3.shape[0])], axis=1)


def _gather_rows(scr3, r, n, d):
    return jnp.concatenate([scr3[c, pl.ds(r, n, stride=d), :] for c in range(scr3.shape[0])], axis=1)


def _scatter_rows(scr3, r, n, d, val):
    for c in range(scr3.shape[0]):
        scr3[c, pl.ds(r, n, stride=d), :] = val[:, c * LANE:(c + 1) * LANE]


def _attn_combine(os_, lses, dils):
    T = os_[0].shape[0]
    tm = min(256, T)
    ng = len(os_)
    n_scr = sum(1 for d in dils if d > 1)

    def body(*refs):
        in_o = refs[:ng]
        in_l = refs[ng:2 * ng]
        o32_ref, ob_ref, ot_ref, lt_ref = refs[2 * ng:2 * ng + 4]
        scr = refs[2 * ng + 4:]
        o_chunk, l_refs, si = [], [], 0
        for g, d in enumerate(dils):
            if d == 1:
                o_chunk.append(lambda hp, g=g: in_o[g][:, hp * LANE:(hp + 1) * LANE])
                l_refs.append(in_l[g])
                continue
            so, sl = scr[2 * si], scr[2 * si + 1]
            si += 1
            for r in range(d):
                _scatter_rows(so, r, tm // d, d, in_o[g][r])
                sl[pl.ds(r, tm // d, stride=d), :] = in_l[g][r]
            o_chunk.append(lambda hp, so=so: so[hp])
            l_refs.append(sl)
        ls = [r[...] for r in l_refs]
        m = ls[0]
        for l in ls[1:]:
            m = jnp.maximum(m, l)
        tot = jnp.exp(ls[0] - m)
        for l in ls[1:]:
            tot = tot + jnp.exp(l - m)
        lt = m + jnp.log(tot)
        lt_ref[...] = lt
        ws = [jnp.exp(l - lt) for l in ls]
        lane = lax.broadcasted_iota(jnp.int32, (tm, LANE), 1)
        first = lane < HEAD_DIM
        for hp in range(N_HEADS // 2):
            cs = slice(hp * LANE, (hp + 1) * LANE)
            acc = jnp.zeros((tm, LANE), F32)
            for g in range(ng):
                wt = jnp.where(first, _lane_col(ws[g], 2 * hp), _lane_col(ws[g], 2 * hp + 1))
                acc = acc + wt * o_chunk[g](hp)
            o32_ref[:, cs] = acc
            ob_ref[:, cs] = acc.astype(BF16)
        ot_ref[...] = o32_ref[...].T.astype(BF16)

    row = pl.BlockSpec((tm, D), lambda t: (t, 0))
    st = pl.BlockSpec((tm, LANE), lambda t: (t, 0))

    def sub_spec(d, w):
        return pl.BlockSpec((tm, w), lambda t: (t, 0)) if d == 1 else pl.BlockSpec((d, tm // d, w), lambda t: (0, t, 0))

    def sub_view(a, d):
        return a if d == 1 else a.reshape(d, T // d, a.shape[1])

    return _pcall(
        body, name="attn_combine", grid=(T // tm,),
        in_specs=[sub_spec(d, D) for d in dils] + [sub_spec(d, LANE) for d in dils],
        out_specs=[row, row, pl.BlockSpec((D, tm), lambda t: (0, t)), st],
        out_shape=[jax.ShapeDtypeStruct((T, D), F32), jax.ShapeDtypeStruct((T, D), BF16),
                   jax.ShapeDtypeStruct((D, T), BF16), jax.ShapeDtypeStruct((T, LANE), F32)],
        scratch_shapes=[pltpu.VMEM(s, F32) for _ in range(n_scr) for s in ((D // LANE, tm, LANE), (tm, LANE))],
        compiler_params=_params("parallel"),
    )(*[sub_view(a, d) for a, d in zip(os_, dils)], *[sub_view(a, d) for a, d in zip(lses, dils)])


def _attn_bwd_prep(dmb, w_out, o32, lse_tot, dils):
    T = dmb.shape[0]
    tm = min(512, T)
    ng = len(dils)

    def body(a_ref, w_ref, o_ref, l_ref, *rest):
        do_refs, st_refs = rest[:ng], rest[ng:2 * ng]
        do_scr, st_scr = rest[2 * ng:]
        do = lax.dot_general(a_ref[...], w_ref[...], (((1,), (1,)), ((), ())), preferred_element_type=F32)
        _stage(do_scr, do)
        prod = do * o_ref[...]
        lane = lax.broadcasted_iota(jnp.int32, (tm, LANE), 1)
        first = lane < HEAD_DIM
        st = jnp.where(lane < N_HEADS, l_ref[...], 0.0)
        for hp in range(N_HEADS // 2):
            pr = prod[:, hp * LANE:(hp + 1) * LANE]
            d0 = jnp.sum(jnp.where(first, pr, 0.0), axis=1, keepdims=True)
            d1 = jnp.sum(jnp.where(first, 0.0, pr), axis=1, keepdims=True)
            st = jnp.where(lane == N_HEADS + 2 * hp, d0, st)
            st = jnp.where(lane == N_HEADS + 2 * hp + 1, d1, st)
        st_scr[...] = st
        for g, d in enumerate(dils):
            if d == 1:
                do_refs[g][...] = do.astype(BF16)
                st_refs[g][...] = st
                continue
            for r in range(d):
                do_refs[g][r] = _gather_rows(do_scr, r, tm // d, d).astype(BF16)
                st_refs[g][r] = st_scr[pl.ds(r, tm // d, stride=d), :]

    row = pl.BlockSpec((tm, D), lambda t: (t, 0))
    stb = pl.BlockSpec((tm, LANE), lambda t: (t, 0))

    def sub_spec(d, w):
        return pl.BlockSpec((tm, w), lambda t: (t, 0)) if d == 1 else pl.BlockSpec((d, tm // d, w), lambda t: (0, t, 0))

    def sub_shape(d, w, dt):
        return jax.ShapeDtypeStruct((T, w) if d == 1 else (d, T // d, w), dt)

    outs = _pcall(
        body, name="attn_bwd_prep", grid=(T // tm,),
        in_specs=[row, pl.BlockSpec((D, D), lambda t: (0, 0)), row, stb],
        out_specs=[sub_spec(d, D) for d in dils] + [sub_spec(d, LANE) for d in dils],
        out_shape=[sub_shape(d, D, BF16) for d in dils] + [sub_shape(d, LANE, F32) for d in dils],
        scratch_shapes=[pltpu.VMEM((D // LANE, tm, LANE), F32), pltpu.VMEM((tm, LANE), F32)],
        compiler_params=_params("parallel"),
    )(dmb, w_out, o32, lse_tot)
    return ([o.reshape(T, D) for o in outs[:ng]], [o.reshape(T, LANE) for o in outs[ng:]])


def _attn_bwd(qkv_g, do_g, st_g, d, slopes, name):
    T = qkv_g.shape[0]
    L = T // d
    nb = L // ATTN_BLOCK
    scale = HEAD_DIM ** -0.5
    nt = (((1,), (1,)), ((), ()))

    def body(qp_ref, qc_ref, qn_ref, kp_ref, kc_ref, kn_ref, vp_ref, vc_ref, vn_ref,
             dp_ref, dc_ref, dn_ref, sp_ref, sc_ref, sn_ref, o_ref):
        valid, dist = _attn_masks(pl.program_id(1), L, d)
        lane = lax.broadcasted_iota(jnp.int32, (ATTN_BLOCK, LANE), 1)
        first = lane < HEAD_DIM
        head_mask = [jnp.where(first, 1.0, 0.0).astype(BF16), jnp.where(first, 0.0, 1.0).astype(BF16)]
        stc = sc_ref[...]
        stw_t = jnp.concatenate([sp_ref[...], stc, sn_ref[...]], axis=0).T
        for hp in range(N_HEADS // 2):
            cs = slice(hp * LANE, (hp + 1) * LANE)
            cat = lambda a, b, c: jnp.concatenate([a[:, cs], b[:, cs], c[:, cs]], axis=0)
            q2, k2, v2, do2 = qc_ref[:, cs], kc_ref[:, cs], vc_ref[:, cs], dc_ref[:, cs]
            qw, kw, vw, dow = cat(qp_ref, qc_ref, qn_ref), cat(kp_ref, kc_ref, kn_ref), cat(vp_ref, vc_ref, vn_ref), cat(dp_ref, dc_ref, dn_ref)
            dqs, dks, dvs = [], [], []
            for hh in range(2):
                h = 2 * hp + hh
                pick = lambda t, hh=hh: t * head_mask[hh]
                bias = float(slopes[h]) * dist
                s = lax.dot_general(pick(q2), kw, nt, preferred_element_type=F32) * scale - bias
                s = jnp.where(valid, s, MASK_VALUE)
                p = jnp.exp(s - _lane_col(stc, h))
                dp = lax.dot_general(pick(do2), vw, nt, preferred_element_type=F32)
                ds = p * (dp - _lane_col(stc, N_HEADS + h)) * scale
                dqs.append(jnp.dot(ds.astype(BF16), kw, preferred_element_type=F32))
                st_ = lax.dot_general(pick(k2), qw, nt, preferred_element_type=F32) * scale - bias
                st_ = jnp.where(valid, st_, MASK_VALUE)
                pt = jnp.exp(st_ - stw_t[h:h + 1, :])
                dvs.append(jnp.dot(pt.astype(BF16), dow, preferred_element_type=F32))
                dpt = lax.dot_general(pick(v2), dow, nt, preferred_element_type=F32)
                dst = pt * (dpt - stw_t[N_HEADS + h:N_HEADS + h + 1, :]) * scale
                dks.append(jnp.dot(dst.astype(BF16), qw, preferred_element_type=F32))
            o_ref[:, hp * LANE:(hp + 1) * LANE] = jnp.where(first, dqs[0], dqs[1]).astype(BF16)
            o_ref[:, D + hp * LANE:D + (hp + 1) * LANE] = jnp.where(first, dks[0], dks[1]).astype(BF16)
            o_ref[:, 2 * D + hp * LANE:2 * D + (hp + 1) * LANE] = jnp.where(first, dvs[0], dvs[1]).astype(BF16)

    specs = (_window_specs(nb, d, 0, D) + _window_specs(nb, d, 1, D) + _window_specs(nb, d, 2, D)
             + _window_specs(nb, d, 0, D) + _window_specs(nb, d, 0, LANE))
    return _pcall(
        body, name=name, grid=(d, nb), in_specs=specs,
        out_specs=pl.BlockSpec((ATTN_BLOCK, 3 * D), lambda r, n: (r * nb + n, 0)),
        out_shape=jax.ShapeDtypeStruct((T, 3 * D), BF16),
        compiler_params=_params("parallel", "parallel"),
    )(*([qkv_g] * 9), *([do_g] * 3), *([st_g] * 3))


QKV_TILE = QKV_SHARD // 3


def _qkv_tile_block(nn):
    dev = nn // 3
    return 4 * (dev % 2) + dev // 2, nn % 3


def _attn_qkv_group(xb, wqkv, gi, d, name, deps=()):
    T = xb.shape[0]
    tq = min(1024, T)
    nsub = tq // d
    ntile = 3 * D // QKV_TILE

    def body(x_ref, w_ref, o_ref, *scr):
        p = jnp.dot(x_ref[...], w_ref[...], preferred_element_type=F32)
        if d == 1:
            o_ref[...] = p.astype(BF16)
        else:
            _stage(scr[0], p)
            for r in range(d):
                o_ref[r] = _gather_rows(scr[0], r, nsub, d).astype(BF16)

    def w_map(n, t):
        slot, sub = _qkv_tile_block(ntile * gi + n)
        return slot, 0, sub

    if d == 1:
        out_spec = pl.BlockSpec((tq, QKV_TILE), lambda n, t: (t, n))
        out_shape = jax.ShapeDtypeStruct((T, 3 * D), BF16)
    else:
        out_spec = pl.BlockSpec((d, nsub, QKV_TILE), lambda n, t: (0, t, n))
        out_shape = jax.ShapeDtypeStruct((d, T // d, 3 * D), BF16)
    out = _pcall(
        body, deps=deps, name=name, grid=(ntile, T // tq),
        in_specs=[pl.BlockSpec((tq, D), lambda n, t: (t, 0)), pl.BlockSpec((None, D, QKV_TILE), w_map)],
        out_specs=out_spec, out_shape=out_shape,
        scratch_shapes=[] if d == 1 else [pltpu.VMEM((QKV_TILE // LANE, tq, LANE), F32)],
        compiler_params=_params("parallel", "parallel"),
    )(xb, wqkv)
    return out.reshape(T, 3 * D)


def _attn_dx_group(dqkv_g, wqkv, gi, d, name, deps=()):
    T = dqkv_g.shape[0]
    tq = min(1024, T)
    nsub = tq // d
    ntile = 3 * D // QKV_TILE

    def body(a_ref, w_ref, o_ref, acc, *stage):
        a = a_ref[...]
        if d > 1:
            a = a.reshape(tq, QKV_TILE)
        p = lax.dot_general(a, w_ref[...], (((1,), (1,)), ((), ())), preferred_element_type=F32)
        k = pl.program_id(1)

        @pl.when(k == 0)
        def _():
            acc[...] = p

        @pl.when(k > 0)
        def _():
            acc[...] += p

        @pl.when(k == ntile - 1)
        def _():
            if d == 1:
                o_ref[...] = acc[...]
            else:
                for r in range(d):
                    _scatter_rows(stage[0], r, nsub, d, acc[r * nsub:(r + 1) * nsub, :])
                o_ref[...] = _unstage(stage[0])

    def w_map(t, n):
        slot, sub = _qkv_tile_block(ntile * gi + n)
        return slot, 0, sub

    if d == 1:
        a_spec = pl.BlockSpec((tq, QKV_TILE), lambda t, n: (t, n))
        a = dqkv_g
    else:
        a_spec = pl.BlockSpec((d, nsub, QKV_TILE), lambda t, n: (0, t, n))
        a = dqkv_g.reshape(d, T // d, 3 * D)
    return _pcall(
        body, deps=deps, name=name, grid=(T // tq, ntile),
        in_specs=[a_spec, pl.BlockSpec((None, D, QKV_TILE), w_map)],
        out_specs=pl.BlockSpec((tq, D), lambda t, n: (t, 0)),
        out_shape=jax.ShapeDtypeStruct((T, D), F32),
        scratch_shapes=[pltpu.VMEM((tq, D), F32)] + ([] if d == 1 else [pltpu.VMEM((D // LANE, tq, LANE), F32)]),
        compiler_params=_params("parallel", "arbitrary"),
    )(a, wqkv)


def _transpose_sub(x, d, name):
    T = x.shape[0]
    tm = LANE * d

    def body(x_ref, o_ref, scr):
        for c in range(D // LANE):
            scr[...] = x_ref[:, c * LANE:(c + 1) * LANE]
            for r in range(d):
                o_ref[r, c * LANE:(c + 1) * LANE, :] = scr[pl.ds(r, LANE, stride=d), :].T.astype(BF16)

    return _pcall(
        body, name=name, grid=(T // tm,),
        in_specs=[pl.BlockSpec((tm, D), lambda t: (t, 0))],
        out_specs=pl.BlockSpec((d, D, LANE), lambda t: (0, 0, t)),
        out_shape=jax.ShapeDtypeStruct((d, D, T // d), BF16),
        scratch_shapes=[pltpu.VMEM((tm, LANE), F32)],
        compiler_params=_params("parallel"),
    )(x)


HBM_SPEC = pl.BlockSpec(memory_space=pltpu.HBM)
SEM_SPEC = pl.BlockSpec(memory_space=pltpu.SEMAPHORE)
ANY_SPEC = pl.BlockSpec(memory_space=pl.ANY)
DATAFLOW = pltpu.SideEffectType.DATAFLOW_SIDE_EFFECTING


def _me_and_peers():
    x, y, c = lax.axis_index("x"), lax.axis_index("y"), lax.axis_index("c")
    return (x, y, c), [(x, y, 1 - c), (1 - x, y, c), (x, 1 - y, c), (1 - x, 1 - y, c)]


def _slot(px, py, pc):
    return 4 * pc + 2 * px + py


def _split_start(srcs, lands, after, start_copies, n_sem, name):
    n = len(srcs)
    n_after = len(after)

    def body(*refs):
        src_refs, land_refs = refs[:n], refs[n:2 * n]
        send_sems, recv_sems = refs[2 * n + n_after], refs[2 * n + n_after + 1]
        token = refs[-1]
        start_copies(src_refs, land_refs, send_sems, recv_sems)
        token[...] = jnp.zeros_like(token)

    outs = _pcall(
        body, name=name,
        in_specs=[HBM_SPEC] * (2 * n) + [ANY_SPEC] * n_after,
        out_shape=(pltpu.SemaphoreType.DMA(n_sem), pltpu.SemaphoreType.DMA(n_sem),
                   *[pltpu.HBM(a.shape, a.dtype) for a in srcs], *[pltpu.HBM(a.shape, a.dtype) for a in lands],
                   jax.ShapeDtypeStruct((8, LANE), F32)),
        out_specs=(SEM_SPEC, SEM_SPEC, *[HBM_SPEC] * (2 * n), pl.BlockSpec(memory_space=pltpu.VMEM)),
        input_output_aliases={i: 2 + i for i in range(2 * n)},
        compiler_params=pltpu.CompilerParams(has_side_effects=DATAFLOW),
    )(*[pltpu.with_memory_space_constraint(a, pltpu.HBM) for a in srcs],
      *[pltpu.with_memory_space_constraint(a, pltpu.HBM) for a in lands], *after)
    return outs[0], outs[1], list(outs[2:2 + n]), list(outs[2 + n:2 + 2 * n]), outs[-1]


def _split_wait(handle, after, wait_copies, name):
    send_sems, recv_sems, srcs, lands, _ = handle
    n = len(srcs)

    def body(*refs):
        src_refs, land_refs = refs[:n], refs[n:2 * n]
        wait_copies(src_refs, land_refs, refs[2 * n], refs[2 * n + 1])

    outs = _pcall(
        body, name=name,
        in_specs=[HBM_SPEC] * (2 * n) + [SEM_SPEC, SEM_SPEC] + [ANY_SPEC] * len(after),
        out_shape=tuple(pltpu.HBM(a.shape, a.dtype) for a in srcs + lands),
        out_specs=tuple([HBM_SPEC] * (2 * n)),
        input_output_aliases={i: i for i in range(2 * n)},
        compiler_params=pltpu.CompilerParams(has_side_effects=DATAFLOW),
    )(*srcs, *lands, send_sems, recv_sems, *after)
    return list(outs[:n]), list(outs[n:])


def _ag_copies(src_refs, land_refs, send_sems, recv_sems, received):
    me, peers = _me_and_peers()
    cps = []
    for i in range(len(src_refs)):
        for k, to in enumerate(peers):
            cps.append(pltpu.make_async_remote_copy(
                src_ref=src_refs[i], dst_ref=land_refs[i].at[_slot(*(to if received else me))],
                send_sem=send_sems.at[4 * i + k], recv_sem=recv_sems.at[4 * i + k], device_id=to,
                device_id_type=MESH))
    return cps


def _ag_start(shards, after, name):
    lands = [lax.empty((N_DEV,) + a.shape, a.dtype) for a in shards]

    def start(src_refs, land_refs, send_sems, recv_sems):
        for cp in _ag_copies(src_refs, land_refs, send_sems, recv_sems, False):
            cp.start()

    return _split_start(shards, lands, after, start, (4 * len(shards),), name)


def _ag_finish(handle, after, name):
    def wait(src_refs, land_refs, send_sems, recv_sems):
        for cp in _ag_copies(src_refs, land_refs, send_sems, recv_sems, True):
            cp.wait_send()
            cp.wait_recv()

    shards, lands = _split_wait(handle, after, wait, name + "_wait")
    n = len(shards)

    def body(*refs):
        src_refs, out_refs = refs[:n], refs[2 * n:3 * n]
        send_sems, recv_sems, local_sems = refs[3 * n:3 * n + 3]
        bounce = refs[3 * n + 3:]
        me, peers = _me_and_peers()
        loads = [pltpu.make_async_copy(src_refs[i], bounce[i], local_sems.at[i]) for i in range(n)]
        mine = [pltpu.make_async_copy(bounce[i], out_refs[i].at[_slot(*me)], local_sems.at[i]) for i in range(n)]
        for cp in loads:
            cp.start()
        cps = []
        for i in range(n):
            for j, chip in enumerate(peers[1:]):
                blk = out_refs[i].at[_slot(*chip)]
                cps.append(pltpu.make_async_remote_copy(
                    src_ref=blk, dst_ref=blk, send_sem=send_sems.at[i, j], recv_sem=recv_sems.at[i, j],
                    device_id=peers[0], device_id_type=MESH))
        for cp in cps:
            cp.start()
        for ld, st in zip(loads, mine):
            ld.wait()
            st.start()
        for cp in cps:
            cp.wait()
        for cp in mine:
            cp.wait()

    outs = _pcall(
        body, name=name + "_pass",
        in_specs=[ANY_SPEC] * (2 * n), out_specs=[ANY_SPEC] * n,
        out_shape=[jax.ShapeDtypeStruct(a.shape, a.dtype) for a in lands],
        input_output_aliases={n + i: i for i in range(n)},
        scratch_shapes=[pltpu.SemaphoreType.DMA((n, 3)), pltpu.SemaphoreType.DMA((n, 3)),
                        pltpu.SemaphoreType.DMA((n,))] + [pltpu.VMEM(a.shape, a.dtype) for a in shards],
        compiler_params=pltpu.CompilerParams(vmem_limit_bytes=VMEM_LIMIT),
    )(*shards, *lands)
    return list(outs)


def _small_all_gather(v, name):
    R, C = v.shape

    def body(x_ref, out_ref, sum_ref, send_sems, recv_sems, local_sem):
        x, y, c = lax.axis_index("x"), lax.axis_index("y"), lax.axis_index("c")
        me, sibling = (x, y, c), (x, y, 1 - c)
        chips = [(1 - x, y), (x, 1 - y), (1 - x, 1 - y)]

        def rows(px, py, pc):
            return out_ref.at[4 * pc + 2 * px + py]

        def copy(k, block, to, src=None):
            return pltpu.make_async_remote_copy(
                src_ref=rows(*block) if src is None else src, dst_ref=rows(*block),
                send_sem=send_sems.at[k], recv_sem=recv_sems.at[k],
                device_id=to, device_id_type=MESH)

        mine = pltpu.make_async_copy(x_ref, rows(*me), local_sem)
        mine.start()
        first = [copy(0, me, sibling, src=x_ref)]
        first += [copy(1 + j, me, (*chip, c), src=x_ref) for j, chip in enumerate(chips)]
        for cp in first:
            cp.start()
        passed = [copy(4 + j, (*chip, c), sibling) for j, chip in enumerate(chips)]
        for j, chip in enumerate(chips):
            copy(1 + j, (*chip, c), me).wait_recv()
            passed[j].start()
        copy(0, sibling, me).wait_recv()
        for j, chip in enumerate(chips):
            copy(4 + j, (*chip, 1 - c), me).wait_recv()
        for cp in first + passed:
            cp.wait_send()
        mine.wait()
        acc = out_ref[0]
        for s in range(1, N_DEV):
            acc = acc + out_ref[s]
        sum_ref[...] = acc

    vm = pl.BlockSpec(memory_space=pltpu.VMEM)
    return _pcall(
        body, name=name, in_specs=[vm], out_specs=[vm, vm],
        out_shape=[jax.ShapeDtypeStruct((N_DEV, R, C), v.dtype), jax.ShapeDtypeStruct((R, C), v.dtype)],
        scratch_shapes=[pltpu.SemaphoreType.DMA((7,)), pltpu.SemaphoreType.DMA((7,)), pltpu.SemaphoreType.DMA],
    )(v)


def _rs_sibling(arrs, name):
    n = len(arrs)

    def body(*refs):
        ins, outs = refs[:n], refs[n:2 * n]
        send_sems, recv_sems = refs[2 * n:]
        x, y, c = lax.axis_index("x"), lax.axis_index("y"), lax.axis_index("c")
        cps = [pltpu.make_async_remote_copy(
            src_ref=ins[i].at[pl.ds(4 * (1 - c), 4)], dst_ref=outs[i],
            send_sem=send_sems.at[i], recv_sem=recv_sems.at[i],
            device_id=(x, y, 1 - c), device_id_type=MESH) for i in range(n)]
        for cp in cps:
            cp.start()
        for cp in cps:
            cp.wait()

    hbm = pl.BlockSpec(memory_space=pl.ANY)
    return _pcall(
        body, name=name, in_specs=[hbm] * n, out_specs=[hbm] * n,
        out_shape=[jax.ShapeDtypeStruct((4,) + a.shape[1:], a.dtype) for a in arrs],
        scratch_shapes=[pltpu.SemaphoreType.DMA((n,)), pltpu.SemaphoreType.DMA((n,))],
    )(*arrs)


def _rs_copies(src_refs, land_refs, send_sems, recv_sems):
    _, peers = _me_and_peers()
    cps = []
    for i in range(len(src_refs)):
        for j, (px, py, pc) in enumerate(peers[1:]):
            cps.append(pltpu.make_async_remote_copy(
                src_ref=src_refs[i].at[2 * px + py], dst_ref=land_refs[i].at[j],
                send_sem=send_sems.at[3 * i + j], recv_sem=recv_sems.at[3 * i + j],
                device_id=(px, py, pc), device_id_type=MESH))
    return cps


def _rs_start(chipsums, after, name):
    lands = [lax.empty((3,) + a.shape[1:], a.dtype) for a in chipsums]

    def start(src_refs, land_refs, send_sems, recv_sems):
        for cp in _rs_copies(src_refs, land_refs, send_sems, recv_sems):
            cp.start()

    return _split_start(chipsums, lands, after, start, (3 * len(chipsums),), name)


def _rs_wait(handle, after, name):
    def wait(src_refs, land_refs, send_sems, recv_sems):
        for cp in _rs_copies(src_refs, land_refs, send_sems, recv_sems):
            cp.wait_send()
            cp.wait_recv()

    return _split_wait(handle, after, wait, name)


def _row_tile(R, C, itemsize=4, budget=2 * 1024 * 1024):
    best = None
    for t in range(16, R + 1, 16):
        if R % t == 0 and t * C * itemsize <= budget:
            best = t
    return best if best is not None else R


def _add_half(arr, recv, c_idx, name):
    _, R, C = arr.shape
    tr = _row_tile(R, C)

    def body(c_ref, a_ref, r_ref, o_ref):
        o_ref[...] = (a_ref[...].astype(F32) + r_ref[...].astype(F32)).astype(o_ref.dtype)

    gs = pltpu.PrefetchScalarGridSpec(
        num_scalar_prefetch=1, grid=(4, R // tr),
        in_specs=[pl.BlockSpec((None, tr, C), lambda q, i, c_ref: (4 * c_ref[0] + q, i, 0)),
                  pl.BlockSpec((None, tr, C), lambda q, i, c_ref: (q, i, 0))],
        out_specs=pl.BlockSpec((None, tr, C), lambda q, i, c_ref: (q, i, 0)))
    return _pcall(body, name=name, grid_spec=gs, out_shape=jax.ShapeDtypeStruct((4, R, C), arr.dtype),
                  compiler_params=_params("parallel", "parallel"))(c_idx, arr, recv)


def _sum_chips(chipsum, recv, q_idx, name):
    _, R, C = chipsum.shape
    tr = _row_tile(R, C)

    def body(q_ref, a_ref, r_ref, o_ref):
        acc = a_ref[...].astype(F32)
        for j in range(3):
            acc = acc + r_ref[j].astype(F32)
        o_ref[...] = acc

    gs = pltpu.PrefetchScalarGridSpec(
        num_scalar_prefetch=1, grid=(R // tr,),
        in_specs=[pl.BlockSpec((None, tr, C), lambda i, q_ref: (q_ref[0], i, 0)),
                  pl.BlockSpec((3, tr, C), lambda i, q_ref: (0, i, 0))],
        out_specs=pl.BlockSpec((tr, C), lambda i, q_ref: (i, 0)))
    return _pcall(body, name=name, grid_spec=gs, out_shape=jax.ShapeDtypeStruct((R, C), F32),
                  compiler_params=_params("parallel"))(q_idx, chipsum, recv)


def _adamw(w, g, m, v, name):
    shape = w.shape
    C = shape[-1]
    R = int(np.prod(shape[:-1]))
    tr = _row_tile(R, C, budget=1024 * 1024)

    def body(w_ref, g_ref, m_ref, v_ref, d_ref, nm_ref, nv_ref):
        gv = g_ref[...]
        mv = ADAM_B1 * m_ref[...] + (1.0 - ADAM_B1) * gv
        vv = ADAM_B2 * v_ref[...] + (1.0 - ADAM_B2) * jnp.square(gv)
        m_hat = mv / (1.0 - ADAM_B1 ** ADAM_STEP)
        v_hat = vv / (1.0 - ADAM_B2 ** ADAM_STEP)
        d_ref[...] = -ADAM_LR * (m_hat / (jnp.sqrt(v_hat) + ADAM_EPS) + ADAM_WD * w_ref[...])
        nm_ref[...] = mv
        nv_ref[...] = vv

    blk = pl.BlockSpec((tr, C), lambda i: (i, 0))
    shp = jax.ShapeDtypeStruct((R, C), F32)
    outs = _pcall(body, name=name, grid=(R // tr,), in_specs=[blk] * 4, out_specs=[blk] * 3,
                  out_shape=[shp] * 3, compiler_params=_params("parallel"))(
        w.reshape(R, C), g.reshape(R, C), m.reshape(R, C), v.reshape(R, C))
    return tuple(o.reshape(shape) for o in outs)


def _pad_cols(w, width):
    return jnp.pad(w, ((0, 0), (0, width - w.shape[1])))


def _pad_rows(w, height):
    return jnp.pad(w, ((0, height - w.shape[0]), (0, 0)))


def _slot_to_device_order(a):
    s = a.shape
    return a.reshape((2, 4) + s[1:]).swapaxes(0, 1).reshape(s)


def _device_to_slot_order(a):
    s = a.shape
    return a.reshape((4, 2) + s[1:]).swapaxes(0, 1).reshape(s)


def kernel(x, ffn1_w_gate, ffn1_w_up, ffn1_w_down, ffn2_w_gate, ffn2_w_up, ffn2_w_down, ln_gain, ln_bias, pool_w_in, pool_w_group, pool_scale, pool_w_out, attn_w_qkv, attn_w_out, loss_target, m_ffn1_w_gate, m_ffn1_w_up, m_ffn1_w_down, m_ffn2_w_gate, m_ffn2_w_up, m_ffn2_w_down, m_ln_gain, m_ln_bias, m_pool_w_in, m_pool_w_group, m_pool_scale, m_pool_w_out, m_attn_w_qkv, m_attn_w_out, v_ffn1_w_gate, v_ffn1_w_up, v_ffn1_w_down, v_ffn2_w_gate, v_ffn2_w_up, v_ffn2_w_down, v_ln_gain, v_ln_bias, v_pool_w_in, v_pool_w_group, v_pool_scale, v_pool_w_out, v_attn_w_qkv, v_attn_w_out):
    T = x.shape[1]
    fs = ffn1_w_gate.shape[2]
    fp = _round_up(fs, LANE)
    rs = D // N_DEV
    x0 = x[0]
    tgt = loss_target[0]
    slopes = _alibi_slopes()
    c_idx = lax.axis_index("c").astype(jnp.int32).reshape(1)
    q_idx = (2 * lax.axis_index("x") + lax.axis_index("y")).astype(jnp.int32).reshape(1)

    gates = (ffn1_w_gate, ffn2_w_gate)
    ups = (ffn1_w_up, ffn2_w_up)
    downs = (ffn1_w_down, ffn2_w_down)
    ffns = [(i, k) for i in range(DEPTH) for k in range(2)]
    wgu_sh = [jnp.concatenate([_pad_cols(gates[k][i], fp), _pad_cols(ups[k][i], fp)], axis=1).astype(BF16)
              for i, k in ffns]
    wd_sh = [_pad_rows(downs[k][i], fp).astype(BF16) for i, k in ffns]
    sq_sh = jnp.concatenate([wd_sh[0], pool_w_in[0].astype(BF16), pool_w_out[0].astype(BF16),
                             pool_w_group[0].reshape(rs // 4, D).astype(BF16)], axis=0)
    qkv_sh = attn_w_qkv[0].astype(BF16)
    aout_sh = attn_w_out[0].astype(BF16)
    ln_sh = jnp.concatenate([ln_gain.reshape(DEPTH * 3, rs), ln_bias.reshape(DEPTH * 3, rs),
                             jnp.zeros((4, rs), F32)], axis=0)

    h0 = _ag_start([wgu_sh[0], ln_sh], (), "ag0")
    x0b, x0t = _transpose_cast(x0, "x_cast", deps=(h0[4],))
    wgu0, ln_all = _ag_finish(h0, (x0b,), "ag0")
    h1 = _ag_start([sq_sh], (wgu0,), "ag1")
    ln_all = _slot_to_device_order(ln_all).transpose(1, 0, 2).reshape(16, D)
    gain = lambda i, s: ln_all[3 * i + s][None]
    bias = lambda i, s: ln_all[DEPTH * 3 + 3 * i + s][None]

    def ffn_fwd(xf, xb, wgu, wd, f, i, s, dep_up=(), dep_down=()):
        g, u, act = _ffn_up(xb, wgu, fp, f"ffn_up{f}", deps=dep_up)
        wd = wd(act) if callable(wd) else wd
        y, yb, yt, xh, rstd = _mm_ln(act, wd, xf, gain(i, s), bias(i, s), MACARON, f"ffn_down_ln{f}",
                                     deps=dep_down() if callable(dep_down) else dep_down)
        return (y, yb, yt), dict(g=g, u=u, act=act, xh=xh, rstd=rstd, wgu=wgu, wd=wd)

    pool_w = {}

    def wd0_after(act):
        (sq_all,) = _ag_finish(h1, (act,), "ag1")
        pool_w["h2"] = _ag_start([wgu_sh[1], wd_sh[1]], (sq_all,), "ag2")
        pool_w["pin"] = _slot_to_device_order(sq_all[:, fp:fp + rs, :]).reshape(D, D)
        pool_w["pout"] = _slot_to_device_order(sq_all[:, fp + rs:fp + 2 * rs, :]).reshape(D, D)
        grp = _slot_to_device_order(sq_all[:, fp + 2 * rs:, :])
        pool_w["grp"] = grp.reshape(N_DEV, N_POOL_GROUPS, rs // 4, POOL_GROUP_DIM).transpose(1, 0, 2, 3).reshape(
            N_POOL_GROUPS, POOL_GROUP_DIM, POOL_GROUP_DIM)
        return sq_all[:, :fp, :].reshape(N_DEV * fp, D)

    (a1, a1b, a1t), s_f0 = ffn_fwd(x0, x0b, wgu0, wd0_after, 0, 0, 0, dep_up=(h1[4],),
                                   dep_down=lambda: (pool_w["h2"][4],))
    h2 = pool_w["h2"]
    w_pin, w_pout, w_grp = pool_w["pin"], pool_w["pout"], pool_w["grp"]
    tm = min(512, T)
    row_spec = pl.BlockSpec((tm, D), lambda i, j, k: (i, 0))
    full_w = pl.BlockSpec((D, D), lambda i, j, k: (0, 0))
    u_pool = _mm(a1b, w_pin, grid=(T // tm, 1, 1), a_spec=row_spec, b_spec=full_w,
                 out_shape=jax.ShapeDtypeStruct((T, D), F32), out_spec=row_spec, name="pool_in")
    mixedb, mixedt = _pool_window(u_pool, False, "pool_window")
    y_pool, ysb, yst = _pool_group(mixedb, w_grp, pool_scale)
    a2, a2b, a2t, xh_p, rstd_p = _mm_ln(ysb, w_pout, a1, gain(0, 1), bias(0, 1), 1.0, "pool_out_ln")
    wgu1, wd1 = _ag_finish(h2, (a2,), "ag2")
    h3 = _ag_start([wgu_sh[2], wd_sh[2]], (wgu1,), "ag3")
    (a3, a3b, a3t), s_f1 = ffn_fwd(a2, a2b, wgu1, wd1.reshape(N_DEV * fp, D), 1, 0, 2, dep_up=(h3[4],))
    wgu2, wd2 = _ag_finish(h3, (a3,), "ag3")
    h4 = _ag_start([qkv_sh, aout_sh], (wgu2,), "ag4")
    (b1, b1b, b1t), s_f2 = ffn_fwd(a3, a3b, wgu2, wd2.reshape(N_DEV * fp, D), 2, 1, 0, dep_up=(h4[4],))
    wqkv_all, aout_all = _ag_finish(h4, (b1,), "ag4")
    h5 = _ag_start([wgu_sh[3], wd_sh[3]], (wqkv_all,), "ag5")
    w_aout = _slot_to_device_order(aout_all).reshape(D, D)
    dils = [d for _, d in DIL_CONFIGS]
    qkv_gs, o_gs, lse_gs = [], [], []
    for gi, d in enumerate(dils):
        qkv_g = _attn_qkv_group(b1b, wqkv_all, gi, d, f"attn_qkv{gi}", deps=(h5[4],) if gi == 0 else ())
        o_g, lse_g = _attn_fwd(qkv_g, d, slopes[gi], f"attn_fwd{gi}")
        qkv_gs.append(qkv_g)
        o_gs.append(o_g)
        lse_gs.append(lse_g)
    o32, ob, ot, lse_tot = _attn_combine(o_gs, lse_gs, dils)
    b2, b2b, b2t, xh_a, rstd_a = _mm_ln(ob, w_aout, b1, gain(1, 1), bias(1, 1), 1.0, "attn_out_ln")
    wgu3, wd3 = _ag_finish(h5, (b2,), "ag5")
    (b3, _, _), s_f3 = ffn_fwd(b2, b2b, wgu3, wd3.reshape(N_DEV * fp, D), 3, 1, 2)

    dy, loss_tile = _loss_head(b3, tgt)
    loss = lax.psum(loss_tile[0, 0], AXES)

    bm = min(512, D)
    dgains, dbiases = {}, {}
    rs_pending = []
    gsums = {}

    def rs_finish(after):
        h, tag = rs_pending.pop()
        chips, lands = _rs_wait(h, after, f"rs_{tag}_wait")
        gsums[tag] = [_sum_chips(a, r, q_idx, f"rs_sum_{tag}{i}") for i, (a, r) in enumerate(zip(chips, lands))]

    def rs_stage(bufs, tag):
        if rs_pending:
            rs_finish((bufs[-1],))
        recv = _rs_sibling(bufs, f"rs_sib_{tag}")
        chips = [_add_half(a, r, c_idx, f"rs_add_{tag}{i}") for i, (a, r) in enumerate(zip(bufs, recv))]
        h = _rs_start(chips, (), f"rs_{tag}")
        rs_pending.append((h, tag))
        return h[4]

    def ffn_bwd(dys, f, i, s, st, xt):
        dxres, dhb, dht, dg, db = _ln_bwd(dys, st["xh"], st["rstd"], gain(i, s), MACARON, f"ffn_ln_bwd{f}")
        dgains[(i, s)], dbiases[(i, s)] = dg, db
        dgu = _ffn_bwd_act(dhb, st["wd"], st["g"], st["u"], fp, f"ffn_bwd_act{f}")
        g_dt = _mm(dht, st["act"], grid=(D // bm, N_DEV, 1),
                   a_spec=pl.BlockSpec((bm, T), lambda r, j, k: (r, 0)),
                   b_spec=pl.BlockSpec((T, fp), lambda r, j, k: (0, j)),
                   out_shape=jax.ShapeDtypeStruct((N_DEV, D, fp), BF16),
                   out_spec=pl.BlockSpec((None, bm, fp), lambda r, j, k: (j, r, 0)), name=f"ffn_dwd{f}")
        g_gu = _mm(xt, dgu, grid=(D // bm, N_DEV, 1),
                   a_spec=pl.BlockSpec((bm, T), lambda r, j, k: (r, 0)),
                   b_spec=pl.BlockSpec((T, 2 * fp), lambda r, j, k: (0, j)),
                   out_shape=jax.ShapeDtypeStruct((N_DEV, D, 2 * fp), BF16),
                   out_spec=pl.BlockSpec((None, bm, 2 * fp), lambda r, j, k: (j, r, 0)), name=f"ffn_dwgu{f}")
        token = rs_stage([g_dt, g_gu], f"f{f}")
        tmx = min(1024, T)
        dx = _mm(dgu, st["wgu"], grid=(T // tmx, 1, N_DEV), nt=True,
                 a_spec=pl.BlockSpec((tmx, 2 * fp), lambda t, j, k: (t, k)),
                 b_spec=pl.BlockSpec((None, D, 2 * fp), lambda t, j, k: (k, 0, 0)),
                 out_shape=jax.ShapeDtypeStruct((T, D), F32),
                 out_spec=pl.BlockSpec((tmx, D), lambda t, j, k: (t, 0)), name=f"ffn_dx{f}", deps=(token,))
        return [dxres, dx]

    def dw_square(at, bmat, name):
        return _mm(at, bmat, grid=(D // bm, 1, 1),
                   a_spec=pl.BlockSpec((bm, T), lambda r, j, k: (r, 0)),
                   b_spec=pl.BlockSpec((T, D), lambda r, j, k: (0, 0)),
                   out_shape=jax.ShapeDtypeStruct((D, D), BF16),
                   out_spec=pl.BlockSpec((bm, D), lambda r, j, k: (r, 0)), name=name)

    def dx_square(a, w, name, deps=()):
        return _mm(a, w, grid=(T // tm, 1, 1), nt=True, a_spec=row_spec, b_spec=full_w,
                   out_shape=jax.ShapeDtypeStruct((T, D), F32), out_spec=row_spec, name=name, deps=deps)

    to_slots = lambda g2d: _device_to_slot_order(g2d.reshape(N_DEV, rs, D))

    d_b2 = ffn_bwd([dy], 3, 1, 2, s_f3, b2t)
    dxres, dmb, dmt, dg, db = _ln_bwd(d_b2, xh_a, rstd_a, gain(1, 1), 1.0, "attn_ln_bwd")
    dgains[(1, 1)], dbiases[(1, 1)] = dg, db
    g_aout = dw_square(ot, dmb, "attn_dwout")
    dobs, statss = _attn_bwd_prep(dmb, w_aout, o32, lse_tot, dils)
    ntile = 3 * D // QKV_TILE
    g_qkv = None
    dqkv_gs = []
    for gi, d in enumerate(dils):
        dqkv_g = _attn_bwd(qkv_gs[gi], dobs[gi], statss[gi], d, slopes[gi], f"attn_bwd{gi}")
        dqkv_gs.append(dqkv_g)
        L = T // d

        def out_map(r, n, k, gi=gi):
            slot, sub = _qkv_tile_block(ntile * gi + n)
            return slot, r, sub

        if d == 1:
            xt, a_spec = b1t, pl.BlockSpec((bm, T), lambda r, n, k: (r, 0))
        else:
            xt = _transpose_sub(b1, d, f"attn_xt{gi}")
            a_spec = pl.BlockSpec((None, bm, L), lambda r, n, k: (k, r, 0))
        g_qkv = _mm(xt, dqkv_g, grid=(D // bm, ntile, d), a_spec=a_spec,
                    b_spec=pl.BlockSpec((L, QKV_TILE), lambda r, n, k: (k, n)),
                    out_shape=jax.ShapeDtypeStruct((N_DEV, D, QKV_SHARD), BF16),
                    out_spec=pl.BlockSpec((None, bm, QKV_TILE), out_map),
                    name=f"attn_dwqkv{gi}", alias=g_qkv)
    token = rs_stage([to_slots(g_aout), g_qkv], "attn")
    dx_attn = [_attn_dx_group(dqkv_gs[gi], wqkv_all, gi, d, f"attn_dx{gi}", deps=(token,) if gi == 0 else ())
               for gi, d in enumerate(dils)]
    d_a3 = ffn_bwd([dxres] + dx_attn, 2, 1, 0, s_f2, a3t)
    d_a2 = ffn_bwd(d_a3, 1, 0, 2, s_f1, a2t)
    dxres, dmb, dmt, dg, db = _ln_bwd(d_a2, xh_p, rstd_p, gain(0, 1), 1.0, "pool_ln_bwd")
    dgains[(0, 1)], dbiases[(0, 1)] = dg, db
    g_pout = dw_square(yst, dmb, "pool_dwout")
    dyb, dscale = _pool_bwd_out(dmb, w_pout, y_pool, pool_scale)
    gd = POOL_GROUP_DIM
    g_grp = _mm(mixedt, dyb, grid=(N_POOL_GROUPS, 1, 1),
                a_spec=pl.BlockSpec((gd, T), lambda g, j, k: (g, 0)),
                b_spec=pl.BlockSpec((T, gd), lambda g, j, k: (0, g)),
                out_shape=jax.ShapeDtypeStruct((N_POOL_GROUPS, gd, gd), BF16),
                out_spec=pl.BlockSpec((None, gd, gd), lambda g, j, k: (g, 0, 0)), name="pool_dwgroup")
    tg = min(1024, T)
    dmixed = _mm(dyb, w_grp, grid=(N_POOL_GROUPS, T // tg, 1), nt=True,
                 a_spec=pl.BlockSpec((tg, gd), lambda g, t, k: (t, g)),
                 b_spec=pl.BlockSpec((None, gd, gd), lambda g, t, k: (g, 0, 0)),
                 out_shape=jax.ShapeDtypeStruct((T, D), F32),
                 out_spec=pl.BlockSpec((tg, gd), lambda g, t, k: (t, g)), name="pool_dmixed")
    dub, _ = _pool_window(dmixed, True, "pool_window_bwd")
    g_pin = dw_square(a1t, dub, "pool_dwin")
    g_grp_slots = _device_to_slot_order(
        g_grp.reshape(N_POOL_GROUPS, N_DEV, rs // 4, gd).transpose(1, 0, 2, 3).reshape(N_DEV, rs // 4, D))
    token = rs_stage([to_slots(g_pout), g_grp_slots, to_slots(g_pin)], "pool")
    dx_pool = dx_square(dub, w_pin, "pool_dx", deps=(token,))
    d_x0 = ffn_bwd([dxres, dx_pool], 0, 0, 0, s_f0, x0t)
    grad_x = _add2(d_x0[0], d_x0[1], "grad_x_add")
    rs_finish((grad_x,))
    grad_x = grad_x[None]
    gw_dt = [gsums[f"f{f}"][0] for f in range(4)]
    gw_gu = [gsums[f"f{f}"][1] for f in range(4)]
    gw_aout, gw_qkv = gsums["attn"]
    gw_pout, gw_grp, gw_pin = gsums["pool"]

    small = jnp.concatenate([dgains[(i, s)] for i in range(DEPTH) for s in range(3)]
                            + [dbiases[(i, s)] for i in range(DEPTH) for s in range(3)]
                            + [dscale, jnp.zeros((3, D), F32)], axis=0)
    _, small_sum = _small_all_gather(small, "ag_small_grads")
    dev = 4 * lax.axis_index("x") + 2 * lax.axis_index("y") + lax.axis_index("c")
    mine = lax.dynamic_slice_in_dim(small_sum, dev * rs, rs, axis=1)
    grads = {
        "ffn1_w_gate": jnp.stack([gw_gu[2 * i][:, :fs] for i in range(DEPTH)]),
        "ffn1_w_up": jnp.stack([gw_gu[2 * i][:, fp:fp + fs] for i in range(DEPTH)]),
        "ffn1_w_down": jnp.stack([gw_dt[2 * i].T[:fs] for i in range(DEPTH)]),
        "ffn2_w_gate": jnp.stack([gw_gu[2 * i + 1][:, :fs] for i in range(DEPTH)]),
        "ffn2_w_up": jnp.stack([gw_gu[2 * i + 1][:, fp:fp + fs] for i in range(DEPTH)]),
        "ffn2_w_down": jnp.stack([gw_dt[2 * i + 1].T[:fs] for i in range(DEPTH)]),
        "ln_gain": mine[0:DEPTH * 3].reshape(DEPTH, 3, rs),
        "ln_bias": mine[DEPTH * 3:2 * DEPTH * 3].reshape(DEPTH, 3, rs),
        "pool_w_in": gw_pin[None],
        "pool_w_group": gw_grp[:rs // 4].reshape(N_POOL_GROUPS, rs // 4, gd)[None],
        "pool_scale": small_sum[2 * DEPTH * 3][None],
        "pool_w_out": gw_pout[None],
        "attn_w_qkv": gw_qkv[None],
        "attn_w_out": gw_aout[None],
    }
    weights = dict(ffn1_w_gate=ffn1_w_gate, ffn1_w_up=ffn1_w_up, ffn1_w_down=ffn1_w_down,
                   ffn2_w_gate=ffn2_w_gate, ffn2_w_up=ffn2_w_up, ffn2_w_down=ffn2_w_down,
                   ln_gain=ln_gain, ln_bias=ln_bias, pool_w_in=pool_w_in, pool_w_group=pool_w_group,
                   pool_scale=pool_scale, pool_w_out=pool_w_out, attn_w_qkv=attn_w_qkv, attn_w_out=attn_w_out)
    ms = dict(ffn1_w_gate=m_ffn1_w_gate, ffn1_w_up=m_ffn1_w_up, ffn1_w_down=m_ffn1_w_down,
              ffn2_w_gate=m_ffn2_w_gate, ffn2_w_up=m_ffn2_w_up, ffn2_w_down=m_ffn2_w_down,
              ln_gain=m_ln_gain, ln_bias=m_ln_bias, pool_w_in=m_pool_w_in, pool_w_group=m_pool_w_group,
              pool_scale=m_pool_scale, pool_w_out=m_pool_w_out, attn_w_qkv=m_attn_w_qkv, attn_w_out=m_attn_w_out)
    vs = dict(ffn1_w_gate=v_ffn1_w_gate, ffn1_w_up=v_ffn1_w_up, ffn1_w_down=v_ffn1_w_down,
              ffn2_w_gate=v_ffn2_w_gate, ffn2_w_up=v_ffn2_w_up, ffn2_w_down=v_ffn2_w_down,
              ln_gain=v_ln_gain, ln_bias=v_ln_bias, pool_w_in=v_pool_w_in, pool_w_group=v_pool_w_group,
              pool_scale=v_pool_scale, pool_w_out=v_pool_w_out, attn_w_qkv=v_attn_w_qkv, attn_w_out=v_attn_w_out)
    names = list(weights)
    deltas, new_m, new_v = {}, {}, {}
    for nme in names:
        deltas[nme], new_m[nme], new_v[nme] = _adamw(weights[nme], grads[nme], ms[nme], vs[nme], f"adamw_{nme}")
    return (loss, grad_x, *[grads[k] for k in names], *[deltas[k] for k in names],
            *[new_m[k] for k in names], *[new_v[k] for k in names])
```

```python
import functools

import numpy as np
import jax
import jax.numpy as jnp
from jax import lax
from jax.experimental import pallas as pl
from jax.experimental.pallas import tpu as pltpu

F32 = jnp.float32
BF16 = jnp.bfloat16

D = 1024
N_DEV = 8
N_HEADS = 16
HEAD_DIM = 64
N_POOL_GROUPS = 4
POOL_GROUP_DIM = 256
POOL_HALF = (1, 2, 4, 8)
DIL_CONFIGS = ((128, 1), (512, 4), (2048, 16))
ATTN_HALO = 64
ATTN_BLOCK = 128
QKV_SHARD = 3 * 3 * D // N_DEV
DEPTH = 2
ALPHA = (2.0 * DEPTH) ** 0.25
MACARON = 0.5
LN_EPS = 1e-5
MASK_VALUE = -1e30
ADAM_LR = 0.001
ADAM_B1 = 0.9
ADAM_B2 = 0.999
ADAM_EPS = 1e-08
ADAM_WD = 0.01
ADAM_STEP = 10
LANE = 128
VMEM_LIMIT = 56 * 1024 * 1024
MESH = pl.DeviceIdType.MESH
AXES = ("x", "y", "c")


def _round_up(n, m):
    return (n + m - 1) // m * m


def _pcall(body, deps=(), **kw):
    if not deps:
        return pl.pallas_call(body, **kw)
    n_in, n_dep = len(kw["in_specs"]), len(deps)

    def wrapped(*refs):
        return body(*refs[:n_in], *refs[n_in + n_dep:])

    kw["in_specs"] = list(kw["in_specs"]) + [pl.BlockSpec(memory_space=pl.ANY)] * n_dep
    call = pl.pallas_call(wrapped, **kw)
    return lambda *args: call(*args, *deps)


def _params(*sem):
    return pltpu.CompilerParams(dimension_semantics=sem, vmem_limit_bytes=VMEM_LIMIT)


def _alibi_slopes():
    n = len(DIL_CONFIGS) * N_HEADS
    s = 2.0 ** (-8.0 * np.arange(1, n + 1) / n)
    return s.reshape(len(DIL_CONFIGS), N_HEADS).astype(np.float32)


def _my_slot():
    return 4 * lax.axis_index("c") + 2 * lax.axis_index("x") + lax.axis_index("y")


def _mm(a, b, *, grid, a_spec, b_spec, out_shape, out_spec, nt=False, name, alias=None, deps=()):
    nk = grid[2]
    dn = (((1,), (1,)), ((), ())) if nt else (((1,), (0,)), ((), ()))
    blk = tuple(s for s in out_spec.block_shape if s is not None)

    def body(*refs):
        a_ref, b_ref = refs[0], refs[1]
        o_ref = refs[3] if alias is not None else refs[2]
        p = lax.dot_general(a_ref[...], b_ref[...], dn, preferred_element_type=F32)
        if nk == 1:
            o_ref[...] = p.astype(o_ref.dtype)
        else:
            acc = refs[-1]
            k = pl.program_id(2)

            @pl.when(k == 0)
            def _():
                acc[...] = p

            @pl.when(k > 0)
            def _():
                acc[...] += p

            @pl.when(k == nk - 1)
            def _():
                o_ref[...] = acc[...].astype(o_ref.dtype)

    in_specs = [a_spec, b_spec]
    args = [a, b]
    aliases = {}
    if alias is not None:
        in_specs.append(pl.BlockSpec(memory_space=pl.ANY))
        args.append(alias)
        aliases = {2: 0}
    return _pcall(
        body, deps=deps, name=name, grid=grid, in_specs=in_specs, out_specs=out_spec, out_shape=out_shape,
        scratch_shapes=[] if nk == 1 else [pltpu.VMEM(blk, F32)],
        input_output_aliases=aliases,
        compiler_params=_params("parallel", "parallel", "arbitrary"),
    )(*args)


def _transpose_cast(x, name, deps=()):
    T = x.shape[0]
    tm = min(512, T)

    def body(x_ref, xb_ref, xt_ref):
        v = x_ref[...]
        xb_ref[...] = v.astype(BF16)
        xt_ref[...] = v.T.astype(BF16)

    return _pcall(
        body, deps=deps, name=name, grid=(T // tm,),
        in_specs=[pl.BlockSpec((tm, D), lambda t: (t, 0))],
        out_specs=[pl.BlockSpec((tm, D), lambda t: (t, 0)), pl.BlockSpec((D, tm), lambda t: (0, t))],
        out_shape=[jax.ShapeDtypeStruct((T, D), BF16), jax.ShapeDtypeStruct((D, T), BF16)],
        compiler_params=_params("parallel"),
    )(x)


def _mm_ln(a, b, xres, gain, bias, hscale, name, deps=()):
    T, K = a.shape
    tm = min(512, T)

    def body(a_ref, b_ref, x_ref, g_ref, bt_ref, y_ref, yb_ref, yt_ref, xh_ref, rs_ref):
        h = jnp.dot(a_ref[...], b_ref[...], preferred_element_type=F32)
        z = ALPHA * x_ref[...] + hscale * h
        mu = jnp.mean(z, axis=-1, keepdims=True)
        zc = z - mu
        var = jnp.mean(zc * zc, axis=-1, keepdims=True)
        rstd = lax.rsqrt(var + LN_EPS)
        xh = zc * rstd
        y = xh * g_ref[...] + bt_ref[...]
        y_ref[...] = y
        yb_ref[...] = y.astype(BF16)
        yt_ref[...] = y.T.astype(BF16)
        xh_ref[...] = xh
        rs_ref[...] = rstd

    row = pl.BlockSpec((tm, D), lambda t: (t, 0))
    vec = pl.BlockSpec((1, D), lambda t: (0, 0))
    return _pcall(
        body, deps=deps, name=name, grid=(T // tm,),
        in_specs=[pl.BlockSpec((tm, K), lambda t: (t, 0)), pl.BlockSpec((K, D), lambda t: (0, 0)), row, vec, vec],
        out_specs=[row, row, pl.BlockSpec((D, tm), lambda t: (0, t)), row, pl.BlockSpec((tm, 1), lambda t: (t, 0))],
        out_shape=[jax.ShapeDtypeStruct((T, D), F32), jax.ShapeDtypeStruct((T, D), BF16),
                   jax.ShapeDtypeStruct((D, T), BF16), jax.ShapeDtypeStruct((T, D), F32),
                   jax.ShapeDtypeStruct((T, 1), F32)],
        compiler_params=_params("parallel"),
    )(a, b, xres, gain, bias)


def _ln_bwd(dys, xhat, rstd, gain, hscale, name):
    T = xhat.shape[0]
    tm = min(512, T)
    n = len(dys)

    def body(*refs):
        dy_refs = refs[:n]
        xh_ref, rs_ref, g_ref, dx_ref, dh_ref, dht_ref, dg_ref, db_ref = refs[n:]
        dy = dy_refs[0][...]
        for r in dy_refs[1:]:
            dy = dy + r[...]
        xh = xh_ref[...]
        dxh = dy * g_ref[...]
        m1 = jnp.mean(dxh, axis=-1, keepdims=True)
        m2 = jnp.mean(dxh * xh, axis=-1, keepdims=True)
        dz = rs_ref[...] * (dxh - m1 - xh * m2)
        dx_ref[...] = ALPHA * dz
        dh = hscale * dz
        dh_ref[...] = dh.astype(BF16)
        dht_ref[...] = dh.T.astype(BF16)
        dg = jnp.sum(dy * xh, axis=0, keepdims=True)
        db = jnp.sum(dy, axis=0, keepdims=True)

        @pl.when(pl.program_id(0) == 0)
        def _():
            dg_ref[...] = dg
            db_ref[...] = db

        @pl.when(pl.program_id(0) > 0)
        def _():
            dg_ref[...] += dg
            db_ref[...] += db

    row = pl.BlockSpec((tm, D), lambda t: (t, 0))
    vec = pl.BlockSpec((1, D), lambda t: (0, 0))
    return _pcall(
        body, name=name, grid=(T // tm,),
        in_specs=[row] * n + [row, pl.BlockSpec((tm, 1), lambda t: (t, 0)), vec],
        out_specs=[row, row, pl.BlockSpec((D, tm), lambda t: (0, t)), vec, vec],
        out_shape=[jax.ShapeDtypeStruct((T, D), F32), jax.ShapeDtypeStruct((T, D), BF16),
                   jax.ShapeDtypeStruct((D, T), BF16), jax.ShapeDtypeStruct((1, D), F32),
                   jax.ShapeDtypeStruct((1, D), F32)],
        compiler_params=_params("arbitrary"),
    )(*dys, xhat, rstd, gain)


def _add2(a, b, name):
    T = a.shape[0]
    tm = min(512, T)

    def body(a_ref, b_ref, o_ref):
        o_ref[...] = a_ref[...] + b_ref[...]

    row = pl.BlockSpec((tm, D), lambda t: (t, 0))
    return _pcall(body, name=name, grid=(T // tm,), in_specs=[row, row], out_specs=row,
                  out_shape=jax.ShapeDtypeStruct((T, D), F32), compiler_params=_params("parallel"))(a, b)


def _loss_head(y, tgt):
    T = y.shape[0]
    tm = min(512, T)

    def body(y_ref, t_ref, dy_ref, l_ref):
        e = y_ref[...] - t_ref[...]
        dy_ref[...] = e * (1.0 / D)
        part = jnp.sum(jnp.sum(e * e, axis=1, keepdims=True), axis=0, keepdims=True) * (0.5 / D)

        @pl.when(pl.program_id(0) == 0)
        def _():
            l_ref[...] = jnp.zeros_like(l_ref)

        l_ref[...] += part

    row = pl.BlockSpec((tm, D), lambda t: (t, 0))
    return _pcall(
        body, name="loss_head", grid=(T // tm,),
        in_specs=[row, row],
        out_specs=[row, pl.BlockSpec((8, LANE), lambda t: (0, 0))],
        out_shape=[jax.ShapeDtypeStruct((T, D), F32), jax.ShapeDtypeStruct((8, LANE), F32)],
        compiler_params=_params("arbitrary"),
    )(y, tgt)


def _sigmoid(v):
    return 0.5 * jnp.tanh(0.5 * v) + 0.5


def _ffn_up(xb, wgu, fp, name, deps=()):
    T = xb.shape[0]
    tm = min(1024, T)

    def body(x_ref, w_ref, g_ref, u_ref, a_ref):
        p = jnp.dot(x_ref[...], w_ref[...], preferred_element_type=F32)
        g = p[:, :fp]
        u = p[:, fp:]
        a = g * _sigmoid(g) * u
        g_ref[...] = g.astype(BF16)
        u_ref[...] = u.astype(BF16)
        a_ref[...] = a.astype(BF16)

    out = pl.BlockSpec((tm, fp), lambda j, t: (t, j))
    shp = jax.ShapeDtypeStruct((T, N_DEV * fp), BF16)
    return _pcall(
        body, deps=deps, name=name, grid=(N_DEV, T // tm),
        in_specs=[pl.BlockSpec((tm, D), lambda j, t: (t, 0)),
                  pl.BlockSpec((None, D, 2 * fp), lambda j, t: (j, 0, 0))],
        out_specs=[out, out, out], out_shape=[shp, shp, shp],
        compiler_params=_params("parallel", "parallel"),
    )(xb, wgu)


def _ffn_bwd_act(dhb, wd, g, u, fp, name):
    T = dhb.shape[0]
    tm = min(1024, T)

    def body(dh_ref, w_ref, g_ref, u_ref, o_ref):
        da = lax.dot_general(dh_ref[...], w_ref[...], (((1,), (1,)), ((), ())), preferred_element_type=F32)
        gv = g_ref[...].astype(F32)
        uv = u_ref[...].astype(F32)
        sig = _sigmoid(gv)
        t = da * sig
        dup = t * gv
        dgate = t * uv * (1.0 + gv - gv * sig)
        o_ref[:, :fp] = dgate.astype(BF16)
        o_ref[:, fp:] = dup.astype(BF16)

    blk = pl.BlockSpec((tm, fp), lambda j, t: (t, j))
    return _pcall(
        body, name=name, grid=(N_DEV, T // tm),
        in_specs=[pl.BlockSpec((tm, D), lambda j, t: (t, 0)), pl.BlockSpec((fp, D), lambda j, t: (j, 0)), blk, blk],
        out_specs=pl.BlockSpec((tm, 2 * fp), lambda j, t: (t, j)),
        out_shape=jax.ShapeDtypeStruct((T, N_DEV * 2 * fp), BF16),
        compiler_params=_params("parallel", "parallel"),
    )(dhb, wd, g, u)


POOL_PAD = 16
POOL_CHUNK = 512


def _pool_window(v, transpose, name):
    T = v.shape[0]
    ch = min(POOL_CHUNK, T)
    ext = ch + 2 * POOL_PAD
    gd = POOL_GROUP_DIM

    def body(v_ref, o_ref, ot_ref, pad_ref):
        pad_ref[0:POOL_PAD, :] = jnp.zeros((POOL_PAD, gd), F32)
        pad_ref[POOL_PAD + T:POOL_PAD + T + POOL_PAD, :] = jnp.zeros((POOL_PAD, gd), F32)
        for gi, hw in enumerate(POOL_HALF):
            @pl.when(pl.program_id(0) == gi)
            def _(hw=hw):
                def count(t):
                    return (jnp.minimum(t + hw, T) - jnp.maximum(t - hw, 0)).astype(F32)

                if transpose:
                    t_all = lax.broadcasted_iota(jnp.int32, (T, gd), 0)
                    pad_ref[POOL_PAD:POOL_PAD + T, :] = v_ref[...] / count(t_all)
                else:
                    pad_ref[POOL_PAD:POOL_PAD + T, :] = v_ref[...]
                shift = hw if transpose else hw - 1
                for c in range(T // ch):
                    e = pad_ref[c * ch:c * ch + ext, :]
                    step = 1
                    while step < 2 * hw:
                        e = e + pltpu.roll(e, step, 0)
                        step *= 2
                    if shift:
                        e = pltpu.roll(e, ext - shift, 0)
                    s = e[POOL_PAD:POOL_PAD + ch, :]
                    center = v_ref[c * ch:(c + 1) * ch, :]
                    if transpose:
                        res = s - center
                    else:
                        t_idx = c * ch + lax.broadcasted_iota(jnp.int32, (ch, gd), 0)
                        res = s / count(t_idx) - center
                    o_ref[c * ch:(c + 1) * ch, :] = res.astype(BF16)
                    ot_ref[:, c * ch:(c + 1) * ch] = res.T.astype(BF16)

    return _pcall(
        body, name=name, grid=(N_POOL_GROUPS,),
        in_specs=[pl.BlockSpec((T, gd), lambda g: (0, g))],
        out_specs=[pl.BlockSpec((T, gd), lambda g: (0, g)), pl.BlockSpec((gd, T), lambda g: (g, 0))],
        out_shape=[jax.ShapeDtypeStruct((T, D), BF16), jax.ShapeDtypeStruct((D, T), BF16)],
        scratch_shapes=[pltpu.VMEM((T + 2 * POOL_PAD, gd), F32)],
        compiler_params=_params("arbitrary"),
    )(v)


def _pool_group(mixedb, wgroup, scale):
    T = mixedb.shape[0]
    tm = min(1024, T)
    gd = POOL_GROUP_DIM

    def body(a_ref, w_ref, s_ref, y_ref, ys_ref, yst_ref):
        y = jnp.dot(a_ref[...], w_ref[...], preferred_element_type=F32)
        ys = y * s_ref[...]
        y_ref[...] = y
        ys_ref[...] = ys.astype(BF16)
        yst_ref[...] = ys.T.astype(BF16)

    blk = pl.BlockSpec((tm, gd), lambda g, t: (t, g))
    return _pcall(
        body, name="pool_group", grid=(N_POOL_GROUPS, T // tm),
        in_specs=[blk, pl.BlockSpec((None, gd, gd), lambda g, t: (g, 0, 0)), pl.BlockSpec((1, gd), lambda g, t: (0, g))],
        out_specs=[blk, blk, pl.BlockSpec((gd, tm), lambda g, t: (g, t))],
        out_shape=[jax.ShapeDtypeStruct((T, D), F32), jax.ShapeDtypeStruct((T, D), BF16),
                   jax.ShapeDtypeStruct((D, T), BF16)],
        compiler_params=_params("parallel", "parallel"),
    )(mixedb, wgroup, scale)


def _pool_bwd_out(dmb, w_out, y, scale):
    T = dmb.shape[0]
    tm = min(512, T)

    def body(a_ref, w_ref, y_ref, s_ref, dy_ref, ds_ref):
        dys = lax.dot_general(a_ref[...], w_ref[...], (((1,), (1,)), ((), ())), preferred_element_type=F32)
        dy_ref[...] = (dys * s_ref[...]).astype(BF16)
        part = jnp.sum(dys * y_ref[...], axis=0, keepdims=True)

        @pl.when(pl.program_id(0) == 0)
        def _():
            ds_ref[...] = part

        @pl.when(pl.program_id(0) > 0)
        def _():
            ds_ref[...] += part

    row = pl.BlockSpec((tm, D), lambda t: (t, 0))
    vec = pl.BlockSpec((1, D), lambda t: (0, 0))
    return _pcall(
        body, name="pool_bwd_out", grid=(T // tm,),
        in_specs=[row, pl.BlockSpec((D, D), lambda t: (0, 0)), row, vec],
        out_specs=[row, vec],
        out_shape=[jax.ShapeDtypeStruct((T, D), BF16), jax.ShapeDtypeStruct((1, D), F32)],
        compiler_params=_params("arbitrary"),
    )(dmb, w_out, y, scale)


def _attn_masks(n, L, d):
    w = ATTN_BLOCK + 2 * ATTN_HALO
    a = lax.broadcasted_iota(jnp.int32, (ATTN_BLOCK, w), 0)
    c = lax.broadcasted_iota(jnp.int32, (ATTN_BLOCK, w), 1)
    rel = c - ATTN_HALO - a
    j = n * ATTN_BLOCK - ATTN_HALO + c
    valid = (jnp.abs(rel) <= ATTN_HALO) & (j >= 0) & (j < L)
    dist = (d * jnp.abs(rel)).astype(F32)
    return valid, dist


def _lane_col(st, idx):
    lane = lax.broadcasted_iota(jnp.int32, st.shape, 1)
    return jnp.sum(jnp.where(lane == idx, st, 0.0), axis=1, keepdims=True)


def _window_specs(nb, d, col, width):
    last = 2 * d * nb - 1

    def prev(r, n):
        return (jnp.maximum(2 * (r * nb + n) - 1, 0), col)

    def cur(r, n):
        return (r * nb + n, col)

    def nxt(r, n):
        return (jnp.minimum(2 * (r * nb + n) + 2, last), col)

    return [pl.BlockSpec((ATTN_HALO, width), prev), pl.BlockSpec((ATTN_BLOCK, width), cur),
            pl.BlockSpec((ATTN_HALO, width), nxt)]


def _attn_fwd(qkv_g, d, slopes, name):
    T = qkv_g.shape[0]
    L = T // d
    nb = L // ATTN_BLOCK

    def body(q_ref, kp_ref, kc_ref, kn_ref, vp_ref, vc_ref, vn_ref, o_ref, lse_ref):
        valid, dist = _attn_masks(pl.program_id(1), L, d)
        lane = lax.broadcasted_iota(jnp.int32, (ATTN_BLOCK, LANE), 1)
        first = lane < HEAD_DIM
        head_mask = [jnp.where(first, 1.0, 0.0).astype(BF16), jnp.where(first, 0.0, 1.0).astype(BF16)]
        lse_acc = jnp.zeros((ATTN_BLOCK, LANE), F32)
        for hp in range(N_HEADS // 2):
            cs = slice(hp * LANE, (hp + 1) * LANE)
            q2 = q_ref[:, cs]
            k2 = jnp.concatenate([kp_ref[:, cs], kc_ref[:, cs], kn_ref[:, cs]], axis=0)
            v2 = jnp.concatenate([vp_ref[:, cs], vc_ref[:, cs], vn_ref[:, cs]], axis=0)
            outs = []
            for hh in range(2):
                h = 2 * hp + hh
                qh = q2 * head_mask[hh]
                s = lax.dot_general(qh, k2, (((1,), (1,)), ((), ())), preferred_element_type=F32)
                s = s * (HEAD_DIM ** -0.5) - float(slopes[h]) * dist
                s = jnp.where(valid, s, MASK_VALUE)
                m = jnp.max(s, axis=1, keepdims=True)
                p = jnp.exp(s - m)
                l = jnp.sum(p, axis=1, keepdims=True)
                o = jnp.dot(p.astype(BF16), v2, preferred_element_type=F32) / l
                outs.append(o)
                lse_acc = jnp.where(lane == h, m + jnp.log(l), lse_acc)
            o_ref[:, cs] = jnp.where(first, outs[0], outs[1])
        lse_ref[...] = lse_acc

    specs = ([pl.BlockSpec((ATTN_BLOCK, D), lambda r, n: (r * nb + n, 0))]
             + _window_specs(nb, d, 1, D) + _window_specs(nb, d, 2, D))
    row = lambda w: pl.BlockSpec((ATTN_BLOCK, w), lambda r, n: (r * nb + n, 0))
    return _pcall(
        body, name=name, grid=(d, nb), in_specs=specs,
        out_specs=[row(D), row(LANE)],
        out_shape=[jax.ShapeDtypeStruct((T, D), F32), jax.ShapeDtypeStruct((T, LANE), F32)],
        compiler_params=_params("parallel", "parallel"),
    )(*([qkv_g] * 7))


def _stage(scr3, val):
    for c in range(val.shape[1] // LANE):
        scr3[c] = val[:, c * LANE:(c + 1) * LANE]


def _unstage(scr3):
    return jnp.concatenate([scr3[c] for c in range(scr3.shape[0])], axis=1)


def _gather_rows(scr3, r, n, d):
    return jnp.concatenate([scr3[c, pl.ds(r, n, stride=d), :] for c in range(scr3.shape[0])], axis=1)


def _scatter_rows(scr3, r, n, d, val):
    for c in range(scr3.shape[0]):
        scr3[c, pl.ds(r, n, stride=d), :] = val[:, c * LANE:(c + 1) * LANE]


def _attn_combine(os_, lses, dils):
    T = os_[0].shape[0]
    tm = min(256, T)
    ng = len(os_)
    n_scr = sum(1 for d in dils if d > 1)

    def body(*refs):
        in_o = refs[:ng]
        in_l = refs[ng:2 * ng]
        o32_ref, ob_ref, ot_ref, lt_ref = refs[2 * ng:2 * ng + 4]
        scr = refs[2 * ng + 4:]
        o_chunk, l_refs, si = [], [], 0
        for g, d in enumerate(dils):
            if d == 1:
                o_chunk.append(lambda hp, g=g: in_o[g][:, hp * LANE:(hp + 1) * LANE])
                l_refs.append(in_l[g])
                continue
            so, sl = scr[2 * si], scr[2 * si + 1]
            si += 1
            for r in range(d):
                _scatter_rows(so, r, tm // d, d, in_o[g][r])
                sl[pl.ds(r, tm // d, stride=d), :] = in_l[g][r]
            o_chunk.append(lambda hp, so=so: so[hp])
            l_refs.append(sl)
        ls = [r[...] for r in l_refs]
        m = ls[0]
        for l in ls[1:]:
            m = jnp.maximum(m, l)
        tot = jnp.exp(ls[0] - m)
        for l in ls[1:]:
            tot = tot + jnp.exp(l - m)
        lt = m + jnp.log(tot)
        lt_ref[...] = lt
        ws = [jnp.exp(l - lt) for l in ls]
        lane = lax.broadcasted_iota(jnp.int32, (tm, LANE), 1)
        first = lane < HEAD_DIM
        for hp in range(N_HEADS // 2):
            cs = slice(hp * LANE, (hp + 1) * LANE)
            acc = jnp.zeros((tm, LANE), F32)
            for g in range(ng):
                wt = jnp.where(first, _lane_col(ws[g], 2 * hp), _lane_col(ws[g], 2 * hp + 1))
                acc = acc + wt * o_chunk[g](hp)
            o32_ref[:, cs] = acc
            ob_ref[:, cs] = acc.astype(BF16)
        ot_ref[...] = o32_ref[...].T.astype(BF16)

    row = pl.BlockSpec((tm, D), lambda t: (t, 0))
    st = pl.BlockSpec((tm, LANE), lambda t: (t, 0))

    def sub_spec(d, w):
        return pl.BlockSpec((tm, w), lambda t: (t, 0)) if d == 1 else pl.BlockSpec((d, tm // d, w), lambda t: (0, t, 0))

    def sub_view(a, d):
        return a if d == 1 else a.reshape(d, T // d, a.shape[1])

    return _pcall(
        body, name="attn_combine", grid=(T // tm,),
        in_specs=[sub_spec(d, D) for d in dils] + [sub_spec(d, LANE) for d in dils],
        out_specs=[row, row, pl.BlockSpec((D, tm), lambda t: (0, t)), st],
        out_shape=[jax.ShapeDtypeStruct((T, D), F32), jax.ShapeDtypeStruct((T, D), BF16),
                   jax.ShapeDtypeStruct((D, T), BF16), jax.ShapeDtypeStruct((T, LANE), F32)],
        scratch_shapes=[pltpu.VMEM(s, F32) for _ in range(n_scr) for s in ((D // LANE, tm, LANE), (tm, LANE))],
        compiler_params=_params("parallel"),
    )(*[sub_view(a, d) for a, d in zip(os_, dils)], *[sub_view(a, d) for a, d in zip(lses, dils)])


def _attn_bwd_prep(dmb, w_out, o32, lse_tot, dils):
    T = dmb.shape[0]
    tm = min(512, T)
    ng = len(dils)

    def body(a_ref, w_ref, o_ref, l_ref, *rest):
        do_refs, st_refs = rest[:ng], rest[ng:2 * ng]
        do_scr, st_scr = rest[2 * ng:]
        do = lax.dot_general(a_ref[...], w_ref[...], (((1,), (1,)), ((), ())), preferred_element_type=F32)
        _stage(do_scr, do)
        prod = do * o_ref[...]
        lane = lax.broadcasted_iota(jnp.int32, (tm, LANE), 1)
        first = lane < HEAD_DIM
        st = jnp.where(lane < N_HEADS, l_ref[...], 0.0)
        for hp in range(N_HEADS // 2):
            pr = prod[:, hp * LANE:(hp + 1) * LANE]
            d0 = jnp.sum(jnp.where(first, pr, 0.0), axis=1, keepdims=True)
            d1 = jnp.sum(jnp.where(first, 0.0, pr), axis=1, keepdims=True)
            st = jnp.where(lane == N_HEADS + 2 * hp, d0, st)
            st = jnp.where(lane == N_HEADS + 2 * hp + 1, d1, st)
        st_scr[...] = st
        for g, d in enumerate(dils):
            if d == 1:
                do_refs[g][...] = do.astype(BF16)
                st_refs[g][...] = st
                continue
            for r in range(d):
                do_refs[g][r] = _gather_rows(do_scr, r, tm // d, d).astype(BF16)
                st_refs[g][r] = st_scr[pl.ds(r, tm // d, stride=d), :]

    row = pl.BlockSpec((tm, D), lambda t: (t, 0))
    stb = pl.BlockSpec((tm, LANE), lambda t: (t, 0))

    def sub_spec(d, w):
        return pl.BlockSpec((tm, w), lambda t: (t, 0)) if d == 1 else pl.BlockSpec((d, tm // d, w), lambda t: (0, t, 0))

    def sub_shape(d, w, dt):
        return jax.ShapeDtypeStruct((T, w) if d == 1 else (d, T // d, w), dt)

    outs = _pcall(
        body, name="attn_bwd_prep", grid=(T // tm,),
        in_specs=[row, pl.BlockSpec((D, D), lambda t: (0, 0)), row, stb],
        out_specs=[sub_spec(d, D) for d in dils] + [sub_spec(d, LANE) for d in dils],
        out_shape=[sub_shape(d, D, BF16) for d in dils] + [sub_shape(d, LANE, F32) for d in dils],
        scratch_shapes=[pltpu.VMEM((D // LANE, tm, LANE), F32), pltpu.VMEM((tm, LANE), F32)],
        compiler_params=_params("parallel"),
    )(dmb, w_out, o32, lse_tot)
    return ([o.reshape(T, D) for o in outs[:ng]], [o.reshape(T, LANE) for o in outs[ng:]])


def _attn_bwd(qkv_g, do_g, st_g, d, slopes, name):
    T = qkv_g.shape[0]
    L = T // d
    nb = L // ATTN_BLOCK
    scale = HEAD_DIM ** -0.5
    nt = (((1,), (1,)), ((), ()))

    def body(qp_ref, qc_ref, qn_ref, kp_ref, kc_ref, kn_ref, vp_ref, vc_ref, vn_ref,
             dp_ref, dc_ref, dn_ref, sp_ref, sc_ref, sn_ref, o_ref):
        valid, dist = _attn_masks(pl.program_id(1), L, d)
        lane = lax.broadcasted_iota(jnp.int32, (ATTN_BLOCK, LANE), 1)
        first = lane < HEAD_DIM
        head_mask = [jnp.where(first, 1.0, 0.0).astype(BF16), jnp.where(first, 0.0, 1.0).astype(BF16)]
        stc = sc_ref[...]
        stw_t = jnp.concatenate([sp_ref[...], stc, sn_ref[...]], axis=0).T
        for hp in range(N_HEADS // 2):
            cs = slice(hp * LANE, (hp + 1) * LANE)
            cat = lambda a, b, c: jnp.concatenate([a[:, cs], b[:, cs], c[:, cs]], axis=0)
            q2, k2, v2, do2 = qc_ref[:, cs], kc_ref[:, cs], vc_ref[:, cs], dc_ref[:, cs]
            qw, kw, vw, dow = cat(qp_ref, qc_ref, qn_ref), cat(kp_ref, kc_ref, kn_ref), cat(vp_ref, vc_ref, vn_ref), cat(dp_ref, dc_ref, dn_ref)
            dqs, dks, dvs = [], [], []
            for hh in range(2):
                h = 2 * hp + hh
                pick = lambda t, hh=hh: t * head_mask[hh]
                bias = float(slopes[h]) * dist
                s = lax.dot_general(pick(q2), kw, nt, preferred_element_type=F32) * scale - bias
                s = jnp.where(valid, s, MASK_VALUE)
                p = jnp.exp(s - _lane_col(stc, h))
                dp = lax.dot_general(pick(do2), vw, nt, preferred_element_type=F32)
                ds = p * (dp - _lane_col(stc, N_HEADS + h)) * scale
                dqs.append(jnp.dot(ds.astype(BF16), kw, preferred_element_type=F32))
                st_ = lax.dot_general(pick(k2), qw, nt, preferred_element_type=F32) * scale - bias
                st_ = jnp.where(valid, st_, MASK_VALUE)
                pt = jnp.exp(st_ - stw_t[h:h + 1, :])
                dvs.append(jnp.dot(pt.astype(BF16), dow, preferred_element_type=F32))
                dpt = lax.dot_general(pick(v2), dow, nt, preferred_element_type=F32)
                dst = pt * (dpt - stw_t[N_HEADS + h:N_HEADS + h + 1, :]) * scale
                dks.append(jnp.dot(dst.astype(BF16), qw, preferred_element_type=F32))
            o_ref[:, hp * LANE:(hp + 1) * LANE] = jnp.where(first, dqs[0], dqs[1]).astype(BF16)
            o_ref[:, D + hp * LANE:D + (hp + 1) * LANE] = jnp.where(first, dks[0], dks[1]).astype(BF16)
            o_ref[:, 2 * D + hp * LANE:2 * D + (hp + 1) * LANE] = jnp.where(first, dvs[0], dvs[1]).astype(BF16)

    specs = (_window_specs(nb, d, 0, D) + _window_specs(nb, d, 1, D) + _window_specs(nb, d, 2, D)
             + _window_specs(nb, d, 0, D) + _window_specs(nb, d, 0, LANE))
    return _pcall(
        body, name=name, grid=(d, nb), in_specs=specs,
        out_specs=pl.BlockSpec((ATTN_BLOCK, 3 * D), lambda r, n: (r * nb + n, 0)),
        out_shape=jax.ShapeDtypeStruct((T, 3 * D), BF16),
        compiler_params=_params("parallel", "parallel"),
    )(*([qkv_g] * 9), *([do_g] * 3), *([st_g] * 3))


QKV_TILE = QKV_SHARD // 3


def _qkv_tile_block(nn):
    dev = nn // 3
    return 4 * (dev % 2) + dev // 2, nn % 3


def _attn_qkv_group(xb, wqkv, gi, d, name, deps=()):
    T = xb.shape[0]
    tq = min(1024, T)
    nsub = tq // d
    tn = 768
    ntile = 3 * D // tn

    def body(x_ref, w_ref, o_ref, *scr):
        p = jnp.dot(x_ref[...], w_ref[...], preferred_element_type=F32)
        if d == 1:
            o_ref[...] = p.astype(BF16)
        else:
            _stage(scr[0], p)
            for r in range(d):
                o_ref[r] = _gather_rows(scr[0], r, nsub, d).astype(BF16)

    if d == 1:
        out_spec = pl.BlockSpec((tq, tn), lambda n, t: (t, n))
        out_shape = jax.ShapeDtypeStruct((T, 3 * D), BF16)
    else:
        out_spec = pl.BlockSpec((d, nsub, tn), lambda n, t: (0, t, n))
        out_shape = jax.ShapeDtypeStruct((d, T // d, 3 * D), BF16)
    out = _pcall(
        body, deps=deps, name=name, grid=(ntile, T // tq),
        in_specs=[pl.BlockSpec((tq, D), lambda n, t: (t, 0)),
                  pl.BlockSpec((D, tn), lambda n, t: (0, ntile * gi + n))],
        out_specs=out_spec, out_shape=out_shape,
        scratch_shapes=[] if d == 1 else [pltpu.VMEM((tn // LANE, tq, LANE), F32)],
        compiler_params=_params("parallel", "parallel"),
    )(xb, wqkv)
    return out.reshape(T, 3 * D)


def _attn_dx_group(dqkv_g, wqkv, gi, d, name, deps=()):
    T = dqkv_g.shape[0]
    tq = min(512, T)
    nsub = tq // d

    def body(a_ref, w_ref, o_ref, *stage):
        a = a_ref[...]
        if d > 1:
            a = a.reshape(tq, 3 * D)
        p = lax.dot_general(a, w_ref[...], (((1,), (1,)), ((), ())), preferred_element_type=F32)
        if d == 1:
            o_ref[...] = p
        else:
            for r in range(d):
                _scatter_rows(stage[0], r, nsub, d, p[r * nsub:(r + 1) * nsub, :])
            o_ref[...] = _unstage(stage[0])

    if d == 1:
        a_spec = pl.BlockSpec((tq, 3 * D), lambda t: (t, 0))
        a = dqkv_g
    else:
        a_spec = pl.BlockSpec((d, nsub, 3 * D), lambda t: (0, t, 0))
        a = dqkv_g.reshape(d, T // d, 3 * D)
    return _pcall(
        body, deps=deps, name=name, grid=(T // tq,),
        in_specs=[a_spec, pl.BlockSpec((D, 3 * D), lambda t: (0, gi))],
        out_specs=pl.BlockSpec((tq, D), lambda t: (t, 0)),
        out_shape=jax.ShapeDtypeStruct((T, D), F32),
        scratch_shapes=[] if d == 1 else [pltpu.VMEM((D // LANE, tq, LANE), F32)],
        compiler_params=_params("parallel"),
    )(a, wqkv)


def _attn_dw_group(xt, dqkv_g, gi, d, prev, name):
    T = dqkv_g.shape[0]
    L = T // d
    bm = 512
    ntile = 3 * D // QKV_TILE

    def body(*refs):
        a_ref, b_ref = refs[0], refs[1]
        o_ref, cat = refs[-2], refs[-1]
        if d == 1:
            a = a_ref[...]
        else:
            @pl.when(pl.program_id(1) == 0)
            def _():
                for r in range(d):
                    cat[:, r * L:(r + 1) * L] = a_ref[r]
            a = cat[...]
        o_ref[...] = jnp.dot(a, b_ref[...], preferred_element_type=F32).astype(BF16)

    def out_map(i, n):
        slot, sub = _qkv_tile_block(ntile * gi + n)
        return slot, i, sub

    a_spec = (pl.BlockSpec((bm, T), lambda i, n: (i, 0)) if d == 1
              else pl.BlockSpec((d, bm, L), lambda i, n: (0, i, 0)))
    in_specs = [a_spec, pl.BlockSpec((T, QKV_TILE), lambda i, n: (0, n))]
    args = [xt, dqkv_g]
    aliases = {}
    if prev is not None:
        in_specs.append(ANY_SPEC)
        args.append(prev)
        aliases = {2: 0}
    return _pcall(
        body, name=name, grid=(D // bm, ntile), in_specs=in_specs,
        out_specs=pl.BlockSpec((None, bm, QKV_TILE), out_map),
        out_shape=jax.ShapeDtypeStruct((N_DEV, D, QKV_SHARD), BF16),
        scratch_shapes=[pltpu.VMEM((bm, T), BF16)],
        input_output_aliases=aliases,
        compiler_params=_params("parallel", "arbitrary"),
    )(*args)


def _transpose_sub(x, d, name):
    T = x.shape[0]
    tm = LANE * d

    def body(x_ref, o_ref, scr):
        for c in range(D // LANE):
            scr[...] = x_ref[:, c * LANE:(c + 1) * LANE]
            for r in range(d):
                o_ref[r, c * LANE:(c + 1) * LANE, :] = scr[pl.ds(r, LANE, stride=d), :].T.astype(BF16)

    return _pcall(
        body, name=name, grid=(T // tm,),
        in_specs=[pl.BlockSpec((tm, D), lambda t: (t, 0))],
        out_specs=pl.BlockSpec((d, D, LANE), lambda t: (0, 0, t)),
        out_shape=jax.ShapeDtypeStruct((d, D, T // d), BF16),
        scratch_shapes=[pltpu.VMEM((tm, LANE), F32)],
        compiler_params=_params("parallel"),
    )(x)


HBM_SPEC = pl.BlockSpec(memory_space=pltpu.HBM)
SEM_SPEC = pl.BlockSpec(memory_space=pltpu.SEMAPHORE)
ANY_SPEC = pl.BlockSpec(memory_space=pl.ANY)
DATAFLOW = pltpu.SideEffectType.DATAFLOW_SIDE_EFFECTING


def _me_and_peers():
    x, y, c = lax.axis_index("x"), lax.axis_index("y"), lax.axis_index("c")
    return (x, y, c), [(x, y, 1 - c), (1 - x, y, c), (x, 1 - y, c), (1 - x, 1 - y, c)]


def _slot(px, py, pc):
    return 4 * pc + 2 * px + py


def _split_start(srcs, lands, after, start_copies, n_sem, name):
    n = len(srcs)
    n_after = len(after)

    def body(*refs):
        src_refs, land_refs = refs[:n], refs[n:2 * n]
        send_sems, recv_sems = refs[2 * n + n_after], refs[2 * n + n_after + 1]
        token = refs[-1]
        start_copies(src_refs, land_refs, send_sems, recv_sems)
        token[...] = jnp.zeros_like(token)

    outs = _pcall(
        body, name=name,
        in_specs=[HBM_SPEC] * (2 * n) + [ANY_SPEC] * n_after,
        out_shape=(pltpu.SemaphoreType.DMA(n_sem), pltpu.SemaphoreType.DMA(n_sem),
                   *[pltpu.HBM(a.shape, a.dtype) for a in srcs], *[pltpu.HBM(a.shape, a.dtype) for a in lands],
                   jax.ShapeDtypeStruct((8, LANE), F32)),
        out_specs=(SEM_SPEC, SEM_SPEC, *[HBM_SPEC] * (2 * n), pl.BlockSpec(memory_space=pltpu.VMEM)),
        input_output_aliases={i: 2 + i for i in range(2 * n)},
        compiler_params=pltpu.CompilerParams(has_side_effects=DATAFLOW),
    )(*[pltpu.with_memory_space_constraint(a, pltpu.HBM) for a in srcs],
      *[pltpu.with_memory_space_constraint(a, pltpu.HBM) for a in lands], *after)
    return outs[0], outs[1], list(outs[2:2 + n]), list(outs[2 + n:2 + 2 * n]), outs[-1]


def _split_wait(handle, after, wait_copies, name):
    send_sems, recv_sems, srcs, lands, _ = handle
    n = len(srcs)

    def body(*refs):
        src_refs, land_refs = refs[:n], refs[n:2 * n]
        wait_copies(src_refs, land_refs, refs[2 * n], refs[2 * n + 1])

    outs = _pcall(
        body, name=name,
        in_specs=[HBM_SPEC] * (2 * n) + [SEM_SPEC, SEM_SPEC] + [ANY_SPEC] * len(after),
        out_shape=tuple(pltpu.HBM(a.shape, a.dtype) for a in srcs + lands),
        out_specs=tuple([HBM_SPEC] * (2 * n)),
        input_output_aliases={i: i for i in range(2 * n)},
        compiler_params=pltpu.CompilerParams(has_side_effects=DATAFLOW),
    )(*srcs, *lands, send_sems, recv_sems, *after)
    return list(outs[:n]), list(outs[n:])


def _ag_copies(src_refs, land_refs, send_sems, recv_sems, received):
    me, peers = _me_and_peers()
    cps = []
    for i in range(len(src_refs)):
        for k, to in enumerate(peers):
            cps.append(pltpu.make_async_remote_copy(
                src_ref=src_refs[i], dst_ref=land_refs[i].at[_slot(*(to if received else me))],
                send_sem=send_sems.at[4 * i + k], recv_sem=recv_sems.at[4 * i + k], device_id=to,
                device_id_type=MESH))
    return cps


def _ag_start(shards, after, name):
    lands = [lax.empty((N_DEV,) + a.shape, a.dtype) for a in shards]

    def start(src_refs, land_refs, send_sems, recv_sems):
        for cp in _ag_copies(src_refs, land_refs, send_sems, recv_sems, False):
            cp.start()

    return _split_start(shards, lands, after, start, (4 * len(shards),), name)


def _ag_finish(handle, after, name):
    def wait(src_refs, land_refs, send_sems, recv_sems):
        for cp in _ag_copies(src_refs, land_refs, send_sems, recv_sems, True):
            cp.wait_send()
            cp.wait_recv()

    shards, lands = _split_wait(handle, after, wait, name + "_wait")
    n = len(shards)

    def body(*refs):
        src_refs, out_refs = refs[:n], refs[2 * n:3 * n]
        send_sems, recv_sems, local_sems = refs[3 * n:3 * n + 3]
        bounce = refs[3 * n + 3:]
        me, peers = _me_and_peers()
        loads = [pltpu.make_async_copy(src_refs[i], bounce[i], local_sems.at[i]) for i in range(n)]
        mine = [pltpu.make_async_copy(bounce[i], out_refs[i].at[_slot(*me)], local_sems.at[i]) for i in range(n)]
        for cp in loads:
            cp.start()
        cps = []
        for i in range(n):
            for j, chip in enumerate(peers[1:]):
                blk = out_refs[i].at[_slot(*chip)]
                cps.append(pltpu.make_async_remote_copy(
                    src_ref=blk, dst_ref=blk, send_sem=send_sems.at[i, j], recv_sem=recv_sems.at[i, j],
                    device_id=peers[0], device_id_type=MESH))
        for cp in cps:
            cp.start()
        for ld, st in zip(loads, mine):
            ld.wait()
            st.start()
        for cp in cps:
            cp.wait()
        for cp in mine:
            cp.wait()

    outs = _pcall(
        body, name=name + "_pass",
        in_specs=[ANY_SPEC] * (2 * n), out_specs=[ANY_SPEC] * n,
        out_shape=[jax.ShapeDtypeStruct(a.shape, a.dtype) for a in lands],
        input_output_aliases={n + i: i for i in range(n)},
        scratch_shapes=[pltpu.SemaphoreType.DMA((n, 3)), pltpu.SemaphoreType.DMA((n, 3)),
                        pltpu.SemaphoreType.DMA((n,))] + [pltpu.VMEM(a.shape, a.dtype) for a in shards],
        compiler_params=pltpu.CompilerParams(vmem_limit_bytes=VMEM_LIMIT),
    )(*shards, *lands)
    return list(outs)


def _small_all_gather(v, name):
    R, C = v.shape

    def body(x_ref, out_ref, sum_ref, send_sems, recv_sems, local_sem):
        x, y, c = lax.axis_index("x"), lax.axis_index("y"), lax.axis_index("c")
        me, sibling = (x, y, c), (x, y, 1 - c)
        chips = [(1 - x, y), (x, 1 - y), (1 - x, 1 - y)]

        def rows(px, py, pc):
            return out_ref.at[4 * pc + 2 * px + py]

        def copy(k, block, to, src=None):
            return pltpu.make_async_remote_copy(
                src_ref=rows(*block) if src is None else src, dst_ref=rows(*block),
                send_sem=send_sems.at[k], recv_sem=recv_sems.at[k],
                device_id=to, device_id_type=MESH)

        mine = pltpu.make_async_copy(x_ref, rows(*me), local_sem)
        mine.start()
        first = [copy(0, me, sibling, src=x_ref)]
        first += [copy(1 + j, me, (*chip, c), src=x_ref) for j, chip in enumerate(chips)]
        for cp in first:
            cp.start()
        passed = [copy(4 + j, (*chip, c), sibling) for j, chip in enumerate(chips)]
        for j, chip in enumerate(chips):
            copy(1 + j, (*chip, c), me).wait_recv()
            passed[j].start()
        copy(0, sibling, me).wait_recv()
        for j, chip in enumerate(chips):
            copy(4 + j, (*chip, 1 - c), me).wait_recv()
        for cp in first + passed:
            cp.wait_send()
        mine.wait()
        acc = out_ref[0]
        for s in range(1, N_DEV):
            acc = acc + out_ref[s]
        sum_ref[...] = acc

    vm = pl.BlockSpec(memory_space=pltpu.VMEM)
    return _pcall(
        body, name=name, in_specs=[vm], out_specs=[vm, vm],
        out_shape=[jax.ShapeDtypeStruct((N_DEV, R, C), v.dtype), jax.ShapeDtypeStruct((R, C), v.dtype)],
        scratch_shapes=[pltpu.SemaphoreType.DMA((7,)), pltpu.SemaphoreType.DMA((7,)), pltpu.SemaphoreType.DMA],
    )(v)


def _rs_sibling(arrs, name):
    n = len(arrs)

    def body(*refs):
        ins, outs = refs[:n], refs[n:2 * n]
        send_sems, recv_sems = refs[2 * n:]
        x, y, c = lax.axis_index("x"), lax.axis_index("y"), lax.axis_index("c")
        cps = [pltpu.make_async_remote_copy(
            src_ref=ins[i].at[pl.ds(4 * (1 - c), 4)], dst_ref=outs[i],
            send_sem=send_sems.at[i], recv_sem=recv_sems.at[i],
            device_id=(x, y, 1 - c), device_id_type=MESH) for i in range(n)]
        for cp in cps:
            cp.start()
        for cp in cps:
            cp.wait()

    hbm = pl.BlockSpec(memory_space=pl.ANY)
    return _pcall(
        body, name=name, in_specs=[hbm] * n, out_specs=[hbm] * n,
        out_shape=[jax.ShapeDtypeStruct((4,) + a.shape[1:], a.dtype) for a in arrs],
        scratch_shapes=[pltpu.SemaphoreType.DMA((n,)), pltpu.SemaphoreType.DMA((n,))],
    )(*arrs)


def _rs_copies(src_refs, land_refs, send_sems, recv_sems):
    _, peers = _me_and_peers()
    cps = []
    for i in range(len(src_refs)):
        for j, (px, py, pc) in enumerate(peers[1:]):
            cps.append(pltpu.make_async_remote_copy(
                src_ref=src_refs[i].at[2 * px + py], dst_ref=land_refs[i].at[j],
                send_sem=send_sems.at[3 * i + j], recv_sem=recv_sems.at[3 * i + j],
                device_id=(px, py, pc), device_id_type=MESH))
    return cps


def _rs_start(chipsums, after, name):
    lands = [lax.empty((3,) + a.shape[1:], a.dtype) for a in chipsums]

    def start(src_refs, land_refs, send_sems, recv_sems):
        for cp in _rs_copies(src_refs, land_refs, send_sems, recv_sems):
            cp.start()

    return _split_start(chipsums, lands, after, start, (3 * len(chipsums),), name)


def _rs_wait(handle, after, name):
    def wait(src_refs, land_refs, send_sems, recv_sems):
        for cp in _rs_copies(src_refs, land_refs, send_sems, recv_sems):
            cp.wait_send()
            cp.wait_recv()

    return _split_wait(handle, after, wait, name)


def _row_tile(R, C, itemsize=4, budget=2 * 1024 * 1024):
    best = None
    for t in range(16, R + 1, 16):
        if R % t == 0 and t * C * itemsize <= budget:
            best = t
    return best if best is not None else R


def _add_half(arr, recv, c_idx, name):
    _, R, C = arr.shape
    tr = _row_tile(R, C)

    def body(c_ref, a_ref, r_ref, o_ref):
        o_ref[...] = (a_ref[...].astype(F32) + r_ref[...].astype(F32)).astype(o_ref.dtype)

    gs = pltpu.PrefetchScalarGridSpec(
        num_scalar_prefetch=1, grid=(4, R // tr),
        in_specs=[pl.BlockSpec((None, tr, C), lambda q, i, c_ref: (4 * c_ref[0] + q, i, 0)),
                  pl.BlockSpec((None, tr, C), lambda q, i, c_ref: (q, i, 0))],
        out_specs=pl.BlockSpec((None, tr, C), lambda q, i, c_ref: (q, i, 0)))
    return _pcall(body, name=name, grid_spec=gs, out_shape=jax.ShapeDtypeStruct((4, R, C), arr.dtype),
                  compiler_params=_params("parallel", "parallel"))(c_idx, arr, recv)


def _sum_chips(chipsum, recv, q_idx, name):
    _, R, C = chipsum.shape
    tr = _row_tile(R, C)

    def body(q_ref, a_ref, r_ref, o_ref):
        acc = a_ref[...].astype(F32)
        for j in range(3):
            acc = acc + r_ref[j].astype(F32)
        o_ref[...] = acc

    gs = pltpu.PrefetchScalarGridSpec(
        num_scalar_prefetch=1, grid=(R // tr,),
        in_specs=[pl.BlockSpec((None, tr, C), lambda i, q_ref: (q_ref[0], i, 0)),
                  pl.BlockSpec((3, tr, C), lambda i, q_ref: (0, i, 0))],
        out_specs=pl.BlockSpec((tr, C), lambda i, q_ref: (i, 0)))
    return _pcall(body, name=name, grid_spec=gs, out_shape=jax.ShapeDtypeStruct((R, C), F32),
                  compiler_params=_params("parallel"))(q_idx, chipsum, recv)


def _adamw(w, g, m, v, name):
    shape = w.shape
    C = shape[-1]
    R = int(np.prod(shape[:-1]))
    tr = _row_tile(R, C, budget=1024 * 1024)

    def body(w_ref, g_ref, m_ref, v_ref, d_ref, nm_ref, nv_ref):
        gv = g_ref[...]
        mv = ADAM_B1 * m_ref[...] + (1.0 - ADAM_B1) * gv
        vv = ADAM_B2 * v_ref[...] + (1.0 - ADAM_B2) * jnp.square(gv)
        m_hat = mv / (1.0 - ADAM_B1 ** ADAM_STEP)
        v_hat = vv / (1.0 - ADAM_B2 ** ADAM_STEP)
        d_ref[...] = -ADAM_LR * (m_hat / (jnp.sqrt(v_hat) + ADAM_EPS) + ADAM_WD * w_ref[...])
        nm_ref[...] = mv
        nv_ref[...] = vv

    blk = pl.BlockSpec((tr, C), lambda i: (i, 0))
    shp = jax.ShapeDtypeStruct((R, C), F32)
    outs = _pcall(body, name=name, grid=(R // tr,), in_specs=[blk] * 4, out_specs=[blk] * 3,
                  out_shape=[shp] * 3, compiler_params=_params("parallel"))(
        w.reshape(R, C), g.reshape(R, C), m.reshape(R, C), v.reshape(R, C))
    return tuple(o.reshape(shape) for o in outs)


def _pad_cols(w, width):
    return jnp.pad(w, ((0, 0), (0, width - w.shape[1])))


def _pad_rows(w, height):
    return jnp.pad(w, ((0, height - w.shape[0]), (0, 0)))


def _slot_to_device_order(a):
    s = a.shape
    return a.reshape((2, 4) + s[1:]).swapaxes(0, 1).reshape(s)


def _device_to_slot_order(a):
    s = a.shape
    return a.reshape((4, 2) + s[1:]).swapaxes(0, 1).reshape(s)


def kernel(x, ffn1_w_gate, ffn1_w_up, ffn1_w_down, ffn2_w_gate, ffn2_w_up, ffn2_w_down, ln_gain, ln_bias, pool_w_in, pool_w_group, pool_scale, pool_w_out, attn_w_qkv, attn_w_out, loss_target, m_ffn1_w_gate, m_ffn1_w_up, m_ffn1_w_down, m_ffn2_w_gate, m_ffn2_w_up, m_ffn2_w_down, m_ln_gain, m_ln_bias, m_pool_w_in, m_pool_w_group, m_pool_scale, m_pool_w_out, m_attn_w_qkv, m_attn_w_out, v_ffn1_w_gate, v_ffn1_w_up, v_ffn1_w_down, v_ffn2_w_gate, v_ffn2_w_up, v_ffn2_w_down, v_ln_gain, v_ln_bias, v_pool_w_in, v_pool_w_group, v_pool_scale, v_pool_w_out, v_attn_w_qkv, v_attn_w_out):
    T = x.shape[1]
    fs = ffn1_w_gate.shape[2]
    fp = _round_up(fs, LANE)
    rs = D // N_DEV
    x0 = x[0]
    tgt = loss_target[0]
    slopes = _alibi_slopes()
    c_idx = lax.axis_index("c").astype(jnp.int32).reshape(1)
    q_idx = (2 * lax.axis_index("x") + lax.axis_index("y")).astype(jnp.int32).reshape(1)

    gates = (ffn1_w_gate, ffn2_w_gate)
    ups = (ffn1_w_up, ffn2_w_up)
    downs = (ffn1_w_down, ffn2_w_down)
    ffns = [(i, k) for i in range(DEPTH) for k in range(2)]
    wgu_sh = [jnp.concatenate([_pad_cols(gates[k][i], fp), _pad_cols(ups[k][i], fp)], axis=1).astype(BF16)
              for i, k in ffns]
    wd_sh = [_pad_rows(downs[k][i], fp).astype(BF16) for i, k in ffns]
    sq_sh = jnp.concatenate([wd_sh[0], pool_w_in[0].astype(BF16), pool_w_out[0].astype(BF16),
                             pool_w_group[0].reshape(rs // 4, D).astype(BF16)], axis=0)
    qkv_sh = attn_w_qkv[0].astype(BF16)
    aout_sh = attn_w_out[0].astype(BF16)
    ln_sh = jnp.concatenate([ln_gain.reshape(DEPTH * 3, rs), ln_bias.reshape(DEPTH * 3, rs),
                             jnp.zeros((4, rs), F32)], axis=0)

    h0 = _ag_start([wgu_sh[0], ln_sh], (), "ag0")
    x0b, x0t = _transpose_cast(x0, "x_cast", deps=(h0[4],))
    wgu0, ln_all = _ag_finish(h0, (x0b,), "ag0")
    h1 = _ag_start([sq_sh], (wgu0,), "ag1")
    ln_all = _slot_to_device_order(ln_all).transpose(1, 0, 2).reshape(16, D)
    gain = lambda i, s: ln_all[3 * i + s][None]
    bias = lambda i, s: ln_all[DEPTH * 3 + 3 * i + s][None]

    def ffn_fwd(xf, xb, wgu, wd, f, i, s, dep_up=(), dep_down=()):
        g, u, act = _ffn_up(xb, wgu, fp, f"ffn_up{f}", deps=dep_up)
        wd = wd(act) if callable(wd) else wd
        y, yb, yt, xh, rstd = _mm_ln(act, wd, xf, gain(i, s), bias(i, s), MACARON, f"ffn_down_ln{f}",
                                     deps=dep_down() if callable(dep_down) else dep_down)
        return (y, yb, yt), dict(g=g, u=u, act=act, xh=xh, rstd=rstd, wgu=wgu, wd=wd)

    pool_w = {}

    def wd0_after(act):
        (sq_all,) = _ag_finish(h1, (act,), "ag1")
        pool_w["h2"] = _ag_start([wgu_sh[1], wd_sh[1]], (sq_all,), "ag2")
        pool_w["pin"] = _slot_to_device_order(sq_all[:, fp:fp + rs, :]).reshape(D, D)
        pool_w["pout"] = _slot_to_device_order(sq_all[:, fp + rs:fp + 2 * rs, :]).reshape(D, D)
        grp = _slot_to_device_order(sq_all[:, fp + 2 * rs:, :])
        pool_w["grp"] = grp.reshape(N_DEV, N_POOL_GROUPS, rs // 4, POOL_GROUP_DIM).transpose(1, 0, 2, 3).reshape(
            N_POOL_GROUPS, POOL_GROUP_DIM, POOL_GROUP_DIM)
        return sq_all[:, :fp, :].reshape(N_DEV * fp, D)

    (a1, a1b, a1t), s_f0 = ffn_fwd(x0, x0b, wgu0, wd0_after, 0, 0, 0, dep_up=(h1[4],),
                                   dep_down=lambda: (pool_w["h2"][4],))
    h2 = pool_w["h2"]
    w_pin, w_pout, w_grp = pool_w["pin"], pool_w["pout"], pool_w["grp"]
    tm = min(512, T)
    row_spec = pl.BlockSpec((tm, D), lambda i, j, k: (i, 0))
    full_w = pl.BlockSpec((D, D), lambda i, j, k: (0, 0))
    u_pool = _mm(a1b, w_pin, grid=(T // tm, 1, 1), a_spec=row_spec, b_spec=full_w,
                 out_shape=jax.ShapeDtypeStruct((T, D), F32), out_spec=row_spec, name="pool_in")
    mixedb, mixedt = _pool_window(u_pool, False, "pool_window")
    y_pool, ysb, yst = _pool_group(mixedb, w_grp, pool_scale)
    a2, a2b, a2t, xh_p, rstd_p = _mm_ln(ysb, w_pout, a1, gain(0, 1), bias(0, 1), 1.0, "pool_out_ln")
    wgu1, wd1 = _ag_finish(h2, (a2,), "ag2")
    h3 = _ag_start([wgu_sh[2], wd_sh[2]], (wgu1,), "ag3")
    (a3, a3b, a3t), s_f1 = ffn_fwd(a2, a2b, wgu1, wd1.reshape(N_DEV * fp, D), 1, 0, 2, dep_up=(h3[4],))
    wgu2, wd2 = _ag_finish(h3, (a3,), "ag3")
    h4 = _ag_start([qkv_sh, aout_sh], (wgu2,), "ag4")
    (b1, b1b, b1t), s_f2 = ffn_fwd(a3, a3b, wgu2, wd2.reshape(N_DEV * fp, D), 2, 1, 0, dep_up=(h4[4],))
    wqkv_all, aout_all = _ag_finish(h4, (b1,), "ag4")
    h5 = _ag_start([wgu_sh[3], wd_sh[3]], (wqkv_all,), "ag5")
    w_aout = _slot_to_device_order(aout_all).reshape(D, D)
    dils = [d for _, d in DIL_CONFIGS]
    wqkv_nat = _slot_to_device_order(wqkv_all).transpose(1, 0, 2).reshape(D, N_DEV * QKV_SHARD)
    qkv_gs, o_gs, lse_gs = [], [], []
    for gi, d in enumerate(dils):
        qkv_g = _attn_qkv_group(b1b, wqkv_nat, gi, d, f"attn_qkv{gi}", deps=(h5[4],) if gi == 0 else ())
        o_g, lse_g = _attn_fwd(qkv_g, d, slopes[gi], f"attn_fwd{gi}")
        qkv_gs.append(qkv_g)
        o_gs.append(o_g)
        lse_gs.append(lse_g)
    o32, ob, ot, lse_tot = _attn_combine(o_gs, lse_gs, dils)
    b2, b2b, b2t, xh_a, rstd_a = _mm_ln(ob, w_aout, b1, gain(1, 1), bias(1, 1), 1.0, "attn_out_ln")
    wgu3, wd3 = _ag_finish(h5, (b2,), "ag5")
    (b3, _, _), s_f3 = ffn_fwd(b2, b2b, wgu3, wd3.reshape(N_DEV * fp, D), 3, 1, 2)

    dy, loss_tile = _loss_head(b3, tgt)
    loss = lax.psum(loss_tile[0, 0], AXES)

    bm = min(512, D)
    dgains, dbiases = {}, {}
    rs_pending = []
    gsums = {}

    def rs_finish(after):
        h, tag = rs_pending.pop()
        chips, lands = _rs_wait(h, after, f"rs_{tag}_wait")
        gsums[tag] = [_sum_chips(a, r, q_idx, f"rs_sum_{tag}{i}") for i, (a, r) in enumerate(zip(chips, lands))]

    def rs_stage(bufs, tag):
        if rs_pending:
            rs_finish((bufs[-1],))
        recv = _rs_sibling(bufs, f"rs_sib_{tag}")
        chips = [_add_half(a, r, c_idx, f"rs_add_{tag}{i}") for i, (a, r) in enumerate(zip(bufs, recv))]
        h = _rs_start(chips, (), f"rs_{tag}")
        rs_pending.append((h, tag))
        return h[4]

    def ffn_bwd(dys, f, i, s, st, xt):
        dxres, dhb, dht, dg, db = _ln_bwd(dys, st["xh"], st["rstd"], gain(i, s), MACARON, f"ffn_ln_bwd{f}")
        dgains[(i, s)], dbiases[(i, s)] = dg, db
        dgu = _ffn_bwd_act(dhb, st["wd"], st["g"], st["u"], fp, f"ffn_bwd_act{f}")
        g_dt = _mm(dht, st["act"], grid=(D // bm, N_DEV, 1),
                   a_spec=pl.BlockSpec((bm, T), lambda r, j, k: (r, 0)),
                   b_spec=pl.BlockSpec((T, fp), lambda r, j, k: (0, j)),
                   out_shape=jax.ShapeDtypeStruct((N_DEV, D, fp), BF16),
                   out_spec=pl.BlockSpec((None, bm, fp), lambda r, j, k: (j, r, 0)), name=f"ffn_dwd{f}")
        g_gu = _mm(xt, dgu, grid=(D // bm, N_DEV, 1),
                   a_spec=pl.BlockSpec((bm, T), lambda r, j, k: (r, 0)),
                   b_spec=pl.BlockSpec((T, 2 * fp), lambda r, j, k: (0, j)),
                   out_shape=jax.ShapeDtypeStruct((N_DEV, D, 2 * fp), BF16),
                   out_spec=pl.BlockSpec((None, bm, 2 * fp), lambda r, j, k: (j, r, 0)), name=f"ffn_dwgu{f}")
        token = rs_stage([g_dt, g_gu], f"f{f}")
        tmx = min(1024, T)
        dx = _mm(dgu, st["wgu"], grid=(T // tmx, 1, N_DEV), nt=True,
                 a_spec=pl.BlockSpec((tmx, 2 * fp), lambda t, j, k: (t, k)),
                 b_spec=pl.BlockSpec((None, D, 2 * fp), lambda t, j, k: (k, 0, 0)),
                 out_shape=jax.ShapeDtypeStruct((T, D), F32),
                 out_spec=pl.BlockSpec((tmx, D), lambda t, j, k: (t, 0)), name=f"ffn_dx{f}", deps=(token,))
        return [dxres, dx]

    def dw_square(at, bmat, name):
        return _mm(at, bmat, grid=(D // bm, 1, 1),
                   a_spec=pl.BlockSpec((bm, T), lambda r, j, k: (r, 0)),
                   b_spec=pl.BlockSpec((T, D), lambda r, j, k: (0, 0)),
                   out_shape=jax.ShapeDtypeStruct((D, D), BF16),
                   out_spec=pl.BlockSpec((bm, D), lambda r, j, k: (r, 0)), name=name)

    def dx_square(a, w, name, deps=()):
        return _mm(a, w, grid=(T // tm, 1, 1), nt=True, a_spec=row_spec, b_spec=full_w,
                   out_shape=jax.ShapeDtypeStruct((T, D), F32), out_spec=row_spec, name=name, deps=deps)

    to_slots = lambda g2d: _device_to_slot_order(g2d.reshape(N_DEV, rs, D))

    d_b2 = ffn_bwd([dy], 3, 1, 2, s_f3, b2t)
    dxres, dmb, dmt, dg, db = _ln_bwd(d_b2, xh_a, rstd_a, gain(1, 1), 1.0, "attn_ln_bwd")
    dgains[(1, 1)], dbiases[(1, 1)] = dg, db
    g_aout = dw_square(ot, dmb, "attn_dwout")
    dobs, statss = _attn_bwd_prep(dmb, w_aout, o32, lse_tot, dils)
    ntile = 3 * D // QKV_TILE
    g_qkv = None
    dqkv_gs = []
    for gi, d in enumerate(dils):
        dqkv_g = _attn_bwd(qkv_gs[gi], dobs[gi], statss[gi], d, slopes[gi], f"attn_bwd{gi}")
        dqkv_gs.append(dqkv_g)
        xt = b1t if d == 1 else _transpose_sub(b1, d, f"attn_xt{gi}")
        g_qkv = _attn_dw_group(xt, dqkv_g, gi, d, g_qkv, f"attn_dwqkv{gi}")
    token = rs_stage([to_slots(g_aout), g_qkv], "attn")
    dx_attn = [_attn_dx_group(dqkv_gs[gi], wqkv_nat, gi, d, f"attn_dx{gi}", deps=(token,) if gi == 0 else ())
               for gi, d in enumerate(dils)]
    d_a3 = ffn_bwd([dxres] + dx_attn, 2, 1, 0, s_f2, a3t)
    d_a2 = ffn_bwd(d_a3, 1, 0, 2, s_f1, a2t)
    dxres, dmb, dmt, dg, db = _ln_bwd(d_a2, xh_p, rstd_p, gain(0, 1), 1.0, "pool_ln_bwd")
    dgains[(0, 1)], dbiases[(0, 1)] = dg, db
    g_pout = dw_square(yst, dmb, "pool_dwout")
    dyb, dscale = _pool_bwd_out(dmb, w_pout, y_pool, pool_scale)
    gd = POOL_GROUP_DIM
    g_grp = _mm(mixedt, dyb, grid=(N_POOL_GROUPS, 1, 1),
                a_spec=pl.BlockSpec((gd, T), lambda g, j, k: (g, 0)),
                b_spec=pl.BlockSpec((T, gd), lambda g, j, k: (0, g)),
                out_shape=jax.ShapeDtypeStruct((N_POOL_GROUPS, gd, gd), BF16),
                out_spec=pl.BlockSpec((None, gd, gd), lambda g, j, k: (g, 0, 0)), name="pool_dwgroup")
    tg = min(1024, T)
    dmixed = _mm(dyb, w_grp, grid=(N_POOL_GROUPS, T // tg, 1), nt=True,
                 a_spec=pl.BlockSpec((tg, gd), lambda g, t, k: (t, g)),
                 b_spec=pl.BlockSpec((None, gd, gd), lambda g, t, k: (g, 0, 0)),
                 out_shape=jax.ShapeDtypeStruct((T, D), F32),
                 out_spec=pl.BlockSpec((tg, gd), lambda g, t, k: (t, g)), name="pool_dmixed")
    dub, _ = _pool_window(dmixed, True, "pool_window_bwd")
    g_pin = dw_square(a1t, dub, "pool_dwin")
    g_grp_slots = _device_to_slot_order(
        g_grp.reshape(N_POOL_GROUPS, N_DEV, rs // 4, gd).transpose(1, 0, 2, 3).reshape(N_DEV, rs // 4, D))
    token = rs_stage([to_slots(g_pout), g_grp_slots, to_slots(g_pin)], "pool")
    dx_pool = dx_square(dub, w_pin, "pool_dx", deps=(token,))
    d_x0 = ffn_bwd([dxres, dx_pool], 0, 0, 0, s_f0, x0t)
    grad_x = _add2(d_x0[0], d_x0[1], "grad_x_add")
    rs_finish((grad_x,))
    grad_x = grad_x[None]
    gw_dt = [gsums[f"f{f}"][0] for f in range(4)]
    gw_gu = [gsums[f"f{f}"][1] for f in range(4)]
    gw_aout, gw_qkv = gsums["attn"]
    gw_pout, gw_grp, gw_pin = gsums["pool"]

    small = jnp.concatenate([dgains[(i, s)] for i in range(DEPTH) for s in range(3)]
                            + [dbiases[(i, s)] for i in range(DEPTH) for s in range(3)]
                            + [dscale, jnp.zeros((3, D), F32)], axis=0)
    _, small_sum = _small_all_gather(small, "ag_small_grads")
    dev = 4 * lax.axis_index("x") + 2 * lax.axis_index("y") + lax.axis_index("c")
    mine = lax.dynamic_slice_in_dim(small_sum, dev * rs, rs, axis=1)
    grads = {
        "ffn1_w_gate": jnp.stack([gw_gu[2 * i][:, :fs] for i in range(DEPTH)]),
        "ffn1_w_up": jnp.stack([gw_gu[2 * i][:, fp:fp + fs] for i in range(DEPTH)]),
        "ffn1_w_down": jnp.stack([gw_dt[2 * i].T[:fs] for i in range(DEPTH)]),
        "ffn2_w_gate": jnp.stack([gw_gu[2 * i + 1][:, :fs] for i in range(DEPTH)]),
        "ffn2_w_up": jnp.stack([gw_gu[2 * i + 1][:, fp:fp + fs] for i in range(DEPTH)]),
        "ffn2_w_down": jnp.stack([gw_dt[2 * i + 1].T[:fs] for i in range(DEPTH)]),
        "ln_gain": mine[0:DEPTH * 3].reshape(DEPTH, 3, rs),
        "ln_bias": mine[DEPTH * 3:2 * DEPTH * 3].reshape(DEPTH, 3, rs),
        "pool_w_in": gw_pin[None],
        "pool_w_group": gw_grp[:rs // 4].reshape(N_POOL_GROUPS, rs // 4, gd)[None],
        "pool_scale": small_sum[2 * DEPTH * 3][None],
        "pool_w_out": gw_pout[None],
        "attn_w_qkv": gw_qkv[None],
        "attn_w_out": gw_aout[None],
    }
    weights = dict(ffn1_w_gate=ffn1_w_gate, ffn1_w_up=ffn1_w_up, ffn1_w_down=ffn1_w_down,
                   ffn2_w_gate=ffn2_w_gate, ffn2_w_up=ffn2_w_up, ffn2_w_down=ffn2_w_down,
                   ln_gain=ln_gain, ln_bias=ln_bias, pool_w_in=pool_w_in, pool_w_group=pool_w_group,
                   pool_scale=pool_scale, pool_w_out=pool_w_out, attn_w_qkv=attn_w_qkv, attn_w_out=attn_w_out)
    ms = dict(ffn1_w_gate=m_ffn1_w_gate, ffn1_w_up=m_ffn1_w_up, ffn1_w_down=m_ffn1_w_down,
              ffn2_w_gate=m_ffn2_w_gate, ffn2_w_up=m_ffn2_w_up, ffn2_w_down=m_ffn2_w_down,
              ln_gain=m_ln_gain, ln_bias=m_ln_bias, pool_w_in=m_pool_w_in, pool_w_group=m_pool_w_group,
              pool_scale=m_pool_scale, pool_w_out=m_pool_w_out, attn_w_qkv=m_attn_w_qkv, attn_w_out=m_attn_w_out)
    vs = dict(ffn1_w_gate=v_ffn1_w_gate, ffn1_w_up=v_ffn1_w_up, ffn1_w_down=v_ffn1_w_down,
              ffn2_w_gate=v_ffn2_w_gate, ffn2_w_up=v_ffn2_w_up, ffn2_w_down=v_ffn2_w_down,
              ln_gain=v_ln_gain, ln_bias=v_ln_bias, pool_w_in=v_pool_w_in, pool_w_group=v_pool_w_group,
              pool_scale=v_pool_scale, pool_w_out=v_pool_w_out, attn_w_qkv=v_attn_w_qkv, attn_w_out=v_attn_w_out)
    names = list(weights)
    deltas, new_m, new_v = {}, {}, {}
    for nme in names:
        deltas[nme], new_m[nme], new_v[nme] = _adamw(weights[nme], grads[nme], ms[nme], vs[nme], f"adamw_{nme}")
    return (loss, grad_x, *[grads[k] for k in names], *[deltas[k] for k in names],
            *[new_m[k] for k in names], *[new_v[k] for k in names])
```

```python
import functools

import numpy as np
import jax
import jax.numpy as jnp
from jax import lax
from jax.experimental import pallas as pl
from jax.experimental.pallas import tpu as pltpu

F32 = jnp.float32
BF16 = jnp.bfloat16

D = 1024
N_DEV = 8
N_HEADS = 16
HEAD_DIM = 64
N_POOL_GROUPS = 4
POOL_GROUP_DIM = 256
POOL_HALF = (1, 2, 4, 8)
DIL_CONFIGS = ((128, 1), (512, 4), (2048, 16))
ATTN_HALO = 64
ATTN_BLOCK = 128
QKV_SHARD = 3 * 3 * D // N_DEV
DEPTH = 2
ALPHA = (2.0 * DEPTH) ** 0.25
MACARON = 0.5
LN_EPS = 1e-5
MASK_VALUE = -1e30
ADAM_LR = 0.001
ADAM_B1 = 0.9
ADAM_B2 = 0.999
ADAM_EPS = 1e-08
ADAM_WD = 0.01
ADAM_STEP = 10
LANE = 128
VMEM_LIMIT = 56 * 1024 * 1024
MESH = pl.DeviceIdType.MESH
AXES = ("x", "y", "c")


def _round_up(n, m):
    return (n + m - 1) // m * m


def _pcall(body, deps=(), **kw):
    if not deps:
        return pl.pallas_call(body, **kw)
    n_in, n_dep = len(kw["in_specs"]), len(deps)

    def wrapped(*refs):
        return body(*refs[:n_in], *refs[n_in + n_dep:])

    kw["in_specs"] = list(kw["in_specs"]) + [pl.BlockSpec(memory_space=pl.ANY)] * n_dep
    call = pl.pallas_call(wrapped, **kw)
    return lambda *args: call(*args, *deps)


def _params(*sem):
    return pltpu.CompilerParams(dimension_semantics=sem, vmem_limit_bytes=VMEM_LIMIT)


def _alibi_slopes():
    n = len(DIL_CONFIGS) * N_HEADS
    s = 2.0 ** (-8.0 * np.arange(1, n + 1) / n)
    return s.reshape(len(DIL_CONFIGS), N_HEADS).astype(np.float32)


def _my_slot():
    return 4 * lax.axis_index("c") + 2 * lax.axis_index("x") + lax.axis_index("y")


def _mm(a, b, *, grid, a_spec, b_spec, out_shape, out_spec, nt=False, name, alias=None, deps=()):
    nk = grid[2]
    dn = (((1,), (1,)), ((), ())) if nt else (((1,), (0,)), ((), ()))
    blk = tuple(s for s in out_spec.block_shape if s is not None)

    def body(*refs):
        a_ref, b_ref = refs[0], refs[1]
        o_ref = refs[3] if alias is not None else refs[2]
        p = lax.dot_general(a_ref[...], b_ref[...], dn, preferred_element_type=F32)
        if nk == 1:
            o_ref[...] = p.astype(o_ref.dtype)
        else:
            acc = refs[-1]
            k = pl.program_id(2)

            @pl.when(k == 0)
            def _():
                acc[...] = p

            @pl.when(k > 0)
            def _():
                acc[...] += p

            @pl.when(k == nk - 1)
            def _():
                o_ref[...] = acc[...].astype(o_ref.dtype)

    in_specs = [a_spec, b_spec]
    args = [a, b]
    aliases = {}
    if alias is not None:
        in_specs.append(pl.BlockSpec(memory_space=pl.ANY))
        args.append(alias)
        aliases = {2: 0}
    return _pcall(
        body, deps=deps, name=name, grid=grid, in_specs=in_specs, out_specs=out_spec, out_shape=out_shape,
        scratch_shapes=[] if nk == 1 else [pltpu.VMEM(blk, F32)],
        input_output_aliases=aliases,
        compiler_params=_params("parallel", "parallel", "arbitrary"),
    )(*args)


def _transpose_cast(x, name, deps=()):
    T = x.shape[0]
    tm = min(512, T)

    def body(x_ref, xb_ref, xt_ref):
        v = x_ref[...]
        xb_ref[...] = v.astype(BF16)
        xt_ref[...] = v.T.astype(BF16)

    return _pcall(
        body, deps=deps, name=name, grid=(T // tm,),
        in_specs=[pl.BlockSpec((tm, D), lambda t: (t, 0))],
        out_specs=[pl.BlockSpec((tm, D), lambda t: (t, 0)), pl.BlockSpec((D, tm), lambda t: (0, t))],
        out_shape=[jax.ShapeDtypeStruct((T, D), BF16), jax.ShapeDtypeStruct((D, T), BF16)],
        compiler_params=_params("parallel"),
    )(x)


def _mm_ln(a, b, xres, gain, bias, hscale, name, deps=()):
    T, K = a.shape
    tm = min(512, T)

    def body(a_ref, b_ref, x_ref, g_ref, bt_ref, y_ref, yb_ref, yt_ref, xh_ref, rs_ref):
        h = jnp.dot(a_ref[...], b_ref[...], preferred_element_type=F32)
        z = ALPHA * x_ref[...] + hscale * h
        mu = jnp.mean(z, axis=-1, keepdims=True)
        zc = z - mu
        var = jnp.mean(zc * zc, axis=-1, keepdims=True)
        rstd = lax.rsqrt(var + LN_EPS)
        xh = zc * rstd
        y = xh * g_ref[...] + bt_ref[...]
        y_ref[...] = y
        yb_ref[...] = y.astype(BF16)
        yt_ref[...] = y.T.astype(BF16)
        xh_ref[...] = xh
        rs_ref[...] = rstd

    row = pl.BlockSpec((tm, D), lambda t: (t, 0))
    vec = pl.BlockSpec((1, D), lambda t: (0, 0))
    return _pcall(
        body, deps=deps, name=name, grid=(T // tm,),
        in_specs=[pl.BlockSpec((tm, K), lambda t: (t, 0)), pl.BlockSpec((K, D), lambda t: (0, 0)), row, vec, vec],
        out_specs=[row, row, pl.BlockSpec((D, tm), lambda t: (0, t)), row, pl.BlockSpec((tm, 1), lambda t: (t, 0))],
        out_shape=[jax.ShapeDtypeStruct((T, D), F32), jax.ShapeDtypeStruct((T, D), BF16),
                   jax.ShapeDtypeStruct((D, T), BF16), jax.ShapeDtypeStruct((T, D), F32),
                   jax.ShapeDtypeStruct((T, 1), F32)],
        compiler_params=_params("parallel"),
    )(a, b, xres, gain, bias)


def _ln_bwd(dys, xhat, rstd, gain, hscale, name):
    T = xhat.shape[0]
    tm = min(512, T)
    n = len(dys)

    def body(*refs):
        dy_refs = refs[:n]
        xh_ref, rs_ref, g_ref, dx_ref, dh_ref, dht_ref, dg_ref, db_ref = refs[n:]
        dy = dy_refs[0][...]
        for r in dy_refs[1:]:
            dy = dy + r[...]
        xh = xh_ref[...]
        dxh = dy * g_ref[...]
        m1 = jnp.mean(dxh, axis=-1, keepdims=True)
        m2 = jnp.mean(dxh * xh, axis=-1, keepdims=True)
        dz = rs_ref[...] * (dxh - m1 - xh * m2)
        dx_ref[...] = ALPHA * dz
        dh = hscale * dz
        dh_ref[...] = dh.astype(BF16)
        dht_ref[...] = dh.T.astype(BF16)
        dg = jnp.sum(dy * xh, axis=0, keepdims=True)
        db = jnp.sum(dy, axis=0, keepdims=True)

        @pl.when(pl.program_id(0) == 0)
        def _():
            dg_ref[...] = dg
            db_ref[...] = db

        @pl.when(pl.program_id(0) > 0)
        def _():
            dg_ref[...] += dg
            db_ref[...] += db

    row = pl.BlockSpec((tm, D), lambda t: (t, 0))
    vec = pl.BlockSpec((1, D), lambda t: (0, 0))
    return _pcall(
        body, name=name, grid=(T // tm,),
        in_specs=[row] * n + [row, pl.BlockSpec((tm, 1), lambda t: (t, 0)), vec],
        out_specs=[row, row, pl.BlockSpec((D, tm), lambda t: (0, t)), vec, vec],
        out_shape=[jax.ShapeDtypeStruct((T, D), F32), jax.ShapeDtypeStruct((T, D), BF16),
                   jax.ShapeDtypeStruct((D, T), BF16), jax.ShapeDtypeStruct((1, D), F32),
                   jax.ShapeDtypeStruct((1, D), F32)],
        compiler_params=_params("arbitrary"),
    )(*dys, xhat, rstd, gain)


def _add2(a, b, name):
    T = a.shape[0]
    tm = min(512, T)

    def body(a_ref, b_ref, o_ref):
        o_ref[...] = a_ref[...] + b_ref[...]

    row = pl.BlockSpec((tm, D), lambda t: (t, 0))
    return _pcall(body, name=name, grid=(T // tm,), in_specs=[row, row], out_specs=row,
                  out_shape=jax.ShapeDtypeStruct((T, D), F32), compiler_params=_params("parallel"))(a, b)


def _loss_head(y, tgt):
    T = y.shape[0]
    tm = min(512, T)

    def body(y_ref, t_ref, dy_ref, l_ref):
        e = y_ref[...] - t_ref[...]
        dy_ref[...] = e * (1.0 / D)
        part = jnp.sum(jnp.sum(e * e, axis=1, keepdims=True), axis=0, keepdims=True) * (0.5 / D)

        @pl.when(pl.program_id(0) == 0)
        def _():
            l_ref[...] = jnp.zeros_like(l_ref)

        l_ref[...] += part

    row = pl.BlockSpec((tm, D), lambda t: (t, 0))
    return _pcall(
        body, name="loss_head", grid=(T // tm,),
        in_specs=[row, row],
        out_specs=[row, pl.BlockSpec((8, LANE), lambda t: (0, 0))],
        out_shape=[jax.ShapeDtypeStruct((T, D), F32), jax.ShapeDtypeStruct((8, LANE), F32)],
        compiler_params=_params("arbitrary"),
    )(y, tgt)


def _sigmoid(v):
    return 0.5 * jnp.tanh(0.5 * v) + 0.5


def _ffn_up(xb, wgu, fp, name, deps=()):
    T = xb.shape[0]
    tm = min(1024, T)

    def body(x_ref, w_ref, g_ref, u_ref, a_ref):
        p = jnp.dot(x_ref[...], w_ref[...], preferred_element_type=F32)
        g = p[:, :fp]
        u = p[:, fp:]
        a = g * _sigmoid(g) * u
        g_ref[...] = g.astype(BF16)
        u_ref[...] = u.astype(BF16)
        a_ref[...] = a.astype(BF16)

    out = pl.BlockSpec((tm, fp), lambda t, j: (t, j))
    shp = jax.ShapeDtypeStruct((T, N_DEV * fp), BF16)
    return _pcall(
        body, deps=deps, name=name, grid=(T // tm, N_DEV),
        in_specs=[pl.BlockSpec((tm, D), lambda t, j: (t, 0)),
                  pl.BlockSpec((None, D, 2 * fp), lambda t, j: (j, 0, 0))],
        out_specs=[out, out, out], out_shape=[shp, shp, shp],
        compiler_params=_params("parallel", "parallel"),
    )(xb, wgu)


def _ffn_bwd_act(dhb, wd, g, u, fp, name):
    T = dhb.shape[0]
    tm = min(1024, T)

    def body(dh_ref, w_ref, g_ref, u_ref, o_ref):
        da = lax.dot_general(dh_ref[...], w_ref[...], (((1,), (1,)), ((), ())), preferred_element_type=F32)
        gv = g_ref[...].astype(F32)
        uv = u_ref[...].astype(F32)
        sig = _sigmoid(gv)
        t = da * sig
        dup = (t * gv).astype(BF16)
        dgate = (t * uv * (1.0 + gv - gv * sig)).astype(BF16)
        for s in range(2):
            o_ref[:, 2 * s * fp:(2 * s + 1) * fp] = dgate[:, s * fp:(s + 1) * fp]
            o_ref[:, (2 * s + 1) * fp:(2 * s + 2) * fp] = dup[:, s * fp:(s + 1) * fp]

    blk = pl.BlockSpec((tm, 2 * fp), lambda t, j: (t, j))
    return _pcall(
        body, name=name, grid=(T // tm, N_DEV // 2),
        in_specs=[pl.BlockSpec((tm, D), lambda t, j: (t, 0)), pl.BlockSpec((2 * fp, D), lambda t, j: (j, 0)), blk, blk],
        out_specs=pl.BlockSpec((tm, 4 * fp), lambda t, j: (t, j)),
        out_shape=jax.ShapeDtypeStruct((T, N_DEV * 2 * fp), BF16),
        compiler_params=_params("parallel", "parallel"),
    )(dhb, wd, g, u)


def _ffn_dwd(dht, act, fp, name):
    T = dht.shape[1]
    bm = 512

    def body(a_ref, b_ref, o_ref):
        p = jnp.dot(a_ref[...], b_ref[...], preferred_element_type=F32)
        o_ref[0] = p[:, :fp].astype(BF16)
        o_ref[1] = p[:, fp:].astype(BF16)

    return _pcall(
        body, name=name, grid=(D // bm, N_DEV // 2),
        in_specs=[pl.BlockSpec((bm, T), lambda i, j: (i, 0)), pl.BlockSpec((T, 2 * fp), lambda i, j: (0, j))],
        out_specs=pl.BlockSpec((2, bm, fp), lambda i, j: (j, i, 0)),
        out_shape=jax.ShapeDtypeStruct((N_DEV, D, fp), BF16),
        compiler_params=_params("parallel", "parallel"),
    )(dht, act)


def _ffn_dx(dgu, wgu, fp, name, deps=()):
    T = dgu.shape[0]
    tm = min(512, T)

    def body(a_ref, w_ref, o_ref):
        acc = None
        for j in range(N_DEV):
            p = lax.dot_general(a_ref[:, j * 2 * fp:(j + 1) * 2 * fp], w_ref[j], (((1,), (1,)), ((), ())),
                                preferred_element_type=F32)
            acc = p if acc is None else acc + p
        o_ref[...] = acc

    return _pcall(
        body, deps=deps, name=name, grid=(T // tm,),
        in_specs=[pl.BlockSpec((tm, N_DEV * 2 * fp), lambda t: (t, 0)),
                  pl.BlockSpec((N_DEV, D, 2 * fp), lambda t: (0, 0, 0))],
        out_specs=pl.BlockSpec((tm, D), lambda t: (t, 0)),
        out_shape=jax.ShapeDtypeStruct((T, D), F32),
        compiler_params=_params("parallel"),
    )(dgu, wgu)


def _slots_to_columns(w, name):
    _, R, C = w.shape
    tr = min(512, R)

    def body(i_ref, o_ref):
        o_ref[...] = i_ref[...]

    return _pcall(
        body, name=name, grid=(N_DEV, R // tr),
        in_specs=[pl.BlockSpec((None, tr, C), lambda j, i: (4 * (j % 2) + j // 2, i, 0))],
        out_specs=pl.BlockSpec((tr, C), lambda j, i: (i, j)),
        out_shape=jax.ShapeDtypeStruct((R, N_DEV * C), w.dtype),
        compiler_params=_params("parallel", "parallel"),
    )(w)


POOL_PAD = 16
POOL_CHUNK = 512


def _pool_window(v, transpose, name):
    T = v.shape[0]
    ch = min(POOL_CHUNK, T)
    ext = ch + 2 * POOL_PAD
    gd = POOL_GROUP_DIM

    def body(v_ref, o_ref, ot_ref, pad_ref):
        pad_ref[0:POOL_PAD, :] = jnp.zeros((POOL_PAD, gd), F32)
        pad_ref[POOL_PAD + T:POOL_PAD + T + POOL_PAD, :] = jnp.zeros((POOL_PAD, gd), F32)
        for gi, hw in enumerate(POOL_HALF):
            @pl.when(pl.program_id(0) == gi)
            def _(hw=hw):
                def count(t):
                    return (jnp.minimum(t + hw, T) - jnp.maximum(t - hw, 0)).astype(F32)

                if transpose:
                    t_all = lax.broadcasted_iota(jnp.int32, (T, gd), 0)
                    pad_ref[POOL_PAD:POOL_PAD + T, :] = v_ref[...] / count(t_all)
                else:
                    pad_ref[POOL_PAD:POOL_PAD + T, :] = v_ref[...]
                shift = hw if transpose else hw - 1
                for c in range(T // ch):
                    e = pad_ref[c * ch:c * ch + ext, :]
                    step = 1
                    while step < 2 * hw:
                        e = e + pltpu.roll(e, step, 0)
                        step *= 2
                    if shift:
                        e = pltpu.roll(e, ext - shift, 0)
                    s = e[POOL_PAD:POOL_PAD + ch, :]
                    center = v_ref[c * ch:(c + 1) * ch, :]
                    if transpose:
                        res = s - center
                    else:
                        t_idx = c * ch + lax.broadcasted_iota(jnp.int32, (ch, gd), 0)
                        res = s / count(t_idx) - center
                    o_ref[c * ch:(c + 1) * ch, :] = res.astype(BF16)
                    ot_ref[:, c * ch:(c + 1) * ch] = res.T.astype(BF16)

    return _pcall(
        body, name=name, grid=(N_POOL_GROUPS,),
        in_specs=[pl.BlockSpec((T, gd), lambda g: (0, g))],
        out_specs=[pl.BlockSpec((T, gd), lambda g: (0, g)), pl.BlockSpec((gd, T), lambda g: (g, 0))],
        out_shape=[jax.ShapeDtypeStruct((T, D), BF16), jax.ShapeDtypeStruct((D, T), BF16)],
        scratch_shapes=[pltpu.VMEM((T + 2 * POOL_PAD, gd), F32)],
        compiler_params=_params("arbitrary"),
    )(v)


def _pool_group(mixedb, wgroup, scale):
    T = mixedb.shape[0]
    tm = min(1024, T)
    gd = POOL_GROUP_DIM

    def body(a_ref, w_ref, s_ref, y_ref, ys_ref, yst_ref):
        y = jnp.dot(a_ref[...], w_ref[...], preferred_element_type=F32)
        ys = y * s_ref[...]
        y_ref[...] = y
        ys_ref[...] = ys.astype(BF16)
        yst_ref[...] = ys.T.astype(BF16)

    blk = pl.BlockSpec((tm, gd), lambda g, t: (t, g))
    return _pcall(
        body, name="pool_group", grid=(N_POOL_GROUPS, T // tm),
        in_specs=[blk, pl.BlockSpec((None, gd, gd), lambda g, t: (g, 0, 0)), pl.BlockSpec((1, gd), lambda g, t: (0, g))],
        out_specs=[blk, blk, pl.BlockSpec((gd, tm), lambda g, t: (g, t))],
        out_shape=[jax.ShapeDtypeStruct((T, D), F32), jax.ShapeDtypeStruct((T, D), BF16),
                   jax.ShapeDtypeStruct((D, T), BF16)],
        compiler_params=_params("parallel", "parallel"),
    )(mixedb, wgroup, scale)


def _pool_bwd_out(dmb, w_out, y, scale):
    T = dmb.shape[0]
    tm = min(512, T)

    def body(a_ref, w_ref, y_ref, s_ref, dy_ref, ds_ref):
        dys = lax.dot_general(a_ref[...], w_ref[...], (((1,), (1,)), ((), ())), preferred_element_type=F32)
        dy_ref[...] = (dys * s_ref[...]).astype(BF16)
        part = jnp.sum(dys * y_ref[...], axis=0, keepdims=True)

        @pl.when(pl.program_id(0) == 0)
        def _():
            ds_ref[...] = part

        @pl.when(pl.program_id(0) > 0)
        def _():
            ds_ref[...] += part

    row = pl.BlockSpec((tm, D), lambda t: (t, 0))
    vec = pl.BlockSpec((1, D), lambda t: (0, 0))
    return _pcall(
        body, name="pool_bwd_out", grid=(T // tm,),
        in_specs=[row, pl.BlockSpec((D, D), lambda t: (0, 0)), row, vec],
        out_specs=[row, vec],
        out_shape=[jax.ShapeDtypeStruct((T, D), BF16), jax.ShapeDtypeStruct((1, D), F32)],
        compiler_params=_params("arbitrary"),
    )(dmb, w_out, y, scale)


def _attn_masks(n, L, d):
    w = ATTN_BLOCK + 2 * ATTN_HALO
    a = lax.broadcasted_iota(jnp.int32, (ATTN_BLOCK, w), 0)
    c = lax.broadcasted_iota(jnp.int32, (ATTN_BLOCK, w), 1)
    rel = c - ATTN_HALO - a
    j = n * ATTN_BLOCK - ATTN_HALO + c
    valid = (jnp.abs(rel) <= ATTN_HALO) & (j >= 0) & (j < L)
    dist = (d * jnp.abs(rel)).astype(F32)
    return valid, dist


def _lane_col(st, idx):
    lane = lax.broadcasted_iota(jnp.int32, st.shape, 1)
    return jnp.sum(jnp.where(lane == idx, st, 0.0), axis=1, keepdims=True)


def _window_specs(nb, d, col, width):
    last = 2 * d * nb - 1

    def prev(r, n):
        return (jnp.maximum(2 * (r * nb + n) - 1, 0), col)

    def cur(r, n):
        return (r * nb + n, col)

    def nxt(r, n):
        return (jnp.minimum(2 * (r * nb + n) + 2, last), col)

    return [pl.BlockSpec((ATTN_HALO, width), prev), pl.BlockSpec((ATTN_BLOCK, width), cur),
            pl.BlockSpec((ATTN_HALO, width), nxt)]


def _attn_fwd(qkv_g, d, slopes, name):
    T = qkv_g.shape[0]
    L = T // d
    nb = L // ATTN_BLOCK

    def body(q_ref, kp_ref, kc_ref, kn_ref, vp_ref, vc_ref, vn_ref, o_ref, lse_ref):
        valid, dist = _attn_masks(pl.program_id(1), L, d)
        lane = lax.broadcasted_iota(jnp.int32, (ATTN_BLOCK, LANE), 1)
        first = lane < HEAD_DIM
        head_mask = [jnp.where(first, 1.0, 0.0).astype(BF16), jnp.where(first, 0.0, 1.0).astype(BF16)]
        lse_acc = jnp.zeros((ATTN_BLOCK, LANE), F32)
        for hp in range(N_HEADS // 2):
            cs = slice(hp * LANE, (hp + 1) * LANE)
            q2 = q_ref[:, cs]
            k2 = jnp.concatenate([kp_ref[:, cs], kc_ref[:, cs], kn_ref[:, cs]], axis=0)
            v2 = jnp.concatenate([vp_ref[:, cs], vc_ref[:, cs], vn_ref[:, cs]], axis=0)
            outs = []
            for hh in range(2):
                h = 2 * hp + hh
                qh = q2 * head_mask[hh]
                s = lax.dot_general(qh, k2, (((1,), (1,)), ((), ())), preferred_element_type=F32)
                s = s * (HEAD_DIM ** -0.5) - float(slopes[h]) * dist
                s = jnp.where(valid, s, MASK_VALUE)
                m = jnp.max(s, axis=1, keepdims=True)
                p = jnp.exp(s - m)
                l = jnp.sum(p, axis=1, keepdims=True)
                o = jnp.dot(p.astype(BF16), v2, preferred_element_type=F32) / l
                outs.append(o)
                lse_acc = jnp.where(lane == h, m + jnp.log(l), lse_acc)
            o_ref[:, cs] = jnp.where(first, outs[0], outs[1])
        lse_ref[...] = lse_acc

    specs = ([pl.BlockSpec((ATTN_BLOCK, D), lambda r, n: (r * nb + n, 0))]
             + _window_specs(nb, d, 1, D) + _window_specs(nb, d, 2, D))
    row = lambda w: pl.BlockSpec((ATTN_BLOCK, w), lambda r, n: (r * nb + n, 0))
    return _pcall(
        body, name=name, grid=(d, nb), in_specs=specs,
        out_specs=[row(D), row(LANE)],
        out_shape=[jax.ShapeDtypeStruct((T, D), F32), jax.ShapeDtypeStruct((T, LANE), F32)],
        compiler_params=_params("parallel", "parallel"),
    )(*([qkv_g] * 7))


def _stage(scr3, val):
    for c in range(val.shape[1] // LANE):
        scr3[c] = val[:, c * LANE:(c + 1) * LANE]


def _unstage(scr3):
    return jnp.concatenate([scr3[c] for c in range(scr3.shape[0])], axis=1)


def _gather_rows(scr3, r, n, d):
    return jnp.concatenate([scr3[c, pl.ds(r, n, stride=d), :] for c in range(scr3.shape[0])], axis=1)


def _scatter_rows(scr3, r, n, d, val):
    for c in range(scr3.shape[0]):
        scr3[c, pl.ds(r, n, stride=d), :] = val[:, c * LANE:(c + 1) * LANE]


def _attn_combine(os_, lses, dils):
    T = os_[0].shape[0]
    tm = min(256, T)
    ng = len(os_)
    n_scr = sum(1 for d in dils if d > 1)

    def body(*refs):
        in_o = refs[:ng]
        in_l = refs[ng:2 * ng]
        o32_ref, ob_ref, ot_ref, lt_ref = refs[2 * ng:2 * ng + 4]
        scr = refs[2 * ng + 4:]
        o_chunk, l_refs, si = [], [], 0
        for g, d in enumerate(dils):
            if d == 1:
                o_chunk.append(lambda hp, g=g: in_o[g][:, hp * LANE:(hp + 1) * LANE])
                l_refs.append(in_l[g])
                continue
            so, sl = scr[2 * si], scr[2 * si + 1]
            si += 1
            for r in range(d):
                _scatter_rows(so, r, tm // d, d, in_o[g][r])
                sl[pl.ds(r, tm // d, stride=d), :] = in_l[g][r]
            o_chunk.append(lambda hp, so=so: so[hp])
            l_refs.append(sl)
        ls = [r[...] for r in l_refs]
        m = ls[0]
        for l in ls[1:]:
            m = jnp.maximum(m, l)
        tot = jnp.exp(ls[0] - m)
        for l in ls[1:]:
            tot = tot + jnp.exp(l - m)
        lt = m + jnp.log(tot)
        lt_ref[...] = lt
        ws = [jnp.exp(l - lt) for l in ls]
        lane = lax.broadcasted_iota(jnp.int32, (tm, LANE), 1)
        first = lane < HEAD_DIM
        for hp in range(N_HEADS // 2):
            cs = slice(hp * LANE, (hp + 1) * LANE)
            acc = jnp.zeros((tm, LANE), F32)
            for g in range(ng):
                wt = jnp.where(first, _lane_col(ws[g], 2 * hp), _lane_col(ws[g], 2 * hp + 1))
                acc = acc + wt * o_chunk[g](hp)
            o32_ref[:, cs] = acc
            ob_ref[:, cs] = acc.astype(BF16)
        ot_ref[...] = o32_ref[...].T.astype(BF16)

    row = pl.BlockSpec((tm, D), lambda t: (t, 0))
    st = pl.BlockSpec((tm, LANE), lambda t: (t, 0))

    def sub_spec(d, w):
        return pl.BlockSpec((tm, w), lambda t: (t, 0)) if d == 1 else pl.BlockSpec((d, tm // d, w), lambda t: (0, t, 0))

    def sub_view(a, d):
        return a if d == 1 else a.reshape(d, T // d, a.shape[1])

    return _pcall(
        body, name="attn_combine", grid=(T // tm,),
        in_specs=[sub_spec(d, D) for d in dils] + [sub_spec(d, LANE) for d in dils],
        out_specs=[row, row, pl.BlockSpec((D, tm), lambda t: (0, t)), st],
        out_shape=[jax.ShapeDtypeStruct((T, D), F32), jax.ShapeDtypeStruct((T, D), BF16),
                   jax.ShapeDtypeStruct((D, T), BF16), jax.ShapeDtypeStruct((T, LANE), F32)],
        scratch_shapes=[pltpu.VMEM(s, F32) for _ in range(n_scr) for s in ((D // LANE, tm, LANE), (tm, LANE))],
        compiler_params=_params("parallel"),
    )(*[sub_view(a, d) for a, d in zip(os_, dils)], *[sub_view(a, d) for a, d in zip(lses, dils)])


def _attn_bwd_prep(dmb, w_out, o32, lse_tot, dils):
    T = dmb.shape[0]
    tm = min(512, T)
    ng = len(dils)

    def body(a_ref, w_ref, o_ref, l_ref, *rest):
        do_refs, st_refs = rest[:ng], rest[ng:2 * ng]
        do_scr, st_scr = rest[2 * ng:]
        do = lax.dot_general(a_ref[...], w_ref[...], (((1,), (1,)), ((), ())), preferred_element_type=F32)
        _stage(do_scr, do)
        prod = do * o_ref[...]
        lane = lax.broadcasted_iota(jnp.int32, (tm, LANE), 1)
        first = lane < HEAD_DIM
        st = jnp.where(lane < N_HEADS, l_ref[...], 0.0)
        for hp in range(N_HEADS // 2):
            pr = prod[:, hp * LANE:(hp + 1) * LANE]
            d0 = jnp.sum(jnp.where(first, pr, 0.0), axis=1, keepdims=True)
            d1 = jnp.sum(jnp.where(first, 0.0, pr), axis=1, keepdims=True)
            st = jnp.where(lane == N_HEADS + 2 * hp, d0, st)
            st = jnp.where(lane == N_HEADS + 2 * hp + 1, d1, st)
        st_scr[...] = st
        for g, d in enumerate(dils):
            if d == 1:
                do_refs[g][...] = do.astype(BF16)
                st_refs[g][...] = st
                continue
            for r in range(d):
                do_refs[g][r] = _gather_rows(do_scr, r, tm // d, d).astype(BF16)
                st_refs[g][r] = st_scr[pl.ds(r, tm // d, stride=d), :]

    row = pl.BlockSpec((tm, D), lambda t: (t, 0))
    stb = pl.BlockSpec((tm, LANE), lambda t: (t, 0))

    def sub_spec(d, w):
        return pl.BlockSpec((tm, w), lambda t: (t, 0)) if d == 1 else pl.BlockSpec((d, tm // d, w), lambda t: (0, t, 0))

    def sub_shape(d, w, dt):
        return jax.ShapeDtypeStruct((T, w) if d == 1 else (d, T // d, w), dt)

    outs = _pcall(
        body, name="attn_bwd_prep", grid=(T // tm,),
        in_specs=[row, pl.BlockSpec((D, D), lambda t: (0, 0)), row, stb],
        out_specs=[sub_spec(d, D) for d in dils] + [sub_spec(d, LANE) for d in dils],
        out_shape=[sub_shape(d, D, BF16) for d in dils] + [sub_shape(d, LANE, F32) for d in dils],
        scratch_shapes=[pltpu.VMEM((D // LANE, tm, LANE), F32), pltpu.VMEM((tm, LANE), F32)],
        compiler_params=_params("parallel"),
    )(dmb, w_out, o32, lse_tot)
    return ([o.reshape(T, D) for o in outs[:ng]], [o.reshape(T, LANE) for o in outs[ng:]])


def _attn_bwd(qkv_g, do_g, st_g, d, slopes, name):
    T = qkv_g.shape[0]
    L = T // d
    nb = L // ATTN_BLOCK
    scale = HEAD_DIM ** -0.5
    nt = (((1,), (1,)), ((), ()))

    def body(qp_ref, qc_ref, qn_ref, kp_ref, kc_ref, kn_ref, vp_ref, vc_ref, vn_ref,
             dp_ref, dc_ref, dn_ref, sp_ref, sc_ref, sn_ref, o_ref):
        valid, dist = _attn_masks(pl.program_id(1), L, d)
        lane = lax.broadcasted_iota(jnp.int32, (ATTN_BLOCK, LANE), 1)
        first = lane < HEAD_DIM
        head_mask = [jnp.where(first, 1.0, 0.0).astype(BF16), jnp.where(first, 0.0, 1.0).astype(BF16)]
        stc = sc_ref[...]
        stw_t = jnp.concatenate([sp_ref[...], stc, sn_ref[...]], axis=0).T
        for hp in range(N_HEADS // 2):
            cs = slice(hp * LANE, (hp + 1) * LANE)
            cat = lambda a, b, c: jnp.concatenate([a[:, cs], b[:, cs], c[:, cs]], axis=0)
            q2, k2, v2, do2 = qc_ref[:, cs], kc_ref[:, cs], vc_ref[:, cs], dc_ref[:, cs]
            qw, kw, vw, dow = cat(qp_ref, qc_ref, qn_ref), cat(kp_ref, kc_ref, kn_ref), cat(vp_ref, vc_ref, vn_ref), cat(dp_ref, dc_ref, dn_ref)
            dqs, dks, dvs = [], [], []
            for hh in range(2):
                h = 2 * hp + hh
                pick = lambda t, hh=hh: t * head_mask[hh]
                bias = float(slopes[h]) * dist
                s = lax.dot_general(pick(q2), kw, nt, preferred_element_type=F32) * scale - bias
                s = jnp.where(valid, s, MASK_VALUE)
                p = jnp.exp(s - _lane_col(stc, h))
                dp = lax.dot_general(pick(do2), vw, nt, preferred_element_type=F32)
                ds = p * (dp - _lane_col(stc, N_HEADS + h)) * scale
                dqs.append(jnp.dot(ds.astype(BF16), kw, preferred_element_type=F32))
                st_ = lax.dot_general(pick(k2), qw, nt, preferred_element_type=F32) * scale - bias
                st_ = jnp.where(valid, st_, MASK_VALUE)
                pt = jnp.exp(st_ - stw_t[h:h + 1, :])
                dvs.append(jnp.dot(pt.astype(BF16), dow, preferred_element_type=F32))
                dpt = lax.dot_general(pick(v2), dow, nt, preferred_element_type=F32)
                dst = pt * (dpt - stw_t[N_HEADS + h:N_HEADS + h + 1, :]) * scale
                dks.append(jnp.dot(dst.astype(BF16), qw, preferred_element_type=F32))
            o_ref[:, hp * LANE:(hp + 1) * LANE] = jnp.where(first, dqs[0], dqs[1]).astype(BF16)
            o_ref[:, D + hp * LANE:D + (hp + 1) * LANE] = jnp.where(first, dks[0], dks[1]).astype(BF16)
            o_ref[:, 2 * D + hp * LANE:2 * D + (hp + 1) * LANE] = jnp.where(first, dvs[0], dvs[1]).astype(BF16)

    specs = (_window_specs(nb, d, 0, D) + _window_specs(nb, d, 1, D) + _window_specs(nb, d, 2, D)
             + _window_specs(nb, d, 0, D) + _window_specs(nb, d, 0, LANE))
    return _pcall(
        body, name=name, grid=(d, nb), in_specs=specs,
        out_specs=pl.BlockSpec((ATTN_BLOCK, 3 * D), lambda r, n: (r * nb + n, 0)),
        out_shape=jax.ShapeDtypeStruct((T, 3 * D), BF16),
        compiler_params=_params("parallel", "parallel"),
    )(*([qkv_g] * 9), *([do_g] * 3), *([st_g] * 3))


QKV_TILE = QKV_SHARD // 3


def _qkv_tile_block(nn):
    dev = nn // 3
    return 4 * (dev % 2) + dev // 2, nn % 3


def _attn_qkv_group(xb, wqkv, gi, d, name, deps=()):
    T = xb.shape[0]
    tq = min(1024, T)
    nsub = tq // d
    tn = 768
    ntile = 3 * D // tn

    def body(x_ref, w_ref, o_ref, *scr):
        p = jnp.dot(x_ref[...], w_ref[...], preferred_element_type=F32)
        if d == 1:
            o_ref[...] = p.astype(BF16)
        else:
            _stage(scr[0], p)
            for r in range(d):
                o_ref[r] = _gather_rows(scr[0], r, nsub, d).astype(BF16)

    if d == 1:
        out_spec = pl.BlockSpec((tq, tn), lambda n, t: (t, n))
        out_shape = jax.ShapeDtypeStruct((T, 3 * D), BF16)
    else:
        out_spec = pl.BlockSpec((d, nsub, tn), lambda n, t: (0, t, n))
        out_shape = jax.ShapeDtypeStruct((d, T // d, 3 * D), BF16)
    out = _pcall(
        body, deps=deps, name=name, grid=(ntile, T // tq),
        in_specs=[pl.BlockSpec((tq, D), lambda n, t: (t, 0)),
                  pl.BlockSpec((D, tn), lambda n, t: (0, ntile * gi + n))],
        out_specs=out_spec, out_shape=out_shape,
        scratch_shapes=[] if d == 1 else [pltpu.VMEM((tn // LANE, tq, LANE), F32)],
        compiler_params=_params("parallel", "parallel"),
    )(xb, wqkv)
    return out.reshape(T, 3 * D)


def _attn_dx_group(dqkv_g, wqkv, gi, d, name, deps=()):
    T = dqkv_g.shape[0]
    tq = min(512, T)
    nsub = tq // d

    def body(a_ref, w_ref, o_ref, *stage):
        a = a_ref[...]
        if d > 1:
            a = a.reshape(tq, 3 * D)
        p = lax.dot_general(a, w_ref[...], (((1,), (1,)), ((), ())), preferred_element_type=F32)
        if d == 1:
            o_ref[...] = p
        else:
            for r in range(d):
                _scatter_rows(stage[0], r, nsub, d, p[r * nsub:(r + 1) * nsub, :])
            o_ref[...] = _unstage(stage[0])

    if d == 1:
        a_spec = pl.BlockSpec((tq, 3 * D), lambda t: (t, 0))
        a = dqkv_g
    else:
        a_spec = pl.BlockSpec((d, nsub, 3 * D), lambda t: (0, t, 0))
        a = dqkv_g.reshape(d, T // d, 3 * D)
    return _pcall(
        body, deps=deps, name=name, grid=(T // tq,),
        in_specs=[a_spec, pl.BlockSpec((D, 3 * D), lambda t: (0, gi))],
        out_specs=pl.BlockSpec((tq, D), lambda t: (t, 0)),
        out_shape=jax.ShapeDtypeStruct((T, D), F32),
        scratch_shapes=[] if d == 1 else [pltpu.VMEM((D // LANE, tq, LANE), F32)],
        compiler_params=_params("parallel"),
    )(a, wqkv)


def _attn_dw_group(xt, dqkv_g, gi, d, prev, name):
    T = dqkv_g.shape[0]
    L = T // d
    bm = 512
    ntile = 3 * D // QKV_TILE

    def body(*refs):
        a_ref, b_ref = refs[0], refs[1]
        o_ref, cat = refs[-2], refs[-1]
        if d == 1:
            a = a_ref[...]
        else:
            @pl.when(pl.program_id(1) == 0)
            def _():
                for r in range(d):
                    cat[:, r * L:(r + 1) * L] = a_ref[r]
            a = cat[...]
        o_ref[...] = jnp.dot(a, b_ref[...], preferred_element_type=F32).astype(BF16)

    def out_map(i, n):
        slot, sub = _qkv_tile_block(ntile * gi + n)
        return slot, i, sub

    a_spec = (pl.BlockSpec((bm, T), lambda i, n: (i, 0)) if d == 1
              else pl.BlockSpec((d, bm, L), lambda i, n: (0, i, 0)))
    in_specs = [a_spec, pl.BlockSpec((T, QKV_TILE), lambda i, n: (0, n))]
    args = [xt, dqkv_g]
    aliases = {}
    if prev is not None:
        in_specs.append(ANY_SPEC)
        args.append(prev)
        aliases = {2: 0}
    return _pcall(
        body, name=name, grid=(D // bm, ntile), in_specs=in_specs,
        out_specs=pl.BlockSpec((None, bm, QKV_TILE), out_map),
        out_shape=jax.ShapeDtypeStruct((N_DEV, D, QKV_SHARD), BF16),
        scratch_shapes=[pltpu.VMEM((bm, T), BF16)],
        input_output_aliases=aliases,
        compiler_params=_params("parallel", "arbitrary"),
    )(*args)


def _transpose_sub(x, d, name):
    T = x.shape[0]
    tm = LANE * d

    def body(x_ref, o_ref, scr):
        for c in range(D // LANE):
            scr[...] = x_ref[:, c * LANE:(c + 1) * LANE]
            for r in range(d):
                o_ref[r, c * LANE:(c + 1) * LANE, :] = scr[pl.ds(r, LANE, stride=d), :].T.astype(BF16)

    return _pcall(
        body, name=name, grid=(T // tm,),
        in_specs=[pl.BlockSpec((tm, D), lambda t: (t, 0))],
        out_specs=pl.BlockSpec((d, D, LANE), lambda t: (0, 0, t)),
        out_shape=jax.ShapeDtypeStruct((d, D, T // d), BF16),
        scratch_shapes=[pltpu.VMEM((tm, LANE), F32)],
        compiler_params=_params("parallel"),
    )(x)


HBM_SPEC = pl.BlockSpec(memory_space=pltpu.HBM)
SEM_SPEC = pl.BlockSpec(memory_space=pltpu.SEMAPHORE)
ANY_SPEC = pl.BlockSpec(memory_space=pl.ANY)
DATAFLOW = pltpu.SideEffectType.DATAFLOW_SIDE_EFFECTING


def _me_and_peers():
    x, y, c = lax.axis_index("x"), lax.axis_index("y"), lax.axis_index("c")
    return (x, y, c), [(x, y, 1 - c), (1 - x, y, c), (x, 1 - y, c), (1 - x, 1 - y, c)]


def _slot(px, py, pc):
    return 4 * pc + 2 * px + py


def _split_start(srcs, lands, after, start_copies, n_sem, name):
    n = len(srcs)
    n_after = len(after)

    def body(*refs):
        src_refs, land_refs = refs[:n], refs[n:2 * n]
        send_sems, recv_sems = refs[2 * n + n_after], refs[2 * n + n_after + 1]
        token = refs[-1]
        start_copies(src_refs, land_refs, send_sems, recv_sems)
        token[...] = jnp.zeros_like(token)

    outs = _pcall(
        body, name=name,
        in_specs=[HBM_SPEC] * (2 * n) + [ANY_SPEC] * n_after,
        out_shape=(pltpu.SemaphoreType.DMA(n_sem), pltpu.SemaphoreType.DMA(n_sem),
                   *[pltpu.HBM(a.shape, a.dtype) for a in srcs], *[pltpu.HBM(a.shape, a.dtype) for a in lands],
                   jax.ShapeDtypeStruct((8, LANE), F32)),
        out_specs=(SEM_SPEC, SEM_SPEC, *[HBM_SPEC] * (2 * n), pl.BlockSpec(memory_space=pltpu.VMEM)),
        input_output_aliases={i: 2 + i for i in range(2 * n)},
        compiler_params=pltpu.CompilerParams(has_side_effects=DATAFLOW),
    )(*[pltpu.with_memory_space_constraint(a, pltpu.HBM) for a in srcs],
      *[pltpu.with_memory_space_constraint(a, pltpu.HBM) for a in lands], *after)
    return outs[0], outs[1], list(outs[2:2 + n]), list(outs[2 + n:2 + 2 * n]), outs[-1]


def _split_wait(handle, after, wait_copies, name):
    send_sems, recv_sems, srcs, lands, _ = handle
    n = len(srcs)

    def body(*refs):
        src_refs, land_refs = refs[:n], refs[n:2 * n]
        wait_copies(src_refs, land_refs, refs[2 * n], refs[2 * n + 1])

    outs = _pcall(
        body, name=name,
        in_specs=[HBM_SPEC] * (2 * n) + [SEM_SPEC, SEM_SPEC] + [ANY_SPEC] * len(after),
        out_shape=tuple(pltpu.HBM(a.shape, a.dtype) for a in srcs + lands),
        out_specs=tuple([HBM_SPEC] * (2 * n)),
        input_output_aliases={i: i for i in range(2 * n)},
        compiler_params=pltpu.CompilerParams(has_side_effects=DATAFLOW),
    )(*srcs, *lands, send_sems, recv_sems, *after)
    return list(outs[:n]), list(outs[n:])


def _ag_copies(src_refs, land_refs, send_sems, recv_sems, received):
    me, peers = _me_and_peers()
    cps = []
    for i in range(len(src_refs)):
        for k, to in enumerate(peers):
            cps.append(pltpu.make_async_remote_copy(
                src_ref=src_refs[i], dst_ref=land_refs[i].at[_slot(*(to if received else me))],
                send_sem=send_sems.at[4 * i + k], recv_sem=recv_sems.at[4 * i + k], device_id=to,
                device_id_type=MESH))
    return cps


def _ag_start(shards, after, name):
    lands = [lax.empty((N_DEV,) + a.shape, a.dtype) for a in shards]

    def start(src_refs, land_refs, send_sems, recv_sems):
        for cp in _ag_copies(src_refs, land_refs, send_sems, recv_sems, False):
            cp.start()

    return _split_start(shards, lands, after, start, (4 * len(shards),), name)


def _ag_finish(handle, after, name):
    def wait(src_refs, land_refs, send_sems, recv_sems):
        for cp in _ag_copies(src_refs, land_refs, send_sems, recv_sems, True):
            cp.wait_send()
            cp.wait_recv()

    shards, lands = _split_wait(handle, after, wait, name + "_wait")
    n = len(shards)

    def body(*refs):
        src_refs, out_refs = refs[:n], refs[2 * n:3 * n]
        send_sems, recv_sems, local_sems = refs[3 * n:3 * n + 3]
        bounce = refs[3 * n + 3:]
        me, peers = _me_and_peers()
        loads = [pltpu.make_async_copy(src_refs[i], bounce[i], local_sems.at[i]) for i in range(n)]
        mine = [pltpu.make_async_copy(bounce[i], out_refs[i].at[_slot(*me)], local_sems.at[i]) for i in range(n)]
        for cp in loads:
            cp.start()
        cps = []
        for i in range(n):
            for j, chip in enumerate(peers[1:]):
                blk = out_refs[i].at[_slot(*chip)]
                cps.append(pltpu.make_async_remote_copy(
                    src_ref=blk, dst_ref=blk, send_sem=send_sems.at[i, j], recv_sem=recv_sems.at[i, j],
                    device_id=peers[0], device_id_type=MESH))
        for cp in cps:
            cp.start()
        for ld, st in zip(loads, mine):
            ld.wait()
            st.start()
        for cp in cps:
            cp.wait()
        for cp in mine:
            cp.wait()

    outs = _pcall(
        body, name=name + "_pass",
        in_specs=[ANY_SPEC] * (2 * n), out_specs=[ANY_SPEC] * n,
        out_shape=[jax.ShapeDtypeStruct(a.shape, a.dtype) for a in lands],
        input_output_aliases={n + i: i for i in range(n)},
        scratch_shapes=[pltpu.SemaphoreType.DMA((n, 3)), pltpu.SemaphoreType.DMA((n, 3)),
                        pltpu.SemaphoreType.DMA((n,))] + [pltpu.VMEM(a.shape, a.dtype) for a in shards],
        compiler_params=pltpu.CompilerParams(vmem_limit_bytes=VMEM_LIMIT),
    )(*shards, *lands)
    return list(outs)


def _small_all_gather(v, name):
    R, C = v.shape

    def body(x_ref, out_ref, sum_ref, send_sems, recv_sems, local_sem):
        x, y, c = lax.axis_index("x"), lax.axis_index("y"), lax.axis_index("c")
        me, sibling = (x, y, c), (x, y, 1 - c)
        chips = [(1 - x, y), (x, 1 - y), (1 - x, 1 - y)]

        def rows(px, py, pc):
            return out_ref.at[4 * pc + 2 * px + py]

        def copy(k, block, to, src=None):
            return pltpu.make_async_remote_copy(
                src_ref=rows(*block) if src is None else src, dst_ref=rows(*block),
                send_sem=send_sems.at[k], recv_sem=recv_sems.at[k],
                device_id=to, device_id_type=MESH)

        mine = pltpu.make_async_copy(x_ref, rows(*me), local_sem)
        mine.start()
        first = [copy(0, me, sibling, src=x_ref)]
        first += [copy(1 + j, me, (*chip, c), src=x_ref) for j, chip in enumerate(chips)]
        for cp in first:
            cp.start()
        passed = [copy(4 + j, (*chip, c), sibling) for j, chip in enumerate(chips)]
        for j, chip in enumerate(chips):
            copy(1 + j, (*chip, c), me).wait_recv()
            passed[j].start()
        copy(0, sibling, me).wait_recv()
        for j, chip in enumerate(chips):
            copy(4 + j, (*chip, 1 - c), me).wait_recv()
        for cp in first + passed:
            cp.wait_send()
        mine.wait()
        acc = out_ref[0]
        for s in range(1, N_DEV):
            acc = acc + out_ref[s]
        sum_ref[...] = acc

    vm = pl.BlockSpec(memory_space=pltpu.VMEM)
    return _pcall(
        body, name=name, in_specs=[vm], out_specs=[vm, vm],
        out_shape=[jax.ShapeDtypeStruct((N_DEV, R, C), v.dtype), jax.ShapeDtypeStruct((R, C), v.dtype)],
        scratch_shapes=[pltpu.SemaphoreType.DMA((7,)), pltpu.SemaphoreType.DMA((7,)), pltpu.SemaphoreType.DMA],
    )(v)


def _rs_sibling(arrs, name):
    n = len(arrs)

    def body(*refs):
        ins, outs = refs[:n], refs[n:2 * n]
        send_sems, recv_sems = refs[2 * n:]
        x, y, c = lax.axis_index("x"), lax.axis_index("y"), lax.axis_index("c")
        cps = [pltpu.make_async_remote_copy(
            src_ref=ins[i].at[pl.ds(4 * (1 - c), 4)], dst_ref=outs[i],
            send_sem=send_sems.at[i], recv_sem=recv_sems.at[i],
            device_id=(x, y, 1 - c), device_id_type=MESH) for i in range(n)]
        for cp in cps:
            cp.start()
        for cp in cps:
            cp.wait()

    hbm = pl.BlockSpec(memory_space=pl.ANY)
    return _pcall(
        body, name=name, in_specs=[hbm] * n, out_specs=[hbm] * n,
        out_shape=[jax.ShapeDtypeStruct((4,) + a.shape[1:], a.dtype) for a in arrs],
        scratch_shapes=[pltpu.SemaphoreType.DMA((n,)), pltpu.SemaphoreType.DMA((n,))],
    )(*arrs)


def _rs_copies(src_refs, land_refs, send_sems, recv_sems):
    _, peers = _me_and_peers()
    cps = []
    for i in range(len(src_refs)):
        for j, (px, py, pc) in enumerate(peers[1:]):
            cps.append(pltpu.make_async_remote_copy(
                src_ref=src_refs[i].at[2 * px + py], dst_ref=land_refs[i].at[j],
                send_sem=send_sems.at[3 * i + j], recv_sem=recv_sems.at[3 * i + j],
                device_id=(px, py, pc), device_id_type=MESH))
    return cps


def _rs_start(chipsums, after, name):
    lands = [lax.empty((3,) + a.shape[1:], a.dtype) for a in chipsums]

    def start(src_refs, land_refs, send_sems, recv_sems):
        for cp in _rs_copies(src_refs, land_refs, send_sems, recv_sems):
            cp.start()

    return _split_start(chipsums, lands, after, start, (3 * len(chipsums),), name)


def _rs_wait(handle, after, name):
    def wait(src_refs, land_refs, send_sems, recv_sems):
        for cp in _rs_copies(src_refs, land_refs, send_sems, recv_sems):
            cp.wait_send()
            cp.wait_recv()

    return _split_wait(handle, after, wait, name)


def _row_tile(R, C, itemsize=4, budget=2 * 1024 * 1024):
    best = None
    for t in range(16, R + 1, 16):
        if R % t == 0 and t * C * itemsize <= budget:
            best = t
    return best if best is not None else R


def _add_half(arr, recv, c_idx, name):
    _, R, C = arr.shape
    tr = _row_tile(R, C)

    def body(c_ref, a_ref, r_ref, o_ref):
        o_ref[...] = (a_ref[...].astype(F32) + r_ref[...].astype(F32)).astype(o_ref.dtype)

    gs = pltpu.PrefetchScalarGridSpec(
        num_scalar_prefetch=1, grid=(4, R // tr),
        in_specs=[pl.BlockSpec((None, tr, C), lambda q, i, c_ref: (4 * c_ref[0] + q, i, 0)),
                  pl.BlockSpec((None, tr, C), lambda q, i, c_ref: (q, i, 0))],
        out_specs=pl.BlockSpec((None, tr, C), lambda q, i, c_ref: (q, i, 0)))
    return _pcall(body, name=name, grid_spec=gs, out_shape=jax.ShapeDtypeStruct((4, R, C), arr.dtype),
                  compiler_params=_params("parallel", "parallel"))(c_idx, arr, recv)


def _sum_chips(chipsum, recv, q_idx, name):
    _, R, C = chipsum.shape
    tr = _row_tile(R, C)

    def body(q_ref, a_ref, r_ref, o_ref):
        acc = a_ref[...].astype(F32)
        for j in range(3):
            acc = acc + r_ref[j].astype(F32)
        o_ref[...] = acc

    gs = pltpu.PrefetchScalarGridSpec(
        num_scalar_prefetch=1, grid=(R // tr,),
        in_specs=[pl.BlockSpec((None, tr, C), lambda i, q_ref: (q_ref[0], i, 0)),
                  pl.BlockSpec((3, tr, C), lambda i, q_ref: (0, i, 0))],
        out_specs=pl.BlockSpec((tr, C), lambda i, q_ref: (i, 0)))
    return _pcall(body, name=name, grid_spec=gs, out_shape=jax.ShapeDtypeStruct((R, C), F32),
                  compiler_params=_params("parallel"))(q_idx, chipsum, recv)


def _adamw(w, g, m, v, name):
    shape = w.shape
    C = shape[-1]
    R = int(np.prod(shape[:-1]))
    tr = _row_tile(R, C, budget=1024 * 1024)

    def body(w_ref, g_ref, m_ref, v_ref, d_ref, nm_ref, nv_ref):
        gv = g_ref[...]
        mv = ADAM_B1 * m_ref[...] + (1.0 - ADAM_B1) * gv
        vv = ADAM_B2 * v_ref[...] + (1.0 - ADAM_B2) * jnp.square(gv)
        m_hat = mv / (1.0 - ADAM_B1 ** ADAM_STEP)
        v_hat = vv / (1.0 - ADAM_B2 ** ADAM_STEP)
        d_ref[...] = -ADAM_LR * (m_hat / (jnp.sqrt(v_hat) + ADAM_EPS) + ADAM_WD * w_ref[...])
        nm_ref[...] = mv
        nv_ref[...] = vv

    blk = pl.BlockSpec((tr, C), lambda i: (i, 0))
    shp = jax.ShapeDtypeStruct((R, C), F32)
    outs = _pcall(body, name=name, grid=(R // tr,), in_specs=[blk] * 4, out_specs=[blk] * 3,
                  out_shape=[shp] * 3, compiler_params=_params("parallel"))(
        w.reshape(R, C), g.reshape(R, C), m.reshape(R, C), v.reshape(R, C))
    return tuple(o.reshape(shape) for o in outs)


def _pad_cols(w, width):
    return jnp.pad(w, ((0, 0), (0, width - w.shape[1])))


def _pad_rows(w, height):
    return jnp.pad(w, ((0, height - w.shape[0]), (0, 0)))


def _slot_to_device_order(a):
    s = a.shape
    return a.reshape((2, 4) + s[1:]).swapaxes(0, 1).reshape(s)


def _device_to_slot_order(a):
    s = a.shape
    return a.reshape((4, 2) + s[1:]).swapaxes(0, 1).reshape(s)


def kernel(x, ffn1_w_gate, ffn1_w_up, ffn1_w_down, ffn2_w_gate, ffn2_w_up, ffn2_w_down, ln_gain, ln_bias, pool_w_in, pool_w_group, pool_scale, pool_w_out, attn_w_qkv, attn_w_out, loss_target, m_ffn1_w_gate, m_ffn1_w_up, m_ffn1_w_down, m_ffn2_w_gate, m_ffn2_w_up, m_ffn2_w_down, m_ln_gain, m_ln_bias, m_pool_w_in, m_pool_w_group, m_pool_scale, m_pool_w_out, m_attn_w_qkv, m_attn_w_out, v_ffn1_w_gate, v_ffn1_w_up, v_ffn1_w_down, v_ffn2_w_gate, v_ffn2_w_up, v_ffn2_w_down, v_ln_gain, v_ln_bias, v_pool_w_in, v_pool_w_group, v_pool_scale, v_pool_w_out, v_attn_w_qkv, v_attn_w_out):
    T = x.shape[1]
    fs = ffn1_w_gate.shape[2]
    fp = _round_up(fs, LANE)
    rs = D // N_DEV
    x0 = x[0]
    tgt = loss_target[0]
    slopes = _alibi_slopes()
    c_idx = lax.axis_index("c").astype(jnp.int32).reshape(1)
    q_idx = (2 * lax.axis_index("x") + lax.axis_index("y")).astype(jnp.int32).reshape(1)

    gates = (ffn1_w_gate, ffn2_w_gate)
    ups = (ffn1_w_up, ffn2_w_up)
    downs = (ffn1_w_down, ffn2_w_down)
    ffns = [(i, k) for i in range(DEPTH) for k in range(2)]
    wgu_sh = [jnp.concatenate([_pad_cols(gates[k][i], fp), _pad_cols(ups[k][i], fp)], axis=1).astype(BF16)
              for i, k in ffns]
    wd_sh = [_pad_rows(downs[k][i], fp).astype(BF16) for i, k in ffns]
    sq_sh = jnp.concatenate([wd_sh[0], pool_w_in[0].astype(BF16), pool_w_out[0].astype(BF16),
                             pool_w_group[0].reshape(rs // 4, D).astype(BF16)], axis=0)
    qkv_sh = attn_w_qkv[0].astype(BF16)
    aout_sh = attn_w_out[0].astype(BF16)
    ln_sh = jnp.concatenate([ln_gain.reshape(DEPTH * 3, rs), ln_bias.reshape(DEPTH * 3, rs),
                             jnp.zeros((4, rs), F32)], axis=0)

    h0 = _ag_start([wgu_sh[0], ln_sh], (), "ag0")
    x0b, x0t = _transpose_cast(x0, "x_cast", deps=(h0[4],))
    wgu0, ln_all = _ag_finish(h0, (x0b,), "ag0")
    h1 = _ag_start([sq_sh], (wgu0,), "ag1")
    ln_all = _slot_to_device_order(ln_all).transpose(1, 0, 2).reshape(16, D)
    gain = lambda i, s: ln_all[3 * i + s][None]
    bias = lambda i, s: ln_all[DEPTH * 3 + 3 * i + s][None]

    def ffn_fwd(xf, xb, wgu, wd, f, i, s, dep_up=(), dep_down=()):
        g, u, act = _ffn_up(xb, wgu, fp, f"ffn_up{f}", deps=dep_up)
        wd = wd(act) if callable(wd) else wd
        y, yb, yt, xh, rstd = _mm_ln(act, wd, xf, gain(i, s), bias(i, s), MACARON, f"ffn_down_ln{f}",
                                     deps=dep_down() if callable(dep_down) else dep_down)
        return (y, yb, yt), dict(g=g, u=u, act=act, xh=xh, rstd=rstd, wgu=wgu, wd=wd)

    pool_w = {}

    def wd0_after(act):
        (sq_all,) = _ag_finish(h1, (act,), "ag1")
        pool_w["h2"] = _ag_start([wgu_sh[1], wd_sh[1]], (sq_all,), "ag2")
        pool_w["pin"] = _slot_to_device_order(sq_all[:, fp:fp + rs, :]).reshape(D, D)
        pool_w["pout"] = _slot_to_device_order(sq_all[:, fp + rs:fp + 2 * rs, :]).reshape(D, D)
        grp = _slot_to_device_order(sq_all[:, fp + 2 * rs:, :])
        pool_w["grp"] = grp.reshape(N_DEV, N_POOL_GROUPS, rs // 4, POOL_GROUP_DIM).transpose(1, 0, 2, 3).reshape(
            N_POOL_GROUPS, POOL_GROUP_DIM, POOL_GROUP_DIM)
        return sq_all[:, :fp, :].reshape(N_DEV * fp, D)

    (a1, a1b, a1t), s_f0 = ffn_fwd(x0, x0b, wgu0, wd0_after, 0, 0, 0, dep_up=(h1[4],),
                                   dep_down=lambda: (pool_w["h2"][4],))
    h2 = pool_w["h2"]
    w_pin, w_pout, w_grp = pool_w["pin"], pool_w["pout"], pool_w["grp"]
    tm = min(512, T)
    row_spec = pl.BlockSpec((tm, D), lambda i, j, k: (i, 0))
    full_w = pl.BlockSpec((D, D), lambda i, j, k: (0, 0))
    u_pool = _mm(a1b, w_pin, grid=(T // tm, 1, 1), a_spec=row_spec, b_spec=full_w,
                 out_shape=jax.ShapeDtypeStruct((T, D), F32), out_spec=row_spec, name="pool_in")
    mixedb, mixedt = _pool_window(u_pool, False, "pool_window")
    y_pool, ysb, yst = _pool_group(mixedb, w_grp, pool_scale)
    a2, a2b, a2t, xh_p, rstd_p = _mm_ln(ysb, w_pout, a1, gain(0, 1), bias(0, 1), 1.0, "pool_out_ln")
    wgu1, wd1 = _ag_finish(h2, (a2,), "ag2")
    h3 = _ag_start([wgu_sh[2], wd_sh[2]], (wgu1,), "ag3")
    (a3, a3b, a3t), s_f1 = ffn_fwd(a2, a2b, wgu1, wd1.reshape(N_DEV * fp, D), 1, 0, 2, dep_up=(h3[4],))
    wgu2, wd2 = _ag_finish(h3, (a3,), "ag3")
    h4 = _ag_start([qkv_sh, aout_sh], (wgu2,), "ag4")
    (b1, b1b, b1t), s_f2 = ffn_fwd(a3, a3b, wgu2, wd2.reshape(N_DEV * fp, D), 2, 1, 0, dep_up=(h4[4],))
    wqkv_all, aout_all = _ag_finish(h4, (b1,), "ag4")
    h5 = _ag_start([wgu_sh[3], wd_sh[3]], (wqkv_all,), "ag5")
    w_aout = _slot_to_device_order(aout_all).reshape(D, D)
    dils = [d for _, d in DIL_CONFIGS]
    wqkv_nat = _slots_to_columns(wqkv_all, "attn_wqkv_cols")
    qkv_gs, o_gs, lse_gs = [], [], []
    for gi, d in enumerate(dils):
        qkv_g = _attn_qkv_group(b1b, wqkv_nat, gi, d, f"attn_qkv{gi}", deps=(h5[4],) if gi == 0 else ())
        o_g, lse_g = _attn_fwd(qkv_g, d, slopes[gi], f"attn_fwd{gi}")
        qkv_gs.append(qkv_g)
        o_gs.append(o_g)
        lse_gs.append(lse_g)
    o32, ob, ot, lse_tot = _attn_combine(o_gs, lse_gs, dils)
    b2, b2b, b2t, xh_a, rstd_a = _mm_ln(ob, w_aout, b1, gain(1, 1), bias(1, 1), 1.0, "attn_out_ln")
    wgu3, wd3 = _ag_finish(h5, (b2,), "ag5")
    (b3, _, _), s_f3 = ffn_fwd(b2, b2b, wgu3, wd3.reshape(N_DEV * fp, D), 3, 1, 2)

    dy, loss_tile = _loss_head(b3, tgt)
    loss = lax.psum(loss_tile[0, 0], AXES)

    bm = min(512, D)
    dgains, dbiases = {}, {}
    rs_pending = []
    gsums = {}

    def rs_finish(after):
        h, tag = rs_pending.pop()
        chips, lands = _rs_wait(h, after, f"rs_{tag}_wait")
        gsums[tag] = [_sum_chips(a, r, q_idx, f"rs_sum_{tag}{i}") for i, (a, r) in enumerate(zip(chips, lands))]

    def rs_stage(bufs, tag):
        if rs_pending:
            rs_finish((bufs[-1],))
        recv = _rs_sibling(bufs, f"rs_sib_{tag}")
        chips = [_add_half(a, r, c_idx, f"rs_add_{tag}{i}") for i, (a, r) in enumerate(zip(bufs, recv))]
        h = _rs_start(chips, (), f"rs_{tag}")
        rs_pending.append((h, tag))
        return h[4]

    def ffn_bwd(dys, f, i, s, st, xt):
        dxres, dhb, dht, dg, db = _ln_bwd(dys, st["xh"], st["rstd"], gain(i, s), MACARON, f"ffn_ln_bwd{f}")
        dgains[(i, s)], dbiases[(i, s)] = dg, db
        dgu = _ffn_bwd_act(dhb, st["wd"], st["g"], st["u"], fp, f"ffn_bwd_act{f}")
        g_dt = _ffn_dwd(dht, st["act"], fp, f"ffn_dwd{f}")
        g_gu = _mm(xt, dgu, grid=(D // bm, N_DEV, 1),
                   a_spec=pl.BlockSpec((bm, T), lambda r, j, k: (r, 0)),
                   b_spec=pl.BlockSpec((T, 2 * fp), lambda r, j, k: (0, j)),
                   out_shape=jax.ShapeDtypeStruct((N_DEV, D, 2 * fp), BF16),
                   out_spec=pl.BlockSpec((None, bm, 2 * fp), lambda r, j, k: (j, r, 0)), name=f"ffn_dwgu{f}")
        token = rs_stage([g_dt, g_gu], f"f{f}")
        dx = _ffn_dx(dgu, st["wgu"], fp, f"ffn_dx{f}", deps=(token,))
        return [dxres, dx]

    def dw_square(at, bmat, name):
        return _mm(at, bmat, grid=(D // bm, 1, 1),
                   a_spec=pl.BlockSpec((bm, T), lambda r, j, k: (r, 0)),
                   b_spec=pl.BlockSpec((T, D), lambda r, j, k: (0, 0)),
                   out_shape=jax.ShapeDtypeStruct((D, D), BF16),
                   out_spec=pl.BlockSpec((bm, D), lambda r, j, k: (r, 0)), name=name)

    def dx_square(a, w, name, deps=()):
        return _mm(a, w, grid=(T // tm, 1, 1), nt=True, a_spec=row_spec, b_spec=full_w,
                   out_shape=jax.ShapeDtypeStruct((T, D), F32), out_spec=row_spec, name=name, deps=deps)

    to_slots = lambda g2d: _device_to_slot_order(g2d.reshape(N_DEV, rs, D))

    d_b2 = ffn_bwd([dy], 3, 1, 2, s_f3, b2t)
    dxres, dmb, dmt, dg, db = _ln_bwd(d_b2, xh_a, rstd_a, gain(1, 1), 1.0, "attn_ln_bwd")
    dgains[(1, 1)], dbiases[(1, 1)] = dg, db
    g_aout = dw_square(ot, dmb, "attn_dwout")
    dobs, statss = _attn_bwd_prep(dmb, w_aout, o32, lse_tot, dils)
    ntile = 3 * D // QKV_TILE
    g_qkv = None
    dqkv_gs = []
    for gi, d in enumerate(dils):
        dqkv_g = _attn_bwd(qkv_gs[gi], dobs[gi], statss[gi], d, slopes[gi], f"attn_bwd{gi}")
        dqkv_gs.append(dqkv_g)
        xt = b1t if d == 1 else _transpose_sub(b1, d, f"attn_xt{gi}")
        g_qkv = _attn_dw_group(xt, dqkv_g, gi, d, g_qkv, f"attn_dwqkv{gi}")
    token = rs_stage([to_slots(g_aout), g_qkv], "attn")
    dx_attn = [_attn_dx_group(dqkv_gs[gi], wqkv_nat, gi, d, f"attn_dx{gi}", deps=(token,) if gi == 0 else ())
               for gi, d in enumerate(dils)]
    d_a3 = ffn_bwd([dxres] + dx_attn, 2, 1, 0, s_f2, a3t)
    d_a2 = ffn_bwd(d_a3, 1, 0, 2, s_f1, a2t)
    dxres, dmb, dmt, dg, db = _ln_bwd(d_a2, xh_p, rstd_p, gain(0, 1), 1.0, "pool_ln_bwd")
    dgains[(0, 1)], dbiases[(0, 1)] = dg, db
    g_pout = dw_square(yst, dmb, "pool_dwout")
    dyb, dscale = _pool_bwd_out(dmb, w_pout, y_pool, pool_scale)
    gd = POOL_GROUP_DIM
    g_grp = _mm(mixedt, dyb, grid=(N_POOL_GROUPS, 1, 1),
                a_spec=pl.BlockSpec((gd, T), lambda g, j, k: (g, 0)),
                b_spec=pl.BlockSpec((T, gd), lambda g, j, k: (0, g)),
                out_shape=jax.ShapeDtypeStruct((N_POOL_GROUPS, gd, gd), BF16),
                out_spec=pl.BlockSpec((None, gd, gd), lambda g, j, k: (g, 0, 0)), name="pool_dwgroup")
    tg = min(1024, T)
    dmixed = _mm(dyb, w_grp, grid=(N_POOL_GROUPS, T // tg, 1), nt=True,
                 a_spec=pl.BlockSpec((tg, gd), lambda g, t, k: (t, g)),
                 b_spec=pl.BlockSpec((None, gd, gd), lambda g, t, k: (g, 0, 0)),
                 out_shape=jax.ShapeDtypeStruct((T, D), F32),
                 out_spec=pl.BlockSpec((tg, gd), lambda g, t, k: (t, g)), name="pool_dmixed")
    dub, _ = _pool_window(dmixed, True, "pool_window_bwd")
    g_pin = dw_square(a1t, dub, "pool_dwin")
    g_grp_slots = _device_to_slot_order(
        g_grp.reshape(N_POOL_GROUPS, N_DEV, rs // 4, gd).transpose(1, 0, 2, 3).reshape(N_DEV, rs // 4, D))
    token = rs_stage([to_slots(g_pout), g_grp_slots, to_slots(g_pin)], "pool")
    dx_pool = dx_square(dub, w_pin, "pool_dx", deps=(token,))
    d_x0 = ffn_bwd([dxres, dx_pool], 0, 0, 0, s_f0, x0t)
    grad_x = _add2(d_x0[0], d_x0[1], "grad_x_add")
    rs_finish((grad_x,))
    grad_x = grad_x[None]
    gw_dt = [gsums[f"f{f}"][0] for f in range(4)]
    gw_gu = [gsums[f"f{f}"][1] for f in range(4)]
    gw_aout, gw_qkv = gsums["attn"]
    gw_pout, gw_grp, gw_pin = gsums["pool"]

    small = jnp.concatenate([dgains[(i, s)] for i in range(DEPTH) for s in range(3)]
                            + [dbiases[(i, s)] for i in range(DEPTH) for s in range(3)]
                            + [dscale, jnp.zeros((3, D), F32)], axis=0)
    _, small_sum = _small_all_gather(small, "ag_small_grads")
    dev = 4 * lax.axis_index("x") + 2 * lax.axis_index("y") + lax.axis_index("c")
    mine = lax.dynamic_slice_in_dim(small_sum, dev * rs, rs, axis=1)
    grads = {
        "ffn1_w_gate": jnp.stack([gw_gu[2 * i][:, :fs] for i in range(DEPTH)]),
        "ffn1_w_up": jnp.stack([gw_gu[2 * i][:, fp:fp + fs] for i in range(DEPTH)]),
        "ffn1_w_down": jnp.stack([gw_dt[2 * i].T[:fs] for i in range(DEPTH)]),
        "ffn2_w_gate": jnp.stack([gw_gu[2 * i + 1][:, :fs] for i in range(DEPTH)]),
        "ffn2_w_up": jnp.stack([gw_gu[2 * i + 1][:, fp:fp + fs] for i in range(DEPTH)]),
        "ffn2_w_down": jnp.stack([gw_dt[2 * i + 1].T[:fs] for i in range(DEPTH)]),
        "ln_gain": mine[0:DEPTH * 3].reshape(DEPTH, 3, rs),
        "ln_bias": mine[DEPTH * 3:2 * DEPTH * 3].reshape(DEPTH, 3, rs),
        "pool_w_in": gw_pin[None],
        "pool_w_group": gw_grp[:rs // 4].reshape(N_POOL_GROUPS, rs // 4, gd)[None],
        "pool_scale": small_sum[2 * DEPTH * 3][None],
        "pool_w_out": gw_pout[None],
        "attn_w_qkv": gw_qkv[None],
        "attn_w_out": gw_aout[None],
    }
    weights = dict(ffn1_w_gate=ffn1_w_gate, ffn1_w_up=ffn1_w_up, ffn1_w_down=ffn1_w_down,
                   ffn2_w_gate=ffn2_w_gate, ffn2_w_up=ffn2_w_up, ffn2_w_down=ffn2_w_down,
                   ln_gain=ln_gain, ln_bias=ln_bias, pool_w_in=pool_w_in, pool_w_group=pool_w_group,
                   pool_scale=pool_scale, pool_w_out=pool_w_out, attn_w_qkv=attn_w_qkv, attn_w_out=attn_w_out)
    ms = dict(ffn1_w_gate=m_ffn1_w_gate, ffn1_w_up=m_ffn1_w_up, ffn1_w_down=m_ffn1_w_down,
              ffn2_w_gate=m_ffn2_w_gate, ffn2_w_up=m_ffn2_w_up, ffn2_w_down=m_ffn2_w_down,
              ln_gain=m_ln_gain, ln_bias=m_ln_bias, pool_w_in=m_pool_w_in, pool_w_group=m_pool_w_group,
              pool_scale=m_pool_scale, pool_w_out=m_pool_w_out, attn_w_qkv=m_attn_w_qkv, attn_w_out=m_attn_w_out)
    vs = dict(ffn1_w_gate=v_ffn1_w_gate, ffn1_w_up=v_ffn1_w_up, ffn1_w_down=v_ffn1_w_down,
              ffn2_w_gate=v_ffn2_w_gate, ffn2_w_up=v_ffn2_w_up, ffn2_w_down=v_ffn2_w_down,
              ln_gain=v_ln_gain, ln_bias=v_ln_bias, pool_w_in=v_pool_w_in, pool_w_group=v_pool_w_group,
              pool_scale=v_pool_scale, pool_w_out=v_pool_w_out, attn_w_qkv=v_attn_w_qkv, attn_w_out=v_attn_w_out)
    names = list(weights)
    deltas, new_m, new_v = {}, {}, {}
    for nme in names:
        deltas[nme], new_m[nme], new_v[nme] = _adamw(weights[nme], grads[nme], ms[nme], vs[nme], f"adamw_{nme}")
    return (loss, grad_x, *[grads[k] for k in names], *[deltas[k] for k in names],
            *[new_m[k] for k in names], *[new_v[k] for k in names])
```

```python
import functools

import numpy as np
import jax
import jax.numpy as jnp
from jax import lax
from jax.experimental import pallas as pl
from jax.experimental.pallas import tpu as pltpu

F32 = jnp.float32
BF16 = jnp.bfloat16

D = 1024
N_DEV = 8
N_HEADS = 16
HEAD_DIM = 64
N_POOL_GROUPS = 4
POOL_GROUP_DIM = 256
POOL_HALF = (1, 2, 4, 8)
DIL_CONFIGS = ((128, 1), (512, 4), (2048, 16))
ATTN_HALO = 64
ATTN_BLOCK = 128
QKV_SHARD = 3 * 3 * D // N_DEV
DEPTH = 2
ALPHA = (2.0 * DEPTH) ** 0.25
MACARON = 0.5
LN_EPS = 1e-5
MASK_VALUE = -1e30
ADAM_LR = 0.001
ADAM_B1 = 0.9
ADAM_B2 = 0.999
ADAM_EPS = 1e-08
ADAM_WD = 0.01
ADAM_STEP = 10
LANE = 128
VMEM_LIMIT = 56 * 1024 * 1024
MESH = pl.DeviceIdType.MESH
AXES = ("x", "y", "c")


def _round_up(n, m):
    return (n + m - 1) // m * m


def _pcall(body, deps=(), **kw):
    if not deps:
        return pl.pallas_call(body, **kw)
    n_in, n_dep = len(kw["in_specs"]), len(deps)

    def wrapped(*refs):
        return body(*refs[:n_in], *refs[n_in + n_dep:])

    kw["in_specs"] = list(kw["in_specs"]) + [pl.BlockSpec(memory_space=pl.ANY)] * n_dep
    call = pl.pallas_call(wrapped, **kw)
    return lambda *args: call(*args, *deps)


def _params(*sem):
    return pltpu.CompilerParams(dimension_semantics=sem, vmem_limit_bytes=VMEM_LIMIT)


def _alibi_slopes():
    n = len(DIL_CONFIGS) * N_HEADS
    s = 2.0 ** (-8.0 * np.arange(1, n + 1) / n)
    return s.reshape(len(DIL_CONFIGS), N_HEADS).astype(np.float32)


def _my_slot():
    return 4 * lax.axis_index("c") + 2 * lax.axis_index("x") + lax.axis_index("y")


def _mm(a, b, *, grid, a_spec, b_spec, out_shape, out_spec, nt=False, name, alias=None, deps=()):
    nk = grid[2]
    dn = (((1,), (1,)), ((), ())) if nt else (((1,), (0,)), ((), ()))
    blk = tuple(s for s in out_spec.block_shape if s is not None)

    def body(*refs):
        a_ref, b_ref = refs[0], refs[1]
        o_ref = refs[3] if alias is not None else refs[2]
        p = lax.dot_general(a_ref[...], b_ref[...], dn, preferred_element_type=F32)
        if nk == 1:
            o_ref[...] = p.astype(o_ref.dtype)
        else:
            acc = refs[-1]
            k = pl.program_id(2)

            @pl.when(k == 0)
            def _():
                acc[...] = p

            @pl.when(k > 0)
            def _():
                acc[...] += p

            @pl.when(k == nk - 1)
            def _():
                o_ref[...] = acc[...].astype(o_ref.dtype)

    in_specs = [a_spec, b_spec]
    args = [a, b]
    aliases = {}
    if alias is not None:
        in_specs.append(pl.BlockSpec(memory_space=pl.ANY))
        args.append(alias)
        aliases = {2: 0}
    return _pcall(
        body, deps=deps, name=name, grid=grid, in_specs=in_specs, out_specs=out_spec, out_shape=out_shape,
        scratch_shapes=[] if nk == 1 else [pltpu.VMEM(blk, F32)],
        input_output_aliases=aliases,
        compiler_params=_params("parallel", "parallel", "arbitrary"),
    )(*args)


def _transpose_cast(x, name, deps=()):
    T = x.shape[0]
    tm = min(512, T)

    def body(x_ref, xb_ref, xt_ref):
        v = x_ref[...]
        xb_ref[...] = v.astype(BF16)
        xt_ref[...] = v.T.astype(BF16)

    return _pcall(
        body, deps=deps, name=name, grid=(T // tm,),
        in_specs=[pl.BlockSpec((tm, D), lambda t: (t, 0))],
        out_specs=[pl.BlockSpec((tm, D), lambda t: (t, 0)), pl.BlockSpec((D, tm), lambda t: (0, t))],
        out_shape=[jax.ShapeDtypeStruct((T, D), BF16), jax.ShapeDtypeStruct((D, T), BF16)],
        compiler_params=_params("parallel"),
    )(x)


def _mm_ln(a, b, xres, gain, bias, hscale, name, deps=()):
    T, K = a.shape
    tm = min(512, T)

    def body(a_ref, b_ref, x_ref, g_ref, bt_ref, y_ref, yb_ref, yt_ref, xh_ref, rs_ref):
        h = jnp.dot(a_ref[...], b_ref[...], preferred_element_type=F32)
        z = ALPHA * x_ref[...] + hscale * h
        mu = jnp.mean(z, axis=-1, keepdims=True)
        zc = z - mu
        var = jnp.mean(zc * zc, axis=-1, keepdims=True)
        rstd = lax.rsqrt(var + LN_EPS)
        xh = zc * rstd
        y = xh * g_ref[...] + bt_ref[...]
        y_ref[...] = y
        yb_ref[...] = y.astype(BF16)
        yt_ref[...] = y.T.astype(BF16)
        xh_ref[...] = xh
        rs_ref[...] = rstd

    row = pl.BlockSpec((tm, D), lambda t: (t, 0))
    vec = pl.BlockSpec((1, D), lambda t: (0, 0))
    return _pcall(
        body, deps=deps, name=name, grid=(T // tm,),
        in_specs=[pl.BlockSpec((tm, K), lambda t: (t, 0)), pl.BlockSpec((K, D), lambda t: (0, 0)), row, vec, vec],
        out_specs=[row, row, pl.BlockSpec((D, tm), lambda t: (0, t)), row, pl.BlockSpec((tm, 1), lambda t: (t, 0))],
        out_shape=[jax.ShapeDtypeStruct((T, D), F32), jax.ShapeDtypeStruct((T, D), BF16),
                   jax.ShapeDtypeStruct((D, T), BF16), jax.ShapeDtypeStruct((T, D), F32),
                   jax.ShapeDtypeStruct((T, 1), F32)],
        compiler_params=_params("parallel"),
    )(a, b, xres, gain, bias)


def _ln_bwd(dys, xhat, rstd, gain, hscale, name):
    T = xhat.shape[0]
    tm = min(512, T)
    n = len(dys)

    def body(*refs):
        dy_refs = refs[:n]
        xh_ref, rs_ref, g_ref, dx_ref, dh_ref, dht_ref, dg_ref, db_ref = refs[n:]
        dy = dy_refs[0][...]
        for r in dy_refs[1:]:
            dy = dy + r[...]
        xh = xh_ref[...]
        dxh = dy * g_ref[...]
        m1 = jnp.mean(dxh, axis=-1, keepdims=True)
        m2 = jnp.mean(dxh * xh, axis=-1, keepdims=True)
        dz = rs_ref[...] * (dxh - m1 - xh * m2)
        dx_ref[...] = ALPHA * dz
        dh = hscale * dz
        dh_ref[...] = dh.astype(BF16)
        dht_ref[...] = dh.T.astype(BF16)
        dg = jnp.sum(dy * xh, axis=0, keepdims=True)
        db = jnp.sum(dy, axis=0, keepdims=True)

        @pl.when(pl.program_id(0) == 0)
        def _():
            dg_ref[...] = dg
            db_ref[...] = db

        @pl.when(pl.program_id(0) > 0)
        def _():
            dg_ref[...] += dg
            db_ref[...] += db

    row = pl.BlockSpec((tm, D), lambda t: (t, 0))
    vec = pl.BlockSpec((1, D), lambda t: (0, 0))
    return _pcall(
        body, name=name, grid=(T // tm,),
        in_specs=[row] * n + [row, pl.BlockSpec((tm, 1), lambda t: (t, 0)), vec],
        out_specs=[row, row, pl.BlockSpec((D, tm), lambda t: (0, t)), vec, vec],
        out_shape=[jax.ShapeDtypeStruct((T, D), F32), jax.ShapeDtypeStruct((T, D), BF16),
                   jax.ShapeDtypeStruct((D, T), BF16), jax.ShapeDtypeStruct((1, D), F32),
                   jax.ShapeDtypeStruct((1, D), F32)],
        compiler_params=_params("arbitrary"),
    )(*dys, xhat, rstd, gain)


def _add2(a, b, name):
    T = a.shape[0]
    tm = min(512, T)

    def body(a_ref, b_ref, o_ref):
        o_ref[...] = a_ref[...] + b_ref[...]

    row = pl.BlockSpec((tm, D), lambda t: (t, 0))
    return _pcall(body, name=name, grid=(T // tm,), in_specs=[row, row], out_specs=row,
                  out_shape=jax.ShapeDtypeStruct((T, D), F32), compiler_params=_params("parallel"))(a, b)


def _loss_head(y, tgt):
    T = y.shape[0]
    tm = min(512, T)

    def body(y_ref, t_ref, dy_ref, l_ref):
        e = y_ref[...] - t_ref[...]
        dy_ref[...] = e * (1.0 / D)
        part = jnp.sum(jnp.sum(e * e, axis=1, keepdims=True), axis=0, keepdims=True) * (0.5 / D)

        @pl.when(pl.program_id(0) == 0)
        def _():
            l_ref[...] = jnp.zeros_like(l_ref)

        l_ref[...] += part

    row = pl.BlockSpec((tm, D), lambda t: (t, 0))
    return _pcall(
        body, name="loss_head", grid=(T // tm,),
        in_specs=[row, row],
        out_specs=[row, pl.BlockSpec((8, LANE), lambda t: (0, 0))],
        out_shape=[jax.ShapeDtypeStruct((T, D), F32), jax.ShapeDtypeStruct((8, LANE), F32)],
        compiler_params=_params("arbitrary"),
    )(y, tgt)


def _sigmoid(v):
    return 0.5 * jnp.tanh(0.5 * v) + 0.5


def _ffn_up(xb, wgu, fp, name, deps=()):
    T = xb.shape[0]
    tm = min(1024, T)

    def body(x_ref, w_ref, dg_ref, du_ref, a_ref):
        x = x_ref[...]
        for s in range(2):
            p = jnp.dot(x, w_ref[s], preferred_element_type=F32)
            g = p[:, :fp]
            u = p[:, fp:]
            sig = _sigmoid(g)
            q = g * sig
            cols = slice(s * fp, (s + 1) * fp)
            dg_ref[:, cols] = (sig * (1.0 + g - q) * u).astype(BF16)
            du_ref[:, cols] = q.astype(BF16)
            a_ref[:, cols] = (q * u).astype(BF16)

    out = pl.BlockSpec((tm, 2 * fp), lambda t, j: (t, j))
    shp = jax.ShapeDtypeStruct((T, N_DEV * fp), BF16)
    return _pcall(
        body, deps=deps, name=name, grid=(T // tm, N_DEV // 2),
        in_specs=[pl.BlockSpec((tm, D), lambda t, j: (t, 0)),
                  pl.BlockSpec((2, D, 2 * fp), lambda t, j: (j, 0, 0))],
        out_specs=[out, out, out], out_shape=[shp, shp, shp],
        compiler_params=_params("parallel", "parallel"),
    )(xb, wgu)


def _ffn_bwd_act(dhb, wd, g, u, fp, name):
    T = dhb.shape[0]
    tm = min(1024, T)

    def body(dh_ref, w_ref, g_ref, u_ref, o_ref):
        da = lax.dot_general(dh_ref[...], w_ref[...], (((1,), (1,)), ((), ())), preferred_element_type=F32)
        dgate = (da * g_ref[...].astype(F32)).astype(BF16)
        dup = (da * u_ref[...].astype(F32)).astype(BF16)
        for s in range(2):
            o_ref[:, 2 * s * fp:(2 * s + 1) * fp] = dgate[:, s * fp:(s + 1) * fp]
            o_ref[:, (2 * s + 1) * fp:(2 * s + 2) * fp] = dup[:, s * fp:(s + 1) * fp]

    blk = pl.BlockSpec((tm, 2 * fp), lambda t, j: (t, j))
    return _pcall(
        body, name=name, grid=(T // tm, N_DEV // 2),
        in_specs=[pl.BlockSpec((tm, D), lambda t, j: (t, 0)), pl.BlockSpec((2 * fp, D), lambda t, j: (j, 0)), blk, blk],
        out_specs=pl.BlockSpec((tm, 4 * fp), lambda t, j: (t, j)),
        out_shape=jax.ShapeDtypeStruct((T, N_DEV * 2 * fp), BF16),
        compiler_params=_params("parallel", "parallel"),
    )(dhb, wd, g, u)


def _ffn_dwd(dht, act, fp, name):
    T = dht.shape[1]
    bm = 512

    def body(a_ref, b_ref, o_ref):
        p = jnp.dot(a_ref[...], b_ref[...], preferred_element_type=F32)
        o_ref[0] = p[:, :fp].astype(BF16)
        o_ref[1] = p[:, fp:].astype(BF16)

    return _pcall(
        body, name=name, grid=(D // bm, N_DEV // 2),
        in_specs=[pl.BlockSpec((bm, T), lambda i, j: (i, 0)), pl.BlockSpec((T, 2 * fp), lambda i, j: (0, j))],
        out_specs=pl.BlockSpec((2, bm, fp), lambda i, j: (j, i, 0)),
        out_shape=jax.ShapeDtypeStruct((N_DEV, D, fp), BF16),
        compiler_params=_params("parallel", "parallel"),
    )(dht, act)


def _ffn_dx(dgu, wgu, fp, name, deps=()):
    T = dgu.shape[0]
    tm = min(512, T)

    def body(a_ref, w_ref, o_ref):
        acc = None
        for j in range(N_DEV):
            p = lax.dot_general(a_ref[:, j * 2 * fp:(j + 1) * 2 * fp], w_ref[j], (((1,), (1,)), ((), ())),
                                preferred_element_type=F32)
            acc = p if acc is None else acc + p
        o_ref[...] = acc

    return _pcall(
        body, deps=deps, name=name, grid=(T // tm,),
        in_specs=[pl.BlockSpec((tm, N_DEV * 2 * fp), lambda t: (t, 0)),
                  pl.BlockSpec((N_DEV, D, 2 * fp), lambda t: (0, 0, 0))],
        out_specs=pl.BlockSpec((tm, D), lambda t: (t, 0)),
        out_shape=jax.ShapeDtypeStruct((T, D), F32),
        compiler_params=_params("parallel"),
    )(dgu, wgu)


def _slots_to_columns(w, name):
    _, R, C = w.shape
    tr = min(512, R)

    def body(i_ref, o_ref):
        o_ref[...] = i_ref[...]

    return _pcall(
        body, name=name, grid=(N_DEV, R // tr),
        in_specs=[pl.BlockSpec((None, tr, C), lambda j, i: (4 * (j % 2) + j // 2, i, 0))],
        out_specs=pl.BlockSpec((tr, C), lambda j, i: (i, j)),
        out_shape=jax.ShapeDtypeStruct((R, N_DEV * C), w.dtype),
        compiler_params=_params("parallel", "parallel"),
    )(w)


POOL_PAD = 16
POOL_CHUNK = 512


def _pool_window(v, transpose, name):
    T = v.shape[0]
    ch = min(POOL_CHUNK, T)
    ext = ch + 2 * POOL_PAD
    gd = POOL_GROUP_DIM

    def body(v_ref, o_ref, ot_ref, pad_ref):
        pad_ref[0:POOL_PAD, :] = jnp.zeros((POOL_PAD, gd), F32)
        pad_ref[POOL_PAD + T:POOL_PAD + T + POOL_PAD, :] = jnp.zeros((POOL_PAD, gd), F32)
        for gi, hw in enumerate(POOL_HALF):
            @pl.when(pl.program_id(0) == gi)
            def _(hw=hw):
                def count(t):
                    return (jnp.minimum(t + hw, T) - jnp.maximum(t - hw, 0)).astype(F32)

                if transpose:
                    t_all = lax.broadcasted_iota(jnp.int32, (T, gd), 0)
                    pad_ref[POOL_PAD:POOL_PAD + T, :] = v_ref[...] / count(t_all)
                else:
                    pad_ref[POOL_PAD:POOL_PAD + T, :] = v_ref[...]
                shift = hw if transpose else hw - 1
                for c in range(T // ch):
                    e = pad_ref[c * ch:c * ch + ext, :]
                    step = 1
                    while step < 2 * hw:
                        e = e + pltpu.roll(e, step, 0)
                        step *= 2
                    if shift:
                        e = pltpu.roll(e, ext - shift, 0)
                    s = e[POOL_PAD:POOL_PAD + ch, :]
                    center = v_ref[c * ch:(c + 1) * ch, :]
                    if transpose:
                        res = s - center
                    else:
                        t_idx = c * ch + lax.broadcasted_iota(jnp.int32, (ch, gd), 0)
                        res = s / count(t_idx) - center
                    o_ref[c * ch:(c + 1) * ch, :] = res.astype(BF16)
                    ot_ref[:, c * ch:(c + 1) * ch] = res.T.astype(BF16)

    return _pcall(
        body, name=name, grid=(N_POOL_GROUPS,),
        in_specs=[pl.BlockSpec((T, gd), lambda g: (0, g))],
        out_specs=[pl.BlockSpec((T, gd), lambda g: (0, g)), pl.BlockSpec((gd, T), lambda g: (g, 0))],
        out_shape=[jax.ShapeDtypeStruct((T, D), BF16), jax.ShapeDtypeStruct((D, T), BF16)],
        scratch_shapes=[pltpu.VMEM((T + 2 * POOL_PAD, gd), F32)],
        compiler_params=_params("arbitrary"),
    )(v)


def _pool_group(mixedb, wgroup, scale):
    T = mixedb.shape[0]
    tm = min(1024, T)
    gd = POOL_GROUP_DIM

    def body(a_ref, w_ref, s_ref, y_ref, ys_ref, yst_ref):
        y = jnp.dot(a_ref[...], w_ref[...], preferred_element_type=F32)
        ys = y * s_ref[...]
        y_ref[...] = y
        ys_ref[...] = ys.astype(BF16)
        yst_ref[...] = ys.T.astype(BF16)

    blk = pl.BlockSpec((tm, gd), lambda g, t: (t, g))
    return _pcall(
        body, name="pool_group", grid=(N_POOL_GROUPS, T // tm),
        in_specs=[blk, pl.BlockSpec((None, gd, gd), lambda g, t: (g, 0, 0)), pl.BlockSpec((1, gd), lambda g, t: (0, g))],
        out_specs=[blk, blk, pl.BlockSpec((gd, tm), lambda g, t: (g, t))],
        out_shape=[jax.ShapeDtypeStruct((T, D), F32), jax.ShapeDtypeStruct((T, D), BF16),
                   jax.ShapeDtypeStruct((D, T), BF16)],
        compiler_params=_params("parallel", "parallel"),
    )(mixedb, wgroup, scale)


def _pool_bwd_out(dmb, w_out, y, scale):
    T = dmb.shape[0]
    tm = min(512, T)

    def body(a_ref, w_ref, y_ref, s_ref, dy_ref, ds_ref):
        dys = lax.dot_general(a_ref[...], w_ref[...], (((1,), (1,)), ((), ())), preferred_element_type=F32)
        dy_ref[...] = (dys * s_ref[...]).astype(BF16)
        part = jnp.sum(dys * y_ref[...], axis=0, keepdims=True)

        @pl.when(pl.program_id(0) == 0)
        def _():
            ds_ref[...] = part

        @pl.when(pl.program_id(0) > 0)
        def _():
            ds_ref[...] += part

    row = pl.BlockSpec((tm, D), lambda t: (t, 0))
    vec = pl.BlockSpec((1, D), lambda t: (0, 0))
    return _pcall(
        body, name="pool_bwd_out", grid=(T // tm,),
        in_specs=[row, pl.BlockSpec((D, D), lambda t: (0, 0)), row, vec],
        out_specs=[row, vec],
        out_shape=[jax.ShapeDtypeStruct((T, D), BF16), jax.ShapeDtypeStruct((1, D), F32)],
        compiler_params=_params("arbitrary"),
    )(dmb, w_out, y, scale)


def _attn_masks(n, L, d):
    w = ATTN_BLOCK + 2 * ATTN_HALO
    a = lax.broadcasted_iota(jnp.int32, (ATTN_BLOCK, w), 0)
    c = lax.broadcasted_iota(jnp.int32, (ATTN_BLOCK, w), 1)
    rel = c - ATTN_HALO - a
    j = n * ATTN_BLOCK - ATTN_HALO + c
    valid = (jnp.abs(rel) <= ATTN_HALO) & (j >= 0) & (j < L)
    dist = (d * jnp.abs(rel)).astype(F32)
    return valid, dist


def _lane_col(st, idx):
    lane = lax.broadcasted_iota(jnp.int32, st.shape, 1)
    return jnp.sum(jnp.where(lane == idx, st, 0.0), axis=1, keepdims=True)


def _window_specs(nb, d, col, width):
    last = 2 * d * nb - 1

    def prev(r, n):
        return (jnp.maximum(2 * (r * nb + n) - 1, 0), col)

    def cur(r, n):
        return (r * nb + n, col)

    def nxt(r, n):
        return (jnp.minimum(2 * (r * nb + n) + 2, last), col)

    return [pl.BlockSpec((ATTN_HALO, width), prev), pl.BlockSpec((ATTN_BLOCK, width), cur),
            pl.BlockSpec((ATTN_HALO, width), nxt)]


def _attn_fwd(qkv_g, d, slopes, name):
    T = qkv_g.shape[0]
    L = T // d
    nb = L // ATTN_BLOCK

    def body(q_ref, kp_ref, kc_ref, kn_ref, vp_ref, vc_ref, vn_ref, o_ref, lse_ref):
        valid, dist = _attn_masks(pl.program_id(1), L, d)
        lane = lax.broadcasted_iota(jnp.int32, (ATTN_BLOCK, LANE), 1)
        first = lane < HEAD_DIM
        head_mask = [jnp.where(first, 1.0, 0.0).astype(BF16), jnp.where(first, 0.0, 1.0).astype(BF16)]
        lse_acc = jnp.zeros((ATTN_BLOCK, LANE), F32)
        for hp in range(N_HEADS // 2):
            cs = slice(hp * LANE, (hp + 1) * LANE)
            q2 = q_ref[:, cs]
            k2 = jnp.concatenate([kp_ref[:, cs], kc_ref[:, cs], kn_ref[:, cs]], axis=0)
            v2 = jnp.concatenate([vp_ref[:, cs], vc_ref[:, cs], vn_ref[:, cs]], axis=0)
            outs = []
            for hh in range(2):
                h = 2 * hp + hh
                qh = q2 * head_mask[hh]
                s = lax.dot_general(qh, k2, (((1,), (1,)), ((), ())), preferred_element_type=F32)
                s = s * (HEAD_DIM ** -0.5) - float(slopes[h]) * dist
                s = jnp.where(valid, s, MASK_VALUE)
                m = jnp.max(s, axis=1, keepdims=True)
                p = jnp.exp(s - m)
                l = jnp.sum(p, axis=1, keepdims=True)
                o = jnp.dot(p.astype(BF16), v2, preferred_element_type=F32) / l
                outs.append(o)
                lse_acc = jnp.where(lane == h, m + jnp.log(l), lse_acc)
            o_ref[:, cs] = jnp.where(first, outs[0], outs[1])
        lse_ref[...] = lse_acc

    specs = ([pl.BlockSpec((ATTN_BLOCK, D), lambda r, n: (r * nb + n, 0))]
             + _window_specs(nb, d, 1, D) + _window_specs(nb, d, 2, D))
    row = lambda w: pl.BlockSpec((ATTN_BLOCK, w), lambda r, n: (r * nb + n, 0))
    return _pcall(
        body, name=name, grid=(d, nb), in_specs=specs,
        out_specs=[row(D), row(LANE)],
        out_shape=[jax.ShapeDtypeStruct((T, D), F32), jax.ShapeDtypeStruct((T, LANE), F32)],
        compiler_params=_params("parallel", "parallel"),
    )(*([qkv_g] * 7))


def _stage(scr3, val):
    for c in range(val.shape[1] // LANE):
        scr3[c] = val[:, c * LANE:(c + 1) * LANE]


def _unstage(scr3):
    return jnp.concatenate([scr3[c] for c in range(scr3.shape[0])], axis=1)


def _gather_rows(scr3, r, n, d):
    return jnp.concatenate([scr3[c, pl.ds(r, n, stride=d), :] for c in range(scr3.shape[0])], axis=1)


def _scatter_rows(scr3, r, n, d, val):
    for c in range(scr3.shape[0]):
        scr3[c, pl.ds(r, n, stride=d), :] = val[:, c * LANE:(c + 1) * LANE]


def _attn_combine(os_, lses, dils):
    T = os_[0].shape[0]
    tm = min(256, T)
    ng = len(os_)
    n_scr = sum(1 for d in dils if d > 1)

    def body(*refs):
        in_o = refs[:ng]
        in_l = refs[ng:2 * ng]
        o32_ref, ob_ref, ot_ref, lt_ref = refs[2 * ng:2 * ng + 4]
        scr = refs[2 * ng + 4:]
        o_chunk, l_refs, si = [], [], 0
        for g, d in enumerate(dils):
            if d == 1:
                o_chunk.append(lambda hp, g=g: in_o[g][:, hp * LANE:(hp + 1) * LANE])
                l_refs.append(in_l[g])
                continue
            so, sl = scr[2 * si], scr[2 * si + 1]
            si += 1
            for r in range(d):
                _scatter_rows(so, r, tm // d, d, in_o[g][r])
                sl[pl.ds(r, tm // d, stride=d), :] = in_l[g][r]
            o_chunk.append(lambda hp, so=so: so[hp])
            l_refs.append(sl)
        ls = [r[...] for r in l_refs]
        m = ls[0]
        for l in ls[1:]:
            m = jnp.maximum(m, l)
        tot = jnp.exp(ls[0] - m)
        for l in ls[1:]:
            tot = tot + jnp.exp(l - m)
        lt = m + jnp.log(tot)
        lt_ref[...] = lt
        ws = [jnp.exp(l - lt) for l in ls]
        lane = lax.broadcasted_iota(jnp.int32, (tm, LANE), 1)
        first = lane < HEAD_DIM
        for hp in range(N_HEADS // 2):
            cs = slice(hp * LANE, (hp + 1) * LANE)
            acc = jnp.zeros((tm, LANE), F32)
            for g in range(ng):
                wt = jnp.where(first, _lane_col(ws[g], 2 * hp), _lane_col(ws[g], 2 * hp + 1))
                acc = acc + wt * o_chunk[g](hp)
            o32_ref[:, cs] = acc
            ob_ref[:, cs] = acc.astype(BF16)
        ot_ref[...] = o32_ref[...].T.astype(BF16)

    row = pl.BlockSpec((tm, D), lambda t: (t, 0))
    st = pl.BlockSpec((tm, LANE), lambda t: (t, 0))

    def sub_spec(d, w):
        return pl.BlockSpec((tm, w), lambda t: (t, 0)) if d == 1 else pl.BlockSpec((d, tm // d, w), lambda t: (0, t, 0))

    def sub_view(a, d):
        return a if d == 1 else a.reshape(d, T // d, a.shape[1])

    return _pcall(
        body, name="attn_combine", grid=(T // tm,),
        in_specs=[sub_spec(d, D) for d in dils] + [sub_spec(d, LANE) for d in dils],
        out_specs=[row, row, pl.BlockSpec((D, tm), lambda t: (0, t)), st],
        out_shape=[jax.ShapeDtypeStruct((T, D), F32), jax.ShapeDtypeStruct((T, D), BF16),
                   jax.ShapeDtypeStruct((D, T), BF16), jax.ShapeDtypeStruct((T, LANE), F32)],
        scratch_shapes=[pltpu.VMEM(s, F32) for _ in range(n_scr) for s in ((D // LANE, tm, LANE), (tm, LANE))],
        compiler_params=_params("parallel"),
    )(*[sub_view(a, d) for a, d in zip(os_, dils)], *[sub_view(a, d) for a, d in zip(lses, dils)])


def _attn_bwd_prep(dmb, w_out, o32, lse_tot, dils):
    T = dmb.shape[0]
    tm = min(512, T)
    ng = len(dils)

    def body(a_ref, w_ref, o_ref, l_ref, *rest):
        do_refs, st_refs = rest[:ng], rest[ng:2 * ng]
        do_scr, st_scr = rest[2 * ng:]
        do = lax.dot_general(a_ref[...], w_ref[...], (((1,), (1,)), ((), ())), preferred_element_type=F32)
        _stage(do_scr, do)
        prod = do * o_ref[...]
        lane = lax.broadcasted_iota(jnp.int32, (tm, LANE), 1)
        first = lane < HEAD_DIM
        st = jnp.where(lane < N_HEADS, l_ref[...], 0.0)
        for hp in range(N_HEADS // 2):
            pr = prod[:, hp * LANE:(hp + 1) * LANE]
            d0 = jnp.sum(jnp.where(first, pr, 0.0), axis=1, keepdims=True)
            d1 = jnp.sum(jnp.where(first, 0.0, pr), axis=1, keepdims=True)
            st = jnp.where(lane == N_HEADS + 2 * hp, d0, st)
            st = jnp.where(lane == N_HEADS + 2 * hp + 1, d1, st)
        st_scr[...] = st
        for g, d in enumerate(dils):
            if d == 1:
                do_refs[g][...] = do.astype(BF16)
                st_refs[g][...] = st
                continue
            for r in range(d):
                do_refs[g][r] = _gather_rows(do_scr, r, tm // d, d).astype(BF16)
                st_refs[g][r] = st_scr[pl.ds(r, tm // d, stride=d), :]

    row = pl.BlockSpec((tm, D), lambda t: (t, 0))
    stb = pl.BlockSpec((tm, LANE), lambda t: (t, 0))

    def sub_spec(d, w):
        return pl.BlockSpec((tm, w), lambda t: (t, 0)) if d == 1 else pl.BlockSpec((d, tm // d, w), lambda t: (0, t, 0))

    def sub_shape(d, w, dt):
        return jax.ShapeDtypeStruct((T, w) if d == 1 else (d, T // d, w), dt)

    outs = _pcall(
        body, name="attn_bwd_prep", grid=(T // tm,),
        in_specs=[row, pl.BlockSpec((D, D), lambda t: (0, 0)), row, stb],
        out_specs=[sub_spec(d, D) for d in dils] + [sub_spec(d, LANE) for d in dils],
        out_shape=[sub_shape(d, D, BF16) for d in dils] + [sub_shape(d, LANE, F32) for d in dils],
        scratch_shapes=[pltpu.VMEM((D // LANE, tm, LANE), F32), pltpu.VMEM((tm, LANE), F32)],
        compiler_params=_params("parallel"),
    )(dmb, w_out, o32, lse_tot)
    return ([o.reshape(T, D) for o in outs[:ng]], [o.reshape(T, LANE) for o in outs[ng:]])


def _attn_bwd(qkv_g, do_g, st_g, d, slopes, name):
    T = qkv_g.shape[0]
    L = T // d
    nb = L // ATTN_BLOCK
    scale = HEAD_DIM ** -0.5
    nt = (((1,), (1,)), ((), ()))

    def body(qp_ref, qc_ref, qn_ref, kp_ref, kc_ref, kn_ref, vp_ref, vc_ref, vn_ref,
             dp_ref, dc_ref, dn_ref, sp_ref, sc_ref, sn_ref, o_ref):
        valid, dist = _attn_masks(pl.program_id(1), L, d)
        lane = lax.broadcasted_iota(jnp.int32, (ATTN_BLOCK, LANE), 1)
        first = lane < HEAD_DIM
        head_mask = [jnp.where(first, 1.0, 0.0).astype(BF16), jnp.where(first, 0.0, 1.0).astype(BF16)]
        stc = sc_ref[...]
        stw_t = jnp.concatenate([sp_ref[...], stc, sn_ref[...]], axis=0).T
        for hp in range(N_HEADS // 2):
            cs = slice(hp * LANE, (hp + 1) * LANE)
            cat = lambda a, b, c: jnp.concatenate([a[:, cs], b[:, cs], c[:, cs]], axis=0)
            q2, k2, v2, do2 = qc_ref[:, cs], kc_ref[:, cs], vc_ref[:, cs], dc_ref[:, cs]
            qw, kw, vw, dow = cat(qp_ref, qc_ref, qn_ref), cat(kp_ref, kc_ref, kn_ref), cat(vp_ref, vc_ref, vn_ref), cat(dp_ref, dc_ref, dn_ref)
            dqs, dks, dvs = [], [], []
            for hh in range(2):
                h = 2 * hp + hh
                pick = lambda t, hh=hh: t * head_mask[hh]
                bias = float(slopes[h]) * dist
                s = lax.dot_general(pick(q2), kw, nt, preferred_element_type=F32) * scale - bias
                s = jnp.where(valid, s, MASK_VALUE)
                p = jnp.exp(s - _lane_col(stc, h))
                dp = lax.dot_general(pick(do2), vw, nt, preferred_element_type=F32)
                ds = p * (dp - _lane_col(stc, N_HEADS + h)) * scale
                dqs.append(jnp.dot(ds.astype(BF16), kw, preferred_element_type=F32))
                st_ = lax.dot_general(pick(k2), qw, nt, preferred_element_type=F32) * scale - bias
                st_ = jnp.where(valid, st_, MASK_VALUE)
                pt = jnp.exp(st_ - stw_t[h:h + 1, :])
                dvs.append(jnp.dot(pt.astype(BF16), dow, preferred_element_type=F32))
                dpt = lax.dot_general(pick(v2), dow, nt, preferred_element_type=F32)
                dst = pt * (dpt - stw_t[N_HEADS + h:N_HEADS + h + 1, :]) * scale
                dks.append(jnp.dot(dst.astype(BF16), qw, preferred_element_type=F32))
            o_ref[:, hp * LANE:(hp + 1) * LANE] = jnp.where(first, dqs[0], dqs[1]).astype(BF16)
            o_ref[:, D + hp * LANE:D + (hp + 1) * LANE] = jnp.where(first, dks[0], dks[1]).astype(BF16)
            o_ref[:, 2 * D + hp * LANE:2 * D + (hp + 1) * LANE] = jnp.where(first, dvs[0], dvs[1]).astype(BF16)

    specs = (_window_specs(nb, d, 0, D) + _window_specs(nb, d, 1, D) + _window_specs(nb, d, 2, D)
             + _window_specs(nb, d, 0, D) + _window_specs(nb, d, 0, LANE))
    return _pcall(
        body, name=name, grid=(d, nb), in_specs=specs,
        out_specs=pl.BlockSpec((ATTN_BLOCK, 3 * D), lambda r, n: (r * nb + n, 0)),
        out_shape=jax.ShapeDtypeStruct((T, 3 * D), BF16),
        compiler_params=_params("parallel", "parallel"),
    )(*([qkv_g] * 9), *([do_g] * 3), *([st_g] * 3))


QKV_TILE = QKV_SHARD // 3


def _qkv_tile_block(nn):
    dev = nn // 3
    return 4 * (dev % 2) + dev // 2, nn % 3


def _attn_qkv_group(xb, wqkv, gi, d, name, deps=()):
    T = xb.shape[0]
    tq = min(1024, T)
    nsub = tq // d
    tn = 768
    ntile = 3 * D // tn

    def body(x_ref, w_ref, o_ref, *scr):
        p = jnp.dot(x_ref[...], w_ref[...], preferred_element_type=F32)
        if d == 1:
            o_ref[...] = p.astype(BF16)
        else:
            _stage(scr[0], p)
            for r in range(d):
                o_ref[r] = _gather_rows(scr[0], r, nsub, d).astype(BF16)

    if d == 1:
        out_spec = pl.BlockSpec((tq, tn), lambda n, t: (t, n))
        out_shape = jax.ShapeDtypeStruct((T, 3 * D), BF16)
    else:
        out_spec = pl.BlockSpec((d, nsub, tn), lambda n, t: (0, t, n))
        out_shape = jax.ShapeDtypeStruct((d, T // d, 3 * D), BF16)
    out = _pcall(
        body, deps=deps, name=name, grid=(ntile, T // tq),
        in_specs=[pl.BlockSpec((tq, D), lambda n, t: (t, 0)),
                  pl.BlockSpec((D, tn), lambda n, t: (0, ntile * gi + n))],
        out_specs=out_spec, out_shape=out_shape,
        scratch_shapes=[] if d == 1 else [pltpu.VMEM((tn // LANE, tq, LANE), F32)],
        compiler_params=_params("parallel", "parallel"),
    )(xb, wqkv)
    return out.reshape(T, 3 * D)


def _attn_dx_group(dqkv_g, wqkv, gi, d, name, deps=()):
    T = dqkv_g.shape[0]
    tq = min(512, T)
    nsub = tq // d

    def body(a_ref, w_ref, o_ref, *stage):
        a = a_ref[...]
        if d > 1:
            a = a.reshape(tq, 3 * D)
        p = lax.dot_general(a, w_ref[...], (((1,), (1,)), ((), ())), preferred_element_type=F32)
        if d == 1:
            o_ref[...] = p
        else:
            for r in range(d):
                _scatter_rows(stage[0], r, nsub, d, p[r * nsub:(r + 1) * nsub, :])
            o_ref[...] = _unstage(stage[0])

    if d == 1:
        a_spec = pl.BlockSpec((tq, 3 * D), lambda t: (t, 0))
        a = dqkv_g
    else:
        a_spec = pl.BlockSpec((d, nsub, 3 * D), lambda t: (0, t, 0))
        a = dqkv_g.reshape(d, T // d, 3 * D)
    return _pcall(
        body, deps=deps, name=name, grid=(T // tq,),
        in_specs=[a_spec, pl.BlockSpec((D, 3 * D), lambda t: (0, gi))],
        out_specs=pl.BlockSpec((tq, D), lambda t: (t, 0)),
        out_shape=jax.ShapeDtypeStruct((T, D), F32),
        scratch_shapes=[] if d == 1 else [pltpu.VMEM((D // LANE, tq, LANE), F32)],
        compiler_params=_params("parallel"),
    )(a, wqkv)


def _attn_dw_group(xt, dqkv_g, gi, d, prev, name):
    T = dqkv_g.shape[0]
    L = T // d
    bm = 512
    ntile = 3 * D // QKV_TILE

    def body(*refs):
        a_ref, b_ref = refs[0], refs[1]
        o_ref, cat = refs[-2], refs[-1]
        if d == 1:
            a = a_ref[...]
        else:
            @pl.when(pl.program_id(1) == 0)
            def _():
                for r in range(d):
                    cat[:, r * L:(r + 1) * L] = a_ref[r]
            a = cat[...]
        o_ref[...] = jnp.dot(a, b_ref[...], preferred_element_type=F32).astype(BF16)

    def out_map(i, n):
        slot, sub = _qkv_tile_block(ntile * gi + n)
        return slot, i, sub

    a_spec = (pl.BlockSpec((bm, T), lambda i, n: (i, 0)) if d == 1
              else pl.BlockSpec((d, bm, L), lambda i, n: (0, i, 0)))
    in_specs = [a_spec, pl.BlockSpec((T, QKV_TILE), lambda i, n: (0, n))]
    args = [xt, dqkv_g]
    aliases = {}
    if prev is not None:
        in_specs.append(ANY_SPEC)
        args.append(prev)
        aliases = {2: 0}
    return _pcall(
        body, name=name, grid=(D // bm, ntile), in_specs=in_specs,
        out_specs=pl.BlockSpec((None, bm, QKV_TILE), out_map),
        out_shape=jax.ShapeDtypeStruct((N_DEV, D, QKV_SHARD), BF16),
        scratch_shapes=[pltpu.VMEM((bm, T), BF16)],
        input_output_aliases=aliases,
        compiler_params=_params("parallel", "arbitrary"),
    )(*args)


def _transpose_sub(x, d, name):
    T = x.shape[0]
    tm = LANE * d

    def body(x_ref, o_ref, scr):
        for c in range(D // LANE):
            scr[...] = x_ref[:, c * LANE:(c + 1) * LANE]
            for r in range(d):
                o_ref[r, c * LANE:(c + 1) * LANE, :] = scr[pl.ds(r, LANE, stride=d), :].T.astype(BF16)

    return _pcall(
        body, name=name, grid=(T // tm,),
        in_specs=[pl.BlockSpec((tm, D), lambda t: (t, 0))],
        out_specs=pl.BlockSpec((d, D, LANE), lambda t: (0, 0, t)),
        out_shape=jax.ShapeDtypeStruct((d, D, T // d), BF16),
        scratch_shapes=[pltpu.VMEM((tm, LANE), F32)],
        compiler_params=_params("parallel"),
    )(x)


HBM_SPEC = pl.BlockSpec(memory_space=pltpu.HBM)
SEM_SPEC = pl.BlockSpec(memory_space=pltpu.SEMAPHORE)
ANY_SPEC = pl.BlockSpec(memory_space=pl.ANY)
DATAFLOW = pltpu.SideEffectType.DATAFLOW_SIDE_EFFECTING


def _me_and_peers():
    x, y, c = lax.axis_index("x"), lax.axis_index("y"), lax.axis_index("c")
    return (x, y, c), [(x, y, 1 - c), (1 - x, y, c), (x, 1 - y, c), (1 - x, 1 - y, c)]


def _slot(px, py, pc):
    return 4 * pc + 2 * px + py


def _split_start(srcs, lands, after, start_copies, n_sem, name):
    n = len(srcs)
    n_after = len(after)

    def body(*refs):
        src_refs, land_refs = refs[:n], refs[n:2 * n]
        send_sems, recv_sems = refs[2 * n + n_after], refs[2 * n + n_after + 1]
        token = refs[-1]
        start_copies(src_refs, land_refs, send_sems, recv_sems)
        token[...] = jnp.zeros_like(token)

    outs = _pcall(
        body, name=name,
        in_specs=[HBM_SPEC] * (2 * n) + [ANY_SPEC] * n_after,
        out_shape=(pltpu.SemaphoreType.DMA(n_sem), pltpu.SemaphoreType.DMA(n_sem),
                   *[pltpu.HBM(a.shape, a.dtype) for a in srcs], *[pltpu.HBM(a.shape, a.dtype) for a in lands],
                   jax.ShapeDtypeStruct((8, LANE), F32)),
        out_specs=(SEM_SPEC, SEM_SPEC, *[HBM_SPEC] * (2 * n), pl.BlockSpec(memory_space=pltpu.VMEM)),
        input_output_aliases={i: 2 + i for i in range(2 * n)},
        compiler_params=pltpu.CompilerParams(has_side_effects=DATAFLOW),
    )(*[pltpu.with_memory_space_constraint(a, pltpu.HBM) for a in srcs],
      *[pltpu.with_memory_space_constraint(a, pltpu.HBM) for a in lands], *after)
    return outs[0], outs[1], list(outs[2:2 + n]), list(outs[2 + n:2 + 2 * n]), outs[-1]


def _split_wait(handle, after, wait_copies, name):
    send_sems, recv_sems, srcs, lands, _ = handle
    n = len(srcs)

    def body(*refs):
        src_refs, land_refs = refs[:n], refs[n:2 * n]
        wait_copies(src_refs, land_refs, refs[2 * n], refs[2 * n + 1])

    outs = _pcall(
        body, name=name,
        in_specs=[HBM_SPEC] * (2 * n) + [SEM_SPEC, SEM_SPEC] + [ANY_SPEC] * len(after),
        out_shape=tuple(pltpu.HBM(a.shape, a.dtype) for a in srcs + lands),
        out_specs=tuple([HBM_SPEC] * (2 * n)),
        input_output_aliases={i: i for i in range(2 * n)},
        compiler_params=pltpu.CompilerParams(has_side_effects=DATAFLOW),
    )(*srcs, *lands, send_sems, recv_sems, *after)
    return list(outs[:n]), list(outs[n:])


def _ag_copies(src_refs, land_refs, send_sems, recv_sems, received):
    me, peers = _me_and_peers()
    cps = []
    for i in range(len(src_refs)):
        for k, to in enumerate(peers):
            cps.append(pltpu.make_async_remote_copy(
                src_ref=src_refs[i], dst_ref=land_refs[i].at[_slot(*(to if received else me))],
                send_sem=send_sems.at[4 * i + k], recv_sem=recv_sems.at[4 * i + k], device_id=to,
                device_id_type=MESH))
    return cps


def _ag_start(shards, after, name):
    lands = [lax.empty((N_DEV,) + a.shape, a.dtype) for a in shards]

    def start(src_refs, land_refs, send_sems, recv_sems):
        for cp in _ag_copies(src_refs, land_refs, send_sems, recv_sems, False):
            cp.start()

    return _split_start(shards, lands, after, start, (4 * len(shards),), name)


def _ag_finish(handle, after, name):
    def wait(src_refs, land_refs, send_sems, recv_sems):
        for cp in _ag_copies(src_refs, land_refs, send_sems, recv_sems, True):
            cp.wait_send()
            cp.wait_recv()

    shards, lands = _split_wait(handle, after, wait, name + "_wait")
    n = len(shards)

    def body(*refs):
        src_refs, out_refs = refs[:n], refs[2 * n:3 * n]
        send_sems, recv_sems, local_sems = refs[3 * n:3 * n + 3]
        bounce = refs[3 * n + 3:]
        me, peers = _me_and_peers()
        loads = [pltpu.make_async_copy(src_refs[i], bounce[i], local_sems.at[i]) for i in range(n)]
        mine = [pltpu.make_async_copy(bounce[i], out_refs[i].at[_slot(*me)], local_sems.at[i]) for i in range(n)]
        for cp in loads:
            cp.start()
        cps = []
        for i in range(n):
            for j, chip in enumerate(peers[1:]):
                blk = out_refs[i].at[_slot(*chip)]
                cps.append(pltpu.make_async_remote_copy(
                    src_ref=blk, dst_ref=blk, send_sem=send_sems.at[i, j], recv_sem=recv_sems.at[i, j],
                    device_id=peers[0], device_id_type=MESH))
        for cp in cps:
            cp.start()
        for ld, st in zip(loads, mine):
            ld.wait()
            st.start()
        for cp in cps:
            cp.wait()
        for cp in mine:
            cp.wait()

    outs = _pcall(
        body, name=name + "_pass",
        in_specs=[ANY_SPEC] * (2 * n), out_specs=[ANY_SPEC] * n,
        out_shape=[jax.ShapeDtypeStruct(a.shape, a.dtype) for a in lands],
        input_output_aliases={n + i: i for i in range(n)},
        scratch_shapes=[pltpu.SemaphoreType.DMA((n, 3)), pltpu.SemaphoreType.DMA((n, 3)),
                        pltpu.SemaphoreType.DMA((n,))] + [pltpu.VMEM(a.shape, a.dtype) for a in shards],
        compiler_params=pltpu.CompilerParams(vmem_limit_bytes=VMEM_LIMIT),
    )(*shards, *lands)
    return list(outs)


def _small_all_gather(v, name):
    R, C = v.shape

    def body(x_ref, out_ref, sum_ref, send_sems, recv_sems, local_sem):
        x, y, c = lax.axis_index("x"), lax.axis_index("y"), lax.axis_index("c")
        me, sibling = (x, y, c), (x, y, 1 - c)
        chips = [(1 - x, y), (x, 1 - y), (1 - x, 1 - y)]

        def rows(px, py, pc):
            return out_ref.at[4 * pc + 2 * px + py]

        def copy(k, block, to, src=None):
            return pltpu.make_async_remote_copy(
                src_ref=rows(*block) if src is None else src, dst_ref=rows(*block),
                send_sem=send_sems.at[k], recv_sem=recv_sems.at[k],
                device_id=to, device_id_type=MESH)

        mine = pltpu.make_async_copy(x_ref, rows(*me), local_sem)
        mine.start()
        first = [copy(0, me, sibling, src=x_ref)]
        first += [copy(1 + j, me, (*chip, c), src=x_ref) for j, chip in enumerate(chips)]
        for cp in first:
            cp.start()
        passed = [copy(4 + j, (*chip, c), sibling) for j, chip in enumerate(chips)]
        for j, chip in enumerate(chips):
            copy(1 + j, (*chip, c), me).wait_recv()
            passed[j].start()
        copy(0, sibling, me).wait_recv()
        for j, chip in enumerate(chips):
            copy(4 + j, (*chip, 1 - c), me).wait_recv()
        for cp in first + passed:
            cp.wait_send()
        mine.wait()
        acc = out_ref[0]
        for s in range(1, N_DEV):
            acc = acc + out_ref[s]
        sum_ref[...] = acc

    vm = pl.BlockSpec(memory_space=pltpu.VMEM)
    return _pcall(
        body, name=name, in_specs=[vm], out_specs=[vm, vm],
        out_shape=[jax.ShapeDtypeStruct((N_DEV, R, C), v.dtype), jax.ShapeDtypeStruct((R, C), v.dtype)],
        scratch_shapes=[pltpu.SemaphoreType.DMA((7,)), pltpu.SemaphoreType.DMA((7,)), pltpu.SemaphoreType.DMA],
    )(v)


def _rs_sibling(arrs, name):
    n = len(arrs)

    def body(*refs):
        ins, outs = refs[:n], refs[n:2 * n]
        send_sems, recv_sems = refs[2 * n:]
        x, y, c = lax.axis_index("x"), lax.axis_index("y"), lax.axis_index("c")
        cps = [pltpu.make_async_remote_copy(
            src_ref=ins[i].at[pl.ds(4 * (1 - c), 4)], dst_ref=outs[i],
            send_sem=send_sems.at[i], recv_sem=recv_sems.at[i],
            device_id=(x, y, 1 - c), device_id_type=MESH) for i in range(n)]
        for cp in cps:
            cp.start()
        for cp in cps:
            cp.wait()

    hbm = pl.BlockSpec(memory_space=pl.ANY)
    return _pcall(
        body, name=name, in_specs=[hbm] * n, out_specs=[hbm] * n,
        out_shape=[jax.ShapeDtypeStruct((4,) + a.shape[1:], a.dtype) for a in arrs],
        scratch_shapes=[pltpu.SemaphoreType.DMA((n,)), pltpu.SemaphoreType.DMA((n,))],
    )(*arrs)


def _rs_copies(src_refs, land_refs, send_sems, recv_sems):
    _, peers = _me_and_peers()
    cps = []
    for i in range(len(src_refs)):
        for j, (px, py, pc) in enumerate(peers[1:]):
            cps.append(pltpu.make_async_remote_copy(
                src_ref=src_refs[i].at[2 * px + py], dst_ref=land_refs[i].at[j],
                send_sem=send_sems.at[3 * i + j], recv_sem=recv_sems.at[3 * i + j],
                device_id=(px, py, pc), device_id_type=MESH))
    return cps


def _rs_start(chipsums, after, name):
    lands = [lax.empty((3,) + a.shape[1:], a.dtype) for a in chipsums]

    def start(src_refs, land_refs, send_sems, recv_sems):
        for cp in _rs_copies(src_refs, land_refs, send_sems, recv_sems):
            cp.start()

    return _split_start(chipsums, lands, after, start, (3 * len(chipsums),), name)


def _rs_wait(handle, after, name):
    def wait(src_refs, land_refs, send_sems, recv_sems):
        for cp in _rs_copies(src_refs, land_refs, send_sems, recv_sems):
            cp.wait_send()
            cp.wait_recv()

    return _split_wait(handle, after, wait, name)


def _row_tile(R, C, itemsize=4, budget=2 * 1024 * 1024):
    best = None
    for t in range(16, R + 1, 16):
        if R % t == 0 and t * C * itemsize <= budget:
            best = t
    return best if best is not None else R


def _add_half(arr, recv, c_idx, name):
    _, R, C = arr.shape
    tr = _row_tile(R, C)

    def body(c_ref, a_ref, r_ref, o_ref):
        o_ref[...] = (a_ref[...].astype(F32) + r_ref[...].astype(F32)).astype(o_ref.dtype)

    gs = pltpu.PrefetchScalarGridSpec(
        num_scalar_prefetch=1, grid=(4, R // tr),
        in_specs=[pl.BlockSpec((None, tr, C), lambda q, i, c_ref: (4 * c_ref[0] + q, i, 0)),
                  pl.BlockSpec((None, tr, C), lambda q, i, c_ref: (q, i, 0))],
        out_specs=pl.BlockSpec((None, tr, C), lambda q, i, c_ref: (q, i, 0)))
    return _pcall(body, name=name, grid_spec=gs, out_shape=jax.ShapeDtypeStruct((4, R, C), arr.dtype),
                  compiler_params=_params("parallel", "parallel"))(c_idx, arr, recv)


def _sum_chips(chipsum, recv, q_idx, name, transposed=False):
    _, R, C = chipsum.shape
    tr = min(512, R) if transposed else _row_tile(R, C)

    def body(q_ref, a_ref, r_ref, o_ref):
        acc = a_ref[...].astype(F32)
        for j in range(3):
            acc = acc + r_ref[j].astype(F32)
        o_ref[...] = acc.T if transposed else acc

    gs = pltpu.PrefetchScalarGridSpec(
        num_scalar_prefetch=1, grid=(R // tr,),
        in_specs=[pl.BlockSpec((None, tr, C), lambda i, q_ref: (q_ref[0], i, 0)),
                  pl.BlockSpec((3, tr, C), lambda i, q_ref: (0, i, 0))],
        out_specs=(pl.BlockSpec((C, tr), lambda i, q_ref: (0, i)) if transposed
                   else pl.BlockSpec((tr, C), lambda i, q_ref: (i, 0))))
    return _pcall(body, name=name, grid_spec=gs,
                  out_shape=jax.ShapeDtypeStruct((C, R) if transposed else (R, C), F32),
                  compiler_params=_params("parallel"))(q_idx, chipsum, recv)


def _adamw(w, g, m, v, name):
    shape = w.shape
    C = shape[-1]
    R = int(np.prod(shape[:-1]))
    tr = _row_tile(R, C, budget=1024 * 1024)

    def body(w_ref, g_ref, m_ref, v_ref, d_ref, nm_ref, nv_ref):
        gv = g_ref[...]
        mv = ADAM_B1 * m_ref[...] + (1.0 - ADAM_B1) * gv
        vv = ADAM_B2 * v_ref[...] + (1.0 - ADAM_B2) * jnp.square(gv)
        m_hat = mv / (1.0 - ADAM_B1 ** ADAM_STEP)
        v_hat = vv / (1.0 - ADAM_B2 ** ADAM_STEP)
        d_ref[...] = -ADAM_LR * (m_hat / (jnp.sqrt(v_hat) + ADAM_EPS) + ADAM_WD * w_ref[...])
        nm_ref[...] = mv
        nv_ref[...] = vv

    blk = pl.BlockSpec((tr, C), lambda i: (i, 0))
    shp = jax.ShapeDtypeStruct((R, C), F32)
    outs = _pcall(body, name=name, grid=(R // tr,), in_specs=[blk] * 4, out_specs=[blk] * 3,
                  out_shape=[shp] * 3, compiler_params=_params("parallel"))(
        w.reshape(R, C), g.reshape(R, C), m.reshape(R, C), v.reshape(R, C))
    return tuple(o.reshape(shape) for o in outs)


def _pad_cols(w, width):
    return jnp.pad(w, ((0, 0), (0, width - w.shape[1])))


def _pad_rows(w, height):
    return jnp.pad(w, ((0, height - w.shape[0]), (0, 0)))


def _slot_to_device_order(a):
    s = a.shape
    return a.reshape((2, 4) + s[1:]).swapaxes(0, 1).reshape(s)


def _device_to_slot_order(a):
    s = a.shape
    return a.reshape((4, 2) + s[1:]).swapaxes(0, 1).reshape(s)


def kernel(x, ffn1_w_gate, ffn1_w_up, ffn1_w_down, ffn2_w_gate, ffn2_w_up, ffn2_w_down, ln_gain, ln_bias, pool_w_in, pool_w_group, pool_scale, pool_w_out, attn_w_qkv, attn_w_out, loss_target, m_ffn1_w_gate, m_ffn1_w_up, m_ffn1_w_down, m_ffn2_w_gate, m_ffn2_w_up, m_ffn2_w_down, m_ln_gain, m_ln_bias, m_pool_w_in, m_pool_w_group, m_pool_scale, m_pool_w_out, m_attn_w_qkv, m_attn_w_out, v_ffn1_w_gate, v_ffn1_w_up, v_ffn1_w_down, v_ffn2_w_gate, v_ffn2_w_up, v_ffn2_w_down, v_ln_gain, v_ln_bias, v_pool_w_in, v_pool_w_group, v_pool_scale, v_pool_w_out, v_attn_w_qkv, v_attn_w_out):
    T = x.shape[1]
    fs = ffn1_w_gate.shape[2]
    fp = _round_up(fs, LANE)
    rs = D // N_DEV
    x0 = x[0]
    tgt = loss_target[0]
    slopes = _alibi_slopes()
    c_idx = lax.axis_index("c").astype(jnp.int32).reshape(1)
    q_idx = (2 * lax.axis_index("x") + lax.axis_index("y")).astype(jnp.int32).reshape(1)

    gates = (ffn1_w_gate, ffn2_w_gate)
    ups = (ffn1_w_up, ffn2_w_up)
    downs = (ffn1_w_down, ffn2_w_down)
    ffns = [(i, k) for i in range(DEPTH) for k in range(2)]
    def padded_cols(w, i):
        return _pad_rows(jnp.swapaxes(w, 1, 2)[i], fp).T

    wgu_sh = [jnp.concatenate([padded_cols(gates[k], i), padded_cols(ups[k], i)], axis=1).astype(BF16)
              for i, k in ffns]
    wd_sh = [_pad_rows(downs[k][i], fp).astype(BF16) for i, k in ffns]
    sq_sh = jnp.concatenate([wd_sh[0], pool_w_in[0].astype(BF16), pool_w_out[0].astype(BF16),
                             pool_w_group[0].reshape(rs // 4, D).astype(BF16)], axis=0)
    qkv_sh = attn_w_qkv[0].astype(BF16)
    aout_sh = attn_w_out[0].astype(BF16)
    ln_sh = jnp.concatenate([ln_gain.reshape(DEPTH * 3, rs), ln_bias.reshape(DEPTH * 3, rs),
                             jnp.zeros((4, rs), F32)], axis=0)

    h0 = _ag_start([wgu_sh[0], ln_sh], (), "ag0")
    x0b, x0t = _transpose_cast(x0, "x_cast", deps=(h0[4],))
    wgu0, ln_all = _ag_finish(h0, (x0b,), "ag0")
    h1 = _ag_start([sq_sh], (wgu0,), "ag1")
    ln_all = _slot_to_device_order(ln_all).transpose(1, 0, 2).reshape(16, D)
    gain = lambda i, s: ln_all[3 * i + s][None]
    bias = lambda i, s: ln_all[DEPTH * 3 + 3 * i + s][None]

    def ffn_fwd(xf, xb, wgu, wd, f, i, s, dep_up=(), dep_down=()):
        g, u, act = _ffn_up(xb, wgu, fp, f"ffn_up{f}", deps=dep_up)
        wd = wd(act) if callable(wd) else wd
        y, yb, yt, xh, rstd = _mm_ln(act, wd, xf, gain(i, s), bias(i, s), MACARON, f"ffn_down_ln{f}",
                                     deps=dep_down() if callable(dep_down) else dep_down)
        return (y, yb, yt), dict(g=g, u=u, act=act, xh=xh, rstd=rstd, wgu=wgu, wd=wd)

    pool_w = {}

    def wd0_after(act):
        (sq_all,) = _ag_finish(h1, (act,), "ag1")
        pool_w["h2"] = _ag_start([wgu_sh[1], wd_sh[1]], (sq_all,), "ag2")
        pool_w["pin"] = _slot_to_device_order(sq_all[:, fp:fp + rs, :]).reshape(D, D)
        pool_w["pout"] = _slot_to_device_order(sq_all[:, fp + rs:fp + 2 * rs, :]).reshape(D, D)
        grp = _slot_to_device_order(sq_all[:, fp + 2 * rs:, :])
        pool_w["grp"] = grp.reshape(N_DEV, N_POOL_GROUPS, rs // 4, POOL_GROUP_DIM).transpose(1, 0, 2, 3).reshape(
            N_POOL_GROUPS, POOL_GROUP_DIM, POOL_GROUP_DIM)
        return sq_all[:, :fp, :].reshape(N_DEV * fp, D)

    (a1, a1b, a1t), s_f0 = ffn_fwd(x0, x0b, wgu0, wd0_after, 0, 0, 0, dep_up=(h1[4],),
                                   dep_down=lambda: (pool_w["h2"][4],))
    h2 = pool_w["h2"]
    w_pin, w_pout, w_grp = pool_w["pin"], pool_w["pout"], pool_w["grp"]
    tm = min(512, T)
    row_spec = pl.BlockSpec((tm, D), lambda i, j, k: (i, 0))
    full_w = pl.BlockSpec((D, D), lambda i, j, k: (0, 0))
    u_pool = _mm(a1b, w_pin, grid=(T // tm, 1, 1), a_spec=row_spec, b_spec=full_w,
                 out_shape=jax.ShapeDtypeStruct((T, D), F32), out_spec=row_spec, name="pool_in")
    mixedb, mixedt = _pool_window(u_pool, False, "pool_window")
    y_pool, ysb, yst = _pool_group(mixedb, w_grp, pool_scale)
    a2, a2b, a2t, xh_p, rstd_p = _mm_ln(ysb, w_pout, a1, gain(0, 1), bias(0, 1), 1.0, "pool_out_ln")
    wgu1, wd1 = _ag_finish(h2, (a2,), "ag2")
    h3 = _ag_start([wgu_sh[2], wd_sh[2]], (wgu1,), "ag3")
    (a3, a3b, a3t), s_f1 = ffn_fwd(a2, a2b, wgu1, wd1.reshape(N_DEV * fp, D), 1, 0, 2, dep_up=(h3[4],))
    wgu2, wd2 = _ag_finish(h3, (a3,), "ag3")
    h4 = _ag_start([qkv_sh, aout_sh], (wgu2,), "ag4")
    (b1, b1b, b1t), s_f2 = ffn_fwd(a3, a3b, wgu2, wd2.reshape(N_DEV * fp, D), 2, 1, 0, dep_up=(h4[4],))
    wqkv_all, aout_all = _ag_finish(h4, (b1,), "ag4")
    h5 = _ag_start([wgu_sh[3], wd_sh[3]], (wqkv_all,), "ag5")
    w_aout = _slot_to_device_order(aout_all).reshape(D, D)
    dils = [d for _, d in DIL_CONFIGS]
    wqkv_nat = _slots_to_columns(wqkv_all, "attn_wqkv_cols")
    qkv_gs, o_gs, lse_gs = [], [], []
    for gi, d in enumerate(dils):
        qkv_g = _attn_qkv_group(b1b, wqkv_nat, gi, d, f"attn_qkv{gi}", deps=(h5[4],) if gi == 0 else ())
        o_g, lse_g = _attn_fwd(qkv_g, d, slopes[gi], f"attn_fwd{gi}")
        qkv_gs.append(qkv_g)
        o_gs.append(o_g)
        lse_gs.append(lse_g)
    o32, ob, ot, lse_tot = _attn_combine(o_gs, lse_gs, dils)
    b2, b2b, b2t, xh_a, rstd_a = _mm_ln(ob, w_aout, b1, gain(1, 1), bias(1, 1), 1.0, "attn_out_ln")
    wgu3, wd3 = _ag_finish(h5, (b2,), "ag5")
    (b3, _, _), s_f3 = ffn_fwd(b2, b2b, wgu3, wd3.reshape(N_DEV * fp, D), 3, 1, 2)

    dy, loss_tile = _loss_head(b3, tgt)
    loss = lax.psum(loss_tile[0, 0], AXES)

    bm = min(512, D)
    dgains, dbiases = {}, {}
    rs_pending = []
    gsums = {}

    def rs_finish(after):
        h, tag = rs_pending.pop()
        chips, lands = _rs_wait(h, after, f"rs_{tag}_wait")
        gsums[tag] = [_sum_chips(a, r, q_idx, f"rs_sum_{tag}{i}", transposed=tag.startswith("f"))
                      for i, (a, r) in enumerate(zip(chips, lands))]

    def rs_stage(bufs, tag):
        if rs_pending:
            rs_finish((bufs[-1],))
        recv = _rs_sibling(bufs, f"rs_sib_{tag}")
        chips = [_add_half(a, r, c_idx, f"rs_add_{tag}{i}") for i, (a, r) in enumerate(zip(bufs, recv))]
        h = _rs_start(chips, (), f"rs_{tag}")
        rs_pending.append((h, tag))
        return h[4]

    def ffn_bwd(dys, f, i, s, st, xt):
        dxres, dhb, dht, dg, db = _ln_bwd(dys, st["xh"], st["rstd"], gain(i, s), MACARON, f"ffn_ln_bwd{f}")
        dgains[(i, s)], dbiases[(i, s)] = dg, db
        dgu = _ffn_bwd_act(dhb, st["wd"], st["g"], st["u"], fp, f"ffn_bwd_act{f}")
        g_dt = _ffn_dwd(dht, st["act"], fp, f"ffn_dwd{f}")
        g_gu = _mm(xt, dgu, grid=(D // bm, N_DEV, 1),
                   a_spec=pl.BlockSpec((bm, T), lambda r, j, k: (r, 0)),
                   b_spec=pl.BlockSpec((T, 2 * fp), lambda r, j, k: (0, j)),
                   out_shape=jax.ShapeDtypeStruct((N_DEV, D, 2 * fp), BF16),
                   out_spec=pl.BlockSpec((None, bm, 2 * fp), lambda r, j, k: (j, r, 0)), name=f"ffn_dwgu{f}")
        token = rs_stage([g_dt, g_gu], f"f{f}")
        dx = _ffn_dx(dgu, st["wgu"], fp, f"ffn_dx{f}", deps=(token,))
        return [dxres, dx]

    def dw_square(at, bmat, name):
        return _mm(at, bmat, grid=(D // bm, 1, 1),
                   a_spec=pl.BlockSpec((bm, T), lambda r, j, k: (r, 0)),
                   b_spec=pl.BlockSpec((T, D), lambda r, j, k: (0, 0)),
                   out_shape=jax.ShapeDtypeStruct((D, D), BF16),
                   out_spec=pl.BlockSpec((bm, D), lambda r, j, k: (r, 0)), name=name)

    def dx_square(a, w, name, deps=()):
        return _mm(a, w, grid=(T // tm, 1, 1), nt=True, a_spec=row_spec, b_spec=full_w,
                   out_shape=jax.ShapeDtypeStruct((T, D), F32), out_spec=row_spec, name=name, deps=deps)

    to_slots = lambda g2d: _device_to_slot_order(g2d.reshape(N_DEV, rs, D))

    d_b2 = ffn_bwd([dy], 3, 1, 2, s_f3, b2t)
    dxres, dmb, dmt, dg, db = _ln_bwd(d_b2, xh_a, rstd_a, gain(1, 1), 1.0, "attn_ln_bwd")
    dgains[(1, 1)], dbiases[(1, 1)] = dg, db
    g_aout = dw_square(ot, dmb, "attn_dwout")
    dobs, statss = _attn_bwd_prep(dmb, w_aout, o32, lse_tot, dils)
    ntile = 3 * D // QKV_TILE
    g_qkv = None
    dqkv_gs = []
    for gi, d in enumerate(dils):
        dqkv_g = _attn_bwd(qkv_gs[gi], dobs[gi], statss[gi], d, slopes[gi], f"attn_bwd{gi}")
        dqkv_gs.append(dqkv_g)
        xt = b1t if d == 1 else _transpose_sub(b1, d, f"attn_xt{gi}")
        g_qkv = _attn_dw_group(xt, dqkv_g, gi, d, g_qkv, f"attn_dwqkv{gi}")
    token = rs_stage([to_slots(g_aout), g_qkv], "attn")
    dx_attn = [_attn_dx_group(dqkv_gs[gi], wqkv_nat, gi, d, f"attn_dx{gi}", deps=(token,) if gi == 0 else ())
               for gi, d in enumerate(dils)]
    d_a3 = ffn_bwd([dxres] + dx_attn, 2, 1, 0, s_f2, a3t)
    d_a2 = ffn_bwd(d_a3, 1, 0, 2, s_f1, a2t)
    dxres, dmb, dmt, dg, db = _ln_bwd(d_a2, xh_p, rstd_p, gain(0, 1), 1.0, "pool_ln_bwd")
    dgains[(0, 1)], dbiases[(0, 1)] = dg, db
    g_pout = dw_square(yst, dmb, "pool_dwout")
    dyb, dscale = _pool_bwd_out(dmb, w_pout, y_pool, pool_scale)
    gd = POOL_GROUP_DIM
    g_grp = _mm(mixedt, dyb, grid=(N_POOL_GROUPS, 1, 1),
                a_spec=pl.BlockSpec((gd, T), lambda g, j, k: (g, 0)),
                b_spec=pl.BlockSpec((T, gd), lambda g, j, k: (0, g)),
                out_shape=jax.ShapeDtypeStruct((N_POOL_GROUPS, gd, gd), BF16),
                out_spec=pl.BlockSpec((None, gd, gd), lambda g, j, k: (g, 0, 0)), name="pool_dwgroup")
    tg = min(1024, T)
    dmixed = _mm(dyb, w_grp, grid=(N_POOL_GROUPS, T // tg, 1), nt=True,
                 a_spec=pl.BlockSpec((tg, gd), lambda g, t, k: (t, g)),
                 b_spec=pl.BlockSpec((None, gd, gd), lambda g, t, k: (g, 0, 0)),
                 out_shape=jax.ShapeDtypeStruct((T, D), F32),
                 out_spec=pl.BlockSpec((tg, gd), lambda g, t, k: (t, g)), name="pool_dmixed")
    dub, _ = _pool_window(dmixed, True, "pool_window_bwd")
    g_pin = dw_square(a1t, dub, "pool_dwin")
    g_grp_slots = _device_to_slot_order(
        g_grp.reshape(N_POOL_GROUPS, N_DEV, rs // 4, gd).transpose(1, 0, 2, 3).reshape(N_DEV, rs // 4, D))
    token = rs_stage([to_slots(g_pout), g_grp_slots, to_slots(g_pin)], "pool")
    dx_pool = dx_square(dub, w_pin, "pool_dx", deps=(token,))
    d_x0 = ffn_bwd([dxres, dx_pool], 0, 0, 0, s_f0, x0t)
    grad_x = _add2(d_x0[0], d_x0[1], "grad_x_add")
    rs_finish((grad_x,))
    grad_x = grad_x[None]
    gw_dt = [gsums[f"f{f}"][0] for f in range(4)]
    gw_gu = [gsums[f"f{f}"][1] for f in range(4)]
    gw_aout, gw_qkv = gsums["attn"]
    gw_pout, gw_grp, gw_pin = gsums["pool"]

    small = jnp.concatenate([dgains[(i, s)] for i in range(DEPTH) for s in range(3)]
                            + [dbiases[(i, s)] for i in range(DEPTH) for s in range(3)]
                            + [dscale, jnp.zeros((3, D), F32)], axis=0)
    _, small_sum = _small_all_gather(small, "ag_small_grads")
    dev = 4 * lax.axis_index("x") + 2 * lax.axis_index("y") + lax.axis_index("c")
    mine = lax.dynamic_slice_in_dim(small_sum, dev * rs, rs, axis=1)
    grads = {
        "ffn1_w_gate": jnp.stack([gw_gu[2 * i][:fs] for i in range(DEPTH)]),
        "ffn1_w_up": jnp.stack([gw_gu[2 * i][fp:fp + fs] for i in range(DEPTH)]),
        "ffn1_w_down": jnp.stack([gw_dt[2 * i][:fs] for i in range(DEPTH)]),
        "ffn2_w_gate": jnp.stack([gw_gu[2 * i + 1][:fs] for i in range(DEPTH)]),
        "ffn2_w_up": jnp.stack([gw_gu[2 * i + 1][fp:fp + fs] for i in range(DEPTH)]),
        "ffn2_w_down": jnp.stack([gw_dt[2 * i + 1][:fs] for i in range(DEPTH)]),
        "ln_gain": mine[0:DEPTH * 3].reshape(DEPTH, 3, rs),
        "ln_bias": mine[DEPTH * 3:2 * DEPTH * 3].reshape(DEPTH, 3, rs),
        "pool_w_in": gw_pin[None],
        "pool_w_group": gw_grp[:rs // 4].reshape(N_POOL_GROUPS, rs // 4, gd)[None],
        "pool_scale": small_sum[2 * DEPTH * 3][None],
        "pool_w_out": gw_pout[None],
        "attn_w_qkv": gw_qkv[None],
        "attn_w_out": gw_aout[None],
    }
    weights = dict(ffn1_w_gate=ffn1_w_gate, ffn1_w_up=ffn1_w_up, ffn1_w_down=ffn1_w_down,
                   ffn2_w_gate=ffn2_w_gate, ffn2_w_up=ffn2_w_up, ffn2_w_down=ffn2_w_down,
                   ln_gain=ln_gain, ln_bias=ln_bias, pool_w_in=pool_w_in, pool_w_group=pool_w_group,
                   pool_scale=pool_scale, pool_w_out=pool_w_out, attn_w_qkv=attn_w_qkv, attn_w_out=attn_w_out)
    ms = dict(ffn1_w_gate=m_ffn1_w_gate, ffn1_w_up=m_ffn1_w_up, ffn1_w_down=m_ffn1_w_down,
              ffn2_w_gate=m_ffn2_w_gate, ffn2_w_up=m_ffn2_w_up, ffn2_w_down=m_ffn2_w_down,
              ln_gain=m_ln_gain, ln_bias=m_ln_bias, pool_w_in=m_pool_w_in, pool_w_group=m_pool_w_group,
              pool_scale=m_pool_scale, pool_w_out=m_pool_w_out, attn_w_qkv=m_attn_w_qkv, attn_w_out=m_attn_w_out)
    vs = dict(ffn1_w_gate=v_ffn1_w_gate, ffn1_w_up=v_ffn1_w_up, ffn1_w_down=v_ffn1_w_down,
              ffn2_w_gate=v_ffn2_w_gate, ffn2_w_up=v_ffn2_w_up, ffn2_w_down=v_ffn2_w_down,
              ln_gain=v_ln_gain, ln_bias=v_ln_bias, pool_w_in=v_pool_w_in, pool_w_group=v_pool_w_group,
              pool_scale=v_pool_scale, pool_w_out=v_pool_w_out, attn_w_qkv=v_attn_w_qkv, attn_w_out=v_attn_w_out)
    names = list(weights)
    col_sharded = ("ffn1_w_gate", "ffn1_w_up", "ffn2_w_gate", "ffn2_w_up")
    deltas, new_m, new_v = {}, {}, {}
    for nme in names:
        if nme in col_sharded:
            outs = _adamw(jnp.swapaxes(weights[nme], 1, 2), grads[nme], jnp.swapaxes(ms[nme], 1, 2),
                          jnp.swapaxes(vs[nme], 1, 2), f"adamw_{nme}")
            deltas[nme], new_m[nme], new_v[nme] = (jnp.swapaxes(o, 1, 2) for o in outs)
            grads[nme] = jnp.swapaxes(grads[nme], 1, 2)
        else:
            deltas[nme], new_m[nme], new_v[nme] = _adamw(weights[nme], grads[nme], ms[nme], vs[nme], f"adamw_{nme}")
    return (loss, grad_x, *[grads[k] for k in names], *[deltas[k] for k in names],
            *[new_m[k] for k in names], *[new_v[k] for k in names])
```

```python
import functools

import numpy as np
import jax
import jax.numpy as jnp
from jax import lax
from jax.experimental import pallas as pl
from jax.experimental.pallas import tpu as pltpu

F32 = jnp.float32
BF16 = jnp.bfloat16

D = 1024
N_DEV = 8
N_HEADS = 16
HEAD_DIM = 64
N_POOL_GROUPS = 4
POOL_GROUP_DIM = 256
POOL_HALF = (1, 2, 4, 8)
DIL_CONFIGS = ((128, 1), (512, 4), (2048, 16))
ATTN_HALO = 64
ATTN_BLOCK = 128
QKV_SHARD = 3 * 3 * D // N_DEV
DEPTH = 2
ALPHA = (2.0 * DEPTH) ** 0.25
MACARON = 0.5
LN_EPS = 1e-5
MASK_VALUE = -1e30
ADAM_LR = 0.001
ADAM_B1 = 0.9
ADAM_B2 = 0.999
ADAM_EPS = 1e-08
ADAM_WD = 0.01
ADAM_STEP = 10
LANE = 128
VMEM_LIMIT = 56 * 1024 * 1024
MESH = pl.DeviceIdType.MESH
AXES = ("x", "y", "c")


def _round_up(n, m):
    return (n + m - 1) // m * m


def _pcall(body, deps=(), **kw):
    if not deps:
        return pl.pallas_call(body, **kw)
    n_in, n_dep = len(kw["in_specs"]), len(deps)

    def wrapped(*refs):
        return body(*refs[:n_in], *refs[n_in + n_dep:])

    kw["in_specs"] = list(kw["in_specs"]) + [pl.BlockSpec(memory_space=pl.ANY)] * n_dep
    call = pl.pallas_call(wrapped, **kw)
    return lambda *args: call(*args, *deps)


def _params(*sem):
    return pltpu.CompilerParams(dimension_semantics=sem, vmem_limit_bytes=VMEM_LIMIT)


def _alibi_slopes():
    n = len(DIL_CONFIGS) * N_HEADS
    s = 2.0 ** (-8.0 * np.arange(1, n + 1) / n)
    return s.reshape(len(DIL_CONFIGS), N_HEADS).astype(np.float32)


def _my_slot():
    return 4 * lax.axis_index("c") + 2 * lax.axis_index("x") + lax.axis_index("y")


def _mm(a, b, *, grid, a_spec, b_spec, out_shape, out_spec, nt=False, name, alias=None, deps=()):
    nk = grid[2]
    dn = (((1,), (1,)), ((), ())) if nt else (((1,), (0,)), ((), ()))
    blk = tuple(s for s in out_spec.block_shape if s is not None)

    def body(*refs):
        a_ref, b_ref = refs[0], refs[1]
        o_ref = refs[3] if alias is not None else refs[2]
        p = lax.dot_general(a_ref[...], b_ref[...], dn, preferred_element_type=F32)
        if nk == 1:
            o_ref[...] = p.astype(o_ref.dtype)
        else:
            acc = refs[-1]
            k = pl.program_id(2)

            @pl.when(k == 0)
            def _():
                acc[...] = p

            @pl.when(k > 0)
            def _():
                acc[...] += p

            @pl.when(k == nk - 1)
            def _():
                o_ref[...] = acc[...].astype(o_ref.dtype)

    in_specs = [a_spec, b_spec]
    args = [a, b]
    aliases = {}
    if alias is not None:
        in_specs.append(pl.BlockSpec(memory_space=pl.ANY))
        args.append(alias)
        aliases = {2: 0}
    return _pcall(
        body, deps=deps, name=name, grid=grid, in_specs=in_specs, out_specs=out_spec, out_shape=out_shape,
        scratch_shapes=[] if nk == 1 else [pltpu.VMEM(blk, F32)],
        input_output_aliases=aliases,
        compiler_params=_params("parallel", "parallel", "arbitrary"),
    )(*args)


def _transpose_cast(x, name, deps=()):
    T = x.shape[0]
    tm = min(512, T)

    def body(x_ref, xb_ref, xt_ref):
        v = x_ref[...]
        xb_ref[...] = v.astype(BF16)
        xt_ref[...] = v.T.astype(BF16)

    return _pcall(
        body, deps=deps, name=name, grid=(T // tm,),
        in_specs=[pl.BlockSpec((tm, D), lambda t: (t, 0))],
        out_specs=[pl.BlockSpec((tm, D), lambda t: (t, 0)), pl.BlockSpec((D, tm), lambda t: (0, t))],
        out_shape=[jax.ShapeDtypeStruct((T, D), BF16), jax.ShapeDtypeStruct((D, T), BF16)],
        compiler_params=_params("parallel"),
    )(x)


def _mm_ln(a, b, xres, gain, bias, hscale, name, deps=()):
    T, K = a.shape
    tm = min(512, T)

    def body(a_ref, b_ref, x_ref, g_ref, bt_ref, y_ref, yb_ref, yt_ref, xh_ref, rs_ref):
        h = jnp.dot(a_ref[...], b_ref[...], preferred_element_type=F32)
        z = ALPHA * x_ref[...] + hscale * h
        mu = jnp.mean(z, axis=-1, keepdims=True)
        zc = z - mu
        var = jnp.mean(zc * zc, axis=-1, keepdims=True)
        rstd = lax.rsqrt(var + LN_EPS)
        xh = zc * rstd
        y = xh * g_ref[...] + bt_ref[...]
        y_ref[...] = y
        yb_ref[...] = y.astype(BF16)
        yt_ref[...] = y.T.astype(BF16)
        xh_ref[...] = xh
        rs_ref[...] = rstd

    row = pl.BlockSpec((tm, D), lambda t: (t, 0))
    vec = pl.BlockSpec((1, D), lambda t: (0, 0))
    return _pcall(
        body, deps=deps, name=name, grid=(T // tm,),
        in_specs=[pl.BlockSpec((tm, K), lambda t: (t, 0)), pl.BlockSpec((K, D), lambda t: (0, 0)), row, vec, vec],
        out_specs=[row, row, pl.BlockSpec((D, tm), lambda t: (0, t)), row, pl.BlockSpec((tm, 1), lambda t: (t, 0))],
        out_shape=[jax.ShapeDtypeStruct((T, D), F32), jax.ShapeDtypeStruct((T, D), BF16),
                   jax.ShapeDtypeStruct((D, T), BF16), jax.ShapeDtypeStruct((T, D), F32),
                   jax.ShapeDtypeStruct((T, 1), F32)],
        compiler_params=_params("parallel"),
    )(a, b, xres, gain, bias)


def _ln_bwd(dys, xhat, rstd, gain, hscale, name):
    T = xhat.shape[0]
    tm = min(512, T)
    n = len(dys)

    def body(*refs):
        dy_refs = refs[:n]
        xh_ref, rs_ref, g_ref, dx_ref, dh_ref, dht_ref, dg_ref, db_ref = refs[n:]
        dy = dy_refs[0][...]
        for r in dy_refs[1:]:
            dy = dy + r[...]
        xh = xh_ref[...]
        dxh = dy * g_ref[...]
        m1 = jnp.mean(dxh, axis=-1, keepdims=True)
        m2 = jnp.mean(dxh * xh, axis=-1, keepdims=True)
        dz = rs_ref[...] * (dxh - m1 - xh * m2)
        dx_ref[...] = ALPHA * dz
        dh = hscale * dz
        dh_ref[...] = dh.astype(BF16)
        dht_ref[...] = dh.T.astype(BF16)
        dg = jnp.sum(dy * xh, axis=0, keepdims=True)
        db = jnp.sum(dy, axis=0, keepdims=True)

        @pl.when(pl.program_id(0) == 0)
        def _():
            dg_ref[...] = dg
            db_ref[...] = db

        @pl.when(pl.program_id(0) > 0)
        def _():
            dg_ref[...] += dg
            db_ref[...] += db

    row = pl.BlockSpec((tm, D), lambda t: (t, 0))
    vec = pl.BlockSpec((1, D), lambda t: (0, 0))
    return _pcall(
        body, name=name, grid=(T // tm,),
        in_specs=[row] * n + [row, pl.BlockSpec((tm, 1), lambda t: (t, 0)), vec],
        out_specs=[row, row, pl.BlockSpec((D, tm), lambda t: (0, t)), vec, vec],
        out_shape=[jax.ShapeDtypeStruct((T, D), F32), jax.ShapeDtypeStruct((T, D), BF16),
                   jax.ShapeDtypeStruct((D, T), BF16), jax.ShapeDtypeStruct((1, D), F32),
                   jax.ShapeDtypeStruct((1, D), F32)],
        compiler_params=_params("arbitrary"),
    )(*dys, xhat, rstd, gain)


def _add2(a, b, name):
    T = a.shape[0]
    tm = min(512, T)

    def body(a_ref, b_ref, o_ref):
        o_ref[...] = a_ref[...] + b_ref[...]

    row = pl.BlockSpec((tm, D), lambda t: (t, 0))
    return _pcall(body, name=name, grid=(T // tm,), in_specs=[row, row], out_specs=row,
                  out_shape=jax.ShapeDtypeStruct((T, D), F32), compiler_params=_params("parallel"))(a, b)


def _loss_head(y, tgt):
    T = y.shape[0]
    tm = min(512, T)

    def body(y_ref, t_ref, dy_ref, l_ref):
        e = y_ref[...] - t_ref[...]
        dy_ref[...] = e * (1.0 / D)
        part = jnp.sum(jnp.sum(e * e, axis=1, keepdims=True), axis=0, keepdims=True) * (0.5 / D)

        @pl.when(pl.program_id(0) == 0)
        def _():
            l_ref[...] = jnp.zeros_like(l_ref)

        l_ref[...] += part

    row = pl.BlockSpec((tm, D), lambda t: (t, 0))
    return _pcall(
        body, name="loss_head", grid=(T // tm,),
        in_specs=[row, row],
        out_specs=[row, pl.BlockSpec((8, LANE), lambda t: (0, 0))],
        out_shape=[jax.ShapeDtypeStruct((T, D), F32), jax.ShapeDtypeStruct((8, LANE), F32)],
        compiler_params=_params("arbitrary"),
    )(y, tgt)


def _sigmoid(v):
    return 0.5 * jnp.tanh(0.5 * v) + 0.5


def _ffn_up(xb, wgu, fp, name, deps=()):
    T = xb.shape[0]
    tm = min(1024, T)

    def body(x_ref, w_ref, dg_ref, du_ref, a_ref):
        x = x_ref[...]
        for s in range(2):
            p = jnp.dot(x, w_ref[s], preferred_element_type=F32)
            g = p[:, :fp]
            u = p[:, fp:]
            sig = _sigmoid(g)
            q = g * sig
            cols = slice(s * fp, (s + 1) * fp)
            dg_ref[:, cols] = (sig * (1.0 + g - q) * u).astype(BF16)
            du_ref[:, cols] = q.astype(BF16)
            a_ref[:, cols] = (q * u).astype(BF16)

    out = pl.BlockSpec((tm, 2 * fp), lambda t, j: (t, j))
    shp = jax.ShapeDtypeStruct((T, N_DEV * fp), BF16)
    return _pcall(
        body, deps=deps, name=name, grid=(T // tm, N_DEV // 2),
        in_specs=[pl.BlockSpec((tm, D), lambda t, j: (t, 0)),
                  pl.BlockSpec((2, D, 2 * fp), lambda t, j: (j, 0, 0))],
        out_specs=[out, out, out], out_shape=[shp, shp, shp],
        compiler_params=_params("parallel", "parallel"),
    )(xb, wgu)


def _ffn_bwd_act(dhb, wd, g, u, fp, name):
    T = dhb.shape[0]
    tm = min(1024, T)

    def body(dh_ref, w_ref, g_ref, u_ref, o_ref):
        da = lax.dot_general(dh_ref[...], w_ref[...], (((1,), (1,)), ((), ())), preferred_element_type=F32)
        dgate = (da * g_ref[...].astype(F32)).astype(BF16)
        dup = (da * u_ref[...].astype(F32)).astype(BF16)
        for s in range(2):
            o_ref[:, 2 * s * fp:(2 * s + 1) * fp] = dgate[:, s * fp:(s + 1) * fp]
            o_ref[:, (2 * s + 1) * fp:(2 * s + 2) * fp] = dup[:, s * fp:(s + 1) * fp]

    blk = pl.BlockSpec((tm, 2 * fp), lambda t, j: (t, j))
    return _pcall(
        body, name=name, grid=(T // tm, N_DEV // 2),
        in_specs=[pl.BlockSpec((tm, D), lambda t, j: (t, 0)), pl.BlockSpec((2 * fp, D), lambda t, j: (j, 0)), blk, blk],
        out_specs=pl.BlockSpec((tm, 4 * fp), lambda t, j: (t, j)),
        out_shape=jax.ShapeDtypeStruct((T, N_DEV * 2 * fp), BF16),
        compiler_params=_params("parallel", "parallel"),
    )(dhb, wd, g, u)


def _ffn_dwd(dht, act, fp, name):
    T = dht.shape[1]
    bm = 512

    def body(a_ref, b_ref, o_ref):
        p = jnp.dot(a_ref[...], b_ref[...], preferred_element_type=F32)
        o_ref[0] = p[:, :fp].astype(BF16)
        o_ref[1] = p[:, fp:].astype(BF16)

    return _pcall(
        body, name=name, grid=(D // bm, N_DEV // 2),
        in_specs=[pl.BlockSpec((bm, T), lambda i, j: (i, 0)), pl.BlockSpec((T, 2 * fp), lambda i, j: (0, j))],
        out_specs=pl.BlockSpec((2, bm, fp), lambda i, j: (j, i, 0)),
        out_shape=jax.ShapeDtypeStruct((N_DEV, D, fp), BF16),
        compiler_params=_params("parallel", "parallel"),
    )(dht, act)


def _ffn_dx(dgu, wgu, res, fp, name, deps=()):
    T = dgu.shape[0]
    tm = min(512, T)

    def body(a_ref, w_ref, r_ref, o_ref):
        acc = r_ref[...]
        for j in range(N_DEV):
            p = lax.dot_general(a_ref[:, j * 2 * fp:(j + 1) * 2 * fp], w_ref[j], (((1,), (1,)), ((), ())),
                                preferred_element_type=F32)
            acc = acc + p
        o_ref[...] = acc

    return _pcall(
        body, deps=deps, name=name, grid=(T // tm,),
        in_specs=[pl.BlockSpec((tm, N_DEV * 2 * fp), lambda t: (t, 0)),
                  pl.BlockSpec((N_DEV, D, 2 * fp), lambda t: (0, 0, 0)),
                  pl.BlockSpec((tm, D), lambda t: (t, 0))],
        out_specs=pl.BlockSpec((tm, D), lambda t: (t, 0)),
        out_shape=jax.ShapeDtypeStruct((T, D), F32),
        compiler_params=_params("parallel"),
    )(dgu, wgu, res)


def _slots_to_columns(w, name):
    _, R, C = w.shape
    tr = min(512, R)

    def body(i_ref, o_ref):
        o_ref[...] = i_ref[...]

    return _pcall(
        body, name=name, grid=(N_DEV, R // tr),
        in_specs=[pl.BlockSpec((None, tr, C), lambda j, i: (4 * (j % 2) + j // 2, i, 0))],
        out_specs=pl.BlockSpec((tr, C), lambda j, i: (i, j)),
        out_shape=jax.ShapeDtypeStruct((R, N_DEV * C), w.dtype),
        compiler_params=_params("parallel", "parallel"),
    )(w)


POOL_PAD = 16
POOL_CHUNK = 512


def _pool_window(v, transpose, name):
    T = v.shape[0]
    ch = min(POOL_CHUNK, T)
    ext = ch + 2 * POOL_PAD
    gd = POOL_GROUP_DIM

    def body(v_ref, o_ref, ot_ref, pad_ref):
        pad_ref[0:POOL_PAD, :] = jnp.zeros((POOL_PAD, gd), F32)
        pad_ref[POOL_PAD + T:POOL_PAD + T + POOL_PAD, :] = jnp.zeros((POOL_PAD, gd), F32)
        for gi, hw in enumerate(POOL_HALF):
            @pl.when(pl.program_id(0) == gi)
            def _(hw=hw):
                def count(t):
                    return (jnp.minimum(t + hw, T) - jnp.maximum(t - hw, 0)).astype(F32)

                if transpose:
                    t_all = lax.broadcasted_iota(jnp.int32, (T, gd), 0)
                    pad_ref[POOL_PAD:POOL_PAD + T, :] = v_ref[...] / count(t_all)
                else:
                    pad_ref[POOL_PAD:POOL_PAD + T, :] = v_ref[...]
                shift = hw if transpose else hw - 1
                for c in range(T // ch):
                    e = pad_ref[c * ch:c * ch + ext, :]
                    step = 1
                    while step < 2 * hw:
                        e = e + pltpu.roll(e, step, 0)
                        step *= 2
                    if shift:
                        e = pltpu.roll(e, ext - shift, 0)
                    s = e[POOL_PAD:POOL_PAD + ch, :]
                    center = v_ref[c * ch:(c + 1) * ch, :]
                    if transpose:
                        res = s - center
                    else:
                        t_idx = c * ch + lax.broadcasted_iota(jnp.int32, (ch, gd), 0)
                        res = s / count(t_idx) - center
                    o_ref[c * ch:(c + 1) * ch, :] = res.astype(BF16)
                    ot_ref[:, c * ch:(c + 1) * ch] = res.T.astype(BF16)

    return _pcall(
        body, name=name, grid=(N_POOL_GROUPS,),
        in_specs=[pl.BlockSpec((T, gd), lambda g: (0, g))],
        out_specs=[pl.BlockSpec((T, gd), lambda g: (0, g)), pl.BlockSpec((gd, T), lambda g: (g, 0))],
        out_shape=[jax.ShapeDtypeStruct((T, D), BF16), jax.ShapeDtypeStruct((D, T), BF16)],
        scratch_shapes=[pltpu.VMEM((T + 2 * POOL_PAD, gd), F32)],
        compiler_params=_params("arbitrary"),
    )(v)


def _pool_group(mixedb, wgroup, scale):
    T = mixedb.shape[0]
    tm = min(1024, T)
    gd = POOL_GROUP_DIM

    def body(a_ref, w_ref, s_ref, y_ref, ys_ref, yst_ref):
        y = jnp.dot(a_ref[...], w_ref[...], preferred_element_type=F32)
        ys = y * s_ref[...]
        y_ref[...] = y
        ys_ref[...] = ys.astype(BF16)
        yst_ref[...] = ys.T.astype(BF16)

    blk = pl.BlockSpec((tm, gd), lambda g, t: (t, g))
    return _pcall(
        body, name="pool_group", grid=(N_POOL_GROUPS, T // tm),
        in_specs=[blk, pl.BlockSpec((None, gd, gd), lambda g, t: (g, 0, 0)), pl.BlockSpec((1, gd), lambda g, t: (0, g))],
        out_specs=[blk, blk, pl.BlockSpec((gd, tm), lambda g, t: (g, t))],
        out_shape=[jax.ShapeDtypeStruct((T, D), F32), jax.ShapeDtypeStruct((T, D), BF16),
                   jax.ShapeDtypeStruct((D, T), BF16)],
        compiler_params=_params("parallel", "parallel"),
    )(mixedb, wgroup, scale)


def _pool_bwd_out(dmb, w_out, y, scale):
    T = dmb.shape[0]
    tm = min(512, T)

    def body(a_ref, w_ref, y_ref, s_ref, dy_ref, ds_ref):
        dys = lax.dot_general(a_ref[...], w_ref[...], (((1,), (1,)), ((), ())), preferred_element_type=F32)
        dy_ref[...] = (dys * s_ref[...]).astype(BF16)
        part = jnp.sum(dys * y_ref[...], axis=0, keepdims=True)

        @pl.when(pl.program_id(0) == 0)
        def _():
            ds_ref[...] = part

        @pl.when(pl.program_id(0) > 0)
        def _():
            ds_ref[...] += part

    row = pl.BlockSpec((tm, D), lambda t: (t, 0))
    vec = pl.BlockSpec((1, D), lambda t: (0, 0))
    return _pcall(
        body, name="pool_bwd_out", grid=(T // tm,),
        in_specs=[row, pl.BlockSpec((D, D), lambda t: (0, 0)), row, vec],
        out_specs=[row, vec],
        out_shape=[jax.ShapeDtypeStruct((T, D), BF16), jax.ShapeDtypeStruct((1, D), F32)],
        compiler_params=_params("arbitrary"),
    )(dmb, w_out, y, scale)


def _attn_bias_table(d, slopes):
    w = ATTN_BLOCK + 2 * ATTN_HALO
    rel = np.arange(w)[None, :] - ATTN_HALO - np.arange(ATTN_BLOCK)[:, None]
    dist = (d * np.abs(rel)).astype(np.float32)
    bias = -np.asarray(slopes, np.float32)[:, None, None] * dist[None]
    return jnp.asarray(np.where(np.abs(rel)[None] <= ATTN_HALO, bias, np.float32(MASK_VALUE)).astype(np.float32))


def _attn_in_range(n, L):
    w = ATTN_BLOCK + 2 * ATTN_HALO
    j = n * ATTN_BLOCK - ATTN_HALO + lax.broadcasted_iota(jnp.int32, (ATTN_BLOCK, w), 1)
    return (j >= 0) & (j < L)


BIAS_SPEC = pl.BlockSpec((N_HEADS, ATTN_BLOCK, ATTN_BLOCK + 2 * ATTN_HALO), lambda r, n: (0, 0, 0))


def _lane_col(st, idx):
    lane = lax.broadcasted_iota(jnp.int32, st.shape, 1)
    return jnp.sum(jnp.where(lane == idx, st, 0.0), axis=1, keepdims=True)


def _window_specs(nb, d, col, width):
    last = 2 * d * nb - 1

    def prev(r, n):
        return (jnp.maximum(2 * (r * nb + n) - 1, 0), col)

    def cur(r, n):
        return (r * nb + n, col)

    def nxt(r, n):
        return (jnp.minimum(2 * (r * nb + n) + 2, last), col)

    return [pl.BlockSpec((ATTN_HALO, width), prev), pl.BlockSpec((ATTN_BLOCK, width), cur),
            pl.BlockSpec((ATTN_HALO, width), nxt)]


def _attn_fwd(qkv_g, d, slopes, name):
    T = qkv_g.shape[0]
    L = T // d
    nb = L // ATTN_BLOCK

    def body(q_ref, kp_ref, kc_ref, kn_ref, vp_ref, vc_ref, vn_ref, b_ref, o_ref, lse_ref):
        in_range = _attn_in_range(pl.program_id(1), L)
        lane = lax.broadcasted_iota(jnp.int32, (ATTN_BLOCK, LANE), 1)
        first = lane < HEAD_DIM
        sc = HEAD_DIM ** -0.5
        head_mask = [jnp.where(first, sc, 0.0).astype(BF16), jnp.where(first, 0.0, sc).astype(BF16)]
        lse_acc = jnp.zeros((ATTN_BLOCK, LANE), F32)
        for hp in range(N_HEADS // 2):
            cs = slice(hp * LANE, (hp + 1) * LANE)
            q2 = q_ref[:, cs]
            k2 = jnp.concatenate([kp_ref[:, cs], kc_ref[:, cs], kn_ref[:, cs]], axis=0)
            v2 = jnp.concatenate([vp_ref[:, cs], vc_ref[:, cs], vn_ref[:, cs]], axis=0)
            outs = []
            for hh in range(2):
                h = 2 * hp + hh
                qh = q2 * head_mask[hh]
                s = lax.dot_general(qh, k2, (((1,), (1,)), ((), ())), preferred_element_type=F32)
                s = jnp.where(in_range, s + b_ref[h], MASK_VALUE)
                m = jnp.max(s, axis=1, keepdims=True)
                p = jnp.exp(s - m)
                l = jnp.sum(p, axis=1, keepdims=True)
                o = jnp.dot(p.astype(BF16), v2, preferred_element_type=F32) / l
                outs.append(o)
                lse_acc = jnp.where(lane == h, m + jnp.log(l), lse_acc)
            o_ref[:, cs] = jnp.where(first, outs[0], outs[1])
        lse_ref[...] = lse_acc

    specs = ([pl.BlockSpec((ATTN_BLOCK, D), lambda r, n: (r * nb + n, 0))]
             + _window_specs(nb, d, 1, D) + _window_specs(nb, d, 2, D) + [BIAS_SPEC])
    row = lambda w: pl.BlockSpec((ATTN_BLOCK, w), lambda r, n: (r * nb + n, 0))
    return _pcall(
        body, name=name, grid=(d, nb), in_specs=specs,
        out_specs=[row(D), row(LANE)],
        out_shape=[jax.ShapeDtypeStruct((T, D), F32), jax.ShapeDtypeStruct((T, LANE), F32)],
        compiler_params=_params("parallel", "parallel"),
    )(*([qkv_g] * 7), _attn_bias_table(d, slopes))


def _stage(scr3, val):
    for c in range(val.shape[1] // LANE):
        scr3[c] = val[:, c * LANE:(c + 1) * LANE]


def _unstage(scr3):
    return jnp.concatenate([scr3[c] for c in range(scr3.shape[0])], axis=1)


def _gather_rows(scr3, r, n, d):
    return jnp.concatenate([scr3[c, pl.ds(r, n, stride=d), :] for c in range(scr3.shape[0])], axis=1)


def _scatter_rows(scr3, r, n, d, val):
    for c in range(scr3.shape[0]):
        scr3[c, pl.ds(r, n, stride=d), :] = val[:, c * LANE:(c + 1) * LANE]


def _attn_combine(os_, lses, dils):
    T = os_[0].shape[0]
    tm = min(256, T)
    ng = len(os_)
    n_scr = sum(1 for d in dils if d > 1)

    def body(*refs):
        in_o = refs[:ng]
        in_l = refs[ng:2 * ng]
        o32_ref, ob_ref, ot_ref, lt_ref = refs[2 * ng:2 * ng + 4]
        scr = refs[2 * ng + 4:]
        o_chunk, l_refs, si = [], [], 0
        for g, d in enumerate(dils):
            if d == 1:
                o_chunk.append(lambda hp, g=g: in_o[g][:, hp * LANE:(hp + 1) * LANE])
                l_refs.append(in_l[g])
                continue
            so, sl = scr[2 * si], scr[2 * si + 1]
            si += 1
            for r in range(d):
                _scatter_rows(so, r, tm // d, d, in_o[g][r])
                sl[pl.ds(r, tm // d, stride=d), :] = in_l[g][r]
            o_chunk.append(lambda hp, so=so: so[hp])
            l_refs.append(sl)
        ls = [r[...] for r in l_refs]
        m = ls[0]
        for l in ls[1:]:
            m = jnp.maximum(m, l)
        tot = jnp.exp(ls[0] - m)
        for l in ls[1:]:
            tot = tot + jnp.exp(l - m)
        lt = m + jnp.log(tot)
        lt_ref[...] = lt
        ws = [jnp.exp(l - lt) for l in ls]
        lane = lax.broadcasted_iota(jnp.int32, (tm, LANE), 1)
        first = lane < HEAD_DIM
        for hp in range(N_HEADS // 2):
            cs = slice(hp * LANE, (hp + 1) * LANE)
            acc = jnp.zeros((tm, LANE), F32)
            for g in range(ng):
                wt = jnp.where(first, _lane_col(ws[g], 2 * hp), _lane_col(ws[g], 2 * hp + 1))
                acc = acc + wt * o_chunk[g](hp)
            o32_ref[:, cs] = acc
            ob_ref[:, cs] = acc.astype(BF16)
        ot_ref[...] = o32_ref[...].T.astype(BF16)

    row = pl.BlockSpec((tm, D), lambda t: (t, 0))
    st = pl.BlockSpec((tm, LANE), lambda t: (t, 0))

    def sub_spec(d, w):
        return pl.BlockSpec((tm, w), lambda t: (t, 0)) if d == 1 else pl.BlockSpec((d, tm // d, w), lambda t: (0, t, 0))

    def sub_view(a, d):
        return a if d == 1 else a.reshape(d, T // d, a.shape[1])

    return _pcall(
        body, name="attn_combine", grid=(T // tm,),
        in_specs=[sub_spec(d, D) for d in dils] + [sub_spec(d, LANE) for d in dils],
        out_specs=[row, row, pl.BlockSpec((D, tm), lambda t: (0, t)), st],
        out_shape=[jax.ShapeDtypeStruct((T, D), F32), jax.ShapeDtypeStruct((T, D), BF16),
                   jax.ShapeDtypeStruct((D, T), BF16), jax.ShapeDtypeStruct((T, LANE), F32)],
        scratch_shapes=[pltpu.VMEM(s, F32) for _ in range(n_scr) for s in ((D // LANE, tm, LANE), (tm, LANE))],
        compiler_params=_params("parallel"),
    )(*[sub_view(a, d) for a, d in zip(os_, dils)], *[sub_view(a, d) for a, d in zip(lses, dils)])


def _attn_bwd_prep(dmb, w_out, o32, lse_tot, dils):
    T = dmb.shape[0]
    tm = min(512, T)
    ng = len(dils)

    def body(a_ref, w_ref, o_ref, l_ref, *rest):
        do_refs, st_refs = rest[:ng], rest[ng:2 * ng]
        do_scr, st_scr = rest[2 * ng:]
        do = lax.dot_general(a_ref[...], w_ref[...], (((1,), (1,)), ((), ())), preferred_element_type=F32)
        _stage(do_scr, do)
        prod = do * o_ref[...]
        lane = lax.broadcasted_iota(jnp.int32, (tm, LANE), 1)
        first = lane < HEAD_DIM
        st = jnp.where(lane < N_HEADS, l_ref[...], 0.0)
        for hp in range(N_HEADS // 2):
            pr = prod[:, hp * LANE:(hp + 1) * LANE]
            d0 = jnp.sum(jnp.where(first, pr, 0.0), axis=1, keepdims=True)
            d1 = jnp.sum(jnp.where(first, 0.0, pr), axis=1, keepdims=True)
            st = jnp.where(lane == N_HEADS + 2 * hp, d0, st)
            st = jnp.where(lane == N_HEADS + 2 * hp + 1, d1, st)
        st_scr[...] = st
        for g, d in enumerate(dils):
            if d == 1:
                do_refs[g][...] = do.astype(BF16)
                st_refs[g][...] = st
                continue
            for r in range(d):
                do_refs[g][r] = _gather_rows(do_scr, r, tm // d, d).astype(BF16)
                st_refs[g][r] = st_scr[pl.ds(r, tm // d, stride=d), :]

    row = pl.BlockSpec((tm, D), lambda t: (t, 0))
    stb = pl.BlockSpec((tm, LANE), lambda t: (t, 0))

    def sub_spec(d, w):
        return pl.BlockSpec((tm, w), lambda t: (t, 0)) if d == 1 else pl.BlockSpec((d, tm // d, w), lambda t: (0, t, 0))

    def sub_shape(d, w, dt):
        return jax.ShapeDtypeStruct((T, w) if d == 1 else (d, T // d, w), dt)

    outs = _pcall(
        body, name="attn_bwd_prep", grid=(T // tm,),
        in_specs=[row, pl.BlockSpec((D, D), lambda t: (0, 0)), row, stb],
        out_specs=[sub_spec(d, D) for d in dils] + [sub_spec(d, LANE) for d in dils],
        out_shape=[sub_shape(d, D, BF16) for d in dils] + [sub_shape(d, LANE, F32) for d in dils],
        scratch_shapes=[pltpu.VMEM((D // LANE, tm, LANE), F32), pltpu.VMEM((tm, LANE), F32)],
        compiler_params=_params("parallel"),
    )(dmb, w_out, o32, lse_tot)
    return ([o.reshape(T, D) for o in outs[:ng]], [o.reshape(T, LANE) for o in outs[ng:]])


def _attn_bwd(qkv_g, do_g, st_g, d, slopes, name):
    T = qkv_g.shape[0]
    L = T // d
    nb = L // ATTN_BLOCK
    scale = HEAD_DIM ** -0.5
    nt = (((1,), (1,)), ((), ()))

    def body(qp_ref, qc_ref, qn_ref, kp_ref, kc_ref, kn_ref, vp_ref, vc_ref, vn_ref,
             dp_ref, dc_ref, dn_ref, sp_ref, sc_ref, sn_ref, b_ref, o_ref):
        in_range = _attn_in_range(pl.program_id(1), L)
        lane = lax.broadcasted_iota(jnp.int32, (ATTN_BLOCK, LANE), 1)
        first = lane < HEAD_DIM
        head_mask = [jnp.where(first, 1.0, 0.0).astype(BF16), jnp.where(first, 0.0, 1.0).astype(BF16)]
        scaled_mask = [jnp.where(first, scale, 0.0).astype(BF16), jnp.where(first, 0.0, scale).astype(BF16)]
        stc = sc_ref[...]
        stw_t = jnp.concatenate([sp_ref[...], stc, sn_ref[...]], axis=0).T
        for hp in range(N_HEADS // 2):
            cs = slice(hp * LANE, (hp + 1) * LANE)
            cat = lambda a, b, c: jnp.concatenate([a[:, cs], b[:, cs], c[:, cs]], axis=0)
            q2, k2, v2, do2 = qc_ref[:, cs], kc_ref[:, cs], vc_ref[:, cs], dc_ref[:, cs]
            qw, kw, vw, dow = cat(qp_ref, qc_ref, qn_ref), cat(kp_ref, kc_ref, kn_ref), cat(vp_ref, vc_ref, vn_ref), cat(dp_ref, dc_ref, dn_ref)
            dqs, dks, dvs = [], [], []
            for hh in range(2):
                h = 2 * hp + hh
                pick = lambda t, hh=hh: t * head_mask[hh]
                pick_scaled = lambda t, hh=hh: t * scaled_mask[hh]
                bias = b_ref[h]
                s = lax.dot_general(pick_scaled(q2), kw, nt, preferred_element_type=F32)
                s = jnp.where(in_range, s + bias, MASK_VALUE)
                p = jnp.exp(s - _lane_col(stc, h))
                dp = lax.dot_general(pick(do2), vw, nt, preferred_element_type=F32)
                ds = p * (dp - _lane_col(stc, N_HEADS + h))
                dqs.append(jnp.dot(ds.astype(BF16), kw, preferred_element_type=F32) * scale)
                st_ = lax.dot_general(pick_scaled(k2), qw, nt, preferred_element_type=F32)
                st_ = jnp.where(in_range, st_ + bias, MASK_VALUE)
                pt = jnp.exp(st_ - stw_t[h:h + 1, :])
                dvs.append(jnp.dot(pt.astype(BF16), dow, preferred_element_type=F32))
                dpt = lax.dot_general(pick(v2), dow, nt, preferred_element_type=F32)
                dst = pt * (dpt - stw_t[N_HEADS + h:N_HEADS + h + 1, :])
                dks.append(jnp.dot(dst.astype(BF16), qw, preferred_element_type=F32) * scale)
            o_ref[:, hp * LANE:(hp + 1) * LANE] = jnp.where(first, dqs[0], dqs[1]).astype(BF16)
            o_ref[:, D + hp * LANE:D + (hp + 1) * LANE] = jnp.where(first, dks[0], dks[1]).astype(BF16)
            o_ref[:, 2 * D + hp * LANE:2 * D + (hp + 1) * LANE] = jnp.where(first, dvs[0], dvs[1]).astype(BF16)

    specs = (_window_specs(nb, d, 0, D) + _window_specs(nb, d, 1, D) + _window_specs(nb, d, 2, D)
             + _window_specs(nb, d, 0, D) + _window_specs(nb, d, 0, LANE) + [BIAS_SPEC])
    return _pcall(
        body, name=name, grid=(d, nb), in_specs=specs,
        out_specs=pl.BlockSpec((ATTN_BLOCK, 3 * D), lambda r, n: (r * nb + n, 0)),
        out_shape=jax.ShapeDtypeStruct((T, 3 * D), BF16),
        compiler_params=_params("parallel", "parallel"),
    )(*([qkv_g] * 9), *([do_g] * 3), *([st_g] * 3), _attn_bias_table(d, slopes))


QKV_TILE = QKV_SHARD // 3


def _qkv_tile_block(nn):
    dev = nn // 3
    return 4 * (dev % 2) + dev // 2, nn % 3


def _attn_qkv_group(xb, wqkv, gi, d, name, deps=()):
    T = xb.shape[0]
    tq = min(1024, T)
    nsub = tq // d
    tn = 768
    ntile = 3 * D // tn

    def body(x_ref, w_ref, o_ref, *scr):
        p = jnp.dot(x_ref[...], w_ref[...], preferred_element_type=F32)
        if d == 1:
            o_ref[...] = p.astype(BF16)
        else:
            _stage(scr[0], p)
            for r in range(d):
                o_ref[r] = _gather_rows(scr[0], r, nsub, d).astype(BF16)

    if d == 1:
        out_spec = pl.BlockSpec((tq, tn), lambda n, t: (t, n))
        out_shape = jax.ShapeDtypeStruct((T, 3 * D), BF16)
    else:
        out_spec = pl.BlockSpec((d, nsub, tn), lambda n, t: (0, t, n))
        out_shape = jax.ShapeDtypeStruct((d, T // d, 3 * D), BF16)
    out = _pcall(
        body, deps=deps, name=name, grid=(ntile, T // tq),
        in_specs=[pl.BlockSpec((tq, D), lambda n, t: (t, 0)),
                  pl.BlockSpec((D, tn), lambda n, t: (0, ntile * gi + n))],
        out_specs=out_spec, out_shape=out_shape,
        scratch_shapes=[] if d == 1 else [pltpu.VMEM((tn // LANE, tq, LANE), F32)],
        compiler_params=_params("parallel", "parallel"),
    )(xb, wqkv)
    return out.reshape(T, 3 * D)


def _attn_dx_group(dqkv_g, wqkv, gi, d, name, deps=()):
    T = dqkv_g.shape[0]
    tq = min(512, T)
    nsub = tq // d

    def body(a_ref, w_ref, o_ref, *stage):
        a = a_ref[...]
        if d > 1:
            a = a.reshape(tq, 3 * D)
        p = lax.dot_general(a, w_ref[...], (((1,), (1,)), ((), ())), preferred_element_type=F32)
        if d == 1:
            o_ref[...] = p
        else:
            for r in range(d):
                _scatter_rows(stage[0], r, nsub, d, p[r * nsub:(r + 1) * nsub, :])
            o_ref[...] = _unstage(stage[0])

    if d == 1:
        a_spec = pl.BlockSpec((tq, 3 * D), lambda t: (t, 0))
        a = dqkv_g
    else:
        a_spec = pl.BlockSpec((d, nsub, 3 * D), lambda t: (0, t, 0))
        a = dqkv_g.reshape(d, T // d, 3 * D)
    return _pcall(
        body, deps=deps, name=name, grid=(T // tq,),
        in_specs=[a_spec, pl.BlockSpec((D, 3 * D), lambda t: (0, gi))],
        out_specs=pl.BlockSpec((tq, D), lambda t: (t, 0)),
        out_shape=jax.ShapeDtypeStruct((T, D), F32),
        scratch_shapes=[] if d == 1 else [pltpu.VMEM((D // LANE, tq, LANE), F32)],
        compiler_params=_params("parallel"),
    )(a, wqkv)


def _attn_dw_group(xt, dqkv_g, gi, d, prev, name):
    T = dqkv_g.shape[0]
    L = T // d
    bm = 512
    ntile = 3 * D // QKV_TILE

    def body(*refs):
        a_ref, b_ref = refs[0], refs[1]
        o_ref, cat = refs[-2], refs[-1]
        if d == 1:
            a = a_ref[...]
        else:
            @pl.when(pl.program_id(1) == 0)
            def _():
                for r in range(d):
                    cat[:, r * L:(r + 1) * L] = a_ref[r]
            a = cat[...]
        o_ref[...] = jnp.dot(a, b_ref[...], preferred_element_type=F32).astype(BF16)

    def out_map(i, n):
        slot, sub = _qkv_tile_block(ntile * gi + n)
        return slot, i, sub

    a_spec = (pl.BlockSpec((bm, T), lambda i, n: (i, 0)) if d == 1
              else pl.BlockSpec((d, bm, L), lambda i, n: (0, i, 0)))
    in_specs = [a_spec, pl.BlockSpec((T, QKV_TILE), lambda i, n: (0, n))]
    args = [xt, dqkv_g]
    aliases = {}
    if prev is not None:
        in_specs.append(ANY_SPEC)
        args.append(prev)
        aliases = {2: 0}
    return _pcall(
        body, name=name, grid=(D // bm, ntile), in_specs=in_specs,
        out_specs=pl.BlockSpec((None, bm, QKV_TILE), out_map),
        out_shape=jax.ShapeDtypeStruct((N_DEV, D, QKV_SHARD), BF16),
        scratch_shapes=[pltpu.VMEM((bm, T), BF16)],
        input_output_aliases=aliases,
        compiler_params=_params("parallel", "arbitrary"),
    )(*args)


def _transpose_sub(x, d, name):
    T = x.shape[0]
    tm = LANE * d

    def body(x_ref, o_ref, scr):
        for c in range(D // LANE):
            scr[...] = x_ref[:, c * LANE:(c + 1) * LANE]
            for r in range(d):
                o_ref[r, c * LANE:(c + 1) * LANE, :] = scr[pl.ds(r, LANE, stride=d), :].T.astype(BF16)

    return _pcall(
        body, name=name, grid=(T // tm,),
        in_specs=[pl.BlockSpec((tm, D), lambda t: (t, 0))],
        out_specs=pl.BlockSpec((d, D, LANE), lambda t: (0, 0, t)),
        out_shape=jax.ShapeDtypeStruct((d, D, T // d), BF16),
        scratch_shapes=[pltpu.VMEM((tm, LANE), F32)],
        compiler_params=_params("parallel"),
    )(x)


HBM_SPEC = pl.BlockSpec(memory_space=pltpu.HBM)
SEM_SPEC = pl.BlockSpec(memory_space=pltpu.SEMAPHORE)
ANY_SPEC = pl.BlockSpec(memory_space=pl.ANY)
DATAFLOW = pltpu.SideEffectType.DATAFLOW_SIDE_EFFECTING


def _me_and_peers():
    x, y, c = lax.axis_index("x"), lax.axis_index("y"), lax.axis_index("c")
    return (x, y, c), [(x, y, 1 - c), (1 - x, y, c), (x, 1 - y, c), (1 - x, 1 - y, c)]


def _slot(px, py, pc):
    return 4 * pc + 2 * px + py


def _split_start(srcs, lands, after, start_copies, n_sem, name):
    n = len(srcs)
    n_after = len(after)

    def body(*refs):
        src_refs, land_refs = refs[:n], refs[n:2 * n]
        send_sems, recv_sems = refs[2 * n + n_after], refs[2 * n + n_after + 1]
        token = refs[-1]
        start_copies(src_refs, land_refs, send_sems, recv_sems)
        token[...] = jnp.zeros_like(token)

    outs = _pcall(
        body, name=name,
        in_specs=[HBM_SPEC] * (2 * n) + [ANY_SPEC] * n_after,
        out_shape=(pltpu.SemaphoreType.DMA(n_sem), pltpu.SemaphoreType.DMA(n_sem),
                   *[pltpu.HBM(a.shape, a.dtype) for a in srcs], *[pltpu.HBM(a.shape, a.dtype) for a in lands],
                   jax.ShapeDtypeStruct((8, LANE), F32)),
        out_specs=(SEM_SPEC, SEM_SPEC, *[HBM_SPEC] * (2 * n), pl.BlockSpec(memory_space=pltpu.VMEM)),
        input_output_aliases={i: 2 + i for i in range(2 * n)},
        compiler_params=pltpu.CompilerParams(has_side_effects=DATAFLOW),
    )(*[pltpu.with_memory_space_constraint(a, pltpu.HBM) for a in srcs],
      *[pltpu.with_memory_space_constraint(a, pltpu.HBM) for a in lands], *after)
    return outs[0], outs[1], list(outs[2:2 + n]), list(outs[2 + n:2 + 2 * n]), outs[-1]


def _split_wait(handle, after, wait_copies, name):
    send_sems, recv_sems, srcs, lands, _ = handle
    n = len(srcs)

    def body(*refs):
        src_refs, land_refs = refs[:n], refs[n:2 * n]
        wait_copies(src_refs, land_refs, refs[2 * n], refs[2 * n + 1])

    outs = _pcall(
        body, name=name,
        in_specs=[HBM_SPEC] * (2 * n) + [SEM_SPEC, SEM_SPEC] + [ANY_SPEC] * len(after),
        out_shape=tuple(pltpu.HBM(a.shape, a.dtype) for a in srcs + lands),
        out_specs=tuple([HBM_SPEC] * (2 * n)),
        input_output_aliases={i: i for i in range(2 * n)},
        compiler_params=pltpu.CompilerParams(has_side_effects=DATAFLOW),
    )(*srcs, *lands, send_sems, recv_sems, *after)
    return list(outs[:n]), list(outs[n:])


def _ag_copies(src_refs, land_refs, send_sems, recv_sems, received):
    me, peers = _me_and_peers()
    cps = []
    for i in range(len(src_refs)):
        for k, to in enumerate(peers):
            cps.append(pltpu.make_async_remote_copy(
                src_ref=src_refs[i], dst_ref=land_refs[i].at[_slot(*(to if received else me))],
                send_sem=send_sems.at[4 * i + k], recv_sem=recv_sems.at[4 * i + k], device_id=to,
                device_id_type=MESH))
    return cps


def _ag_start(shards, after, name):
    lands = [lax.empty((N_DEV,) + a.shape, a.dtype) for a in shards]

    def start(src_refs, land_refs, send_sems, recv_sems):
        for cp in _ag_copies(src_refs, land_refs, send_sems, recv_sems, False):
            cp.start()

    return _split_start(shards, lands, after, start, (4 * len(shards),), name)


def _ag_finish(handle, after, name):
    def wait(src_refs, land_refs, send_sems, recv_sems):
        for cp in _ag_copies(src_refs, land_refs, send_sems, recv_sems, True):
            cp.wait_send()
            cp.wait_recv()

    shards, lands = _split_wait(handle, after, wait, name + "_wait")
    n = len(shards)

    def body(*refs):
        src_refs, out_refs = refs[:n], refs[2 * n:3 * n]
        send_sems, recv_sems, local_sems = refs[3 * n:3 * n + 3]
        bounce = refs[3 * n + 3:]
        me, peers = _me_and_peers()
        loads = [pltpu.make_async_copy(src_refs[i], bounce[i], local_sems.at[i]) for i in range(n)]
        mine = [pltpu.make_async_copy(bounce[i], out_refs[i].at[_slot(*me)], local_sems.at[i]) for i in range(n)]
        for cp in loads:
            cp.start()
        cps = []
        for i in range(n):
            for j, chip in enumerate(peers[1:]):
                blk = out_refs[i].at[_slot(*chip)]
                cps.append(pltpu.make_async_remote_copy(
                    src_ref=blk, dst_ref=blk, send_sem=send_sems.at[i, j], recv_sem=recv_sems.at[i, j],
                    device_id=peers[0], device_id_type=MESH))
        for cp in cps:
            cp.start()
        for ld, st in zip(loads, mine):
            ld.wait()
            st.start()
        for cp in cps:
            cp.wait()
        for cp in mine:
            cp.wait()

    outs = _pcall(
        body, name=name + "_pass",
        in_specs=[ANY_SPEC] * (2 * n), out_specs=[ANY_SPEC] * n,
        out_shape=[jax.ShapeDtypeStruct(a.shape, a.dtype) for a in lands],
        input_output_aliases={n + i: i for i in range(n)},
        scratch_shapes=[pltpu.SemaphoreType.DMA((n, 3)), pltpu.SemaphoreType.DMA((n, 3)),
                        pltpu.SemaphoreType.DMA((n,))] + [pltpu.VMEM(a.shape, a.dtype) for a in shards],
        compiler_params=pltpu.CompilerParams(vmem_limit_bytes=VMEM_LIMIT),
    )(*shards, *lands)
    return list(outs)


def _small_all_gather(v, name):
    R, C = v.shape

    def body(x_ref, out_ref, sum_ref, send_sems, recv_sems, local_sem):
        x, y, c = lax.axis_index("x"), lax.axis_index("y"), lax.axis_index("c")
        me, sibling = (x, y, c), (x, y, 1 - c)
        chips = [(1 - x, y), (x, 1 - y), (1 - x, 1 - y)]

        def rows(px, py, pc):
            return out_ref.at[4 * pc + 2 * px + py]

        def copy(k, block, to, src=None):
            return pltpu.make_async_remote_copy(
                src_ref=rows(*block) if src is None else src, dst_ref=rows(*block),
                send_sem=send_sems.at[k], recv_sem=recv_sems.at[k],
                device_id=to, device_id_type=MESH)

        mine = pltpu.make_async_copy(x_ref, rows(*me), local_sem)
        mine.start()
        first = [copy(0, me, sibling, src=x_ref)]
        first += [copy(1 + j, me, (*chip, c), src=x_ref) for j, chip in enumerate(chips)]
        for cp in first:
            cp.start()
        passed = [copy(4 + j, (*chip, c), sibling) for j, chip in enumerate(chips)]
        for j, chip in enumerate(chips):
            copy(1 + j, (*chip, c), me).wait_recv()
            passed[j].start()
        copy(0, sibling, me).wait_recv()
        for j, chip in enumerate(chips):
            copy(4 + j, (*chip, 1 - c), me).wait_recv()
        for cp in first + passed:
            cp.wait_send()
        mine.wait()
        acc = out_ref[0]
        for s in range(1, N_DEV):
            acc = acc + out_ref[s]
        sum_ref[...] = acc

    vm = pl.BlockSpec(memory_space=pltpu.VMEM)
    return _pcall(
        body, name=name, in_specs=[vm], out_specs=[vm, vm],
        out_shape=[jax.ShapeDtypeStruct((N_DEV, R, C), v.dtype), jax.ShapeDtypeStruct((R, C), v.dtype)],
        scratch_shapes=[pltpu.SemaphoreType.DMA((7,)), pltpu.SemaphoreType.DMA((7,)), pltpu.SemaphoreType.DMA],
    )(v)


def _rs_sibling(arrs, name):
    n = len(arrs)

    def body(*refs):
        ins, outs = refs[:n], refs[n:2 * n]
        send_sems, recv_sems = refs[2 * n:]
        x, y, c = lax.axis_index("x"), lax.axis_index("y"), lax.axis_index("c")
        cps = [pltpu.make_async_remote_copy(
            src_ref=ins[i].at[pl.ds(4 * (1 - c), 4)], dst_ref=outs[i],
            send_sem=send_sems.at[i], recv_sem=recv_sems.at[i],
            device_id=(x, y, 1 - c), device_id_type=MESH) for i in range(n)]
        for cp in cps:
            cp.start()
        for cp in cps:
            cp.wait()

    hbm = pl.BlockSpec(memory_space=pl.ANY)
    return _pcall(
        body, name=name, in_specs=[hbm] * n, out_specs=[hbm] * n,
        out_shape=[jax.ShapeDtypeStruct((4,) + a.shape[1:], a.dtype) for a in arrs],
        scratch_shapes=[pltpu.SemaphoreType.DMA((n,)), pltpu.SemaphoreType.DMA((n,))],
    )(*arrs)


def _rs_copies(src_refs, land_refs, send_sems, recv_sems):
    _, peers = _me_and_peers()
    cps = []
    for i in range(len(src_refs)):
        for j, (px, py, pc) in enumerate(peers[1:]):
            cps.append(pltpu.make_async_remote_copy(
                src_ref=src_refs[i].at[2 * px + py], dst_ref=land_refs[i].at[j],
                send_sem=send_sems.at[3 * i + j], recv_sem=recv_sems.at[3 * i + j],
                device_id=(px, py, pc), device_id_type=MESH))
    return cps


def _rs_start(chipsums, after, name):
    lands = [lax.empty((3,) + a.shape[1:], a.dtype) for a in chipsums]

    def start(src_refs, land_refs, send_sems, recv_sems):
        for cp in _rs_copies(src_refs, land_refs, send_sems, recv_sems):
            cp.start()

    return _split_start(chipsums, lands, after, start, (3 * len(chipsums),), name)


def _rs_wait(handle, after, name):
    def wait(src_refs, land_refs, send_sems, recv_sems):
        for cp in _rs_copies(src_refs, land_refs, send_sems, recv_sems):
            cp.wait_send()
            cp.wait_recv()

    return _split_wait(handle, after, wait, name)


def _row_tile(R, C, itemsize=4, budget=2 * 1024 * 1024):
    best = None
    for t in range(16, R + 1, 16):
        if R % t == 0 and t * C * itemsize <= budget:
            best = t
    return best if best is not None else R


def _add_half(arr, recv, c_idx, name):
    _, R, C = arr.shape
    tr = _row_tile(R, C)

    def body(c_ref, a_ref, r_ref, o_ref):
        o_ref[...] = (a_ref[...].astype(F32) + r_ref[...].astype(F32)).astype(o_ref.dtype)

    gs = pltpu.PrefetchScalarGridSpec(
        num_scalar_prefetch=1, grid=(4, R // tr),
        in_specs=[pl.BlockSpec((None, tr, C), lambda q, i, c_ref: (4 * c_ref[0] + q, i, 0)),
                  pl.BlockSpec((None, tr, C), lambda q, i, c_ref: (q, i, 0))],
        out_specs=pl.BlockSpec((None, tr, C), lambda q, i, c_ref: (q, i, 0)))
    return _pcall(body, name=name, grid_spec=gs, out_shape=jax.ShapeDtypeStruct((4, R, C), arr.dtype),
                  compiler_params=_params("parallel", "parallel"))(c_idx, arr, recv)


def _sum_chips(chipsum, recv, q_idx, name, transposed=False):
    _, R, C = chipsum.shape
    tr = min(512, R) if transposed else _row_tile(R, C)

    def body(q_ref, a_ref, r_ref, o_ref):
        acc = a_ref[...].astype(F32)
        for j in range(3):
            acc = acc + r_ref[j].astype(F32)
        o_ref[...] = acc.T if transposed else acc

    gs = pltpu.PrefetchScalarGridSpec(
        num_scalar_prefetch=1, grid=(R // tr,),
        in_specs=[pl.BlockSpec((None, tr, C), lambda i, q_ref: (q_ref[0], i, 0)),
                  pl.BlockSpec((3, tr, C), lambda i, q_ref: (0, i, 0))],
        out_specs=(pl.BlockSpec((C, tr), lambda i, q_ref: (0, i)) if transposed
                   else pl.BlockSpec((tr, C), lambda i, q_ref: (i, 0))))
    return _pcall(body, name=name, grid_spec=gs,
                  out_shape=jax.ShapeDtypeStruct((C, R) if transposed else (R, C), F32),
                  compiler_params=_params("parallel"))(q_idx, chipsum, recv)


def _adamw(w, g, m, v, name):
    shape = w.shape
    C = shape[-1]
    R = int(np.prod(shape[:-1]))
    tr = _row_tile(R, C, budget=1024 * 1024)

    def body(w_ref, g_ref, m_ref, v_ref, d_ref, nm_ref, nv_ref):
        gv = g_ref[...]
        mv = ADAM_B1 * m_ref[...] + (1.0 - ADAM_B1) * gv
        vv = ADAM_B2 * v_ref[...] + (1.0 - ADAM_B2) * jnp.square(gv)
        m_hat = mv / (1.0 - ADAM_B1 ** ADAM_STEP)
        v_hat = vv / (1.0 - ADAM_B2 ** ADAM_STEP)
        d_ref[...] = -ADAM_LR * (m_hat / (jnp.sqrt(v_hat) + ADAM_EPS) + ADAM_WD * w_ref[...])
        nm_ref[...] = mv
        nv_ref[...] = vv

    blk = pl.BlockSpec((tr, C), lambda i: (i, 0))
    shp = jax.ShapeDtypeStruct((R, C), F32)
    outs = _pcall(body, name=name, grid=(R // tr,), in_specs=[blk] * 4, out_specs=[blk] * 3,
                  out_shape=[shp] * 3, compiler_params=_params("parallel"))(
        w.reshape(R, C), g.reshape(R, C), m.reshape(R, C), v.reshape(R, C))
    return tuple(o.reshape(shape) for o in outs)


def _pad_cols(w, width):
    return jnp.pad(w, ((0, 0), (0, width - w.shape[1])))


def _pad_rows(w, height):
    return jnp.pad(w, ((0, height - w.shape[0]), (0, 0)))


def _slot_to_device_order(a):
    s = a.shape
    return a.reshape((2, 4) + s[1:]).swapaxes(0, 1).reshape(s)


def _device_to_slot_order(a):
    s = a.shape
    return a.reshape((4, 2) + s[1:]).swapaxes(0, 1).reshape(s)


def kernel(x, ffn1_w_gate, ffn1_w_up, ffn1_w_down, ffn2_w_gate, ffn2_w_up, ffn2_w_down, ln_gain, ln_bias, pool_w_in, pool_w_group, pool_scale, pool_w_out, attn_w_qkv, attn_w_out, loss_target, m_ffn1_w_gate, m_ffn1_w_up, m_ffn1_w_down, m_ffn2_w_gate, m_ffn2_w_up, m_ffn2_w_down, m_ln_gain, m_ln_bias, m_pool_w_in, m_pool_w_group, m_pool_scale, m_pool_w_out, m_attn_w_qkv, m_attn_w_out, v_ffn1_w_gate, v_ffn1_w_up, v_ffn1_w_down, v_ffn2_w_gate, v_ffn2_w_up, v_ffn2_w_down, v_ln_gain, v_ln_bias, v_pool_w_in, v_pool_w_group, v_pool_scale, v_pool_w_out, v_attn_w_qkv, v_attn_w_out):
    T = x.shape[1]
    fs = ffn1_w_gate.shape[2]
    fp = _round_up(fs, LANE)
    rs = D // N_DEV
    x0 = x[0]
    tgt = loss_target[0]
    slopes = _alibi_slopes()
    c_idx = lax.axis_index("c").astype(jnp.int32).reshape(1)
    q_idx = (2 * lax.axis_index("x") + lax.axis_index("y")).astype(jnp.int32).reshape(1)

    gates = (ffn1_w_gate, ffn2_w_gate)
    ups = (ffn1_w_up, ffn2_w_up)
    downs = (ffn1_w_down, ffn2_w_down)
    ffns = [(i, k) for i in range(DEPTH) for k in range(2)]
    def padded_cols(w, i):
        return _pad_rows(jnp.swapaxes(w, 1, 2)[i], fp).T

    wgu_sh = [jnp.concatenate([padded_cols(gates[k], i), padded_cols(ups[k], i)], axis=1).astype(BF16)
              for i, k in ffns]
    wd_sh = [_pad_rows(downs[k][i], fp).astype(BF16) for i, k in ffns]
    sq_sh = jnp.concatenate([wd_sh[0], pool_w_in[0].astype(BF16), pool_w_out[0].astype(BF16),
                             pool_w_group[0].reshape(rs // 4, D).astype(BF16)], axis=0)
    qkv_sh = attn_w_qkv[0].astype(BF16)
    aout_sh = attn_w_out[0].astype(BF16)
    ln_sh = jnp.concatenate([ln_gain.reshape(DEPTH * 3, rs), ln_bias.reshape(DEPTH * 3, rs),
                             jnp.zeros((4, rs), F32)], axis=0)

    h0 = _ag_start([wgu_sh[0], ln_sh], (), "ag0")
    x0b, x0t = _transpose_cast(x0, "x_cast", deps=(h0[4],))
    wgu0, ln_all = _ag_finish(h0, (x0b,), "ag0")
    h1 = _ag_start([sq_sh], (wgu0,), "ag1")
    ln_all = _slot_to_device_order(ln_all).transpose(1, 0, 2).reshape(16, D)
    gain = lambda i, s: ln_all[3 * i + s][None]
    bias = lambda i, s: ln_all[DEPTH * 3 + 3 * i + s][None]

    def ffn_fwd(xf, xb, wgu, wd, f, i, s, dep_up=(), dep_down=()):
        g, u, act = _ffn_up(xb, wgu, fp, f"ffn_up{f}", deps=dep_up)
        wd = wd(act) if callable(wd) else wd
        y, yb, yt, xh, rstd = _mm_ln(act, wd, xf, gain(i, s), bias(i, s), MACARON, f"ffn_down_ln{f}",
                                     deps=dep_down() if callable(dep_down) else dep_down)
        return (y, yb, yt), dict(g=g, u=u, act=act, xh=xh, rstd=rstd, wgu=wgu, wd=wd)

    pool_w = {}

    def wd0_after(act):
        (sq_all,) = _ag_finish(h1, (act,), "ag1")
        pool_w["h2"] = _ag_start([wgu_sh[1], wd_sh[1]], (sq_all,), "ag2")
        pool_w["pin"] = _slot_to_device_order(sq_all[:, fp:fp + rs, :]).reshape(D, D)
        pool_w["pout"] = _slot_to_device_order(sq_all[:, fp + rs:fp + 2 * rs, :]).reshape(D, D)
        grp = _slot_to_device_order(sq_all[:, fp + 2 * rs:, :])
        pool_w["grp"] = grp.reshape(N_DEV, N_POOL_GROUPS, rs // 4, POOL_GROUP_DIM).transpose(1, 0, 2, 3).reshape(
            N_POOL_GROUPS, POOL_GROUP_DIM, POOL_GROUP_DIM)
        return sq_all[:, :fp, :].reshape(N_DEV * fp, D)

    (a1, a1b, a1t), s_f0 = ffn_fwd(x0, x0b, wgu0, wd0_after, 0, 0, 0, dep_up=(h1[4],),
                                   dep_down=lambda: (pool_w["h2"][4],))
    h2 = pool_w["h2"]
    w_pin, w_pout, w_grp = pool_w["pin"], pool_w["pout"], pool_w["grp"]
    tm = min(512, T)
    row_spec = pl.BlockSpec((tm, D), lambda i, j, k: (i, 0))
    full_w = pl.BlockSpec((D, D), lambda i, j, k: (0, 0))
    u_pool = _mm(a1b, w_pin, grid=(T // tm, 1, 1), a_spec=row_spec, b_spec=full_w,
                 out_shape=jax.ShapeDtypeStruct((T, D), F32), out_spec=row_spec, name="pool_in")
    mixedb, mixedt = _pool_window(u_pool, False, "pool_window")
    y_pool, ysb, yst = _pool_group(mixedb, w_grp, pool_scale)
    a2, a2b, a2t, xh_p, rstd_p = _mm_ln(ysb, w_pout, a1, gain(0, 1), bias(0, 1), 1.0, "pool_out_ln")
    wgu1, wd1 = _ag_finish(h2, (a2,), "ag2")
    h3 = _ag_start([wgu_sh[2], wd_sh[2]], (wgu1,), "ag3")
    (a3, a3b, a3t), s_f1 = ffn_fwd(a2, a2b, wgu1, wd1.reshape(N_DEV * fp, D), 1, 0, 2, dep_up=(h3[4],))
    wgu2, wd2 = _ag_finish(h3, (a3,), "ag3")
    h4 = _ag_start([qkv_sh, aout_sh], (wgu2,), "ag4")
    (b1, b1b, b1t), s_f2 = ffn_fwd(a3, a3b, wgu2, wd2.reshape(N_DEV * fp, D), 2, 1, 0, dep_up=(h4[4],))
    wqkv_all, aout_all = _ag_finish(h4, (b1,), "ag4")
    h5 = _ag_start([wgu_sh[3], wd_sh[3]], (wqkv_all,), "ag5")
    w_aout = _slot_to_device_order(aout_all).reshape(D, D)
    dils = [d for _, d in DIL_CONFIGS]
    wqkv_nat = _slots_to_columns(wqkv_all, "attn_wqkv_cols")
    qkv_gs, o_gs, lse_gs = [], [], []
    for gi, d in enumerate(dils):
        qkv_g = _attn_qkv_group(b1b, wqkv_nat, gi, d, f"attn_qkv{gi}", deps=(h5[4],) if gi == 0 else ())
        o_g, lse_g = _attn_fwd(qkv_g, d, slopes[gi], f"attn_fwd{gi}")
        qkv_gs.append(qkv_g)
        o_gs.append(o_g)
        lse_gs.append(lse_g)
    o32, ob, ot, lse_tot = _attn_combine(o_gs, lse_gs, dils)
    b2, b2b, b2t, xh_a, rstd_a = _mm_ln(ob, w_aout, b1, gain(1, 1), bias(1, 1), 1.0, "attn_out_ln")
    wgu3, wd3 = _ag_finish(h5, (b2,), "ag5")
    (b3, _, _), s_f3 = ffn_fwd(b2, b2b, wgu3, wd3.reshape(N_DEV * fp, D), 3, 1, 2)

    dy, loss_tile = _loss_head(b3, tgt)
    loss = lax.psum(loss_tile[0, 0], AXES)

    bm = min(512, D)
    dgains, dbiases = {}, {}
    rs_pending = []
    gsums = {}

    def rs_finish(after):
        h, tag = rs_pending.pop()
        chips, lands = _rs_wait(h, after, f"rs_{tag}_wait")
        gsums[tag] = [_sum_chips(a, r, q_idx, f"rs_sum_{tag}{i}", transposed=tag.startswith("f"))
                      for i, (a, r) in enumerate(zip(chips, lands))]

    def rs_stage(bufs, tag):
        if rs_pending:
            rs_finish((bufs[-1],))
        recv = _rs_sibling(bufs, f"rs_sib_{tag}")
        chips = [_add_half(a, r, c_idx, f"rs_add_{tag}{i}") for i, (a, r) in enumerate(zip(bufs, recv))]
        h = _rs_start(chips, (), f"rs_{tag}")
        rs_pending.append((h, tag))
        return h[4]

    def ffn_bwd(dys, f, i, s, st, xt):
        dxres, dhb, dht, dg, db = _ln_bwd(dys, st["xh"], st["rstd"], gain(i, s), MACARON, f"ffn_ln_bwd{f}")
        dgains[(i, s)], dbiases[(i, s)] = dg, db
        dgu = _ffn_bwd_act(dhb, st["wd"], st["g"], st["u"], fp, f"ffn_bwd_act{f}")
        g_dt = _ffn_dwd(dht, st["act"], fp, f"ffn_dwd{f}")
        g_gu = _mm(xt, dgu, grid=(D // bm, N_DEV, 1),
                   a_spec=pl.BlockSpec((bm, T), lambda r, j, k: (r, 0)),
                   b_spec=pl.BlockSpec((T, 2 * fp), lambda r, j, k: (0, j)),
                   out_shape=jax.ShapeDtypeStruct((N_DEV, D, 2 * fp), BF16),
                   out_spec=pl.BlockSpec((None, bm, 2 * fp), lambda r, j, k: (j, r, 0)), name=f"ffn_dwgu{f}")
        token = rs_stage([g_dt, g_gu], f"f{f}")
        return [_ffn_dx(dgu, st["wgu"], dxres, fp, f"ffn_dx{f}", deps=(token,))]

    def dw_square(at, bmat, name):
        return _mm(at, bmat, grid=(D // bm, 1, 1),
                   a_spec=pl.BlockSpec((bm, T), lambda r, j, k: (r, 0)),
                   b_spec=pl.BlockSpec((T, D), lambda r, j, k: (0, 0)),
                   out_shape=jax.ShapeDtypeStruct((D, D), BF16),
                   out_spec=pl.BlockSpec((bm, D), lambda r, j, k: (r, 0)), name=name)

    def dx_square(a, w, name, deps=()):
        return _mm(a, w, grid=(T // tm, 1, 1), nt=True, a_spec=row_spec, b_spec=full_w,
                   out_shape=jax.ShapeDtypeStruct((T, D), F32), out_spec=row_spec, name=name, deps=deps)

    to_slots = lambda g2d: _device_to_slot_order(g2d.reshape(N_DEV, rs, D))

    d_b2 = ffn_bwd([dy], 3, 1, 2, s_f3, b2t)
    dxres, dmb, dmt, dg, db = _ln_bwd(d_b2, xh_a, rstd_a, gain(1, 1), 1.0, "attn_ln_bwd")
    dgains[(1, 1)], dbiases[(1, 1)] = dg, db
    g_aout = dw_square(ot, dmb, "attn_dwout")
    dobs, statss = _attn_bwd_prep(dmb, w_aout, o32, lse_tot, dils)
    ntile = 3 * D // QKV_TILE
    g_qkv = None
    dqkv_gs = []
    for gi, d in enumerate(dils):
        dqkv_g = _attn_bwd(qkv_gs[gi], dobs[gi], statss[gi], d, slopes[gi], f"attn_bwd{gi}")
        dqkv_gs.append(dqkv_g)
        xt = b1t if d == 1 else _transpose_sub(b1, d, f"attn_xt{gi}")
        g_qkv = _attn_dw_group(xt, dqkv_g, gi, d, g_qkv, f"attn_dwqkv{gi}")
    token = rs_stage([to_slots(g_aout), g_qkv], "attn")
    dx_attn = [_attn_dx_group(dqkv_gs[gi], wqkv_nat, gi, d, f"attn_dx{gi}", deps=(token,) if gi == 0 else ())
               for gi, d in enumerate(dils)]
    d_a3 = ffn_bwd([dxres] + dx_attn, 2, 1, 0, s_f2, a3t)
    d_a2 = ffn_bwd(d_a3, 1, 0, 2, s_f1, a2t)
    dxres, dmb, dmt, dg, db = _ln_bwd(d_a2, xh_p, rstd_p, gain(0, 1), 1.0, "pool_ln_bwd")
    dgains[(0, 1)], dbiases[(0, 1)] = dg, db
    g_pout = dw_square(yst, dmb, "pool_dwout")
    dyb, dscale = _pool_bwd_out(dmb, w_pout, y_pool, pool_scale)
    gd = POOL_GROUP_DIM
    g_grp = _mm(mixedt, dyb, grid=(N_POOL_GROUPS, 1, 1),
                a_spec=pl.BlockSpec((gd, T), lambda g, j, k: (g, 0)),
                b_spec=pl.BlockSpec((T, gd), lambda g, j, k: (0, g)),
                out_shape=jax.ShapeDtypeStruct((N_POOL_GROUPS, gd, gd), BF16),
                out_spec=pl.BlockSpec((None, gd, gd), lambda g, j, k: (g, 0, 0)), name="pool_dwgroup")
    tg = min(1024, T)
    dmixed = _mm(dyb, w_grp, grid=(N_POOL_GROUPS, T // tg, 1), nt=True,
                 a_spec=pl.BlockSpec((tg, gd), lambda g, t, k: (t, g)),
                 b_spec=pl.BlockSpec((None, gd, gd), lambda g, t, k: (g, 0, 0)),
                 out_shape=jax.ShapeDtypeStruct((T, D), F32),
                 out_spec=pl.BlockSpec((tg, gd), lambda g, t, k: (t, g)), name="pool_dmixed")
    dub, _ = _pool_window(dmixed, True, "pool_window_bwd")
    g_pin = dw_square(a1t, dub, "pool_dwin")
    g_grp_slots = _device_to_slot_order(
        g_grp.reshape(N_POOL_GROUPS, N_DEV, rs // 4, gd).transpose(1, 0, 2, 3).reshape(N_DEV, rs // 4, D))
    token = rs_stage([to_slots(g_pout), g_grp_slots, to_slots(g_pin)], "pool")
    dx_pool = dx_square(dub, w_pin, "pool_dx", deps=(token,))
    d_x0 = ffn_bwd([dxres, dx_pool], 0, 0, 0, s_f0, x0t)
    grad_x = d_x0[0]
    rs_finish((grad_x,))
    grad_x = grad_x[None]
    gw_dt = [gsums[f"f{f}"][0] for f in range(4)]
    gw_gu = [gsums[f"f{f}"][1] for f in range(4)]
    gw_aout, gw_qkv = gsums["attn"]
    gw_pout, gw_grp, gw_pin = gsums["pool"]

    small = jnp.concatenate([dgains[(i, s)] for i in range(DEPTH) for s in range(3)]
                            + [dbiases[(i, s)] for i in range(DEPTH) for s in range(3)]
                            + [dscale, jnp.zeros((3, D), F32)], axis=0)
    _, small_sum = _small_all_gather(small, "ag_small_grads")
    dev = 4 * lax.axis_index("x") + 2 * lax.axis_index("y") + lax.axis_index("c")
    mine = lax.dynamic_slice_in_dim(small_sum, dev * rs, rs, axis=1)
    grads = {
        "ffn1_w_gate": jnp.stack([gw_gu[2 * i][:fs] for i in range(DEPTH)]),
        "ffn1_w_up": jnp.stack([gw_gu[2 * i][fp:fp + fs] for i in range(DEPTH)]),
        "ffn1_w_down": jnp.stack([gw_dt[2 * i][:fs] for i in range(DEPTH)]),
        "ffn2_w_gate": jnp.stack([gw_gu[2 * i + 1][:fs] for i in range(DEPTH)]),
        "ffn2_w_up": jnp.stack([gw_gu[2 * i + 1][fp:fp + fs] for i in range(DEPTH)]),
        "ffn2_w_down": jnp.stack([gw_dt[2 * i + 1][:fs] for i in range(DEPTH)]),
        "ln_gain": mine[0:DEPTH * 3].reshape(DEPTH, 3, rs),
        "ln_bias": mine[DEPTH * 3:2 * DEPTH * 3].reshape(DEPTH, 3, rs),
        "pool_w_in": gw_pin[None],
        "pool_w_group": gw_grp[:rs // 4].reshape(N_POOL_GROUPS, rs // 4, gd)[None],
        "pool_scale": small_sum[2 * DEPTH * 3][None],
        "pool_w_out": gw_pout[None],
        "attn_w_qkv": gw_qkv[None],
        "attn_w_out": gw_aout[None],
    }
    weights = dict(ffn1_w_gate=ffn1_w_gate, ffn1_w_up=ffn1_w_up, ffn1_w_down=ffn1_w_down,
                   ffn2_w_gate=ffn2_w_gate, ffn2_w_up=ffn2_w_up, ffn2_w_down=ffn2_w_down,
                   ln_gain=ln_gain, ln_bias=ln_bias, pool_w_in=pool_w_in, pool_w_group=pool_w_group,
                   pool_scale=pool_scale, pool_w_out=pool_w_out, attn_w_qkv=attn_w_qkv, attn_w_out=attn_w_out)
    ms = dict(ffn1_w_gate=m_ffn1_w_gate, ffn1_w_up=m_ffn1_w_up, ffn1_w_down=m_ffn1_w_down,
              ffn2_w_gate=m_ffn2_w_gate, ffn2_w_up=m_ffn2_w_up, ffn2_w_down=m_ffn2_w_down,
              ln_gain=m_ln_gain, ln_bias=m_ln_bias, pool_w_in=m_pool_w_in, pool_w_group=m_pool_w_group,
              pool_scale=m_pool_scale, pool_w_out=m_pool_w_out, attn_w_qkv=m_attn_w_qkv, attn_w_out=m_attn_w_out)
    vs = dict(ffn1_w_gate=v_ffn1_w_gate, ffn1_w_up=v_ffn1_w_up, ffn1_w_down=v_ffn1_w_down,
              ffn2_w_gate=v_ffn2_w_gate, ffn2_w_up=v_ffn2_w_up, ffn2_w_down=v_ffn2_w_down,
              ln_gain=v_ln_gain, ln_bias=v_ln_bias, pool_w_in=v_pool_w_in, pool_w_group=v_pool_w_group,
              pool_scale=v_pool_scale, pool_w_out=v_pool_w_out, attn_w_qkv=v_attn_w_qkv, attn_w_out=v_attn_w_out)
    names = list(weights)
    col_sharded = ("ffn1_w_gate", "ffn1_w_up", "ffn2_w_gate", "ffn2_w_up")
    deltas, new_m, new_v = {}, {}, {}
    for nme in names:
        if nme in col_sharded:
            outs = _adamw(jnp.swapaxes(weights[nme], 1, 2), grads[nme], jnp.swapaxes(ms[nme], 1, 2),
                          jnp.swapaxes(vs[nme], 1, 2), f"adamw_{nme}")
            deltas[nme], new_m[nme], new_v[nme] = (jnp.swapaxes(o, 1, 2) for o in outs)
            grads[nme] = jnp.swapaxes(grads[nme], 1, 2)
        else:
            deltas[nme], new_m[nme], new_v[nme] = _adamw(weights[nme], grads[nme], ms[nme], vs[nme], f"adamw_{nme}")
    return (loss, grad_x, *[grads[k] for k in names], *[deltas[k] for k in names],
            *[new_m[k] for k in names], *[new_v[k] for k in names])
```

```python
import functools

import numpy as np
import jax
import jax.numpy as jnp
from jax import lax
from jax.experimental import pallas as pl
from jax.experimental.pallas import tpu as pltpu

F32 = jnp.float32
BF16 = jnp.bfloat16

D = 1024
N_DEV = 8
N_HEADS = 16
HEAD_DIM = 64
N_POOL_GROUPS = 4
POOL_GROUP_DIM = 256
POOL_HALF = (1, 2, 4, 8)
DIL_CONFIGS = ((128, 1), (512, 4), (2048, 16))
ATTN_HALO = 64
ATTN_BLOCK = 128
QKV_SHARD = 3 * 3 * D // N_DEV
DEPTH = 2
ALPHA = (2.0 * DEPTH) ** 0.25
MACARON = 0.5
LN_EPS = 1e-5
MASK_VALUE = -1e30
ADAM_LR = 0.001
ADAM_B1 = 0.9
ADAM_B2 = 0.999
ADAM_EPS = 1e-08
ADAM_WD = 0.01
ADAM_STEP = 10
LANE = 128
VMEM_LIMIT = 56 * 1024 * 1024
MESH = pl.DeviceIdType.MESH
AXES = ("x", "y", "c")


def _round_up(n, m):
    return (n + m - 1) // m * m


def _pcall(body, deps=(), **kw):
    if not deps:
        return pl.pallas_call(body, **kw)
    n_in, n_dep = len(kw["in_specs"]), len(deps)

    def wrapped(*refs):
        return body(*refs[:n_in], *refs[n_in + n_dep:])

    kw["in_specs"] = list(kw["in_specs"]) + [pl.BlockSpec(memory_space=pl.ANY)] * n_dep
    call = pl.pallas_call(wrapped, **kw)
    return lambda *args: call(*args, *deps)


def _params(*sem):
    return pltpu.CompilerParams(dimension_semantics=sem, vmem_limit_bytes=VMEM_LIMIT)


def _alibi_slopes():
    n = len(DIL_CONFIGS) * N_HEADS
    s = 2.0 ** (-8.0 * np.arange(1, n + 1) / n)
    return s.reshape(len(DIL_CONFIGS), N_HEADS).astype(np.float32)


def _my_slot():
    return 4 * lax.axis_index("c") + 2 * lax.axis_index("x") + lax.axis_index("y")


def _mm(a, b, *, grid, a_spec, b_spec, out_shape, out_spec, nt=False, name, alias=None, deps=()):
    nk = grid[2]
    dn = (((1,), (1,)), ((), ())) if nt else (((1,), (0,)), ((), ()))
    blk = tuple(s for s in out_spec.block_shape if s is not None)

    def body(*refs):
        a_ref, b_ref = refs[0], refs[1]
        o_ref = refs[3] if alias is not None else refs[2]
        p = lax.dot_general(a_ref[...], b_ref[...], dn, preferred_element_type=F32)
        if nk == 1:
            o_ref[...] = p.astype(o_ref.dtype)
        else:
            acc = refs[-1]
            k = pl.program_id(2)

            @pl.when(k == 0)
            def _():
                acc[...] = p

            @pl.when(k > 0)
            def _():
                acc[...] += p

            @pl.when(k == nk - 1)
            def _():
                o_ref[...] = acc[...].astype(o_ref.dtype)

    in_specs = [a_spec, b_spec]
    args = [a, b]
    aliases = {}
    if alias is not None:
        in_specs.append(pl.BlockSpec(memory_space=pl.ANY))
        args.append(alias)
        aliases = {2: 0}
    return _pcall(
        body, deps=deps, name=name, grid=grid, in_specs=in_specs, out_specs=out_spec, out_shape=out_shape,
        scratch_shapes=[] if nk == 1 else [pltpu.VMEM(blk, F32)],
        input_output_aliases=aliases,
        compiler_params=_params("parallel", "parallel", "arbitrary"),
    )(*args)


def _transpose_cast(x, name, deps=()):
    T = x.shape[0]
    tm = min(512, T)

    def body(x_ref, xb_ref, xt_ref):
        v = x_ref[...]
        xb_ref[...] = v.astype(BF16)
        xt_ref[...] = v.T.astype(BF16)

    return _pcall(
        body, deps=deps, name=name, grid=(T // tm,),
        in_specs=[pl.BlockSpec((tm, D), lambda t: (t, 0))],
        out_specs=[pl.BlockSpec((tm, D), lambda t: (t, 0)), pl.BlockSpec((D, tm), lambda t: (0, t))],
        out_shape=[jax.ShapeDtypeStruct((T, D), BF16), jax.ShapeDtypeStruct((D, T), BF16)],
        compiler_params=_params("parallel"),
    )(x)


def _mm_ln(a, b, xres, gain, bias, hscale, name, deps=()):
    T, K = a.shape
    tm = min(512, T)

    def body(a_ref, b_ref, x_ref, g_ref, bt_ref, y_ref, yb_ref, yt_ref, xh_ref, rs_ref):
        h = jnp.dot(a_ref[...], b_ref[...], preferred_element_type=F32)
        z = ALPHA * x_ref[...] + hscale * h
        mu = jnp.mean(z, axis=-1, keepdims=True)
        zc = z - mu
        var = jnp.mean(zc * zc, axis=-1, keepdims=True)
        rstd = lax.rsqrt(var + LN_EPS)
        xh = zc * rstd
        y = xh * g_ref[...] + bt_ref[...]
        y_ref[...] = y
        yb_ref[...] = y.astype(BF16)
        yt_ref[...] = y.T.astype(BF16)
        xh_ref[...] = xh
        rs_ref[...] = rstd

    row = pl.BlockSpec((tm, D), lambda t: (t, 0))
    vec = pl.BlockSpec((1, D), lambda t: (0, 0))
    return _pcall(
        body, deps=deps, name=name, grid=(T // tm,),
        in_specs=[pl.BlockSpec((tm, K), lambda t: (t, 0)), pl.BlockSpec((K, D), lambda t: (0, 0)), row, vec, vec],
        out_specs=[row, row, pl.BlockSpec((D, tm), lambda t: (0, t)), row, pl.BlockSpec((tm, 1), lambda t: (t, 0))],
        out_shape=[jax.ShapeDtypeStruct((T, D), F32), jax.ShapeDtypeStruct((T, D), BF16),
                   jax.ShapeDtypeStruct((D, T), BF16), jax.ShapeDtypeStruct((T, D), F32),
                   jax.ShapeDtypeStruct((T, 1), F32)],
        compiler_params=_params("parallel"),
    )(a, b, xres, gain, bias)


def _ln_bwd(dys, xhat, rstd, gain, hscale, name):
    T = xhat.shape[0]
    tm = min(512, T)
    n = len(dys)

    def body(*refs):
        dy_refs = refs[:n]
        xh_ref, rs_ref, g_ref, dx_ref, dh_ref, dht_ref, dg_ref, db_ref = refs[n:]
        dy = dy_refs[0][...]
        for r in dy_refs[1:]:
            dy = dy + r[...]
        xh = xh_ref[...]
        dxh = dy * g_ref[...]
        m1 = jnp.mean(dxh, axis=-1, keepdims=True)
        m2 = jnp.mean(dxh * xh, axis=-1, keepdims=True)
        dz = rs_ref[...] * (dxh - m1 - xh * m2)
        dx_ref[...] = ALPHA * dz
        dh = hscale * dz
        dh_ref[...] = dh.astype(BF16)
        dht_ref[...] = dh.T.astype(BF16)
        dg = jnp.sum(dy * xh, axis=0, keepdims=True)
        db = jnp.sum(dy, axis=0, keepdims=True)

        @pl.when(pl.program_id(0) == 0)
        def _():
            dg_ref[...] = dg
            db_ref[...] = db

        @pl.when(pl.program_id(0) > 0)
        def _():
            dg_ref[...] += dg
            db_ref[...] += db

    row = pl.BlockSpec((tm, D), lambda t: (t, 0))
    vec = pl.BlockSpec((1, D), lambda t: (0, 0))
    return _pcall(
        body, name=name, grid=(T // tm,),
        in_specs=[row] * n + [row, pl.BlockSpec((tm, 1), lambda t: (t, 0)), vec],
        out_specs=[row, row, pl.BlockSpec((D, tm), lambda t: (0, t)), vec, vec],
        out_shape=[jax.ShapeDtypeStruct((T, D), F32), jax.ShapeDtypeStruct((T, D), BF16),
                   jax.ShapeDtypeStruct((D, T), BF16), jax.ShapeDtypeStruct((1, D), F32),
                   jax.ShapeDtypeStruct((1, D), F32)],
        compiler_params=_params("arbitrary"),
    )(*dys, xhat, rstd, gain)


def _add2(a, b, name):
    T = a.shape[0]
    tm = min(512, T)

    def body(a_ref, b_ref, o_ref):
        o_ref[...] = a_ref[...] + b_ref[...]

    row = pl.BlockSpec((tm, D), lambda t: (t, 0))
    return _pcall(body, name=name, grid=(T // tm,), in_specs=[row, row], out_specs=row,
                  out_shape=jax.ShapeDtypeStruct((T, D), F32), compiler_params=_params("parallel"))(a, b)


def _loss_head(y, tgt):
    T = y.shape[0]
    tm = min(512, T)

    def body(y_ref, t_ref, dy_ref, l_ref):
        e = y_ref[...] - t_ref[...]
        dy_ref[...] = e * (1.0 / D)
        part = jnp.sum(jnp.sum(e * e, axis=1, keepdims=True), axis=0, keepdims=True) * (0.5 / D)

        @pl.when(pl.program_id(0) == 0)
        def _():
            l_ref[...] = jnp.zeros_like(l_ref)

        l_ref[...] += part

    row = pl.BlockSpec((tm, D), lambda t: (t, 0))
    return _pcall(
        body, name="loss_head", grid=(T // tm,),
        in_specs=[row, row],
        out_specs=[row, pl.BlockSpec((8, LANE), lambda t: (0, 0))],
        out_shape=[jax.ShapeDtypeStruct((T, D), F32), jax.ShapeDtypeStruct((8, LANE), F32)],
        compiler_params=_params("arbitrary"),
    )(y, tgt)


def _sigmoid(v):
    return 0.5 * jnp.tanh(0.5 * v) + 0.5


def _ffn_up(xb, wgu, fp, name, deps=()):
    T = xb.shape[0]
    tm = min(1024, T)

    def body(x_ref, w_ref, dg_ref, du_ref, a_ref):
        x = x_ref[...]
        for s in range(2):
            p = jnp.dot(x, w_ref[s], preferred_element_type=F32)
            g = p[:, :fp]
            u = p[:, fp:]
            sig = _sigmoid(g)
            q = g * sig
            cols = slice(s * fp, (s + 1) * fp)
            dg_ref[:, cols] = (sig * (1.0 + g - q) * u).astype(BF16)
            du_ref[:, cols] = q.astype(BF16)
            a_ref[:, cols] = (q * u).astype(BF16)

    out = pl.BlockSpec((tm, 2 * fp), lambda t, j: (t, j))
    shp = jax.ShapeDtypeStruct((T, N_DEV * fp), BF16)
    return _pcall(
        body, deps=deps, name=name, grid=(T // tm, N_DEV // 2),
        in_specs=[pl.BlockSpec((tm, D), lambda t, j: (t, 0)),
                  pl.BlockSpec((2, D, 2 * fp), lambda t, j: (j, 0, 0))],
        out_specs=[out, out, out], out_shape=[shp, shp, shp],
        compiler_params=_params("parallel", "parallel"),
    )(xb, wgu)


def _ffn_bwd_act(dhb, wd, g, u, fp, name):
    T = dhb.shape[0]
    tm = min(1024, T)

    def body(dh_ref, w_ref, g_ref, u_ref, o_ref):
        da = lax.dot_general(dh_ref[...], w_ref[...], (((1,), (1,)), ((), ())), preferred_element_type=F32)
        dgate = (da * g_ref[...].astype(F32)).astype(BF16)
        dup = (da * u_ref[...].astype(F32)).astype(BF16)
        for s in range(2):
            o_ref[:, 2 * s * fp:(2 * s + 1) * fp] = dgate[:, s * fp:(s + 1) * fp]
            o_ref[:, (2 * s + 1) * fp:(2 * s + 2) * fp] = dup[:, s * fp:(s + 1) * fp]

    blk = pl.BlockSpec((tm, 2 * fp), lambda t, j: (t, j))
    return _pcall(
        body, name=name, grid=(T // tm, N_DEV // 2),
        in_specs=[pl.BlockSpec((tm, D), lambda t, j: (t, 0)), pl.BlockSpec((2 * fp, D), lambda t, j: (j, 0)), blk, blk],
        out_specs=pl.BlockSpec((tm, 4 * fp), lambda t, j: (t, j)),
        out_shape=jax.ShapeDtypeStruct((T, N_DEV * 2 * fp), BF16),
        compiler_params=_params("parallel", "parallel"),
    )(dhb, wd, g, u)


def _ffn_dwd(dht, act, fp, name):
    T = dht.shape[1]
    bm = 512

    def body(a_ref, b_ref, o_ref):
        p = jnp.dot(a_ref[...], b_ref[...], preferred_element_type=F32)
        o_ref[0] = p[:, :fp].astype(BF16)
        o_ref[1] = p[:, fp:].astype(BF16)

    return _pcall(
        body, name=name, grid=(D // bm, N_DEV // 2),
        in_specs=[pl.BlockSpec((bm, T), lambda i, j: (i, 0)), pl.BlockSpec((T, 2 * fp), lambda i, j: (0, j))],
        out_specs=pl.BlockSpec((2, bm, fp), lambda i, j: (j, i, 0)),
        out_shape=jax.ShapeDtypeStruct((N_DEV, D, fp), BF16),
        compiler_params=_params("parallel", "parallel"),
    )(dht, act)


def _ffn_dx(dgu, wgu, res, fp, name, deps=()):
    T = dgu.shape[0]
    tm = min(512, T)

    def body(a_ref, w_ref, r_ref, o_ref):
        acc = r_ref[...]
        for j in range(N_DEV):
            p = lax.dot_general(a_ref[:, j * 2 * fp:(j + 1) * 2 * fp], w_ref[j], (((1,), (1,)), ((), ())),
                                preferred_element_type=F32)
            acc = acc + p
        o_ref[...] = acc

    return _pcall(
        body, deps=deps, name=name, grid=(T // tm,),
        in_specs=[pl.BlockSpec((tm, N_DEV * 2 * fp), lambda t: (t, 0)),
                  pl.BlockSpec((N_DEV, D, 2 * fp), lambda t: (0, 0, 0)),
                  pl.BlockSpec((tm, D), lambda t: (t, 0))],
        out_specs=pl.BlockSpec((tm, D), lambda t: (t, 0)),
        out_shape=jax.ShapeDtypeStruct((T, D), F32),
        compiler_params=_params("parallel"),
    )(dgu, wgu, res)


def _slots_to_columns(w, name):
    _, R, C = w.shape
    tr = min(512, R)

    def body(i_ref, o_ref):
        o_ref[...] = i_ref[...]

    return _pcall(
        body, name=name, grid=(N_DEV, R // tr),
        in_specs=[pl.BlockSpec((None, tr, C), lambda j, i: (4 * (j % 2) + j // 2, i, 0))],
        out_specs=pl.BlockSpec((tr, C), lambda j, i: (i, j)),
        out_shape=jax.ShapeDtypeStruct((R, N_DEV * C), w.dtype),
        compiler_params=_params("parallel", "parallel"),
    )(w)


POOL_PAD = 16
POOL_CHUNK = 512


def _pool_window(v, transpose, name):
    T = v.shape[0]
    ch = min(POOL_CHUNK, T)
    ext = ch + 2 * POOL_PAD
    gd = POOL_GROUP_DIM

    def body(v_ref, o_ref, ot_ref, pad_ref):
        pad_ref[0:POOL_PAD, :] = jnp.zeros((POOL_PAD, gd), F32)
        pad_ref[POOL_PAD + T:POOL_PAD + T + POOL_PAD, :] = jnp.zeros((POOL_PAD, gd), F32)
        for gi, hw in enumerate(POOL_HALF):
            @pl.when(pl.program_id(0) == gi)
            def _(hw=hw):
                def count(t):
                    return (jnp.minimum(t + hw, T) - jnp.maximum(t - hw, 0)).astype(F32)

                if transpose:
                    t_all = lax.broadcasted_iota(jnp.int32, (T, gd), 0)
                    pad_ref[POOL_PAD:POOL_PAD + T, :] = v_ref[...] / count(t_all)
                else:
                    pad_ref[POOL_PAD:POOL_PAD + T, :] = v_ref[...]
                shift = hw if transpose else hw - 1
                for c in range(T // ch):
                    e = pad_ref[c * ch:c * ch + ext, :]
                    step = 1
                    while step < 2 * hw:
                        e = e + pltpu.roll(e, step, 0)
                        step *= 2
                    if shift:
                        e = pltpu.roll(e, ext - shift, 0)
                    s = e[POOL_PAD:POOL_PAD + ch, :]
                    center = v_ref[c * ch:(c + 1) * ch, :]
                    if transpose:
                        res = s - center
                    else:
                        t_idx = c * ch + lax.broadcasted_iota(jnp.int32, (ch, gd), 0)
                        res = s / count(t_idx) - center
                    o_ref[c * ch:(c + 1) * ch, :] = res.astype(BF16)
                    ot_ref[:, c * ch:(c + 1) * ch] = res.T.astype(BF16)

    return _pcall(
        body, name=name, grid=(N_POOL_GROUPS,),
        in_specs=[pl.BlockSpec((T, gd), lambda g: (0, g))],
        out_specs=[pl.BlockSpec((T, gd), lambda g: (0, g)), pl.BlockSpec((gd, T), lambda g: (g, 0))],
        out_shape=[jax.ShapeDtypeStruct((T, D), BF16), jax.ShapeDtypeStruct((D, T), BF16)],
        scratch_shapes=[pltpu.VMEM((T + 2 * POOL_PAD, gd), F32)],
        compiler_params=_params("arbitrary"),
    )(v)


def _pool_group(mixedb, wgroup, scale):
    T = mixedb.shape[0]
    tm = min(1024, T)
    gd = POOL_GROUP_DIM

    def body(a_ref, w_ref, s_ref, y_ref, ys_ref, yst_ref):
        y = jnp.dot(a_ref[...], w_ref[...], preferred_element_type=F32)
        ys = y * s_ref[...]
        y_ref[...] = y
        ys_ref[...] = ys.astype(BF16)
        yst_ref[...] = ys.T.astype(BF16)

    blk = pl.BlockSpec((tm, gd), lambda g, t: (t, g))
    return _pcall(
        body, name="pool_group", grid=(N_POOL_GROUPS, T // tm),
        in_specs=[blk, pl.BlockSpec((None, gd, gd), lambda g, t: (g, 0, 0)), pl.BlockSpec((1, gd), lambda g, t: (0, g))],
        out_specs=[blk, blk, pl.BlockSpec((gd, tm), lambda g, t: (g, t))],
        out_shape=[jax.ShapeDtypeStruct((T, D), F32), jax.ShapeDtypeStruct((T, D), BF16),
                   jax.ShapeDtypeStruct((D, T), BF16)],
        compiler_params=_params("parallel", "parallel"),
    )(mixedb, wgroup, scale)


def _pool_bwd_out(dmb, w_out, y, scale):
    T = dmb.shape[0]
    tm = min(512, T)

    def body(a_ref, w_ref, y_ref, s_ref, dy_ref, ds_ref):
        dys = lax.dot_general(a_ref[...], w_ref[...], (((1,), (1,)), ((), ())), preferred_element_type=F32)
        dy_ref[...] = (dys * s_ref[...]).astype(BF16)
        part = jnp.sum(dys * y_ref[...], axis=0, keepdims=True)

        @pl.when(pl.program_id(0) == 0)
        def _():
            ds_ref[...] = part

        @pl.when(pl.program_id(0) > 0)
        def _():
            ds_ref[...] += part

    row = pl.BlockSpec((tm, D), lambda t: (t, 0))
    vec = pl.BlockSpec((1, D), lambda t: (0, 0))
    return _pcall(
        body, name="pool_bwd_out", grid=(T // tm,),
        in_specs=[row, pl.BlockSpec((D, D), lambda t: (0, 0)), row, vec],
        out_specs=[row, vec],
        out_shape=[jax.ShapeDtypeStruct((T, D), BF16), jax.ShapeDtypeStruct((1, D), F32)],
        compiler_params=_params("arbitrary"),
    )(dmb, w_out, y, scale)


def _attn_bias_table(d, slopes):
    w = ATTN_BLOCK + 2 * ATTN_HALO
    rel = np.arange(w)[None, :] - ATTN_HALO - np.arange(ATTN_BLOCK)[:, None]
    dist = (d * np.abs(rel)).astype(np.float32)
    bias = -np.asarray(slopes, np.float32)[:, None, None] * dist[None]
    return jnp.asarray(np.where(np.abs(rel)[None] <= ATTN_HALO, bias, np.float32(MASK_VALUE)).astype(np.float32))


def _attn_in_range(n, L):
    w = ATTN_BLOCK + 2 * ATTN_HALO
    j = n * ATTN_BLOCK - ATTN_HALO + lax.broadcasted_iota(jnp.int32, (ATTN_BLOCK, w), 1)
    return (j >= 0) & (j < L)


BIAS_SPEC = pl.BlockSpec((N_HEADS, ATTN_BLOCK, ATTN_BLOCK + 2 * ATTN_HALO), lambda r, n: (0, 0, 0))


def _lane_col(st, idx):
    lane = lax.broadcasted_iota(jnp.int32, st.shape, 1)
    return jnp.sum(jnp.where(lane == idx, st, 0.0), axis=1, keepdims=True)


def _window_specs(nb, d, col, width):
    last = 2 * d * nb - 1

    def prev(r, n):
        return (jnp.maximum(2 * (r * nb + n) - 1, 0), col)

    def cur(r, n):
        return (r * nb + n, col)

    def nxt(r, n):
        return (jnp.minimum(2 * (r * nb + n) + 2, last), col)

    return [pl.BlockSpec((ATTN_HALO, width), prev), pl.BlockSpec((ATTN_BLOCK, width), cur),
            pl.BlockSpec((ATTN_HALO, width), nxt)]


def _attn_fwd(qkv_g, d, slopes, name):
    T = qkv_g.shape[0]
    L = T // d
    nb = L // ATTN_BLOCK

    def body(q_ref, kp_ref, kc_ref, kn_ref, vp_ref, vc_ref, vn_ref, b_ref, o_ref, lse_ref):
        in_range = _attn_in_range(pl.program_id(1), L)
        lane = lax.broadcasted_iota(jnp.int32, (ATTN_BLOCK, LANE), 1)
        first = lane < HEAD_DIM
        sc = HEAD_DIM ** -0.5
        head_mask = [jnp.where(first, sc, 0.0).astype(BF16), jnp.where(first, 0.0, sc).astype(BF16)]
        lse_acc = jnp.zeros((ATTN_BLOCK, LANE), F32)
        for hp in range(N_HEADS // 2):
            cs = slice(hp * LANE, (hp + 1) * LANE)
            q2 = q_ref[:, cs]
            k2 = jnp.concatenate([kp_ref[:, cs], kc_ref[:, cs], kn_ref[:, cs]], axis=0)
            v2 = jnp.concatenate([vp_ref[:, cs], vc_ref[:, cs], vn_ref[:, cs]], axis=0)
            outs = []
            for hh in range(2):
                h = 2 * hp + hh
                qh = q2 * head_mask[hh]
                s = lax.dot_general(qh, k2, (((1,), (1,)), ((), ())), preferred_element_type=F32)
                s = jnp.where(in_range, s + b_ref[h], MASK_VALUE)
                m = jnp.max(s, axis=1, keepdims=True)
                p = jnp.exp(s - m)
                l = jnp.sum(p, axis=1, keepdims=True)
                o = jnp.dot(p.astype(BF16), v2, preferred_element_type=F32) / l
                outs.append(o)
                lse_acc = jnp.where(lane == h, m + jnp.log(l), lse_acc)
            o_ref[:, cs] = jnp.where(first, outs[0], outs[1])
        lse_ref[...] = lse_acc

    specs = ([pl.BlockSpec((ATTN_BLOCK, D), lambda r, n: (r * nb + n, 0))]
             + _window_specs(nb, d, 1, D) + _window_specs(nb, d, 2, D) + [BIAS_SPEC])
    row = lambda w: pl.BlockSpec((ATTN_BLOCK, w), lambda r, n: (r * nb + n, 0))
    return _pcall(
        body, name=name, grid=(d, nb), in_specs=specs,
        out_specs=[row(D), row(LANE)],
        out_shape=[jax.ShapeDtypeStruct((T, D), F32), jax.ShapeDtypeStruct((T, LANE), F32)],
        compiler_params=_params("parallel", "parallel"),
    )(*([qkv_g] * 7), _attn_bias_table(d, slopes))


def _stage(scr3, val):
    for c in range(val.shape[1] // LANE):
        scr3[c] = val[:, c * LANE:(c + 1) * LANE]


def _unstage(scr3):
    return jnp.concatenate([scr3[c] for c in range(scr3.shape[0])], axis=1)


def _gather_rows(scr3, r, n, d):
    return jnp.concatenate([scr3[c, pl.ds(r, n, stride=d), :] for c in range(scr3.shape[0])], axis=1)


def _scatter_rows(scr3, r, n, d, val):
    for c in range(scr3.shape[0]):
        scr3[c, pl.ds(r, n, stride=d), :] = val[:, c * LANE:(c + 1) * LANE]


def _attn_combine(os_, lses, dils):
    T = os_[0].shape[0]
    tm = min(256, T)
    ng = len(os_)
    n_scr = sum(1 for d in dils if d > 1)

    def body(*refs):
        in_o = refs[:ng]
        in_l = refs[ng:2 * ng]
        o32_ref, ob_ref, ot_ref, lt_ref = refs[2 * ng:2 * ng + 4]
        scr = refs[2 * ng + 4:]
        o_chunk, l_refs, si = [], [], 0
        for g, d in enumerate(dils):
            if d == 1:
                o_chunk.append(lambda hp, g=g: in_o[g][:, hp * LANE:(hp + 1) * LANE])
                l_refs.append(in_l[g])
                continue
            so, sl = scr[2 * si], scr[2 * si + 1]
            si += 1
            for r in range(d):
                _scatter_rows(so, r, tm // d, d, in_o[g][r])
                sl[pl.ds(r, tm // d, stride=d), :] = in_l[g][r]
            o_chunk.append(lambda hp, so=so: so[hp])
            l_refs.append(sl)
        ls = [r[...] for r in l_refs]
        m = ls[0]
        for l in ls[1:]:
            m = jnp.maximum(m, l)
        tot = jnp.exp(ls[0] - m)
        for l in ls[1:]:
            tot = tot + jnp.exp(l - m)
        lt = m + jnp.log(tot)
        lt_ref[...] = lt
        ws = [jnp.exp(l - lt) for l in ls]
        lane = lax.broadcasted_iota(jnp.int32, (tm, LANE), 1)
        first = lane < HEAD_DIM
        for hp in range(N_HEADS // 2):
            cs = slice(hp * LANE, (hp + 1) * LANE)
            acc = jnp.zeros((tm, LANE), F32)
            for g in range(ng):
                wt = jnp.where(first, _lane_col(ws[g], 2 * hp), _lane_col(ws[g], 2 * hp + 1))
                acc = acc + wt * o_chunk[g](hp)
            o32_ref[:, cs] = acc
            ob_ref[:, cs] = acc.astype(BF16)
        ot_ref[...] = o32_ref[...].T.astype(BF16)

    row = pl.BlockSpec((tm, D), lambda t: (t, 0))
    st = pl.BlockSpec((tm, LANE), lambda t: (t, 0))

    def sub_spec(d, w):
        return pl.BlockSpec((tm, w), lambda t: (t, 0)) if d == 1 else pl.BlockSpec((d, tm // d, w), lambda t: (0, t, 0))

    def sub_view(a, d):
        return a if d == 1 else a.reshape(d, T // d, a.shape[1])

    return _pcall(
        body, name="attn_combine", grid=(T // tm,),
        in_specs=[sub_spec(d, D) for d in dils] + [sub_spec(d, LANE) for d in dils],
        out_specs=[row, row, pl.BlockSpec((D, tm), lambda t: (0, t)), st],
        out_shape=[jax.ShapeDtypeStruct((T, D), F32), jax.ShapeDtypeStruct((T, D), BF16),
                   jax.ShapeDtypeStruct((D, T), BF16), jax.ShapeDtypeStruct((T, LANE), F32)],
        scratch_shapes=[pltpu.VMEM(s, F32) for _ in range(n_scr) for s in ((D // LANE, tm, LANE), (tm, LANE))],
        compiler_params=_params("parallel"),
    )(*[sub_view(a, d) for a, d in zip(os_, dils)], *[sub_view(a, d) for a, d in zip(lses, dils)])


def _attn_bwd_prep(dmb, w_out, o32, lse_tot, dils):
    T = dmb.shape[0]
    tm = min(512, T)
    ng = len(dils)

    def body(a_ref, w_ref, o_ref, l_ref, *rest):
        do_refs, st_refs = rest[:ng], rest[ng:2 * ng]
        do_scr, st_scr = rest[2 * ng:]
        do = lax.dot_general(a_ref[...], w_ref[...], (((1,), (1,)), ((), ())), preferred_element_type=F32)
        _stage(do_scr, do)
        prod = do * o_ref[...]
        lane = lax.broadcasted_iota(jnp.int32, (tm, LANE), 1)
        first = lane < HEAD_DIM
        st = jnp.where(lane < N_HEADS, l_ref[...], 0.0)
        for hp in range(N_HEADS // 2):
            pr = prod[:, hp * LANE:(hp + 1) * LANE]
            d0 = jnp.sum(jnp.where(first, pr, 0.0), axis=1, keepdims=True)
            d1 = jnp.sum(jnp.where(first, 0.0, pr), axis=1, keepdims=True)
            st = jnp.where(lane == N_HEADS + 2 * hp, d0, st)
            st = jnp.where(lane == N_HEADS + 2 * hp + 1, d1, st)
        st_scr[...] = st
        for g, d in enumerate(dils):
            if d == 1:
                do_refs[g][...] = do.astype(BF16)
                st_refs[g][...] = st
                continue
            for r in range(d):
                do_refs[g][r] = _gather_rows(do_scr, r, tm // d, d).astype(BF16)
                st_refs[g][r] = st_scr[pl.ds(r, tm // d, stride=d), :]

    row = pl.BlockSpec((tm, D), lambda t: (t, 0))
    stb = pl.BlockSpec((tm, LANE), lambda t: (t, 0))

    def sub_spec(d, w):
        return pl.BlockSpec((tm, w), lambda t: (t, 0)) if d == 1 else pl.BlockSpec((d, tm // d, w), lambda t: (0, t, 0))

    def sub_shape(d, w, dt):
        return jax.ShapeDtypeStruct((T, w) if d == 1 else (d, T // d, w), dt)

    outs = _pcall(
        body, name="attn_bwd_prep", grid=(T // tm,),
        in_specs=[row, pl.BlockSpec((D, D), lambda t: (0, 0)), row, stb],
        out_specs=[sub_spec(d, D) for d in dils] + [sub_spec(d, LANE) for d in dils],
        out_shape=[sub_shape(d, D, BF16) for d in dils] + [sub_shape(d, LANE, F32) for d in dils],
        scratch_shapes=[pltpu.VMEM((D // LANE, tm, LANE), F32), pltpu.VMEM((tm, LANE), F32)],
        compiler_params=_params("parallel"),
    )(dmb, w_out, o32, lse_tot)
    return ([o.reshape(T, D) for o in outs[:ng]], [o.reshape(T, LANE) for o in outs[ng:]])


def _attn_bwd(qkv_g, do_g, st_g, d, slopes, name):
    T = qkv_g.shape[0]
    L = T // d
    nb = L // ATTN_BLOCK
    scale = HEAD_DIM ** -0.5
    nt = (((1,), (1,)), ((), ()))

    tn = (((0,), (0,)), ((), ()))
    half = ATTN_HALO

    def body(q_ref, kp_ref, kc_ref, kn_ref, vp_ref, vc_ref, vn_ref, do_ref, st_ref, b_ref, o_ref,
             dq_new, dq_hold, win_k, win_v, carry_k, carry_v, hold_k, hold_v):
        n = pl.program_id(1)

        @pl.when(n == 0)
        def _():
            carry_k[...] = jnp.zeros_like(carry_k)
            carry_v[...] = jnp.zeros_like(carry_v)

        @pl.when(n == nb)
        def _():
            win_k[...] = jnp.zeros_like(win_k)
            win_v[...] = jnp.zeros_like(win_v)

        @pl.when(n < nb)
        def _():
            in_range = _attn_in_range(n, L)
            lane = lax.broadcasted_iota(jnp.int32, (ATTN_BLOCK, LANE), 1)
            first = lane < HEAD_DIM
            lane_w = lax.broadcasted_iota(jnp.int32, (ATTN_BLOCK + 2 * half, LANE), 1)
            first_w = lane_w < HEAD_DIM
            head_mask = [jnp.where(first, 1.0, 0.0).astype(BF16), jnp.where(first, 0.0, 1.0).astype(BF16)]
            scaled_mask = [jnp.where(first, scale, 0.0).astype(BF16), jnp.where(first, 0.0, scale).astype(BF16)]
            stc = st_ref[...]
            for hp in range(N_HEADS // 2):
                cs = slice(hp * LANE, (hp + 1) * LANE)
                q2, do2 = q_ref[:, cs], do_ref[:, cs]
                kw = jnp.concatenate([kp_ref[:, cs], kc_ref[:, cs], kn_ref[:, cs]], axis=0)
                vw = jnp.concatenate([vp_ref[:, cs], vc_ref[:, cs], vn_ref[:, cs]], axis=0)
                dqs, dks, dvs = [], [], []
                for hh in range(2):
                    h = 2 * hp + hh
                    s = lax.dot_general(q2 * scaled_mask[hh], kw, nt, preferred_element_type=F32)
                    s = jnp.where(in_range, s + b_ref[h], MASK_VALUE)
                    p = jnp.exp(s - _lane_col(stc, h))
                    dp = lax.dot_general(do2 * head_mask[hh], vw, nt, preferred_element_type=F32)
                    ds = (p * (dp - _lane_col(stc, N_HEADS + h))).astype(BF16)
                    dqs.append(jnp.dot(ds, kw, preferred_element_type=F32))
                    dvs.append(lax.dot_general(p.astype(BF16), do2, tn, preferred_element_type=F32))
                    dks.append(lax.dot_general(ds, q2, tn, preferred_element_type=F32))
                dq_new[:, cs] = (jnp.where(first, dqs[0], dqs[1]) * scale).astype(BF16)
                win_k[:, cs] = jnp.where(first_w, dks[0], dks[1]) * scale
                win_v[:, cs] = jnp.where(first_w, dvs[0], dvs[1])

        done_k = carry_k[...] + win_k[0:ATTN_BLOCK, :]
        done_v = carry_v[...] + win_v[0:ATTN_BLOCK, :]

        @pl.when(n >= 1)
        def _():
            o_ref[:, 0:D] = dq_hold[...]
            o_ref[0:half, D:2 * D] = hold_k[...].astype(BF16)
            o_ref[half:ATTN_BLOCK, D:2 * D] = done_k[0:half, :].astype(BF16)
            o_ref[0:half, 2 * D:3 * D] = hold_v[...].astype(BF16)
            o_ref[half:ATTN_BLOCK, 2 * D:3 * D] = done_v[0:half, :].astype(BF16)

        hold_k[...] = done_k[half:ATTN_BLOCK, :]
        hold_v[...] = done_v[half:ATTN_BLOCK, :]
        carry_k[...] = win_k[ATTN_BLOCK:, :]
        carry_v[...] = win_v[ATTN_BLOCK:, :]
        dq_hold[...] = dq_new[...]

    last = 2 * d * nb - 1

    def cur(col, width):
        return pl.BlockSpec((ATTN_BLOCK, width), lambda r, n: (r * nb + jnp.minimum(n, nb - 1), col))

    def window(col):
        blk = lambda r, n: r * nb + jnp.minimum(n, nb - 1)
        return [pl.BlockSpec((half, D), lambda r, n: (jnp.maximum(2 * blk(r, n) - 1, 0), col)),
                pl.BlockSpec((ATTN_BLOCK, D), lambda r, n: (blk(r, n), col)),
                pl.BlockSpec((half, D), lambda r, n: (jnp.minimum(2 * blk(r, n) + 2, last), col))]

    w = ATTN_BLOCK + 2 * half
    return _pcall(
        body, name=name, grid=(d, nb + 1),
        in_specs=[cur(0, D)] + window(1) + window(2) + [cur(0, D), cur(0, LANE), BIAS_SPEC],
        out_specs=pl.BlockSpec((ATTN_BLOCK, 3 * D), lambda r, n: (r * nb + jnp.maximum(n - 1, 0), 0)),
        out_shape=jax.ShapeDtypeStruct((T, 3 * D), BF16),
        scratch_shapes=[pltpu.VMEM((ATTN_BLOCK, D), BF16), pltpu.VMEM((ATTN_BLOCK, D), BF16),
                        pltpu.VMEM((w, D), F32), pltpu.VMEM((w, D), F32),
                        pltpu.VMEM((ATTN_BLOCK, D), F32), pltpu.VMEM((ATTN_BLOCK, D), F32),
                        pltpu.VMEM((half, D), F32), pltpu.VMEM((half, D), F32)],
        compiler_params=_params("parallel", "arbitrary"),
    )(qkv_g, *([qkv_g] * 6), do_g, st_g, _attn_bias_table(d, slopes))


QKV_TILE = QKV_SHARD // 3


def _qkv_tile_block(nn):
    dev = nn // 3
    return 4 * (dev % 2) + dev // 2, nn % 3


def _attn_qkv_group(xb, wqkv, gi, d, name, deps=()):
    T = xb.shape[0]
    tq = min(1024, T)
    nsub = tq // d
    tn = 768
    ntile = 3 * D // tn

    def body(x_ref, w_ref, o_ref, *scr):
        p = jnp.dot(x_ref[...], w_ref[...], preferred_element_type=F32)
        if d == 1:
            o_ref[...] = p.astype(BF16)
        else:
            _stage(scr[0], p)
            for r in range(d):
                o_ref[r] = _gather_rows(scr[0], r, nsub, d).astype(BF16)

    if d == 1:
        out_spec = pl.BlockSpec((tq, tn), lambda n, t: (t, n))
        out_shape = jax.ShapeDtypeStruct((T, 3 * D), BF16)
    else:
        out_spec = pl.BlockSpec((d, nsub, tn), lambda n, t: (0, t, n))
        out_shape = jax.ShapeDtypeStruct((d, T // d, 3 * D), BF16)
    out = _pcall(
        body, deps=deps, name=name, grid=(ntile, T // tq),
        in_specs=[pl.BlockSpec((tq, D), lambda n, t: (t, 0)),
                  pl.BlockSpec((D, tn), lambda n, t: (0, ntile * gi + n))],
        out_specs=out_spec, out_shape=out_shape,
        scratch_shapes=[] if d == 1 else [pltpu.VMEM((tn // LANE, tq, LANE), F32)],
        compiler_params=_params("parallel", "parallel"),
    )(xb, wqkv)
    return out.reshape(T, 3 * D)


def _attn_dx_group(dqkv_g, wqkv, gi, d, name, deps=()):
    T = dqkv_g.shape[0]
    tq = min(512, T)
    nsub = tq // d

    def body(a_ref, w_ref, o_ref, *stage):
        a = a_ref[...]
        if d > 1:
            a = a.reshape(tq, 3 * D)
        p = lax.dot_general(a, w_ref[...], (((1,), (1,)), ((), ())), preferred_element_type=F32)
        if d == 1:
            o_ref[...] = p
        else:
            for r in range(d):
                _scatter_rows(stage[0], r, nsub, d, p[r * nsub:(r + 1) * nsub, :])
            o_ref[...] = _unstage(stage[0])

    if d == 1:
        a_spec = pl.BlockSpec((tq, 3 * D), lambda t: (t, 0))
        a = dqkv_g
    else:
        a_spec = pl.BlockSpec((d, nsub, 3 * D), lambda t: (0, t, 0))
        a = dqkv_g.reshape(d, T // d, 3 * D)
    return _pcall(
        body, deps=deps, name=name, grid=(T // tq,),
        in_specs=[a_spec, pl.BlockSpec((D, 3 * D), lambda t: (0, gi))],
        out_specs=pl.BlockSpec((tq, D), lambda t: (t, 0)),
        out_shape=jax.ShapeDtypeStruct((T, D), F32),
        scratch_shapes=[] if d == 1 else [pltpu.VMEM((D // LANE, tq, LANE), F32)],
        compiler_params=_params("parallel"),
    )(a, wqkv)


def _attn_dw_group(xt, dqkv_g, gi, d, prev, name):
    T = dqkv_g.shape[0]
    L = T // d
    bm = 512
    ntile = 3 * D // QKV_TILE

    def body(*refs):
        a_ref, b_ref = refs[0], refs[1]
        o_ref, cat = refs[-2], refs[-1]
        if d == 1:
            a = a_ref[...]
        else:
            @pl.when(pl.program_id(1) == 0)
            def _():
                for r in range(d):
                    cat[:, r * L:(r + 1) * L] = a_ref[r]
            a = cat[...]
        o_ref[...] = jnp.dot(a, b_ref[...], preferred_element_type=F32).astype(BF16)

    def out_map(i, n):
        slot, sub = _qkv_tile_block(ntile * gi + n)
        return slot, i, sub

    a_spec = (pl.BlockSpec((bm, T), lambda i, n: (i, 0)) if d == 1
              else pl.BlockSpec((d, bm, L), lambda i, n: (0, i, 0)))
    in_specs = [a_spec, pl.BlockSpec((T, QKV_TILE), lambda i, n: (0, n))]
    args = [xt, dqkv_g]
    aliases = {}
    if prev is not None:
        in_specs.append(ANY_SPEC)
        args.append(prev)
        aliases = {2: 0}
    return _pcall(
        body, name=name, grid=(D // bm, ntile), in_specs=in_specs,
        out_specs=pl.BlockSpec((None, bm, QKV_TILE), out_map),
        out_shape=jax.ShapeDtypeStruct((N_DEV, D, QKV_SHARD), BF16),
        scratch_shapes=[pltpu.VMEM((bm, T), BF16)],
        input_output_aliases=aliases,
        compiler_params=_params("parallel", "arbitrary"),
    )(*args)


def _transpose_sub(x, d, name):
    T = x.shape[0]
    tm = LANE * d

    def body(x_ref, o_ref, scr):
        for c in range(D // LANE):
            scr[...] = x_ref[:, c * LANE:(c + 1) * LANE]
            for r in range(d):
                o_ref[r, c * LANE:(c + 1) * LANE, :] = scr[pl.ds(r, LANE, stride=d), :].T.astype(BF16)

    return _pcall(
        body, name=name, grid=(T // tm,),
        in_specs=[pl.BlockSpec((tm, D), lambda t: (t, 0))],
        out_specs=pl.BlockSpec((d, D, LANE), lambda t: (0, 0, t)),
        out_shape=jax.ShapeDtypeStruct((d, D, T // d), BF16),
        scratch_shapes=[pltpu.VMEM((tm, LANE), F32)],
        compiler_params=_params("parallel"),
    )(x)


HBM_SPEC = pl.BlockSpec(memory_space=pltpu.HBM)
SEM_SPEC = pl.BlockSpec(memory_space=pltpu.SEMAPHORE)
ANY_SPEC = pl.BlockSpec(memory_space=pl.ANY)
DATAFLOW = pltpu.SideEffectType.DATAFLOW_SIDE_EFFECTING


def _me_and_peers():
    x, y, c = lax.axis_index("x"), lax.axis_index("y"), lax.axis_index("c")
    return (x, y, c), [(x, y, 1 - c), (1 - x, y, c), (x, 1 - y, c), (1 - x, 1 - y, c)]


def _slot(px, py, pc):
    return 4 * pc + 2 * px + py


def _split_start(srcs, lands, after, start_copies, n_sem, name):
    n = len(srcs)
    n_after = len(after)

    def body(*refs):
        src_refs, land_refs = refs[:n], refs[n:2 * n]
        send_sems, recv_sems = refs[2 * n + n_after], refs[2 * n + n_after + 1]
        token = refs[-1]
        start_copies(src_refs, land_refs, send_sems, recv_sems)
        token[...] = jnp.zeros_like(token)

    outs = _pcall(
        body, name=name,
        in_specs=[HBM_SPEC] * (2 * n) + [ANY_SPEC] * n_after,
        out_shape=(pltpu.SemaphoreType.DMA(n_sem), pltpu.SemaphoreType.DMA(n_sem),
                   *[pltpu.HBM(a.shape, a.dtype) for a in srcs], *[pltpu.HBM(a.shape, a.dtype) for a in lands],
                   jax.ShapeDtypeStruct((8, LANE), F32)),
        out_specs=(SEM_SPEC, SEM_SPEC, *[HBM_SPEC] * (2 * n), pl.BlockSpec(memory_space=pltpu.VMEM)),
        input_output_aliases={i: 2 + i for i in range(2 * n)},
        compiler_params=pltpu.CompilerParams(has_side_effects=DATAFLOW),
    )(*[pltpu.with_memory_space_constraint(a, pltpu.HBM) for a in srcs],
      *[pltpu.with_memory_space_constraint(a, pltpu.HBM) for a in lands], *after)
    return outs[0], outs[1], list(outs[2:2 + n]), list(outs[2 + n:2 + 2 * n]), outs[-1]


def _split_wait(handle, after, wait_copies, name):
    send_sems, recv_sems, srcs, lands, _ = handle
    n = len(srcs)

    def body(*refs):
        src_refs, land_refs = refs[:n], refs[n:2 * n]
        wait_copies(src_refs, land_refs, refs[2 * n], refs[2 * n + 1])

    outs = _pcall(
        body, name=name,
        in_specs=[HBM_SPEC] * (2 * n) + [SEM_SPEC, SEM_SPEC] + [ANY_SPEC] * len(after),
        out_shape=tuple(pltpu.HBM(a.shape, a.dtype) for a in srcs + lands),
        out_specs=tuple([HBM_SPEC] * (2 * n)),
        input_output_aliases={i: i for i in range(2 * n)},
        compiler_params=pltpu.CompilerParams(has_side_effects=DATAFLOW),
    )(*srcs, *lands, send_sems, recv_sems, *after)
    return list(outs[:n]), list(outs[n:])


def _ag_copies(src_refs, land_refs, send_sems, recv_sems, received):
    me, peers = _me_and_peers()
    cps = []
    for i in range(len(src_refs)):
        for k, to in enumerate(peers):
            cps.append(pltpu.make_async_remote_copy(
                src_ref=src_refs[i], dst_ref=land_refs[i].at[_slot(*(to if received else me))],
                send_sem=send_sems.at[4 * i + k], recv_sem=recv_sems.at[4 * i + k], device_id=to,
                device_id_type=MESH))
    return cps


def _ag_start(shards, after, name):
    lands = [lax.empty((N_DEV,) + a.shape, a.dtype) for a in shards]

    def start(src_refs, land_refs, send_sems, recv_sems):
        for cp in _ag_copies(src_refs, land_refs, send_sems, recv_sems, False):
            cp.start()

    return _split_start(shards, lands, after, start, (4 * len(shards),), name)


def _ag_finish(handle, after, name):
    def wait(src_refs, land_refs, send_sems, recv_sems):
        for cp in _ag_copies(src_refs, land_refs, send_sems, recv_sems, True):
            cp.wait_send()
            cp.wait_recv()

    shards, lands = _split_wait(handle, after, wait, name + "_wait")
    n = len(shards)

    def body(*refs):
        src_refs, out_refs = refs[:n], refs[2 * n:3 * n]
        send_sems, recv_sems, local_sems = refs[3 * n:3 * n + 3]
        bounce = refs[3 * n + 3:]
        me, peers = _me_and_peers()
        loads = [pltpu.make_async_copy(src_refs[i], bounce[i], local_sems.at[i]) for i in range(n)]
        mine = [pltpu.make_async_copy(bounce[i], out_refs[i].at[_slot(*me)], local_sems.at[i]) for i in range(n)]
        for cp in loads:
            cp.start()
        cps = []
        for i in range(n):
            for j, chip in enumerate(peers[1:]):
                blk = out_refs[i].at[_slot(*chip)]
                cps.append(pltpu.make_async_remote_copy(
                    src_ref=blk, dst_ref=blk, send_sem=send_sems.at[i, j], recv_sem=recv_sems.at[i, j],
                    device_id=peers[0], device_id_type=MESH))
        for cp in cps:
            cp.start()
        for ld, st in zip(loads, mine):
            ld.wait()
            st.start()
        for cp in cps:
            cp.wait()
        for cp in mine:
            cp.wait()

    outs = _pcall(
        body, name=name + "_pass",
        in_specs=[ANY_SPEC] * (2 * n), out_specs=[ANY_SPEC] * n,
        out_shape=[jax.ShapeDtypeStruct(a.shape, a.dtype) for a in lands],
        input_output_aliases={n + i: i for i in range(n)},
        scratch_shapes=[pltpu.SemaphoreType.DMA((n, 3)), pltpu.SemaphoreType.DMA((n, 3)),
                        pltpu.SemaphoreType.DMA((n,))] + [pltpu.VMEM(a.shape, a.dtype) for a in shards],
        compiler_params=pltpu.CompilerParams(vmem_limit_bytes=VMEM_LIMIT),
    )(*shards, *lands)
    return list(outs)


def _small_all_gather(v, name):
    R, C = v.shape

    def body(x_ref, out_ref, sum_ref, send_sems, recv_sems, local_sem):
        x, y, c = lax.axis_index("x"), lax.axis_index("y"), lax.axis_index("c")
        me, sibling = (x, y, c), (x, y, 1 - c)
        chips = [(1 - x, y), (x, 1 - y), (1 - x, 1 - y)]

        def rows(px, py, pc):
            return out_ref.at[4 * pc + 2 * px + py]

        def copy(k, block, to, src=None):
            return pltpu.make_async_remote_copy(
                src_ref=rows(*block) if src is None else src, dst_ref=rows(*block),
                send_sem=send_sems.at[k], recv_sem=recv_sems.at[k],
                device_id=to, device_id_type=MESH)

        mine = pltpu.make_async_copy(x_ref, rows(*me), local_sem)
        mine.start()
        first = [copy(0, me, sibling, src=x_ref)]
        first += [copy(1 + j, me, (*chip, c), src=x_ref) for j, chip in enumerate(chips)]
        for cp in first:
            cp.start()
        passed = [copy(4 + j, (*chip, c), sibling) for j, chip in enumerate(chips)]
        for j, chip in enumerate(chips):
            copy(1 + j, (*chip, c), me).wait_recv()
            passed[j].start()
        copy(0, sibling, me).wait_recv()
        for j, chip in enumerate(chips):
            copy(4 + j, (*chip, 1 - c), me).wait_recv()
        for cp in first + passed:
            cp.wait_send()
        mine.wait()
        acc = out_ref[0]
        for s in range(1, N_DEV):
            acc = acc + out_ref[s]
        sum_ref[...] = acc

    vm = pl.BlockSpec(memory_space=pltpu.VMEM)
    return _pcall(
        body, name=name, in_specs=[vm], out_specs=[vm, vm],
        out_shape=[jax.ShapeDtypeStruct((N_DEV, R, C), v.dtype), jax.ShapeDtypeStruct((R, C), v.dtype)],
        scratch_shapes=[pltpu.SemaphoreType.DMA((7,)), pltpu.SemaphoreType.DMA((7,)), pltpu.SemaphoreType.DMA],
    )(v)


def _rs_sibling(arrs, name):
    n = len(arrs)

    def body(*refs):
        ins, outs = refs[:n], refs[n:2 * n]
        send_sems, recv_sems = refs[2 * n:]
        x, y, c = lax.axis_index("x"), lax.axis_index("y"), lax.axis_index("c")
        cps = [pltpu.make_async_remote_copy(
            src_ref=ins[i].at[pl.ds(4 * (1 - c), 4)], dst_ref=outs[i],
            send_sem=send_sems.at[i], recv_sem=recv_sems.at[i],
            device_id=(x, y, 1 - c), device_id_type=MESH) for i in range(n)]
        for cp in cps:
            cp.start()
        for cp in cps:
            cp.wait()

    hbm = pl.BlockSpec(memory_space=pl.ANY)
    return _pcall(
        body, name=name, in_specs=[hbm] * n, out_specs=[hbm] * n,
        out_shape=[jax.ShapeDtypeStruct((4,) + a.shape[1:], a.dtype) for a in arrs],
        scratch_shapes=[pltpu.SemaphoreType.DMA((n,)), pltpu.SemaphoreType.DMA((n,))],
    )(*arrs)


def _rs_copies(src_refs, land_refs, send_sems, recv_sems):
    _, peers = _me_and_peers()
    cps = []
    for i in range(len(src_refs)):
        for j, (px, py, pc) in enumerate(peers[1:]):
            cps.append(pltpu.make_async_remote_copy(
                src_ref=src_refs[i].at[2 * px + py], dst_ref=land_refs[i].at[j],
                send_sem=send_sems.at[3 * i + j], recv_sem=recv_sems.at[3 * i + j],
                device_id=(px, py, pc), device_id_type=MESH))
    return cps


def _rs_start(chipsums, after, name):
    lands = [lax.empty((3,) + a.shape[1:], a.dtype) for a in chipsums]

    def start(src_refs, land_refs, send_sems, recv_sems):
        for cp in _rs_copies(src_refs, land_refs, send_sems, recv_sems):
            cp.start()

    return _split_start(chipsums, lands, after, start, (3 * len(chipsums),), name)


def _rs_wait(handle, after, name):
    def wait(src_refs, land_refs, send_sems, recv_sems):
        for cp in _rs_copies(src_refs, land_refs, send_sems, recv_sems):
            cp.wait_send()
            cp.wait_recv()

    return _split_wait(handle, after, wait, name)


def _row_tile(R, C, itemsize=4, budget=2 * 1024 * 1024):
    best = None
    for t in range(16, R + 1, 16):
        if R % t == 0 and t * C * itemsize <= budget:
            best = t
    return best if best is not None else R


def _add_half(arr, recv, c_idx, name):
    _, R, C = arr.shape
    tr = _row_tile(R, C)

    def body(c_ref, a_ref, r_ref, o_ref):
        o_ref[...] = (a_ref[...].astype(F32) + r_ref[...].astype(F32)).astype(o_ref.dtype)

    gs = pltpu.PrefetchScalarGridSpec(
        num_scalar_prefetch=1, grid=(4, R // tr),
        in_specs=[pl.BlockSpec((None, tr, C), lambda q, i, c_ref: (4 * c_ref[0] + q, i, 0)),
                  pl.BlockSpec((None, tr, C), lambda q, i, c_ref: (q, i, 0))],
        out_specs=pl.BlockSpec((None, tr, C), lambda q, i, c_ref: (q, i, 0)))
    return _pcall(body, name=name, grid_spec=gs, out_shape=jax.ShapeDtypeStruct((4, R, C), arr.dtype),
                  compiler_params=_params("parallel", "parallel"))(c_idx, arr, recv)


def _sum_chips(chipsum, recv, q_idx, name, transposed=False):
    _, R, C = chipsum.shape
    tr = min(512, R) if transposed else _row_tile(R, C)

    def body(q_ref, a_ref, r_ref, o_ref):
        acc = a_ref[...].astype(F32)
        for j in range(3):
            acc = acc + r_ref[j].astype(F32)
        o_ref[...] = acc.T if transposed else acc

    gs = pltpu.PrefetchScalarGridSpec(
        num_scalar_prefetch=1, grid=(R // tr,),
        in_specs=[pl.BlockSpec((None, tr, C), lambda i, q_ref: (q_ref[0], i, 0)),
                  pl.BlockSpec((3, tr, C), lambda i, q_ref: (0, i, 0))],
        out_specs=(pl.BlockSpec((C, tr), lambda i, q_ref: (0, i)) if transposed
                   else pl.BlockSpec((tr, C), lambda i, q_ref: (i, 0))))
    return _pcall(body, name=name, grid_spec=gs,
                  out_shape=jax.ShapeDtypeStruct((C, R) if transposed else (R, C), F32),
                  compiler_params=_params("parallel"))(q_idx, chipsum, recv)


def _adamw(w, g, m, v, name):
    shape = w.shape
    C = shape[-1]
    R = int(np.prod(shape[:-1]))
    tr = _row_tile(R, C, budget=1024 * 1024)

    def body(w_ref, g_ref, m_ref, v_ref, d_ref, nm_ref, nv_ref):
        gv = g_ref[...]
        mv = ADAM_B1 * m_ref[...] + (1.0 - ADAM_B1) * gv
        vv = ADAM_B2 * v_ref[...] + (1.0 - ADAM_B2) * jnp.square(gv)
        m_hat = mv / (1.0 - ADAM_B1 ** ADAM_STEP)
        v_hat = vv / (1.0 - ADAM_B2 ** ADAM_STEP)
        d_ref[...] = -ADAM_LR * (m_hat / (jnp.sqrt(v_hat) + ADAM_EPS) + ADAM_WD * w_ref[...])
        nm_ref[...] = mv
        nv_ref[...] = vv

    blk = pl.BlockSpec((tr, C), lambda i: (i, 0))
    shp = jax.ShapeDtypeStruct((R, C), F32)
    outs = _pcall(body, name=name, grid=(R // tr,), in_specs=[blk] * 4, out_specs=[blk] * 3,
                  out_shape=[shp] * 3, compiler_params=_params("parallel"))(
        w.reshape(R, C), g.reshape(R, C), m.reshape(R, C), v.reshape(R, C))
    return tuple(o.reshape(shape) for o in outs)


def _pad_cols(w, width):
    return jnp.pad(w, ((0, 0), (0, width - w.shape[1])))


def _pad_rows(w, height):
    return jnp.pad(w, ((0, height - w.shape[0]), (0, 0)))


def _slot_to_device_order(a):
    s = a.shape
    return a.reshape((2, 4) + s[1:]).swapaxes(0, 1).reshape(s)


def _device_to_slot_order(a):
    s = a.shape
    return a.reshape((4, 2) + s[1:]).swapaxes(0, 1).reshape(s)


def kernel(x, ffn1_w_gate, ffn1_w_up, ffn1_w_down, ffn2_w_gate, ffn2_w_up, ffn2_w_down, ln_gain, ln_bias, pool_w_in, pool_w_group, pool_scale, pool_w_out, attn_w_qkv, attn_w_out, loss_target, m_ffn1_w_gate, m_ffn1_w_up, m_ffn1_w_down, m_ffn2_w_gate, m_ffn2_w_up, m_ffn2_w_down, m_ln_gain, m_ln_bias, m_pool_w_in, m_pool_w_group, m_pool_scale, m_pool_w_out, m_attn_w_qkv, m_attn_w_out, v_ffn1_w_gate, v_ffn1_w_up, v_ffn1_w_down, v_ffn2_w_gate, v_ffn2_w_up, v_ffn2_w_down, v_ln_gain, v_ln_bias, v_pool_w_in, v_pool_w_group, v_pool_scale, v_pool_w_out, v_attn_w_qkv, v_attn_w_out):
    T = x.shape[1]
    fs = ffn1_w_gate.shape[2]
    fp = _round_up(fs, LANE)
    rs = D // N_DEV
    x0 = x[0]
    tgt = loss_target[0]
    slopes = _alibi_slopes()
    c_idx = lax.axis_index("c").astype(jnp.int32).reshape(1)
    q_idx = (2 * lax.axis_index("x") + lax.axis_index("y")).astype(jnp.int32).reshape(1)

    gates = (ffn1_w_gate, ffn2_w_gate)
    ups = (ffn1_w_up, ffn2_w_up)
    downs = (ffn1_w_down, ffn2_w_down)
    ffns = [(i, k) for i in range(DEPTH) for k in range(2)]
    def padded_cols(w, i):
        return _pad_rows(jnp.swapaxes(w, 1, 2)[i], fp).T

    wgu_sh = [jnp.concatenate([padded_cols(gates[k], i), padded_cols(ups[k], i)], axis=1).astype(BF16)
              for i, k in ffns]
    wd_sh = [_pad_rows(downs[k][i], fp).astype(BF16) for i, k in ffns]
    sq_sh = jnp.concatenate([wd_sh[0], pool_w_in[0].astype(BF16), pool_w_out[0].astype(BF16),
                             pool_w_group[0].reshape(rs // 4, D).astype(BF16)], axis=0)
    qkv_sh = attn_w_qkv[0].astype(BF16)
    aout_sh = attn_w_out[0].astype(BF16)
    ln_sh = jnp.concatenate([ln_gain.reshape(DEPTH * 3, rs), ln_bias.reshape(DEPTH * 3, rs),
                             jnp.zeros((4, rs), F32)], axis=0)

    h0 = _ag_start([wgu_sh[0], ln_sh], (), "ag0")
    x0b, x0t = _transpose_cast(x0, "x_cast", deps=(h0[4],))
    wgu0, ln_all = _ag_finish(h0, (x0b,), "ag0")
    h1 = _ag_start([sq_sh], (wgu0,), "ag1")
    ln_all = _slot_to_device_order(ln_all).transpose(1, 0, 2).reshape(16, D)
    gain = lambda i, s: ln_all[3 * i + s][None]
    bias = lambda i, s: ln_all[DEPTH * 3 + 3 * i + s][None]

    def ffn_fwd(xf, xb, wgu, wd, f, i, s, dep_up=(), dep_down=()):
        g, u, act = _ffn_up(xb, wgu, fp, f"ffn_up{f}", deps=dep_up)
        wd = wd(act) if callable(wd) else wd
        y, yb, yt, xh, rstd = _mm_ln(act, wd, xf, gain(i, s), bias(i, s), MACARON, f"ffn_down_ln{f}",
                                     deps=dep_down() if callable(dep_down) else dep_down)
        return (y, yb, yt), dict(g=g, u=u, act=act, xh=xh, rstd=rstd, wgu=wgu, wd=wd)

    pool_w = {}

    def wd0_after(act):
        (sq_all,) = _ag_finish(h1, (act,), "ag1")
        pool_w["h2"] = _ag_start([wgu_sh[1], wd_sh[1]], (sq_all,), "ag2")
        pool_w["pin"] = _slot_to_device_order(sq_all[:, fp:fp + rs, :]).reshape(D, D)
        pool_w["pout"] = _slot_to_device_order(sq_all[:, fp + rs:fp + 2 * rs, :]).reshape(D, D)
        grp = _slot_to_device_order(sq_all[:, fp + 2 * rs:, :])
        pool_w["grp"] = grp.reshape(N_DEV, N_POOL_GROUPS, rs // 4, POOL_GROUP_DIM).transpose(1, 0, 2, 3).reshape(
            N_POOL_GROUPS, POOL_GROUP_DIM, POOL_GROUP_DIM)
        return sq_all[:, :fp, :].reshape(N_DEV * fp, D)

    (a1, a1b, a1t), s_f0 = ffn_fwd(x0, x0b, wgu0, wd0_after, 0, 0, 0, dep_up=(h1[4],),
                                   dep_down=lambda: (pool_w["h2"][4],))
    h2 = pool_w["h2"]
    w_pin, w_pout, w_grp = pool_w["pin"], pool_w["pout"], pool_w["grp"]
    tm = min(512, T)
    row_spec = pl.BlockSpec((tm, D), lambda i, j, k: (i, 0))
    full_w = pl.BlockSpec((D, D), lambda i, j, k: (0, 0))
    u_pool = _mm(a1b, w_pin, grid=(T // tm, 1, 1), a_spec=row_spec, b_spec=full_w,
                 out_shape=jax.ShapeDtypeStruct((T, D), F32), out_spec=row_spec, name="pool_in")
    mixedb, mixedt = _pool_window(u_pool, False, "pool_window")
    y_pool, ysb, yst = _pool_group(mixedb, w_grp, pool_scale)
    a2, a2b, a2t, xh_p, rstd_p = _mm_ln(ysb, w_pout, a1, gain(0, 1), bias(0, 1), 1.0, "pool_out_ln")
    wgu1, wd1 = _ag_finish(h2, (a2,), "ag2")
    h3 = _ag_start([wgu_sh[2], wd_sh[2]], (wgu1,), "ag3")
    (a3, a3b, a3t), s_f1 = ffn_fwd(a2, a2b, wgu1, wd1.reshape(N_DEV * fp, D), 1, 0, 2, dep_up=(h3[4],))
    wgu2, wd2 = _ag_finish(h3, (a3,), "ag3")
    h4 = _ag_start([qkv_sh, aout_sh], (wgu2,), "ag4")
    (b1, b1b, b1t), s_f2 = ffn_fwd(a3, a3b, wgu2, wd2.reshape(N_DEV * fp, D), 2, 1, 0, dep_up=(h4[4],))
    wqkv_all, aout_all = _ag_finish(h4, (b1,), "ag4")
    h5 = _ag_start([wgu_sh[3], wd_sh[3]], (wqkv_all,), "ag5")
    w_aout = _slot_to_device_order(aout_all).reshape(D, D)
    dils = [d for _, d in DIL_CONFIGS]
    wqkv_nat = _slots_to_columns(wqkv_all, "attn_wqkv_cols")
    qkv_gs, o_gs, lse_gs = [], [], []
    for gi, d in enumerate(dils):
        qkv_g = _attn_qkv_group(b1b, wqkv_nat, gi, d, f"attn_qkv{gi}", deps=(h5[4],) if gi == 0 else ())
        o_g, lse_g = _attn_fwd(qkv_g, d, slopes[gi], f"attn_fwd{gi}")
        qkv_gs.append(qkv_g)
        o_gs.append(o_g)
        lse_gs.append(lse_g)
    o32, ob, ot, lse_tot = _attn_combine(o_gs, lse_gs, dils)
    b2, b2b, b2t, xh_a, rstd_a = _mm_ln(ob, w_aout, b1, gain(1, 1), bias(1, 1), 1.0, "attn_out_ln")
    wgu3, wd3 = _ag_finish(h5, (b2,), "ag5")
    (b3, _, _), s_f3 = ffn_fwd(b2, b2b, wgu3, wd3.reshape(N_DEV * fp, D), 3, 1, 2)

    dy, loss_tile = _loss_head(b3, tgt)
    loss = lax.psum(loss_tile[0, 0], AXES)

    bm = min(512, D)
    dgains, dbiases = {}, {}
    rs_pending = []
    gsums = {}

    def rs_finish(after):
        h, tag = rs_pending.pop()
        chips, lands = _rs_wait(h, after, f"rs_{tag}_wait")
        gsums[tag] = [_sum_chips(a, r, q_idx, f"rs_sum_{tag}{i}", transposed=tag.startswith("f"))
                      for i, (a, r) in enumerate(zip(chips, lands))]

    def rs_stage(bufs, tag):
        if rs_pending:
            rs_finish((bufs[-1],))
        recv = _rs_sibling(bufs, f"rs_sib_{tag}")
        chips = [_add_half(a, r, c_idx, f"rs_add_{tag}{i}") for i, (a, r) in enumerate(zip(bufs, recv))]
        h = _rs_start(chips, (), f"rs_{tag}")
        rs_pending.append((h, tag))
        return h[4]

    def ffn_bwd(dys, f, i, s, st, xt):
        dxres, dhb, dht, dg, db = _ln_bwd(dys, st["xh"], st["rstd"], gain(i, s), MACARON, f"ffn_ln_bwd{f}")
        dgains[(i, s)], dbiases[(i, s)] = dg, db
        dgu = _ffn_bwd_act(dhb, st["wd"], st["g"], st["u"], fp, f"ffn_bwd_act{f}")
        g_dt = _ffn_dwd(dht, st["act"], fp, f"ffn_dwd{f}")
        g_gu = _mm(xt, dgu, grid=(D // bm, N_DEV, 1),
                   a_spec=pl.BlockSpec((bm, T), lambda r, j, k: (r, 0)),
                   b_spec=pl.BlockSpec((T, 2 * fp), lambda r, j, k: (0, j)),
                   out_shape=jax.ShapeDtypeStruct((N_DEV, D, 2 * fp), BF16),
                   out_spec=pl.BlockSpec((None, bm, 2 * fp), lambda r, j, k: (j, r, 0)), name=f"ffn_dwgu{f}")
        token = rs_stage([g_dt, g_gu], f"f{f}")
        return [_ffn_dx(dgu, st["wgu"], dxres, fp, f"ffn_dx{f}", deps=(token,))]

    def dw_square(at, bmat, name):
        return _mm(at, bmat, grid=(D // bm, 1, 1),
                   a_spec=pl.BlockSpec((bm, T), lambda r, j, k: (r, 0)),
                   b_spec=pl.BlockSpec((T, D), lambda r, j, k: (0, 0)),
                   out_shape=jax.ShapeDtypeStruct((D, D), BF16),
                   out_spec=pl.BlockSpec((bm, D), lambda r, j, k: (r, 0)), name=name)

    def dx_square(a, w, name, deps=()):
        return _mm(a, w, grid=(T // tm, 1, 1), nt=True, a_spec=row_spec, b_spec=full_w,
                   out_shape=jax.ShapeDtypeStruct((T, D), F32), out_spec=row_spec, name=name, deps=deps)

    to_slots = lambda g2d: _device_to_slot_order(g2d.reshape(N_DEV, rs, D))

    d_b2 = ffn_bwd([dy], 3, 1, 2, s_f3, b2t)
    dxres, dmb, dmt, dg, db = _ln_bwd(d_b2, xh_a, rstd_a, gain(1, 1), 1.0, "attn_ln_bwd")
    dgains[(1, 1)], dbiases[(1, 1)] = dg, db
    g_aout = dw_square(ot, dmb, "attn_dwout")
    dobs, statss = _attn_bwd_prep(dmb, w_aout, o32, lse_tot, dils)
    ntile = 3 * D // QKV_TILE
    g_qkv = None
    dqkv_gs = []
    for gi, d in enumerate(dils):
        dqkv_g = _attn_bwd(qkv_gs[gi], dobs[gi], statss[gi], d, slopes[gi], f"attn_bwd{gi}")
        dqkv_gs.append(dqkv_g)
        xt = b1t if d == 1 else _transpose_sub(b1, d, f"attn_xt{gi}")
        g_qkv = _attn_dw_group(xt, dqkv_g, gi, d, g_qkv, f"attn_dwqkv{gi}")
    token = rs_stage([to_slots(g_aout), g_qkv], "attn")
    dx_attn = [_attn_dx_group(dqkv_gs[gi], wqkv_nat, gi, d, f"attn_dx{gi}", deps=(token,) if gi == 0 else ())
               for gi, d in enumerate(dils)]
    d_a3 = ffn_bwd([dxres] + dx_attn, 2, 1, 0, s_f2, a3t)
    d_a2 = ffn_bwd(d_a3, 1, 0, 2, s_f1, a2t)
    dxres, dmb, dmt, dg, db = _ln_bwd(d_a2, xh_p, rstd_p, gain(0, 1), 1.0, "pool_ln_bwd")
    dgains[(0, 1)], dbiases[(0, 1)] = dg, db
    g_pout = dw_square(yst, dmb, "pool_dwout")
    dyb, dscale = _pool_bwd_out(dmb, w_pout, y_pool, pool_scale)
    gd = POOL_GROUP_DIM
    g_grp = _mm(mixedt, dyb, grid=(N_POOL_GROUPS, 1, 1),
                a_spec=pl.BlockSpec((gd, T), lambda g, j, k: (g, 0)),
                b_spec=pl.BlockSpec((T, gd), lambda g, j, k: (0, g)),
                out_shape=jax.ShapeDtypeStruct((N_POOL_GROUPS, gd, gd), BF16),
                out_spec=pl.BlockSpec((None, gd, gd), lambda g, j, k: (g, 0, 0)), name="pool_dwgroup")
    tg = min(1024, T)
    dmixed = _mm(dyb, w_grp, grid=(N_POOL_GROUPS, T // tg, 1), nt=True,
                 a_spec=pl.BlockSpec((tg, gd), lambda g, t, k: (t, g)),
                 b_spec=pl.BlockSpec((None, gd, gd), lambda g, t, k: (g, 0, 0)),
                 out_shape=jax.ShapeDtypeStruct((T, D), F32),
                 out_spec=pl.BlockSpec((tg, gd), lambda g, t, k: (t, g)), name="pool_dmixed")
    dub, _ = _pool_window(dmixed, True, "pool_window_bwd")
    g_pin = dw_square(a1t, dub, "pool_dwin")
    g_grp_slots = _device_to_slot_order(
        g_grp.reshape(N_POOL_GROUPS, N_DEV, rs // 4, gd).transpose(1, 0, 2, 3).reshape(N_DEV, rs // 4, D))
    token = rs_stage([to_slots(g_pout), g_grp_slots, to_slots(g_pin)], "pool")
    dx_pool = dx_square(dub, w_pin, "pool_dx", deps=(token,))
    d_x0 = ffn_bwd([dxres, dx_pool], 0, 0, 0, s_f0, x0t)
    grad_x = d_x0[0]
    rs_finish((grad_x,))
    grad_x = grad_x[None]
    gw_dt = [gsums[f"f{f}"][0] for f in range(4)]
    gw_gu = [gsums[f"f{f}"][1] for f in range(4)]
    gw_aout, gw_qkv = gsums["attn"]
    gw_pout, gw_grp, gw_pin = gsums["pool"]

    small = jnp.concatenate([dgains[(i, s)] for i in range(DEPTH) for s in range(3)]
                            + [dbiases[(i, s)] for i in range(DEPTH) for s in range(3)]
                            + [dscale, jnp.zeros((3, D), F32)], axis=0)
    _, small_sum = _small_all_gather(small, "ag_small_grads")
    dev = 4 * lax.axis_index("x") + 2 * lax.axis_index("y") + lax.axis_index("c")
    mine = lax.dynamic_slice_in_dim(small_sum, dev * rs, rs, axis=1)
    grads = {
        "ffn1_w_gate": jnp.stack([gw_gu[2 * i][:fs] for i in range(DEPTH)]),
        "ffn1_w_up": jnp.stack([gw_gu[2 * i][fp:fp + fs] for i in range(DEPTH)]),
        "ffn1_w_down": jnp.stack([gw_dt[2 * i][:fs] for i in range(DEPTH)]),
        "ffn2_w_gate": jnp.stack([gw_gu[2 * i + 1][:fs] for i in range(DEPTH)]),
        "ffn2_w_up": jnp.stack([gw_gu[2 * i + 1][fp:fp + fs] for i in range(DEPTH)]),
        "ffn2_w_down": jnp.stack([gw_dt[2 * i + 1][:fs] for i in range(DEPTH)]),
        "ln_gain": mine[0:DEPTH * 3].reshape(DEPTH, 3, rs),
        "ln_bias": mine[DEPTH * 3:2 * DEPTH * 3].reshape(DEPTH, 3, rs),
        "pool_w_in": gw_pin[None],
        "pool_w_group": gw_grp[:rs // 4].reshape(N_POOL_GROUPS, rs // 4, gd)[None],
        "pool_scale": small_sum[2 * DEPTH * 3][None],
        "pool_w_out": gw_pout[None],
        "attn_w_qkv": gw_qkv[None],
        "attn_w_out": gw_aout[None],
    }
    weights = dict(ffn1_w_gate=ffn1_w_gate, ffn1_w_up=ffn1_w_up, ffn1_w_down=ffn1_w_down,
                   ffn2_w_gate=ffn2_w_gate, ffn2_w_up=ffn2_w_up, ffn2_w_down=ffn2_w_down,
                   ln_gain=ln_gain, ln_bias=ln_bias, pool_w_in=pool_w_in, pool_w_group=pool_w_group,
                   pool_scale=pool_scale, pool_w_out=pool_w_out, attn_w_qkv=attn_w_qkv, attn_w_out=attn_w_out)
    ms = dict(ffn1_w_gate=m_ffn1_w_gate, ffn1_w_up=m_ffn1_w_up, ffn1_w_down=m_ffn1_w_down,
              ffn2_w_gate=m_ffn2_w_gate, ffn2_w_up=m_ffn2_w_up, ffn2_w_down=m_ffn2_w_down,
              ln_gain=m_ln_gain, ln_bias=m_ln_bias, pool_w_in=m_pool_w_in, pool_w_group=m_pool_w_group,
              pool_scale=m_pool_scale, pool_w_out=m_pool_w_out, attn_w_qkv=m_attn_w_qkv, attn_w_out=m_attn_w_out)
    vs = dict(ffn1_w_gate=v_ffn1_w_gate, ffn1_w_up=v_ffn1_w_up, ffn1_w_down=v_ffn1_w_down,
              ffn2_w_gate=v_ffn2_w_gate, ffn2_w_up=v_ffn2_w_up, ffn2_w_down=v_ffn2_w_down,
              ln_gain=v_ln_gain, ln_bias=v_ln_bias, pool_w_in=v_pool_w_in, pool_w_group=v_pool_w_group,
              pool_scale=v_pool_scale, pool_w_out=v_pool_w_out, attn_w_qkv=v_attn_w_qkv, attn_w_out=v_attn_w_out)
    names = list(weights)
    col_sharded = ("ffn1_w_gate", "ffn1_w_up", "ffn2_w_gate", "ffn2_w_up")
    deltas, new_m, new_v = {}, {}, {}
    for nme in names:
        if nme in col_sharded:
            outs = _adamw(jnp.swapaxes(weights[nme], 1, 2), grads[nme], jnp.swapaxes(ms[nme], 1, 2),
                          jnp.swapaxes(vs[nme], 1, 2), f"adamw_{nme}")
            deltas[nme], new_m[nme], new_v[nme] = (jnp.swapaxes(o, 1, 2) for o in outs)
            grads[nme] = jnp.swapaxes(grads[nme], 1, 2)
        else:
            deltas[nme], new_m[nme], new_v[nme] = _adamw(weights[nme], grads[nme], ms[nme], vs[nme], f"adamw_{nme}")
    return (loss, grad_x, *[grads[k] for k in names], *[deltas[k] for k in names],
            *[new_m[k] for k in names], *[new_v[k] for k in names])
```

```python
import functools

import numpy as np
import jax
import jax.numpy as jnp
from jax import lax
from jax.experimental import pallas as pl
from jax.experimental.pallas import tpu as pltpu

F32 = jnp.float32
BF16 = jnp.bfloat16

D = 1024
N_DEV = 8
N_HEADS = 16
HEAD_DIM = 64
N_POOL_GROUPS = 4
POOL_GROUP_DIM = 256
POOL_HALF = (1, 2, 4, 8)
DIL_CONFIGS = ((128, 1), (512, 4), (2048, 16))
ATTN_HALO = 64
ATTN_BLOCK = 128
QKV_SHARD = 3 * 3 * D // N_DEV
DEPTH = 2
ALPHA = (2.0 * DEPTH) ** 0.25
MACARON = 0.5
LN_EPS = 1e-5
MASK_VALUE = -1e30
ADAM_LR = 0.001
ADAM_B1 = 0.9
ADAM_B2 = 0.999
ADAM_EPS = 1e-08
ADAM_WD = 0.01
ADAM_STEP = 10
LANE = 128
VMEM_LIMIT = 56 * 1024 * 1024
MESH = pl.DeviceIdType.MESH
AXES = ("x", "y", "c")


def _round_up(n, m):
    return (n + m - 1) // m * m


def _pcall(body, deps=(), **kw):
    if not deps:
        return pl.pallas_call(body, **kw)
    n_in, n_dep = len(kw["in_specs"]), len(deps)

    def wrapped(*refs):
        return body(*refs[:n_in], *refs[n_in + n_dep:])

    kw["in_specs"] = list(kw["in_specs"]) + [pl.BlockSpec(memory_space=pl.ANY)] * n_dep
    call = pl.pallas_call(wrapped, **kw)
    return lambda *args: call(*args, *deps)


def _params(*sem):
    return pltpu.CompilerParams(dimension_semantics=sem, vmem_limit_bytes=VMEM_LIMIT)


def _alibi_slopes():
    n = len(DIL_CONFIGS) * N_HEADS
    s = 2.0 ** (-8.0 * np.arange(1, n + 1) / n)
    return s.reshape(len(DIL_CONFIGS), N_HEADS).astype(np.float32)


def _my_slot():
    return 4 * lax.axis_index("c") + 2 * lax.axis_index("x") + lax.axis_index("y")


def _mm(a, b, *, grid, a_spec, b_spec, out_shape, out_spec, nt=False, name, alias=None, deps=()):
    nk = grid[2]
    dn = (((1,), (1,)), ((), ())) if nt else (((1,), (0,)), ((), ()))
    blk = tuple(s for s in out_spec.block_shape if s is not None)

    def body(*refs):
        a_ref, b_ref = refs[0], refs[1]
        o_ref = refs[3] if alias is not None else refs[2]
        p = lax.dot_general(a_ref[...], b_ref[...], dn, preferred_element_type=F32)
        if nk == 1:
            o_ref[...] = p.astype(o_ref.dtype)
        else:
            acc = refs[-1]
            k = pl.program_id(2)

            @pl.when(k == 0)
            def _():
                acc[...] = p

            @pl.when(k > 0)
            def _():
                acc[...] += p

            @pl.when(k == nk - 1)
            def _():
                o_ref[...] = acc[...].astype(o_ref.dtype)

    in_specs = [a_spec, b_spec]
    args = [a, b]
    aliases = {}
    if alias is not None:
        in_specs.append(pl.BlockSpec(memory_space=pl.ANY))
        args.append(alias)
        aliases = {2: 0}
    return _pcall(
        body, deps=deps, name=name, grid=grid, in_specs=in_specs, out_specs=out_spec, out_shape=out_shape,
        scratch_shapes=[] if nk == 1 else [pltpu.VMEM(blk, F32)],
        input_output_aliases=aliases,
        compiler_params=_params("parallel", "parallel", "arbitrary"),
    )(*args)


def _transpose_cast(x, name, deps=()):
    T = x.shape[0]
    tm = min(512, T)

    def body(x_ref, xb_ref, xt_ref):
        v = x_ref[...]
        xb_ref[...] = v.astype(BF16)
        xt_ref[...] = v.T.astype(BF16)

    return _pcall(
        body, deps=deps, name=name, grid=(T // tm,),
        in_specs=[pl.BlockSpec((tm, D), lambda t: (t, 0))],
        out_specs=[pl.BlockSpec((tm, D), lambda t: (t, 0)), pl.BlockSpec((D, tm), lambda t: (0, t))],
        out_shape=[jax.ShapeDtypeStruct((T, D), BF16), jax.ShapeDtypeStruct((D, T), BF16)],
        compiler_params=_params("parallel"),
    )(x)


def _mm_ln(a, b, xres, gain, bias, hscale, name, deps=()):
    T, K = a.shape
    tm = min(512, T)

    def body(a_ref, b_ref, x_ref, g_ref, bt_ref, y_ref, yb_ref, yt_ref, xh_ref, rs_ref):
        h = jnp.dot(a_ref[...], b_ref[...], preferred_element_type=F32)
        z = ALPHA * x_ref[...] + hscale * h
        mu = jnp.mean(z, axis=-1, keepdims=True)
        zc = z - mu
        var = jnp.mean(zc * zc, axis=-1, keepdims=True)
        rstd = lax.rsqrt(var + LN_EPS)
        xh = zc * rstd
        y = xh * g_ref[...] + bt_ref[...]
        y_ref[...] = y
        yb_ref[...] = y.astype(BF16)
        yt_ref[...] = y.T.astype(BF16)
        xh_ref[...] = xh
        rs_ref[...] = rstd

    row = pl.BlockSpec((tm, D), lambda t: (t, 0))
    vec = pl.BlockSpec((1, D), lambda t: (0, 0))
    return _pcall(
        body, deps=deps, name=name, grid=(T // tm,),
        in_specs=[pl.BlockSpec((tm, K), lambda t: (t, 0)), pl.BlockSpec((K, D), lambda t: (0, 0)), row, vec, vec],
        out_specs=[row, row, pl.BlockSpec((D, tm), lambda t: (0, t)), row, pl.BlockSpec((tm, 1), lambda t: (t, 0))],
        out_shape=[jax.ShapeDtypeStruct((T, D), F32), jax.ShapeDtypeStruct((T, D), BF16),
                   jax.ShapeDtypeStruct((D, T), BF16), jax.ShapeDtypeStruct((T, D), F32),
                   jax.ShapeDtypeStruct((T, 1), F32)],
        compiler_params=_params("parallel"),
    )(a, b, xres, gain, bias)


def _ln_bwd(dys, xhat, rstd, gain, hscale, name):
    T = xhat.shape[0]
    tm = min(512, T)
    n = len(dys)

    def body(*refs):
        dy_refs = refs[:n]
        xh_ref, rs_ref, g_ref, dx_ref, dh_ref, dht_ref, dg_ref, db_ref = refs[n:]
        dy = dy_refs[0][...]
        for r in dy_refs[1:]:
            dy = dy + r[...]
        xh = xh_ref[...]
        dxh = dy * g_ref[...]
        m1 = jnp.mean(dxh, axis=-1, keepdims=True)
        m2 = jnp.mean(dxh * xh, axis=-1, keepdims=True)
        dz = rs_ref[...] * (dxh - m1 - xh * m2)
        dx_ref[...] = ALPHA * dz
        dh = hscale * dz
        dh_ref[...] = dh.astype(BF16)
        dht_ref[...] = dh.T.astype(BF16)
        dg = jnp.sum(dy * xh, axis=0, keepdims=True)
        db = jnp.sum(dy, axis=0, keepdims=True)

        @pl.when(pl.program_id(0) == 0)
        def _():
            dg_ref[...] = dg
            db_ref[...] = db

        @pl.when(pl.program_id(0) > 0)
        def _():
            dg_ref[...] += dg
            db_ref[...] += db

    row = pl.BlockSpec((tm, D), lambda t: (t, 0))
    vec = pl.BlockSpec((1, D), lambda t: (0, 0))
    return _pcall(
        body, name=name, grid=(T // tm,),
        in_specs=[row] * n + [row, pl.BlockSpec((tm, 1), lambda t: (t, 0)), vec],
        out_specs=[row, row, pl.BlockSpec((D, tm), lambda t: (0, t)), vec, vec],
        out_shape=[jax.ShapeDtypeStruct((T, D), F32), jax.ShapeDtypeStruct((T, D), BF16),
                   jax.ShapeDtypeStruct((D, T), BF16), jax.ShapeDtypeStruct((1, D), F32),
                   jax.ShapeDtypeStruct((1, D), F32)],
        compiler_params=_params("arbitrary"),
    )(*dys, xhat, rstd, gain)


def _add2(a, b, name):
    T = a.shape[0]
    tm = min(512, T)

    def body(a_ref, b_ref, o_ref):
        o_ref[...] = a_ref[...] + b_ref[...]

    row = pl.BlockSpec((tm, D), lambda t: (t, 0))
    return _pcall(body, name=name, grid=(T // tm,), in_specs=[row, row], out_specs=row,
                  out_shape=jax.ShapeDtypeStruct((T, D), F32), compiler_params=_params("parallel"))(a, b)


def _loss_head(y, tgt):
    T = y.shape[0]
    tm = min(512, T)

    def body(y_ref, t_ref, dy_ref, l_ref):
        e = y_ref[...] - t_ref[...]
        dy_ref[...] = e * (1.0 / D)
        part = jnp.sum(jnp.sum(e * e, axis=1, keepdims=True), axis=0, keepdims=True) * (0.5 / D)

        @pl.when(pl.program_id(0) == 0)
        def _():
            l_ref[...] = jnp.zeros_like(l_ref)

        l_ref[...] += part

    row = pl.BlockSpec((tm, D), lambda t: (t, 0))
    return _pcall(
        body, name="loss_head", grid=(T // tm,),
        in_specs=[row, row],
        out_specs=[row, pl.BlockSpec((8, LANE), lambda t: (0, 0))],
        out_shape=[jax.ShapeDtypeStruct((T, D), F32), jax.ShapeDtypeStruct((8, LANE), F32)],
        compiler_params=_params("arbitrary"),
    )(y, tgt)


def _sigmoid(v):
    return 0.5 * jnp.tanh(0.5 * v) + 0.5


def _ffn_up(xb, wgu, fp, name, deps=()):
    T = xb.shape[0]
    tm = min(1024, T)

    def body(x_ref, w_ref, dg_ref, du_ref, a_ref):
        x = x_ref[...]
        for s in range(2):
            p = jnp.dot(x, w_ref[s], preferred_element_type=F32)
            g = p[:, :fp]
            u = p[:, fp:]
            sig = _sigmoid(g)
            q = g * sig
            cols = slice(s * fp, (s + 1) * fp)
            dg_ref[:, cols] = (sig * (1.0 + g - q) * u).astype(BF16)
            du_ref[:, cols] = q.astype(BF16)
            a_ref[:, cols] = (q * u).astype(BF16)

    out = pl.BlockSpec((tm, 2 * fp), lambda t, j: (t, j))
    shp = jax.ShapeDtypeStruct((T, N_DEV * fp), BF16)
    return _pcall(
        body, deps=deps, name=name, grid=(T // tm, N_DEV // 2),
        in_specs=[pl.BlockSpec((tm, D), lambda t, j: (t, 0)),
                  pl.BlockSpec((2, D, 2 * fp), lambda t, j: (j, 0, 0))],
        out_specs=[out, out, out], out_shape=[shp, shp, shp],
        compiler_params=_params("parallel", "parallel"),
    )(xb, wgu)


def _ffn_bwd_act(dhb, wd, g, u, fp, name):
    T = dhb.shape[0]
    tm = min(1024, T)

    def body(dh_ref, w_ref, g_ref, u_ref, o_ref):
        da = lax.dot_general(dh_ref[...], w_ref[...], (((1,), (1,)), ((), ())), preferred_element_type=F32)
        dgate = (da * g_ref[...].astype(F32)).astype(BF16)
        dup = (da * u_ref[...].astype(F32)).astype(BF16)
        for s in range(2):
            o_ref[:, 2 * s * fp:(2 * s + 1) * fp] = dgate[:, s * fp:(s + 1) * fp]
            o_ref[:, (2 * s + 1) * fp:(2 * s + 2) * fp] = dup[:, s * fp:(s + 1) * fp]

    blk = pl.BlockSpec((tm, 2 * fp), lambda t, j: (t, j))
    return _pcall(
        body, name=name, grid=(T // tm, N_DEV // 2),
        in_specs=[pl.BlockSpec((tm, D), lambda t, j: (t, 0)), pl.BlockSpec((2 * fp, D), lambda t, j: (j, 0)), blk, blk],
        out_specs=pl.BlockSpec((tm, 4 * fp), lambda t, j: (t, j)),
        out_shape=jax.ShapeDtypeStruct((T, N_DEV * 2 * fp), BF16),
        compiler_params=_params("parallel", "parallel"),
    )(dhb, wd, g, u)


def _ffn_dwd(dht, act, fp, name):
    T = dht.shape[1]
    bm = 512

    def body(a_ref, b_ref, o_ref):
        p = jnp.dot(a_ref[...], b_ref[...], preferred_element_type=F32)
        o_ref[0] = p[:, :fp].astype(BF16)
        o_ref[1] = p[:, fp:].astype(BF16)

    return _pcall(
        body, name=name, grid=(D // bm, N_DEV // 2),
        in_specs=[pl.BlockSpec((bm, T), lambda i, j: (i, 0)), pl.BlockSpec((T, 2 * fp), lambda i, j: (0, j))],
        out_specs=pl.BlockSpec((2, bm, fp), lambda i, j: (j, i, 0)),
        out_shape=jax.ShapeDtypeStruct((N_DEV, D, fp), BF16),
        compiler_params=_params("parallel", "parallel"),
    )(dht, act)


def _ffn_dx(dgu, wgu, res, fp, name, deps=()):
    T = dgu.shape[0]
    tm = min(512, T)

    def body(a_ref, w_ref, r_ref, o_ref):
        acc = r_ref[...]
        for j in range(N_DEV):
            p = lax.dot_general(a_ref[:, j * 2 * fp:(j + 1) * 2 * fp], w_ref[j], (((1,), (1,)), ((), ())),
                                preferred_element_type=F32)
            acc = acc + p
        o_ref[...] = acc

    return _pcall(
        body, deps=deps, name=name, grid=(T // tm,),
        in_specs=[pl.BlockSpec((tm, N_DEV * 2 * fp), lambda t: (t, 0)),
                  pl.BlockSpec((N_DEV, D, 2 * fp), lambda t: (0, 0, 0)),
                  pl.BlockSpec((tm, D), lambda t: (t, 0))],
        out_specs=pl.BlockSpec((tm, D), lambda t: (t, 0)),
        out_shape=jax.ShapeDtypeStruct((T, D), F32),
        compiler_params=_params("parallel"),
    )(dgu, wgu, res)


def _slots_to_columns(w, name):
    _, R, C = w.shape
    tr = min(512, R)

    def body(i_ref, o_ref):
        o_ref[...] = i_ref[...]

    return _pcall(
        body, name=name, grid=(N_DEV, R // tr),
        in_specs=[pl.BlockSpec((None, tr, C), lambda j, i: (4 * (j % 2) + j // 2, i, 0))],
        out_specs=pl.BlockSpec((tr, C), lambda j, i: (i, j)),
        out_shape=jax.ShapeDtypeStruct((R, N_DEV * C), w.dtype),
        compiler_params=_params("parallel", "parallel"),
    )(w)


POOL_PAD = 16
POOL_CHUNK = 512


def _pool_window(v, transpose, name):
    T = v.shape[0]
    ch = min(POOL_CHUNK, T)
    ext = ch + 2 * POOL_PAD
    gd = POOL_GROUP_DIM

    def body(v_ref, o_ref, ot_ref, pad_ref):
        pad_ref[0:POOL_PAD, :] = jnp.zeros((POOL_PAD, gd), F32)
        pad_ref[POOL_PAD + T:POOL_PAD + T + POOL_PAD, :] = jnp.zeros((POOL_PAD, gd), F32)
        for gi, hw in enumerate(POOL_HALF):
            @pl.when(pl.program_id(0) == gi)
            def _(hw=hw):
                def count(t):
                    return (jnp.minimum(t + hw, T) - jnp.maximum(t - hw, 0)).astype(F32)

                if transpose:
                    t_all = lax.broadcasted_iota(jnp.int32, (T, gd), 0)
                    pad_ref[POOL_PAD:POOL_PAD + T, :] = v_ref[...] / count(t_all)
                else:
                    pad_ref[POOL_PAD:POOL_PAD + T, :] = v_ref[...]
                shift = hw if transpose else hw - 1
                for c in range(T // ch):
                    e = pad_ref[c * ch:c * ch + ext, :]
                    step = 1
                    while step < 2 * hw:
                        e = e + pltpu.roll(e, step, 0)
                        step *= 2
                    if shift:
                        e = pltpu.roll(e, ext - shift, 0)
                    s = e[POOL_PAD:POOL_PAD + ch, :]
                    center = v_ref[c * ch:(c + 1) * ch, :]
                    if transpose:
                        res = s - center
                    else:
                        t_idx = c * ch + lax.broadcasted_iota(jnp.int32, (ch, gd), 0)
                        res = s / count(t_idx) - center
                    o_ref[c * ch:(c + 1) * ch, :] = res.astype(BF16)
                    ot_ref[:, c * ch:(c + 1) * ch] = res.T.astype(BF16)

    return _pcall(
        body, name=name, grid=(N_POOL_GROUPS,),
        in_specs=[pl.BlockSpec((T, gd), lambda g: (0, g))],
        out_specs=[pl.BlockSpec((T, gd), lambda g: (0, g)), pl.BlockSpec((gd, T), lambda g: (g, 0))],
        out_shape=[jax.ShapeDtypeStruct((T, D), BF16), jax.ShapeDtypeStruct((D, T), BF16)],
        scratch_shapes=[pltpu.VMEM((T + 2 * POOL_PAD, gd), F32)],
        compiler_params=_params("arbitrary"),
    )(v)


def _pool_group(mixedb, wgroup, scale):
    T = mixedb.shape[0]
    tm = min(1024, T)
    gd = POOL_GROUP_DIM

    def body(a_ref, w_ref, s_ref, y_ref, ys_ref, yst_ref):
        y = jnp.dot(a_ref[...], w_ref[...], preferred_element_type=F32)
        ys = y * s_ref[...]
        y_ref[...] = y
        ys_ref[...] = ys.astype(BF16)
        yst_ref[...] = ys.T.astype(BF16)

    blk = pl.BlockSpec((tm, gd), lambda g, t: (t, g))
    return _pcall(
        body, name="pool_group", grid=(N_POOL_GROUPS, T // tm),
        in_specs=[blk, pl.BlockSpec((None, gd, gd), lambda g, t: (g, 0, 0)), pl.BlockSpec((1, gd), lambda g, t: (0, g))],
        out_specs=[blk, blk, pl.BlockSpec((gd, tm), lambda g, t: (g, t))],
        out_shape=[jax.ShapeDtypeStruct((T, D), F32), jax.ShapeDtypeStruct((T, D), BF16),
                   jax.ShapeDtypeStruct((D, T), BF16)],
        compiler_params=_params("parallel", "parallel"),
    )(mixedb, wgroup, scale)


def _pool_bwd_out(dmb, w_out, y, scale):
    T = dmb.shape[0]
    tm = min(512, T)

    def body(a_ref, w_ref, y_ref, s_ref, dy_ref, ds_ref):
        dys = lax.dot_general(a_ref[...], w_ref[...], (((1,), (1,)), ((), ())), preferred_element_type=F32)
        dy_ref[...] = (dys * s_ref[...]).astype(BF16)
        part = jnp.sum(dys * y_ref[...], axis=0, keepdims=True)

        @pl.when(pl.program_id(0) == 0)
        def _():
            ds_ref[...] = part

        @pl.when(pl.program_id(0) > 0)
        def _():
            ds_ref[...] += part

    row = pl.BlockSpec((tm, D), lambda t: (t, 0))
    vec = pl.BlockSpec((1, D), lambda t: (0, 0))
    return _pcall(
        body, name="pool_bwd_out", grid=(T // tm,),
        in_specs=[row, pl.BlockSpec((D, D), lambda t: (0, 0)), row, vec],
        out_specs=[row, vec],
        out_shape=[jax.ShapeDtypeStruct((T, D), BF16), jax.ShapeDtypeStruct((1, D), F32)],
        compiler_params=_params("arbitrary"),
    )(dmb, w_out, y, scale)


def _attn_bias_table(d, slopes):
    w = ATTN_BLOCK + 2 * ATTN_HALO
    rel = np.arange(w)[None, :] - ATTN_HALO - np.arange(ATTN_BLOCK)[:, None]
    dist = (d * np.abs(rel)).astype(np.float32)
    bias = -np.asarray(slopes, np.float32)[:, None, None] * dist[None]
    return jnp.asarray(np.where(np.abs(rel)[None] <= ATTN_HALO, bias, np.float32(MASK_VALUE)).astype(np.float32))


def _attn_in_range(n, L):
    w = ATTN_BLOCK + 2 * ATTN_HALO
    j = n * ATTN_BLOCK - ATTN_HALO + lax.broadcasted_iota(jnp.int32, (ATTN_BLOCK, w), 1)
    return (j >= 0) & (j < L)


BIAS_SPEC = pl.BlockSpec((N_HEADS, ATTN_BLOCK, ATTN_BLOCK + 2 * ATTN_HALO), lambda r, n: (0, 0, 0))


def _lane_col(st, idx):
    lane = lax.broadcasted_iota(jnp.int32, st.shape, 1)
    return jnp.sum(jnp.where(lane == idx, st, 0.0), axis=1, keepdims=True)


def _window_specs(nb, d, col, width):
    last = 2 * d * nb - 1

    def prev(r, n):
        return (jnp.maximum(2 * (r * nb + n) - 1, 0), col)

    def cur(r, n):
        return (r * nb + n, col)

    def nxt(r, n):
        return (jnp.minimum(2 * (r * nb + n) + 2, last), col)

    return [pl.BlockSpec((ATTN_HALO, width), prev), pl.BlockSpec((ATTN_BLOCK, width), cur),
            pl.BlockSpec((ATTN_HALO, width), nxt)]


def _attn_fwd(qkv_g, d, slopes, name):
    T = qkv_g.shape[0]
    L = T // d
    nb = L // ATTN_BLOCK

    def body(q_ref, kp_ref, kc_ref, kn_ref, vp_ref, vc_ref, vn_ref, b_ref, o_ref, lse_ref):
        in_range = _attn_in_range(pl.program_id(1), L)
        lane = lax.broadcasted_iota(jnp.int32, (ATTN_BLOCK, LANE), 1)
        first = lane < HEAD_DIM
        sc = HEAD_DIM ** -0.5
        head_mask = [jnp.where(first, sc, 0.0).astype(BF16), jnp.where(first, 0.0, sc).astype(BF16)]
        lse_acc = jnp.zeros((ATTN_BLOCK, LANE), F32)
        for hp in range(N_HEADS // 2):
            cs = slice(hp * LANE, (hp + 1) * LANE)
            q2 = q_ref[:, cs]
            k2 = jnp.concatenate([kp_ref[:, cs], kc_ref[:, cs], kn_ref[:, cs]], axis=0)
            v2 = jnp.concatenate([vp_ref[:, cs], vc_ref[:, cs], vn_ref[:, cs]], axis=0)
            outs = []
            for hh in range(2):
                h = 2 * hp + hh
                qh = q2 * head_mask[hh]
                s = lax.dot_general(qh, k2, (((1,), (1,)), ((), ())), preferred_element_type=F32)
                s = jnp.where(in_range, s + b_ref[h], MASK_VALUE)
                m = jnp.max(s, axis=1, keepdims=True)
                p = jnp.exp(s - m)
                l = jnp.sum(p, axis=1, keepdims=True)
                o = jnp.dot(p.astype(BF16), v2, preferred_element_type=F32) / l
                outs.append(o)
                lse_acc = jnp.where(lane == h, m + jnp.log(l), lse_acc)
            o_ref[:, cs] = jnp.where(first, outs[0], outs[1])
        lse_ref[...] = lse_acc

    specs = ([pl.BlockSpec((ATTN_BLOCK, D), lambda r, n: (r * nb + n, 0))]
             + _window_specs(nb, d, 1, D) + _window_specs(nb, d, 2, D) + [BIAS_SPEC])
    row = lambda w: pl.BlockSpec((ATTN_BLOCK, w), lambda r, n: (r * nb + n, 0))
    return _pcall(
        body, name=name, grid=(d, nb), in_specs=specs,
        out_specs=[row(D), row(LANE)],
        out_shape=[jax.ShapeDtypeStruct((T, D), F32), jax.ShapeDtypeStruct((T, LANE), F32)],
        compiler_params=_params("parallel", "parallel"),
    )(*([qkv_g] * 7), _attn_bias_table(d, slopes))


def _stage(scr3, val):
    for c in range(val.shape[1] // LANE):
        scr3[c] = val[:, c * LANE:(c + 1) * LANE]


def _unstage(scr3):
    return jnp.concatenate([scr3[c] for c in range(scr3.shape[0])], axis=1)


def _gather_rows(scr3, r, n, d):
    return jnp.concatenate([scr3[c, pl.ds(r, n, stride=d), :] for c in range(scr3.shape[0])], axis=1)


def _scatter_rows(scr3, r, n, d, val):
    for c in range(scr3.shape[0]):
        scr3[c, pl.ds(r, n, stride=d), :] = val[:, c * LANE:(c + 1) * LANE]


def _attn_combine(os_, lses, dils):
    T = os_[0].shape[0]
    tm = min(256, T)
    ng = len(os_)
    n_scr = sum(1 for d in dils if d > 1)

    def body(*refs):
        in_o = refs[:ng]
        in_l = refs[ng:2 * ng]
        o32_ref, ob_ref, ot_ref, lt_ref = refs[2 * ng:2 * ng + 4]
        scr = refs[2 * ng + 4:]
        o_chunk, l_refs, si = [], [], 0
        for g, d in enumerate(dils):
            if d == 1:
                o_chunk.append(lambda hp, g=g: in_o[g][:, hp * LANE:(hp + 1) * LANE])
                l_refs.append(in_l[g])
                continue
            so, sl = scr[2 * si], scr[2 * si + 1]
            si += 1
            for r in range(d):
                _scatter_rows(so, r, tm // d, d, in_o[g][r])
                sl[pl.ds(r, tm // d, stride=d), :] = in_l[g][r]
            o_chunk.append(lambda hp, so=so: so[hp])
            l_refs.append(sl)
        ls = [r[...] for r in l_refs]
        m = ls[0]
        for l in ls[1:]:
            m = jnp.maximum(m, l)
        tot = jnp.exp(ls[0] - m)
        for l in ls[1:]:
            tot = tot + jnp.exp(l - m)
        lt = m + jnp.log(tot)
        lt_ref[...] = lt
        ws = [jnp.exp(l - lt) for l in ls]
        lane = lax.broadcasted_iota(jnp.int32, (tm, LANE), 1)
        first = lane < HEAD_DIM
        for hp in range(N_HEADS // 2):
            cs = slice(hp * LANE, (hp + 1) * LANE)
            acc = jnp.zeros((tm, LANE), F32)
            for g in range(ng):
                wt = jnp.where(first, _lane_col(ws[g], 2 * hp), _lane_col(ws[g], 2 * hp + 1))
                acc = acc + wt * o_chunk[g](hp)
            o32_ref[:, cs] = acc
            ob_ref[:, cs] = acc.astype(BF16)
        ot_ref[...] = o32_ref[...].T.astype(BF16)

    row = pl.BlockSpec((tm, D), lambda t: (t, 0))
    st = pl.BlockSpec((tm, LANE), lambda t: (t, 0))

    def sub_spec(d, w):
        return pl.BlockSpec((tm, w), lambda t: (t, 0)) if d == 1 else pl.BlockSpec((d, tm // d, w), lambda t: (0, t, 0))

    def sub_view(a, d):
        return a if d == 1 else a.reshape(d, T // d, a.shape[1])

    return _pcall(
        body, name="attn_combine", grid=(T // tm,),
        in_specs=[sub_spec(d, D) for d in dils] + [sub_spec(d, LANE) for d in dils],
        out_specs=[row, row, pl.BlockSpec((D, tm), lambda t: (0, t)), st],
        out_shape=[jax.ShapeDtypeStruct((T, D), F32), jax.ShapeDtypeStruct((T, D), BF16),
                   jax.ShapeDtypeStruct((D, T), BF16), jax.ShapeDtypeStruct((T, LANE), F32)],
        scratch_shapes=[pltpu.VMEM(s, F32) for _ in range(n_scr) for s in ((D // LANE, tm, LANE), (tm, LANE))],
        compiler_params=_params("parallel"),
    )(*[sub_view(a, d) for a, d in zip(os_, dils)], *[sub_view(a, d) for a, d in zip(lses, dils)])


def _attn_bwd_prep(dmb, w_out, o32, lse_tot, dils):
    T = dmb.shape[0]
    tm = min(512, T)
    ng = len(dils)

    def body(a_ref, w_ref, o_ref, l_ref, *rest):
        do_refs, st_refs = rest[:ng], rest[ng:2 * ng]
        do_scr, st_scr = rest[2 * ng:]
        do = lax.dot_general(a_ref[...], w_ref[...], (((1,), (1,)), ((), ())), preferred_element_type=F32)
        _stage(do_scr, do)
        prod = do * o_ref[...]
        lane = lax.broadcasted_iota(jnp.int32, (tm, LANE), 1)
        first = lane < HEAD_DIM
        st = jnp.where(lane < N_HEADS, l_ref[...], 0.0)
        for hp in range(N_HEADS // 2):
            pr = prod[:, hp * LANE:(hp + 1) * LANE]
            d0 = jnp.sum(jnp.where(first, pr, 0.0), axis=1, keepdims=True)
            d1 = jnp.sum(jnp.where(first, 0.0, pr), axis=1, keepdims=True)
            st = jnp.where(lane == N_HEADS + 2 * hp, d0, st)
            st = jnp.where(lane == N_HEADS + 2 * hp + 1, d1, st)
        st_scr[...] = st
        for g, d in enumerate(dils):
            if d == 1:
                do_refs[g][...] = do.astype(BF16)
                st_refs[g][...] = st
                continue
            for r in range(d):
                do_refs[g][r] = _gather_rows(do_scr, r, tm // d, d).astype(BF16)
                st_refs[g][r] = st_scr[pl.ds(r, tm // d, stride=d), :]

    row = pl.BlockSpec((tm, D), lambda t: (t, 0))
    stb = pl.BlockSpec((tm, LANE), lambda t: (t, 0))

    def sub_spec(d, w):
        return pl.BlockSpec((tm, w), lambda t: (t, 0)) if d == 1 else pl.BlockSpec((d, tm // d, w), lambda t: (0, t, 0))

    def sub_shape(d, w, dt):
        return jax.ShapeDtypeStruct((T, w) if d == 1 else (d, T // d, w), dt)

    outs = _pcall(
        body, name="attn_bwd_prep", grid=(T // tm,),
        in_specs=[row, pl.BlockSpec((D, D), lambda t: (0, 0)), row, stb],
        out_specs=[sub_spec(d, D) for d in dils] + [sub_spec(d, LANE) for d in dils],
        out_shape=[sub_shape(d, D, BF16) for d in dils] + [sub_shape(d, LANE, F32) for d in dils],
        scratch_shapes=[pltpu.VMEM((D // LANE, tm, LANE), F32), pltpu.VMEM((tm, LANE), F32)],
        compiler_params=_params("parallel"),
    )(dmb, w_out, o32, lse_tot)
    return ([o.reshape(T, D) for o in outs[:ng]], [o.reshape(T, LANE) for o in outs[ng:]])


def _attn_bwd(qkv_g, do_g, st_g, d, slopes, name):
    T = qkv_g.shape[0]
    L = T // d
    nb = L // ATTN_BLOCK
    scale = HEAD_DIM ** -0.5
    nt = (((1,), (1,)), ((), ()))

    def body(qp_ref, qc_ref, qn_ref, kp_ref, kc_ref, kn_ref, vp_ref, vc_ref, vn_ref,
             dp_ref, dc_ref, dn_ref, sp_ref, sc_ref, sn_ref, b_ref, o_ref):
        in_range = _attn_in_range(pl.program_id(1), L)
        lane = lax.broadcasted_iota(jnp.int32, (ATTN_BLOCK, LANE), 1)
        first = lane < HEAD_DIM
        head_mask = [jnp.where(first, 1.0, 0.0).astype(BF16), jnp.where(first, 0.0, 1.0).astype(BF16)]
        scaled_mask = [jnp.where(first, scale, 0.0).astype(BF16), jnp.where(first, 0.0, scale).astype(BF16)]
        stc = sc_ref[...]
        stw_t = jnp.concatenate([sp_ref[...], stc, sn_ref[...]], axis=0).T
        for hp in range(N_HEADS // 2):
            cs = slice(hp * LANE, (hp + 1) * LANE)
            cat = lambda a, b, c: jnp.concatenate([a[:, cs], b[:, cs], c[:, cs]], axis=0)
            q2, k2, v2, do2 = qc_ref[:, cs], kc_ref[:, cs], vc_ref[:, cs], dc_ref[:, cs]
            qw, kw, vw, dow = cat(qp_ref, qc_ref, qn_ref), cat(kp_ref, kc_ref, kn_ref), cat(vp_ref, vc_ref, vn_ref), cat(dp_ref, dc_ref, dn_ref)
            dqs, dks, dvs = [], [], []
            for hh in range(2):
                h = 2 * hp + hh
                pick = lambda t, hh=hh: t * head_mask[hh]
                pick_scaled = lambda t, hh=hh: t * scaled_mask[hh]
                bias = b_ref[h]
                s = lax.dot_general(pick_scaled(q2), kw, nt, preferred_element_type=F32)
                s = jnp.where(in_range, s + bias, MASK_VALUE)
                p = jnp.exp(s - _lane_col(stc, h))
                dp = lax.dot_general(pick(do2), vw, nt, preferred_element_type=F32)
                ds = p * (dp - _lane_col(stc, N_HEADS + h))
                dqs.append(jnp.dot(ds.astype(BF16), kw, preferred_element_type=F32) * scale)
                st_ = lax.dot_general(pick_scaled(k2), qw, nt, preferred_element_type=F32)
                st_ = jnp.where(in_range, st_ + bias, MASK_VALUE)
                pt = jnp.exp(st_ - stw_t[h:h + 1, :])
                dvs.append(jnp.dot(pt.astype(BF16), dow, preferred_element_type=F32))
                dpt = lax.dot_general(pick(v2), dow, nt, preferred_element_type=F32)
                dst = pt * (dpt - stw_t[N_HEADS + h:N_HEADS + h + 1, :])
                dks.append(jnp.dot(dst.astype(BF16), qw, preferred_element_type=F32) * scale)
            o_ref[:, hp * LANE:(hp + 1) * LANE] = jnp.where(first, dqs[0], dqs[1]).astype(BF16)
            o_ref[:, D + hp * LANE:D + (hp + 1) * LANE] = jnp.where(first, dks[0], dks[1]).astype(BF16)
            o_ref[:, 2 * D + hp * LANE:2 * D + (hp + 1) * LANE] = jnp.where(first, dvs[0], dvs[1]).astype(BF16)

    specs = (_window_specs(nb, d, 0, D) + _window_specs(nb, d, 1, D) + _window_specs(nb, d, 2, D)
             + _window_specs(nb, d, 0, D) + _window_specs(nb, d, 0, LANE) + [BIAS_SPEC])
    return _pcall(
        body, name=name, grid=(d, nb), in_specs=specs,
        out_specs=pl.BlockSpec((ATTN_BLOCK, 3 * D), lambda r, n: (r * nb + n, 0)),
        out_shape=jax.ShapeDtypeStruct((T, 3 * D), BF16),
        compiler_params=_params("parallel", "parallel"),
    )(*([qkv_g] * 9), *([do_g] * 3), *([st_g] * 3), _attn_bias_table(d, slopes))


QKV_TILE = QKV_SHARD // 3


def _qkv_tile_block(nn):
    dev = nn // 3
    return 4 * (dev % 2) + dev // 2, nn % 3


def _attn_qkv_group(xb, wqkv, gi, d, name, deps=()):
    T = xb.shape[0]
    tq = min(1024, T)
    nsub = tq // d
    tn = 768
    ntile = 3 * D // tn

    def body(x_ref, w_ref, o_ref, *scr):
        p = jnp.dot(x_ref[...], w_ref[...], preferred_element_type=F32)
        if d == 1:
            o_ref[...] = p.astype(BF16)
        else:
            _stage(scr[0], p)
            for r in range(d):
                o_ref[r] = _gather_rows(scr[0], r, nsub, d).astype(BF16)

    if d == 1:
        out_spec = pl.BlockSpec((tq, tn), lambda n, t: (t, n))
        out_shape = jax.ShapeDtypeStruct((T, 3 * D), BF16)
    else:
        out_spec = pl.BlockSpec((d, nsub, tn), lambda n, t: (0, t, n))
        out_shape = jax.ShapeDtypeStruct((d, T // d, 3 * D), BF16)
    out = _pcall(
        body, deps=deps, name=name, grid=(ntile, T // tq),
        in_specs=[pl.BlockSpec((tq, D), lambda n, t: (t, 0)),
                  pl.BlockSpec((D, tn), lambda n, t: (0, ntile * gi + n))],
        out_specs=out_spec, out_shape=out_shape,
        scratch_shapes=[] if d == 1 else [pltpu.VMEM((tn // LANE, tq, LANE), F32)],
        compiler_params=_params("parallel", "parallel"),
    )(xb, wqkv)
    return out.reshape(T, 3 * D)


def _attn_dx_group(dqkv_g, wqkv, gi, d, name, deps=()):
    T = dqkv_g.shape[0]
    tq = min(512, T)
    nsub = tq // d

    def body(a_ref, w_ref, o_ref, *stage):
        a = a_ref[...]
        if d > 1:
            a = a.reshape(tq, 3 * D)
        p = lax.dot_general(a, w_ref[...], (((1,), (1,)), ((), ())), preferred_element_type=F32)
        if d == 1:
            o_ref[...] = p
        else:
            for r in range(d):
                _scatter_rows(stage[0], r, nsub, d, p[r * nsub:(r + 1) * nsub, :])
            o_ref[...] = _unstage(stage[0])

    if d == 1:
        a_spec = pl.BlockSpec((tq, 3 * D), lambda t: (t, 0))
        a = dqkv_g
    else:
        a_spec = pl.BlockSpec((d, nsub, 3 * D), lambda t: (0, t, 0))
        a = dqkv_g.reshape(d, T // d, 3 * D)
    return _pcall(
        body, deps=deps, name=name, grid=(T // tq,),
        in_specs=[a_spec, pl.BlockSpec((D, 3 * D), lambda t: (0, gi))],
        out_specs=pl.BlockSpec((tq, D), lambda t: (t, 0)),
        out_shape=jax.ShapeDtypeStruct((T, D), F32),
        scratch_shapes=[] if d == 1 else [pltpu.VMEM((D // LANE, tq, LANE), F32)],
        compiler_params=_params("parallel"),
    )(a, wqkv)


def _attn_dw_group(xt, dqkv_g, gi, d, prev, name):
    T = dqkv_g.shape[0]
    L = T // d
    bm = 512
    ntile = 3 * D // QKV_TILE

    def body(*refs):
        a_ref, b_ref = refs[0], refs[1]
        o_ref, cat = refs[-2], refs[-1]
        if d == 1:
            a = a_ref[...]
        else:
            @pl.when(pl.program_id(1) == 0)
            def _():
                for r in range(d):
                    cat[:, r * L:(r + 1) * L] = a_ref[r]
            a = cat[...]
        o_ref[...] = jnp.dot(a, b_ref[...], preferred_element_type=F32).astype(BF16)

    def out_map(i, n):
        slot, sub = _qkv_tile_block(ntile * gi + n)
        return slot, i, sub

    a_spec = (pl.BlockSpec((bm, T), lambda i, n: (i, 0)) if d == 1
              else pl.BlockSpec((d, bm, L), lambda i, n: (0, i, 0)))
    in_specs = [a_spec, pl.BlockSpec((T, QKV_TILE), lambda i, n: (0, n))]
    args = [xt, dqkv_g]
    aliases = {}
    if prev is not None:
        in_specs.append(ANY_SPEC)
        args.append(prev)
        aliases = {2: 0}
    return _pcall(
        body, name=name, grid=(D // bm, ntile), in_specs=in_specs,
        out_specs=pl.BlockSpec((None, bm, QKV_TILE), out_map),
        out_shape=jax.ShapeDtypeStruct((N_DEV, D, QKV_SHARD), BF16),
        scratch_shapes=[pltpu.VMEM((bm, T), BF16)],
        input_output_aliases=aliases,
        compiler_params=_params("parallel", "arbitrary"),
    )(*args)


def _transpose_sub(x, d, name):
    T = x.shape[0]
    tm = LANE * d

    def body(x_ref, o_ref, scr):
        for c in range(D // LANE):
            scr[...] = x_ref[:, c * LANE:(c + 1) * LANE]
            for r in range(d):
                o_ref[r, c * LANE:(c + 1) * LANE, :] = scr[pl.ds(r, LANE, stride=d), :].T.astype(BF16)

    return _pcall(
        body, name=name, grid=(T // tm,),
        in_specs=[pl.BlockSpec((tm, D), lambda t: (t, 0))],
        out_specs=pl.BlockSpec((d, D, LANE), lambda t: (0, 0, t)),
        out_shape=jax.ShapeDtypeStruct((d, D, T // d), BF16),
        scratch_shapes=[pltpu.VMEM((tm, LANE), F32)],
        compiler_params=_params("parallel"),
    )(x)


HBM_SPEC = pl.BlockSpec(memory_space=pltpu.HBM)
SEM_SPEC = pl.BlockSpec(memory_space=pltpu.SEMAPHORE)
ANY_SPEC = pl.BlockSpec(memory_space=pl.ANY)
DATAFLOW = pltpu.SideEffectType.DATAFLOW_SIDE_EFFECTING


def _me_and_peers():
    x, y, c = lax.axis_index("x"), lax.axis_index("y"), lax.axis_index("c")
    return (x, y, c), [(x, y, 1 - c), (1 - x, y, c), (x, 1 - y, c), (1 - x, 1 - y, c)]


def _slot(px, py, pc):
    return 4 * pc + 2 * px + py


def _split_start(srcs, lands, after, start_copies, n_sem, name):
    n = len(srcs)
    n_after = len(after)

    def body(*refs):
        src_refs, land_refs = refs[:n], refs[n:2 * n]
        send_sems, recv_sems = refs[2 * n + n_after], refs[2 * n + n_after + 1]
        token = refs[-1]
        start_copies(src_refs, land_refs, send_sems, recv_sems)
        token[...] = jnp.zeros_like(token)

    outs = _pcall(
        body, name=name,
        in_specs=[HBM_SPEC] * (2 * n) + [ANY_SPEC] * n_after,
        out_shape=(pltpu.SemaphoreType.DMA(n_sem), pltpu.SemaphoreType.DMA(n_sem),
                   *[pltpu.HBM(a.shape, a.dtype) for a in srcs], *[pltpu.HBM(a.shape, a.dtype) for a in lands],
                   jax.ShapeDtypeStruct((8, LANE), F32)),
        out_specs=(SEM_SPEC, SEM_SPEC, *[HBM_SPEC] * (2 * n), pl.BlockSpec(memory_space=pltpu.VMEM)),
        input_output_aliases={i: 2 + i for i in range(2 * n)},
        compiler_params=pltpu.CompilerParams(has_side_effects=DATAFLOW),
    )(*[pltpu.with_memory_space_constraint(a, pltpu.HBM) for a in srcs],
      *[pltpu.with_memory_space_constraint(a, pltpu.HBM) for a in lands], *after)
    return outs[0], outs[1], list(outs[2:2 + n]), list(outs[2 + n:2 + 2 * n]), outs[-1]


def _split_wait(handle, after, wait_copies, name):
    send_sems, recv_sems, srcs, lands, _ = handle
    n = len(srcs)

    def body(*refs):
        src_refs, land_refs = refs[:n], refs[n:2 * n]
        wait_copies(src_refs, land_refs, refs[2 * n], refs[2 * n + 1])

    outs = _pcall(
        body, name=name,
        in_specs=[HBM_SPEC] * (2 * n) + [SEM_SPEC, SEM_SPEC] + [ANY_SPEC] * len(after),
        out_shape=tuple(pltpu.HBM(a.shape, a.dtype) for a in srcs + lands),
        out_specs=tuple([HBM_SPEC] * (2 * n)),
        input_output_aliases={i: i for i in range(2 * n)},
        compiler_params=pltpu.CompilerParams(has_side_effects=DATAFLOW),
    )(*srcs, *lands, send_sems, recv_sems, *after)
    return list(outs[:n]), list(outs[n:])


def _ag_copies(src_refs, land_refs, send_sems, recv_sems, received):
    me, peers = _me_and_peers()
    cps = []
    for i in range(len(src_refs)):
        for k, to in enumerate(peers):
            cps.append(pltpu.make_async_remote_copy(
                src_ref=src_refs[i], dst_ref=land_refs[i].at[_slot(*(to if received else me))],
                send_sem=send_sems.at[4 * i + k], recv_sem=recv_sems.at[4 * i + k], device_id=to,
                device_id_type=MESH))
    return cps


def _ag_start(shards, after, name):
    lands = [lax.empty((N_DEV,) + a.shape, a.dtype) for a in shards]

    def start(src_refs, land_refs, send_sems, recv_sems):
        for cp in _ag_copies(src_refs, land_refs, send_sems, recv_sems, False):
            cp.start()

    return _split_start(shards, lands, after, start, (4 * len(shards),), name)


def _ag_finish(handle, after, name):
    def wait(src_refs, land_refs, send_sems, recv_sems):
        for cp in _ag_copies(src_refs, land_refs, send_sems, recv_sems, True):
            cp.wait_send()
            cp.wait_recv()

    shards, lands = _split_wait(handle, after, wait, name + "_wait")
    n = len(shards)

    def body(*refs):
        src_refs, out_refs = refs[:n], refs[2 * n:3 * n]
        send_sems, recv_sems, local_sems = refs[3 * n:3 * n + 3]
        bounce = refs[3 * n + 3:]
        me, peers = _me_and_peers()
        loads = [pltpu.make_async_copy(src_refs[i], bounce[i], local_sems.at[i]) for i in range(n)]
        mine = [pltpu.make_async_copy(bounce[i], out_refs[i].at[_slot(*me)], local_sems.at[i]) for i in range(n)]
        for cp in loads:
            cp.start()
        cps = []
        for i in range(n):
            for j, chip in enumerate(peers[1:]):
                blk = out_refs[i].at[_slot(*chip)]
                cps.append(pltpu.make_async_remote_copy(
                    src_ref=blk, dst_ref=blk, send_sem=send_sems.at[i, j], recv_sem=recv_sems.at[i, j],
                    device_id=peers[0], device_id_type=MESH))
        for cp in cps:
            cp.start()
        for ld, st in zip(loads, mine):
            ld.wait()
            st.start()
        for cp in cps:
            cp.wait()
        for cp in mine:
            cp.wait()

    outs = _pcall(
        body, name=name + "_pass",
        in_specs=[ANY_SPEC] * (2 * n), out_specs=[ANY_SPEC] * n,
        out_shape=[jax.ShapeDtypeStruct(a.shape, a.dtype) for a in lands],
        input_output_aliases={n + i: i for i in range(n)},
        scratch_shapes=[pltpu.SemaphoreType.DMA((n, 3)), pltpu.SemaphoreType.DMA((n, 3)),
                        pltpu.SemaphoreType.DMA((n,))] + [pltpu.VMEM(a.shape, a.dtype) for a in shards],
        compiler_params=pltpu.CompilerParams(vmem_limit_bytes=VMEM_LIMIT),
    )(*shards, *lands)
    return list(outs)


def _small_all_gather(v, name):
    R, C = v.shape

    def body(x_ref, out_ref, sum_ref, send_sems, recv_sems, local_sem):
        x, y, c = lax.axis_index("x"), lax.axis_index("y"), lax.axis_index("c")
        me, sibling = (x, y, c), (x, y, 1 - c)
        chips = [(1 - x, y), (x, 1 - y), (1 - x, 1 - y)]

        def rows(px, py, pc):
            return out_ref.at[4 * pc + 2 * px + py]

        def copy(k, block, to, src=None):
            return pltpu.make_async_remote_copy(
                src_ref=rows(*block) if src is None else src, dst_ref=rows(*block),
                send_sem=send_sems.at[k], recv_sem=recv_sems.at[k],
                device_id=to, device_id_type=MESH)

        mine = pltpu.make_async_copy(x_ref, rows(*me), local_sem)
        mine.start()
        first = [copy(0, me, sibling, src=x_ref)]
        first += [copy(1 + j, me, (*chip, c), src=x_ref) for j, chip in enumerate(chips)]
        for cp in first:
            cp.start()
        passed = [copy(4 + j, (*chip, c), sibling) for j, chip in enumerate(chips)]
        for j, chip in enumerate(chips):
            copy(1 + j, (*chip, c), me).wait_recv()
            passed[j].start()
        copy(0, sibling, me).wait_recv()
        for j, chip in enumerate(chips):
            copy(4 + j, (*chip, 1 - c), me).wait_recv()
        for cp in first + passed:
            cp.wait_send()
        mine.wait()
        acc = out_ref[0]
        for s in range(1, N_DEV):
            acc = acc + out_ref[s]
        sum_ref[...] = acc

    vm = pl.BlockSpec(memory_space=pltpu.VMEM)
    return _pcall(
        body, name=name, in_specs=[vm], out_specs=[vm, vm],
        out_shape=[jax.ShapeDtypeStruct((N_DEV, R, C), v.dtype), jax.ShapeDtypeStruct((R, C), v.dtype)],
        scratch_shapes=[pltpu.SemaphoreType.DMA((7,)), pltpu.SemaphoreType.DMA((7,)), pltpu.SemaphoreType.DMA],
    )(v)


def _rs_sibling(arrs, name):
    n = len(arrs)

    def body(*refs):
        ins, outs = refs[:n], refs[n:2 * n]
        send_sems, recv_sems = refs[2 * n:]
        x, y, c = lax.axis_index("x"), lax.axis_index("y"), lax.axis_index("c")
        cps = [pltpu.make_async_remote_copy(
            src_ref=ins[i].at[pl.ds(4 * (1 - c), 4)], dst_ref=outs[i],
            send_sem=send_sems.at[i], recv_sem=recv_sems.at[i],
            device_id=(x, y, 1 - c), device_id_type=MESH) for i in range(n)]
        for cp in cps:
            cp.start()
        for cp in cps:
            cp.wait()

    hbm = pl.BlockSpec(memory_space=pl.ANY)
    return _pcall(
        body, name=name, in_specs=[hbm] * n, out_specs=[hbm] * n,
        out_shape=[jax.ShapeDtypeStruct((4,) + a.shape[1:], a.dtype) for a in arrs],
        scratch_shapes=[pltpu.SemaphoreType.DMA((n,)), pltpu.SemaphoreType.DMA((n,))],
    )(*arrs)


def _rs_copies(src_refs, land_refs, send_sems, recv_sems):
    _, peers = _me_and_peers()
    cps = []
    for i in range(len(src_refs)):
        for j, (px, py, pc) in enumerate(peers[1:]):
            cps.append(pltpu.make_async_remote_copy(
                src_ref=src_refs[i].at[2 * px + py], dst_ref=land_refs[i].at[j],
                send_sem=send_sems.at[3 * i + j], recv_sem=recv_sems.at[3 * i + j],
                device_id=(px, py, pc), device_id_type=MESH))
    return cps


def _rs_start(chipsums, after, name):
    lands = [lax.empty((3,) + a.shape[1:], a.dtype) for a in chipsums]

    def start(src_refs, land_refs, send_sems, recv_sems):
        for cp in _rs_copies(src_refs, land_refs, send_sems, recv_sems):
            cp.start()

    return _split_start(chipsums, lands, after, start, (3 * len(chipsums),), name)


def _rs_wait(handle, after, name):
    def wait(src_refs, land_refs, send_sems, recv_sems):
        for cp in _rs_copies(src_refs, land_refs, send_sems, recv_sems):
            cp.wait_send()
            cp.wait_recv()

    return _split_wait(handle, after, wait, name)


def _row_tile(R, C, itemsize=4, budget=2 * 1024 * 1024):
    best = None
    for t in range(16, R + 1, 16):
        if R % t == 0 and t * C * itemsize <= budget:
            best = t
    return best if best is not None else R


def _add_half(arr, recv, c_idx, name):
    _, R, C = arr.shape
    tr = _row_tile(R, C)

    def body(c_ref, a_ref, r_ref, o_ref):
        o_ref[...] = (a_ref[...].astype(F32) + r_ref[...].astype(F32)).astype(o_ref.dtype)

    gs = pltpu.PrefetchScalarGridSpec(
        num_scalar_prefetch=1, grid=(4, R // tr),
        in_specs=[pl.BlockSpec((None, tr, C), lambda q, i, c_ref: (4 * c_ref[0] + q, i, 0)),
                  pl.BlockSpec((None, tr, C), lambda q, i, c_ref: (q, i, 0))],
        out_specs=pl.BlockSpec((None, tr, C), lambda q, i, c_ref: (q, i, 0)))
    return _pcall(body, name=name, grid_spec=gs, out_shape=jax.ShapeDtypeStruct((4, R, C), arr.dtype),
                  compiler_params=_params("parallel", "parallel"))(c_idx, arr, recv)


def _sum_chips(chipsum, recv, q_idx, name, transposed=False):
    _, R, C = chipsum.shape
    tr = min(512, R) if transposed else _row_tile(R, C)

    def body(q_ref, a_ref, r_ref, o_ref):
        acc = a_ref[...].astype(F32)
        for j in range(3):
            acc = acc + r_ref[j].astype(F32)
        o_ref[...] = acc.T if transposed else acc

    gs = pltpu.PrefetchScalarGridSpec(
        num_scalar_prefetch=1, grid=(R // tr,),
        in_specs=[pl.BlockSpec((None, tr, C), lambda i, q_ref: (q_ref[0], i, 0)),
                  pl.BlockSpec((3, tr, C), lambda i, q_ref: (0, i, 0))],
        out_specs=(pl.BlockSpec((C, tr), lambda i, q_ref: (0, i)) if transposed
                   else pl.BlockSpec((tr, C), lambda i, q_ref: (i, 0))))
    return _pcall(body, name=name, grid_spec=gs,
                  out_shape=jax.ShapeDtypeStruct((C, R) if transposed else (R, C), F32),
                  compiler_params=_params("parallel"))(q_idx, chipsum, recv)


def _adamw(w, g, m, v, name):
    shape = w.shape
    C = shape[-1]
    R = int(np.prod(shape[:-1]))
    tr = _row_tile(R, C, budget=1024 * 1024)

    def body(w_ref, g_ref, m_ref, v_ref, d_ref, nm_ref, nv_ref):
        gv = g_ref[...]
        mv = ADAM_B1 * m_ref[...] + (1.0 - ADAM_B1) * gv
        vv = ADAM_B2 * v_ref[...] + (1.0 - ADAM_B2) * jnp.square(gv)
        m_hat = mv / (1.0 - ADAM_B1 ** ADAM_STEP)
        v_hat = vv / (1.0 - ADAM_B2 ** ADAM_STEP)
        d_ref[...] = -ADAM_LR * (m_hat / (jnp.sqrt(v_hat) + ADAM_EPS) + ADAM_WD * w_ref[...])
        nm_ref[...] = mv
        nv_ref[...] = vv

    blk = pl.BlockSpec((tr, C), lambda i: (i, 0))
    shp = jax.ShapeDtypeStruct((R, C), F32)
    outs = _pcall(body, name=name, grid=(R // tr,), in_specs=[blk] * 4, out_specs=[blk] * 3,
                  out_shape=[shp] * 3, compiler_params=_params("parallel"))(
        w.reshape(R, C), g.reshape(R, C), m.reshape(R, C), v.reshape(R, C))
    return tuple(o.reshape(shape) for o in outs)


def _pad_cols(w, width):
    return jnp.pad(w, ((0, 0), (0, width - w.shape[1])))


def _pad_rows(w, height):
    return jnp.pad(w, ((0, height - w.shape[0]), (0, 0)))


def _slot_to_device_order(a):
    s = a.shape
    return a.reshape((2, 4) + s[1:]).swapaxes(0, 1).reshape(s)


def _device_to_slot_order(a):
    s = a.shape
    return a.reshape((4, 2) + s[1:]).swapaxes(0, 1).reshape(s)


def kernel(x, ffn1_w_gate, ffn1_w_up, ffn1_w_down, ffn2_w_gate, ffn2_w_up, ffn2_w_down, ln_gain, ln_bias, pool_w_in, pool_w_group, pool_scale, pool_w_out, attn_w_qkv, attn_w_out, loss_target, m_ffn1_w_gate, m_ffn1_w_up, m_ffn1_w_down, m_ffn2_w_gate, m_ffn2_w_up, m_ffn2_w_down, m_ln_gain, m_ln_bias, m_pool_w_in, m_pool_w_group, m_pool_scale, m_pool_w_out, m_attn_w_qkv, m_attn_w_out, v_ffn1_w_gate, v_ffn1_w_up, v_ffn1_w_down, v_ffn2_w_gate, v_ffn2_w_up, v_ffn2_w_down, v_ln_gain, v_ln_bias, v_pool_w_in, v_pool_w_group, v_pool_scale, v_pool_w_out, v_attn_w_qkv, v_attn_w_out):
    T = x.shape[1]
    fs = ffn1_w_gate.shape[2]
    fp = _round_up(fs, LANE)
    rs = D // N_DEV
    x0 = x[0]
    tgt = loss_target[0]
    slopes = _alibi_slopes()
    c_idx = lax.axis_index("c").astype(jnp.int32).reshape(1)
    q_idx = (2 * lax.axis_index("x") + lax.axis_index("y")).astype(jnp.int32).reshape(1)

    gates = (ffn1_w_gate, ffn2_w_gate)
    ups = (ffn1_w_up, ffn2_w_up)
    downs = (ffn1_w_down, ffn2_w_down)
    ffns = [(i, k) for i in range(DEPTH) for k in range(2)]
    def padded_cols(w, i):
        return _pad_rows(jnp.swapaxes(w, 1, 2)[i], fp).T

    wgu_sh = [jnp.concatenate([padded_cols(gates[k], i), padded_cols(ups[k], i)], axis=1).astype(BF16)
              for i, k in ffns]
    wd_sh = [_pad_rows(downs[k][i], fp).astype(BF16) for i, k in ffns]
    sq_sh = jnp.concatenate([wd_sh[0], pool_w_in[0].astype(BF16), pool_w_out[0].astype(BF16),
                             pool_w_group[0].reshape(rs // 4, D).astype(BF16)], axis=0)
    qkv_sh = attn_w_qkv[0].astype(BF16)
    aout_sh = attn_w_out[0].astype(BF16)
    ln_sh = jnp.concatenate([ln_gain.reshape(DEPTH * 3, rs), ln_bias.reshape(DEPTH * 3, rs),
                             jnp.zeros((4, rs), F32)], axis=0)

    h0 = _ag_start([wgu_sh[0], ln_sh], (), "ag0")
    x0b, x0t = _transpose_cast(x0, "x_cast", deps=(h0[4],))
    wgu0, ln_all = _ag_finish(h0, (x0b,), "ag0")
    h1 = _ag_start([sq_sh], (wgu0,), "ag1")
    ln_all = _slot_to_device_order(ln_all).transpose(1, 0, 2).reshape(16, D)
    gain = lambda i, s: ln_all[3 * i + s][None]
    bias = lambda i, s: ln_all[DEPTH * 3 + 3 * i + s][None]

    def ffn_fwd(xf, xb, wgu, wd, f, i, s, dep_up=(), dep_down=()):
        g, u, act = _ffn_up(xb, wgu, fp, f"ffn_up{f}", deps=dep_up)
        wd = wd(act) if callable(wd) else wd
        y, yb, yt, xh, rstd = _mm_ln(act, wd, xf, gain(i, s), bias(i, s), MACARON, f"ffn_down_ln{f}",
                                     deps=dep_down() if callable(dep_down) else dep_down)
        return (y, yb, yt), dict(g=g, u=u, act=act, xh=xh, rstd=rstd, wgu=wgu, wd=wd)

    pool_w = {}

    def wd0_after(act):
        (sq_all,) = _ag_finish(h1, (act,), "ag1")
        pool_w["h2"] = _ag_start([wgu_sh[1], wd_sh[1]], (sq_all,), "ag2")
        pool_w["pin"] = _slot_to_device_order(sq_all[:, fp:fp + rs, :]).reshape(D, D)
        pool_w["pout"] = _slot_to_device_order(sq_all[:, fp + rs:fp + 2 * rs, :]).reshape(D, D)
        grp = _slot_to_device_order(sq_all[:, fp + 2 * rs:, :])
        pool_w["grp"] = grp.reshape(N_DEV, N_POOL_GROUPS, rs // 4, POOL_GROUP_DIM).transpose(1, 0, 2, 3).reshape(
            N_POOL_GROUPS, POOL_GROUP_DIM, POOL_GROUP_DIM)
        return sq_all[:, :fp, :].reshape(N_DEV * fp, D)

    (a1, a1b, a1t), s_f0 = ffn_fwd(x0, x0b, wgu0, wd0_after, 0, 0, 0, dep_up=(h1[4],),
                                   dep_down=lambda: (pool_w["h2"][4],))
    h2 = pool_w["h2"]
    w_pin, w_pout, w_grp = pool_w["pin"], pool_w["pout"], pool_w["grp"]
    tm = min(512, T)
    row_spec = pl.BlockSpec((tm, D), lambda i, j, k: (i, 0))
    full_w = pl.BlockSpec((D, D), lambda i, j, k: (0, 0))
    u_pool = _mm(a1b, w_pin, grid=(T // tm, 1, 1), a_spec=row_spec, b_spec=full_w,
                 out_shape=jax.ShapeDtypeStruct((T, D), F32), out_spec=row_spec, name="pool_in")
    mixedb, mixedt = _pool_window(u_pool, False, "pool_window")
    y_pool, ysb, yst = _pool_group(mixedb, w_grp, pool_scale)
    a2, a2b, a2t, xh_p, rstd_p = _mm_ln(ysb, w_pout, a1, gain(0, 1), bias(0, 1), 1.0, "pool_out_ln")
    wgu1, wd1 = _ag_finish(h2, (a2,), "ag2")
    h3 = _ag_start([wgu_sh[2], wd_sh[2]], (wgu1,), "ag3")
    (a3, a3b, a3t), s_f1 = ffn_fwd(a2, a2b, wgu1, wd1.reshape(N_DEV * fp, D), 1, 0, 2, dep_up=(h3[4],))
    wgu2, wd2 = _ag_finish(h3, (a3,), "ag3")
    h4 = _ag_start([qkv_sh, aout_sh], (wgu2,), "ag4")
    (b1, b1b, b1t), s_f2 = ffn_fwd(a3, a3b, wgu2, wd2.reshape(N_DEV * fp, D), 2, 1, 0, dep_up=(h4[4],))
    wqkv_all, aout_all = _ag_finish(h4, (b1,), "ag4")
    h5 = _ag_start([wgu_sh[3], wd_sh[3]], (wqkv_all,), "ag5")
    w_aout = _slot_to_device_order(aout_all).reshape(D, D)
    dils = [d for _, d in DIL_CONFIGS]
    wqkv_nat = _slots_to_columns(wqkv_all, "attn_wqkv_cols")
    qkv_gs, o_gs, lse_gs = [], [], []
    for gi, d in enumerate(dils):
        qkv_g = _attn_qkv_group(b1b, wqkv_nat, gi, d, f"attn_qkv{gi}", deps=(h5[4],) if gi == 0 else ())
        o_g, lse_g = _attn_fwd(qkv_g, d, slopes[gi], f"attn_fwd{gi}")
        qkv_gs.append(qkv_g)
        o_gs.append(o_g)
        lse_gs.append(lse_g)
    o32, ob, ot, lse_tot = _attn_combine(o_gs, lse_gs, dils)
    b2, b2b, b2t, xh_a, rstd_a = _mm_ln(ob, w_aout, b1, gain(1, 1), bias(1, 1), 1.0, "attn_out_ln")
    wgu3, wd3 = _ag_finish(h5, (b2,), "ag5")
    (b3, _, _), s_f3 = ffn_fwd(b2, b2b, wgu3, wd3.reshape(N_DEV * fp, D), 3, 1, 2)

    dy, loss_tile = _loss_head(b3, tgt)
    loss = lax.psum(loss_tile[0, 0], AXES)

    bm = min(512, D)
    dgains, dbiases = {}, {}
    rs_pending = []
    gsums = {}

    def rs_finish(after):
        h, tag = rs_pending.pop()
        chips, lands = _rs_wait(h, after, f"rs_{tag}_wait")
        gsums[tag] = [_sum_chips(a, r, q_idx, f"rs_sum_{tag}{i}", transposed=tag.startswith("f"))
                      for i, (a, r) in enumerate(zip(chips, lands))]

    def rs_stage(bufs, tag):
        if rs_pending:
            rs_finish((bufs[-1],))
        recv = _rs_sibling(bufs, f"rs_sib_{tag}")
        chips = [_add_half(a, r, c_idx, f"rs_add_{tag}{i}") for i, (a, r) in enumerate(zip(bufs, recv))]
        h = _rs_start(chips, (), f"rs_{tag}")
        rs_pending.append((h, tag))
        return h[4]

    def ffn_bwd(dys, f, i, s, st, xt):
        dxres, dhb, dht, dg, db = _ln_bwd(dys, st["xh"], st["rstd"], gain(i, s), MACARON, f"ffn_ln_bwd{f}")
        dgains[(i, s)], dbiases[(i, s)] = dg, db
        dgu = _ffn_bwd_act(dhb, st["wd"], st["g"], st["u"], fp, f"ffn_bwd_act{f}")
        g_dt = _ffn_dwd(dht, st["act"], fp, f"ffn_dwd{f}")
        g_gu = _mm(xt, dgu, grid=(D // bm, N_DEV, 1),
                   a_spec=pl.BlockSpec((bm, T), lambda r, j, k: (r, 0)),
                   b_spec=pl.BlockSpec((T, 2 * fp), lambda r, j, k: (0, j)),
                   out_shape=jax.ShapeDtypeStruct((N_DEV, D, 2 * fp), BF16),
                   out_spec=pl.BlockSpec((None, bm, 2 * fp), lambda r, j, k: (j, r, 0)), name=f"ffn_dwgu{f}")
        token = rs_stage([g_dt, g_gu], f"f{f}")
        return [_ffn_dx(dgu, st["wgu"], dxres, fp, f"ffn_dx{f}", deps=(token,))]

    def dw_square(at, bmat, name):
        return _mm(at, bmat, grid=(D // bm, 1, 1),
                   a_spec=pl.BlockSpec((bm, T), lambda r, j, k: (r, 0)),
                   b_spec=pl.BlockSpec((T, D), lambda r, j, k: (0, 0)),
                   out_shape=jax.ShapeDtypeStruct((D, D), BF16),
                   out_spec=pl.BlockSpec((bm, D), lambda r, j, k: (r, 0)), name=name)

    def dx_square(a, w, name, deps=()):
        return _mm(a, w, grid=(T // tm, 1, 1), nt=True, a_spec=row_spec, b_spec=full_w,
                   out_shape=jax.ShapeDtypeStruct((T, D), F32), out_spec=row_spec, name=name, deps=deps)

    to_slots = lambda g2d: _device_to_slot_order(g2d.reshape(N_DEV, rs, D))

    d_b2 = ffn_bwd([dy], 3, 1, 2, s_f3, b2t)
    dxres, dmb, dmt, dg, db = _ln_bwd(d_b2, xh_a, rstd_a, gain(1, 1), 1.0, "attn_ln_bwd")
    dgains[(1, 1)], dbiases[(1, 1)] = dg, db
    g_aout = dw_square(ot, dmb, "attn_dwout")
    dobs, statss = _attn_bwd_prep(dmb, w_aout, o32, lse_tot, dils)
    ntile = 3 * D // QKV_TILE
    g_qkv = None
    dqkv_gs = []
    for gi, d in enumerate(dils):
        dqkv_g = _attn_bwd(qkv_gs[gi], dobs[gi], statss[gi], d, slopes[gi], f"attn_bwd{gi}")
        dqkv_gs.append(dqkv_g)
        xt = b1t if d == 1 else _transpose_sub(b1, d, f"attn_xt{gi}")
        g_qkv = _attn_dw_group(xt, dqkv_g, gi, d, g_qkv, f"attn_dwqkv{gi}")
    token = rs_stage([to_slots(g_aout), g_qkv], "attn")
    dx_attn = [_attn_dx_group(dqkv_gs[gi], wqkv_nat, gi, d, f"attn_dx{gi}", deps=(token,) if gi == 0 else ())
               for gi, d in enumerate(dils)]
    d_a3 = ffn_bwd([dxres] + dx_attn, 2, 1, 0, s_f2, a3t)
    d_a2 = ffn_bwd(d_a3, 1, 0, 2, s_f1, a2t)
    dxres, dmb, dmt, dg, db = _ln_bwd(d_a2, xh_p, rstd_p, gain(0, 1), 1.0, "pool_ln_bwd")
    dgains[(0, 1)], dbiases[(0, 1)] = dg, db
    g_pout = dw_square(yst, dmb, "pool_dwout")
    dyb, dscale = _pool_bwd_out(dmb, w_pout, y_pool, pool_scale)
    gd = POOL_GROUP_DIM
    g_grp = _mm(mixedt, dyb, grid=(N_POOL_GROUPS, 1, 1),
                a_spec=pl.BlockSpec((gd, T), lambda g, j, k: (g, 0)),
                b_spec=pl.BlockSpec((T, gd), lambda g, j, k: (0, g)),
                out_shape=jax.ShapeDtypeStruct((N_POOL_GROUPS, gd, gd), BF16),
                out_spec=pl.BlockSpec((None, gd, gd), lambda g, j, k: (g, 0, 0)), name="pool_dwgroup")
    tg = min(1024, T)
    dmixed = _mm(dyb, w_grp, grid=(N_POOL_GROUPS, T // tg, 1), nt=True,
                 a_spec=pl.BlockSpec((tg, gd), lambda g, t, k: (t, g)),
                 b_spec=pl.BlockSpec((None, gd, gd), lambda g, t, k: (g, 0, 0)),
                 out_shape=jax.ShapeDtypeStruct((T, D), F32),
                 out_spec=pl.BlockSpec((tg, gd), lambda g, t, k: (t, g)), name="pool_dmixed")
    dub, _ = _pool_window(dmixed, True, "pool_window_bwd")
    g_pin = dw_square(a1t, dub, "pool_dwin")
    g_grp_slots = _device_to_slot_order(
        g_grp.reshape(N_POOL_GROUPS, N_DEV, rs // 4, gd).transpose(1, 0, 2, 3).reshape(N_DEV, rs // 4, D))
    token = rs_stage([to_slots(g_pout), g_grp_slots, to_slots(g_pin)], "pool")
    dx_pool = dx_square(dub, w_pin, "pool_dx", deps=(token,))
    d_x0 = ffn_bwd([dxres, dx_pool], 0, 0, 0, s_f0, x0t)
    grad_x = d_x0[0]
    gw_aout, gw_qkv = gsums["attn"]
    gw_pout, gw_grp, gw_pin = gsums["pool"]

    small = jnp.concatenate([dgains[(i, s)] for i in range(DEPTH) for s in range(3)]
                            + [dbiases[(i, s)] for i in range(DEPTH) for s in range(3)]
                            + [dscale, jnp.zeros((3, D), F32)], axis=0)
    _, small_sum = _small_all_gather(small, "ag_small_grads")
    dev = 4 * lax.axis_index("x") + 2 * lax.axis_index("y") + lax.axis_index("c")
    mine = lax.dynamic_slice_in_dim(small_sum, dev * rs, rs, axis=1)
    ffn_grad = {
        "gate": lambda k: jnp.stack([gsums[f"f{2 * i + k}"][1][:fs] for i in range(DEPTH)]),
        "up": lambda k: jnp.stack([gsums[f"f{2 * i + k}"][1][fp:fp + fs] for i in range(DEPTH)]),
        "down": lambda k: jnp.stack([gsums[f"f{2 * i + k}"][0][:fs] for i in range(DEPTH)]),
    }
    grads = {
        "ffn2_w_gate": ffn_grad["gate"](1),
        "ffn2_w_up": ffn_grad["up"](1),
        "ffn2_w_down": ffn_grad["down"](1),
        "ln_gain": mine[0:DEPTH * 3].reshape(DEPTH, 3, rs),
        "ln_bias": mine[DEPTH * 3:2 * DEPTH * 3].reshape(DEPTH, 3, rs),
        "pool_w_in": gw_pin[None],
        "pool_w_group": gw_grp[:rs // 4].reshape(N_POOL_GROUPS, rs // 4, gd)[None],
        "pool_scale": small_sum[2 * DEPTH * 3][None],
        "pool_w_out": gw_pout[None],
        "attn_w_qkv": gw_qkv[None],
        "attn_w_out": gw_aout[None],
    }
    weights = dict(ffn1_w_gate=ffn1_w_gate, ffn1_w_up=ffn1_w_up, ffn1_w_down=ffn1_w_down,
                   ffn2_w_gate=ffn2_w_gate, ffn2_w_up=ffn2_w_up, ffn2_w_down=ffn2_w_down,
                   ln_gain=ln_gain, ln_bias=ln_bias, pool_w_in=pool_w_in, pool_w_group=pool_w_group,
                   pool_scale=pool_scale, pool_w_out=pool_w_out, attn_w_qkv=attn_w_qkv, attn_w_out=attn_w_out)
    ms = dict(ffn1_w_gate=m_ffn1_w_gate, ffn1_w_up=m_ffn1_w_up, ffn1_w_down=m_ffn1_w_down,
              ffn2_w_gate=m_ffn2_w_gate, ffn2_w_up=m_ffn2_w_up, ffn2_w_down=m_ffn2_w_down,
              ln_gain=m_ln_gain, ln_bias=m_ln_bias, pool_w_in=m_pool_w_in, pool_w_group=m_pool_w_group,
              pool_scale=m_pool_scale, pool_w_out=m_pool_w_out, attn_w_qkv=m_attn_w_qkv, attn_w_out=m_attn_w_out)
    vs = dict(ffn1_w_gate=v_ffn1_w_gate, ffn1_w_up=v_ffn1_w_up, ffn1_w_down=v_ffn1_w_down,
              ffn2_w_gate=v_ffn2_w_gate, ffn2_w_up=v_ffn2_w_up, ffn2_w_down=v_ffn2_w_down,
              ln_gain=v_ln_gain, ln_bias=v_ln_bias, pool_w_in=v_pool_w_in, pool_w_group=v_pool_w_group,
              pool_scale=v_pool_scale, pool_w_out=v_pool_w_out, attn_w_qkv=v_attn_w_qkv, attn_w_out=v_attn_w_out)
    names = list(weights)
    col_sharded = ("ffn1_w_gate", "ffn1_w_up", "ffn2_w_gate", "ffn2_w_up")
    deltas, new_m, new_v = {}, {}, {}

    def update(nme):
        if nme in col_sharded:
            outs = _adamw(jnp.swapaxes(weights[nme], 1, 2), grads[nme], jnp.swapaxes(ms[nme], 1, 2),
                          jnp.swapaxes(vs[nme], 1, 2), f"adamw_{nme}")
            deltas[nme], new_m[nme], new_v[nme] = (jnp.swapaxes(o, 1, 2) for o in outs)
            grads[nme] = jnp.swapaxes(grads[nme], 1, 2)
        else:
            deltas[nme], new_m[nme], new_v[nme] = _adamw(weights[nme], grads[nme], ms[nme], vs[nme], f"adamw_{nme}")

    late = ("ffn1_w_gate", "ffn1_w_up", "ffn1_w_down")
    for nme in names:
        if nme not in late:
            update(nme)
    rs_finish((grad_x,) + tuple(deltas[nme] for nme in names if nme not in late))
    grad_x = grad_x[None]
    for part, nme in zip(("gate", "up", "down"), late):
        grads[nme] = ffn_grad[part](0)
        update(nme)
    return (loss, grad_x, *[grads[k] for k in names], *[deltas[k] for k in names],
            *[new_m[k] for k in names], *[new_v[k] for k in names])
```

```python
import functools

import numpy as np
import jax
import jax.numpy as jnp
from jax import lax
from jax.experimental import pallas as pl
from jax.experimental.pallas import tpu as pltpu

F32 = jnp.float32
BF16 = jnp.bfloat16

D = 1024
N_DEV = 8
N_HEADS = 16
HEAD_DIM = 64
N_POOL_GROUPS = 4
POOL_GROUP_DIM = 256
POOL_HALF = (1, 2, 4, 8)
DIL_CONFIGS = ((128, 1), (512, 4), (2048, 16))
ATTN_HALO = 64
ATTN_BLOCK = 128
QKV_SHARD = 3 * 3 * D // N_DEV
DEPTH = 2
ALPHA = (2.0 * DEPTH) ** 0.25
MACARON = 0.5
LN_EPS = 1e-5
MASK_VALUE = -1e30
ADAM_LR = 0.001
ADAM_B1 = 0.9
ADAM_B2 = 0.999
ADAM_EPS = 1e-08
ADAM_WD = 0.01
ADAM_STEP = 10
LANE = 128
VMEM_LIMIT = 56 * 1024 * 1024
MESH = pl.DeviceIdType.MESH
AXES = ("x", "y", "c")


def _round_up(n, m):
    return (n + m - 1) // m * m


def _pcall(body, deps=(), **kw):
    if not deps:
        return pl.pallas_call(body, **kw)
    n_in, n_dep = len(kw["in_specs"]), len(deps)

    def wrapped(*refs):
        return body(*refs[:n_in], *refs[n_in + n_dep:])

    kw["in_specs"] = list(kw["in_specs"]) + [pl.BlockSpec(memory_space=pl.ANY)] * n_dep
    call = pl.pallas_call(wrapped, **kw)
    return lambda *args: call(*args, *deps)


def _params(*sem):
    return pltpu.CompilerParams(dimension_semantics=sem, vmem_limit_bytes=VMEM_LIMIT)


def _alibi_slopes():
    n = len(DIL_CONFIGS) * N_HEADS
    s = 2.0 ** (-8.0 * np.arange(1, n + 1) / n)
    return s.reshape(len(DIL_CONFIGS), N_HEADS).astype(np.float32)


def _my_slot():
    return 4 * lax.axis_index("c") + 2 * lax.axis_index("x") + lax.axis_index("y")


def _mm(a, b, *, grid, a_spec, b_spec, out_shape, out_spec, nt=False, name, alias=None, deps=()):
    nk = grid[2]
    dn = (((1,), (1,)), ((), ())) if nt else (((1,), (0,)), ((), ()))
    blk = tuple(s for s in out_spec.block_shape if s is not None)

    def body(*refs):
        a_ref, b_ref = refs[0], refs[1]
        o_ref = refs[3] if alias is not None else refs[2]
        p = lax.dot_general(a_ref[...], b_ref[...], dn, preferred_element_type=F32)
        if nk == 1:
            o_ref[...] = p.astype(o_ref.dtype)
        else:
            acc = refs[-1]
            k = pl.program_id(2)

            @pl.when(k == 0)
            def _():
                acc[...] = p

            @pl.when(k > 0)
            def _():
                acc[...] += p

            @pl.when(k == nk - 1)
            def _():
                o_ref[...] = acc[...].astype(o_ref.dtype)

    in_specs = [a_spec, b_spec]
    args = [a, b]
    aliases = {}
    if alias is not None:
        in_specs.append(pl.BlockSpec(memory_space=pl.ANY))
        args.append(alias)
        aliases = {2: 0}
    return _pcall(
        body, deps=deps, name=name, grid=grid, in_specs=in_specs, out_specs=out_spec, out_shape=out_shape,
        scratch_shapes=[] if nk == 1 else [pltpu.VMEM(blk, F32)],
        input_output_aliases=aliases,
        compiler_params=_params("parallel", "parallel", "arbitrary"),
    )(*args)


def _transpose_cast(x, name, deps=()):
    T = x.shape[0]
    tm = min(512, T)

    def body(x_ref, xb_ref, xt_ref):
        v = x_ref[...]
        xb_ref[...] = v.astype(BF16)
        xt_ref[...] = v.T.astype(BF16)

    return _pcall(
        body, deps=deps, name=name, grid=(T // tm,),
        in_specs=[pl.BlockSpec((tm, D), lambda t: (t, 0))],
        out_specs=[pl.BlockSpec((tm, D), lambda t: (t, 0)), pl.BlockSpec((D, tm), lambda t: (0, t))],
        out_shape=[jax.ShapeDtypeStruct((T, D), BF16), jax.ShapeDtypeStruct((D, T), BF16)],
        compiler_params=_params("parallel"),
    )(x)


def _mm_ln(a, b, xres, gain, bias, hscale, name, deps=()):
    T, K = a.shape
    tm = min(512, T)

    def body(a_ref, b_ref, x_ref, g_ref, bt_ref, y_ref, yb_ref, yt_ref, xh_ref, rs_ref):
        h = jnp.dot(a_ref[...], b_ref[...], preferred_element_type=F32)
        z = ALPHA * x_ref[...] + hscale * h
        mu = jnp.mean(z, axis=-1, keepdims=True)
        zc = z - mu
        var = jnp.mean(zc * zc, axis=-1, keepdims=True)
        rstd = lax.rsqrt(var + LN_EPS)
        xh = zc * rstd
        y = xh * g_ref[...] + bt_ref[...]
        y_ref[...] = y
        yb_ref[...] = y.astype(BF16)
        yt_ref[...] = y.T.astype(BF16)
        xh_ref[...] = xh
        rs_ref[...] = rstd

    row = pl.BlockSpec((tm, D), lambda t: (t, 0))
    vec = pl.BlockSpec((1, D), lambda t: (0, 0))
    return _pcall(
        body, deps=deps, name=name, grid=(T // tm,),
        in_specs=[pl.BlockSpec((tm, K), lambda t: (t, 0)), pl.BlockSpec((K, D), lambda t: (0, 0)), row, vec, vec],
        out_specs=[row, row, pl.BlockSpec((D, tm), lambda t: (0, t)), row, pl.BlockSpec((tm, 1), lambda t: (t, 0))],
        out_shape=[jax.ShapeDtypeStruct((T, D), F32), jax.ShapeDtypeStruct((T, D), BF16),
                   jax.ShapeDtypeStruct((D, T), BF16), jax.ShapeDtypeStruct((T, D), F32),
                   jax.ShapeDtypeStruct((T, 1), F32)],
        compiler_params=_params("parallel"),
    )(a, b, xres, gain, bias)


def _ln_bwd(dys, xhat, rstd, gain, hscale, name):
    T = xhat.shape[0]
    tm = min(512, T)
    n = len(dys)

    def body(*refs):
        dy_refs = refs[:n]
        xh_ref, rs_ref, g_ref, dx_ref, dh_ref, dht_ref, dg_ref, db_ref = refs[n:]
        dy = dy_refs[0][...]
        for r in dy_refs[1:]:
            dy = dy + r[...]
        xh = xh_ref[...]
        dxh = dy * g_ref[...]
        m1 = jnp.mean(dxh, axis=-1, keepdims=True)
        m2 = jnp.mean(dxh * xh, axis=-1, keepdims=True)
        dz = rs_ref[...] * (dxh - m1 - xh * m2)
        dx_ref[...] = ALPHA * dz
        dh = hscale * dz
        dh_ref[...] = dh.astype(BF16)
        dht_ref[...] = dh.T.astype(BF16)
        dg = jnp.sum(dy * xh, axis=0, keepdims=True)
        db = jnp.sum(dy, axis=0, keepdims=True)

        @pl.when(pl.program_id(0) == 0)
        def _():
            dg_ref[...] = dg
            db_ref[...] = db

        @pl.when(pl.program_id(0) > 0)
        def _():
            dg_ref[...] += dg
            db_ref[...] += db

    row = pl.BlockSpec((tm, D), lambda t: (t, 0))
    vec = pl.BlockSpec((1, D), lambda t: (0, 0))
    return _pcall(
        body, name=name, grid=(T // tm,),
        in_specs=[row] * n + [row, pl.BlockSpec((tm, 1), lambda t: (t, 0)), vec],
        out_specs=[row, row, pl.BlockSpec((D, tm), lambda t: (0, t)), vec, vec],
        out_shape=[jax.ShapeDtypeStruct((T, D), F32), jax.ShapeDtypeStruct((T, D), BF16),
                   jax.ShapeDtypeStruct((D, T), BF16), jax.ShapeDtypeStruct((1, D), F32),
                   jax.ShapeDtypeStruct((1, D), F32)],
        compiler_params=_params("arbitrary"),
    )(*dys, xhat, rstd, gain)


def _add2(a, b, name):
    T = a.shape[0]
    tm = min(512, T)

    def body(a_ref, b_ref, o_ref):
        o_ref[...] = a_ref[...] + b_ref[...]

    row = pl.BlockSpec((tm, D), lambda t: (t, 0))
    return _pcall(body, name=name, grid=(T // tm,), in_specs=[row, row], out_specs=row,
                  out_shape=jax.ShapeDtypeStruct((T, D), F32), compiler_params=_params("parallel"))(a, b)


def _loss_head(y, tgt):
    T = y.shape[0]
    tm = min(512, T)

    def body(y_ref, t_ref, dy_ref, l_ref):
        e = y_ref[...] - t_ref[...]
        dy_ref[...] = e * (1.0 / D)
        part = jnp.sum(jnp.sum(e * e, axis=1, keepdims=True), axis=0, keepdims=True) * (0.5 / D)

        @pl.when(pl.program_id(0) == 0)
        def _():
            l_ref[...] = jnp.zeros_like(l_ref)

        l_ref[...] += part

    row = pl.BlockSpec((tm, D), lambda t: (t, 0))
    return _pcall(
        body, name="loss_head", grid=(T // tm,),
        in_specs=[row, row],
        out_specs=[row, pl.BlockSpec((8, LANE), lambda t: (0, 0))],
        out_shape=[jax.ShapeDtypeStruct((T, D), F32), jax.ShapeDtypeStruct((8, LANE), F32)],
        compiler_params=_params("arbitrary"),
    )(y, tgt)


def _sigmoid(v):
    return 0.5 * jnp.tanh(0.5 * v) + 0.5


def _ffn_up(xb, wgu, fp, name, deps=()):
    T = xb.shape[0]
    tm = min(1024, T)

    def body(x_ref, w_ref, dg_ref, du_ref, a_ref):
        x = x_ref[...]
        for s in range(2):
            p = jnp.dot(x, w_ref[s], preferred_element_type=F32)
            g = p[:, :fp]
            u = p[:, fp:]
            sig = _sigmoid(g)
            q = g * sig
            cols = slice(s * fp, (s + 1) * fp)
            dg_ref[:, cols] = (sig * (1.0 + g - q) * u).astype(BF16)
            du_ref[:, cols] = q.astype(BF16)
            a_ref[:, cols] = (q * u).astype(BF16)

    out = pl.BlockSpec((tm, 2 * fp), lambda t, j: (t, j))
    shp = jax.ShapeDtypeStruct((T, N_DEV * fp), BF16)
    return _pcall(
        body, deps=deps, name=name, grid=(T // tm, N_DEV // 2),
        in_specs=[pl.BlockSpec((tm, D), lambda t, j: (t, 0)),
                  pl.BlockSpec((2, D, 2 * fp), lambda t, j: (j, 0, 0))],
        out_specs=[out, out, out], out_shape=[shp, shp, shp],
        compiler_params=_params("parallel", "parallel"),
    )(xb, wgu)


def _ffn_bwd_act(dhb, wd, g, u, fp, name):
    T = dhb.shape[0]
    tm = min(1024, T)

    def body(dh_ref, w_ref, g_ref, u_ref, o_ref):
        da = lax.dot_general(dh_ref[...], w_ref[...], (((1,), (1,)), ((), ())), preferred_element_type=F32)
        dgate = (da * g_ref[...].astype(F32)).astype(BF16)
        dup = (da * u_ref[...].astype(F32)).astype(BF16)
        for s in range(2):
            o_ref[:, 2 * s * fp:(2 * s + 1) * fp] = dgate[:, s * fp:(s + 1) * fp]
            o_ref[:, (2 * s + 1) * fp:(2 * s + 2) * fp] = dup[:, s * fp:(s + 1) * fp]

    blk = pl.BlockSpec((tm, 2 * fp), lambda t, j: (t, j))
    return _pcall(
        body, name=name, grid=(T // tm, N_DEV // 2),
        in_specs=[pl.BlockSpec((tm, D), lambda t, j: (t, 0)), pl.BlockSpec((2 * fp, D), lambda t, j: (j, 0)), blk, blk],
        out_specs=pl.BlockSpec((tm, 4 * fp), lambda t, j: (t, j)),
        out_shape=jax.ShapeDtypeStruct((T, N_DEV * 2 * fp), BF16),
        compiler_params=_params("parallel", "parallel"),
    )(dhb, wd, g, u)


def _ffn_dwd(dht, act, fp, name):
    T = dht.shape[1]
    bm = 512

    def body(a_ref, b_ref, o_ref):
        p = jnp.dot(a_ref[...], b_ref[...], preferred_element_type=F32)
        o_ref[0] = p[:, :fp].astype(BF16)
        o_ref[1] = p[:, fp:].astype(BF16)

    return _pcall(
        body, name=name, grid=(D // bm, N_DEV // 2),
        in_specs=[pl.BlockSpec((bm, T), lambda i, j: (i, 0)), pl.BlockSpec((T, 2 * fp), lambda i, j: (0, j))],
        out_specs=pl.BlockSpec((2, bm, fp), lambda i, j: (j, i, 0)),
        out_shape=jax.ShapeDtypeStruct((N_DEV, D, fp), BF16),
        compiler_params=_params("parallel", "parallel"),
    )(dht, act)


def _ffn_dx(dgu, wgu, res, fp, name, deps=()):
    T = dgu.shape[0]
    tm = min(512, T)

    def body(a_ref, w_ref, r_ref, o_ref):
        acc = r_ref[...]
        for j in range(N_DEV):
            p = lax.dot_general(a_ref[:, j * 2 * fp:(j + 1) * 2 * fp], w_ref[j], (((1,), (1,)), ((), ())),
                                preferred_element_type=F32)
            acc = acc + p
        o_ref[...] = acc

    return _pcall(
        body, deps=deps, name=name, grid=(T // tm,),
        in_specs=[pl.BlockSpec((tm, N_DEV * 2 * fp), lambda t: (t, 0)),
                  pl.BlockSpec((N_DEV, D, 2 * fp), lambda t: (0, 0, 0)),
                  pl.BlockSpec((tm, D), lambda t: (t, 0))],
        out_specs=pl.BlockSpec((tm, D), lambda t: (t, 0)),
        out_shape=jax.ShapeDtypeStruct((T, D), F32),
        compiler_params=_params("parallel"),
    )(dgu, wgu, res)


def _slots_to_columns(w, name):
    _, R, C = w.shape
    tr = min(512, R)

    def body(i_ref, o_ref):
        o_ref[...] = i_ref[...]

    return _pcall(
        body, name=name, grid=(N_DEV, R // tr),
        in_specs=[pl.BlockSpec((None, tr, C), lambda j, i: (4 * (j % 2) + j // 2, i, 0))],
        out_specs=pl.BlockSpec((tr, C), lambda j, i: (i, j)),
        out_shape=jax.ShapeDtypeStruct((R, N_DEV * C), w.dtype),
        compiler_params=_params("parallel", "parallel"),
    )(w)


POOL_PAD = 16
POOL_CHUNK = 512


def _pool_window(v, transpose, name):
    T = v.shape[0]
    ch = min(POOL_CHUNK, T)
    ext = ch + 2 * POOL_PAD
    gd = POOL_GROUP_DIM

    def body(v_ref, o_ref, ot_ref, pad_ref):
        pad_ref[0:POOL_PAD, :] = jnp.zeros((POOL_PAD, gd), F32)
        pad_ref[POOL_PAD + T:POOL_PAD + T + POOL_PAD, :] = jnp.zeros((POOL_PAD, gd), F32)
        for gi, hw in enumerate(POOL_HALF):
            @pl.when(pl.program_id(0) == gi)
            def _(hw=hw):
                def count(t):
                    return (jnp.minimum(t + hw, T) - jnp.maximum(t - hw, 0)).astype(F32)

                if transpose:
                    t_all = lax.broadcasted_iota(jnp.int32, (T, gd), 0)
                    pad_ref[POOL_PAD:POOL_PAD + T, :] = v_ref[...] / count(t_all)
                else:
                    pad_ref[POOL_PAD:POOL_PAD + T, :] = v_ref[...]
                shift = hw if transpose else hw - 1
                for c in range(T // ch):
                    e = pad_ref[c * ch:c * ch + ext, :]
                    step = 1
                    while step < 2 * hw:
                        e = e + pltpu.roll(e, step, 0)
                        step *= 2
                    if shift:
                        e = pltpu.roll(e, ext - shift, 0)
                    s = e[POOL_PAD:POOL_PAD + ch, :]
                    center = v_ref[c * ch:(c + 1) * ch, :]
                    if transpose:
                        res = s - center
                    else:
                        t_idx = c * ch + lax.broadcasted_iota(jnp.int32, (ch, gd), 0)
                        res = s / count(t_idx) - center
                    o_ref[c * ch:(c + 1) * ch, :] = res.astype(BF16)
                    ot_ref[:, c * ch:(c + 1) * ch] = res.T.astype(BF16)

    return _pcall(
        body, name=name, grid=(N_POOL_GROUPS,),
        in_specs=[pl.BlockSpec((T, gd), lambda g: (0, g))],
        out_specs=[pl.BlockSpec((T, gd), lambda g: (0, g)), pl.BlockSpec((gd, T), lambda g: (g, 0))],
        out_shape=[jax.ShapeDtypeStruct((T, D), BF16), jax.ShapeDtypeStruct((D, T), BF16)],
        scratch_shapes=[pltpu.VMEM((T + 2 * POOL_PAD, gd), F32)],
        compiler_params=_params("arbitrary"),
    )(v)


def _pool_group(mixedb, wgroup, scale):
    T = mixedb.shape[0]
    tm = min(1024, T)
    gd = POOL_GROUP_DIM

    def body(a_ref, w_ref, s_ref, y_ref, ys_ref, yst_ref):
        y = jnp.dot(a_ref[...], w_ref[...], preferred_element_type=F32)
        ys = y * s_ref[...]
        y_ref[...] = y
        ys_ref[...] = ys.astype(BF16)
        yst_ref[...] = ys.T.astype(BF16)

    blk = pl.BlockSpec((tm, gd), lambda g, t: (t, g))
    return _pcall(
        body, name="pool_group", grid=(N_POOL_GROUPS, T // tm),
        in_specs=[blk, pl.BlockSpec((None, gd, gd), lambda g, t: (g, 0, 0)), pl.BlockSpec((1, gd), lambda g, t: (0, g))],
        out_specs=[blk, blk, pl.BlockSpec((gd, tm), lambda g, t: (g, t))],
        out_shape=[jax.ShapeDtypeStruct((T, D), F32), jax.ShapeDtypeStruct((T, D), BF16),
                   jax.ShapeDtypeStruct((D, T), BF16)],
        compiler_params=_params("parallel", "parallel"),
    )(mixedb, wgroup, scale)


def _pool_bwd_out(dmb, w_out, y, scale):
    T = dmb.shape[0]
    tm = min(512, T)

    def body(a_ref, w_ref, y_ref, s_ref, dy_ref, ds_ref):
        dys = lax.dot_general(a_ref[...], w_ref[...], (((1,), (1,)), ((), ())), preferred_element_type=F32)
        dy_ref[...] = (dys * s_ref[...]).astype(BF16)
        part = jnp.sum(dys * y_ref[...], axis=0, keepdims=True)

        @pl.when(pl.program_id(0) == 0)
        def _():
            ds_ref[...] = part

        @pl.when(pl.program_id(0) > 0)
        def _():
            ds_ref[...] += part

    row = pl.BlockSpec((tm, D), lambda t: (t, 0))
    vec = pl.BlockSpec((1, D), lambda t: (0, 0))
    return _pcall(
        body, name="pool_bwd_out", grid=(T // tm,),
        in_specs=[row, pl.BlockSpec((D, D), lambda t: (0, 0)), row, vec],
        out_specs=[row, vec],
        out_shape=[jax.ShapeDtypeStruct((T, D), BF16), jax.ShapeDtypeStruct((1, D), F32)],
        compiler_params=_params("arbitrary"),
    )(dmb, w_out, y, scale)


def _attn_bias_table(d, slopes):
    w = ATTN_BLOCK + 2 * ATTN_HALO
    rel = np.arange(w)[None, :] - ATTN_HALO - np.arange(ATTN_BLOCK)[:, None]
    dist = (d * np.abs(rel)).astype(np.float32)
    bias = -np.asarray(slopes, np.float32)[:, None, None] * dist[None]
    return jnp.asarray(np.where(np.abs(rel)[None] <= ATTN_HALO, bias, np.float32(MASK_VALUE)).astype(np.float32))


def _attn_in_range(n, L):
    w = ATTN_BLOCK + 2 * ATTN_HALO
    j = n * ATTN_BLOCK - ATTN_HALO + lax.broadcasted_iota(jnp.int32, (ATTN_BLOCK, w), 1)
    return (j >= 0) & (j < L)


BIAS_SPEC = pl.BlockSpec((N_HEADS, ATTN_BLOCK, ATTN_BLOCK + 2 * ATTN_HALO), lambda r, n: (0, 0, 0))


def _lane_col(st, idx):
    lane = lax.broadcasted_iota(jnp.int32, st.shape, 1)
    return jnp.sum(jnp.where(lane == idx, st, 0.0), axis=1, keepdims=True)


def _window_specs(nb, d, col, width):
    last = 2 * d * nb - 1

    def prev(r, n):
        return (jnp.maximum(2 * (r * nb + n) - 1, 0), col)

    def cur(r, n):
        return (r * nb + n, col)

    def nxt(r, n):
        return (jnp.minimum(2 * (r * nb + n) + 2, last), col)

    return [pl.BlockSpec((ATTN_HALO, width), prev), pl.BlockSpec((ATTN_BLOCK, width), cur),
            pl.BlockSpec((ATTN_HALO, width), nxt)]


def _attn_fwd(qkv_g, d, slopes, name):
    T = qkv_g.shape[0]
    L = T // d
    nb = L // ATTN_BLOCK

    def body(q_ref, kp_ref, kc_ref, kn_ref, vp_ref, vc_ref, vn_ref, b_ref, o_ref, lse_ref):
        in_range = _attn_in_range(pl.program_id(1), L)
        lane = lax.broadcasted_iota(jnp.int32, (ATTN_BLOCK, LANE), 1)
        first = lane < HEAD_DIM
        sc = HEAD_DIM ** -0.5
        head_mask = [jnp.where(first, sc, 0.0).astype(BF16), jnp.where(first, 0.0, sc).astype(BF16)]
        lse_acc = jnp.zeros((ATTN_BLOCK, LANE), F32)
        for hp in range(N_HEADS // 2):
            cs = slice(hp * LANE, (hp + 1) * LANE)
            q2 = q_ref[:, cs]
            k2 = jnp.concatenate([kp_ref[:, cs], kc_ref[:, cs], kn_ref[:, cs]], axis=0)
            v2 = jnp.concatenate([vp_ref[:, cs], vc_ref[:, cs], vn_ref[:, cs]], axis=0)
            outs = []
            for hh in range(2):
                h = 2 * hp + hh
                qh = q2 * head_mask[hh]
                s = lax.dot_general(qh, k2, (((1,), (1,)), ((), ())), preferred_element_type=F32)
                s = jnp.where(in_range, s + b_ref[h], MASK_VALUE)
                m = jnp.max(s, axis=1, keepdims=True)
                p = jnp.exp(s - m)
                l = jnp.sum(p, axis=1, keepdims=True)
                o = jnp.dot(p.astype(BF16), v2, preferred_element_type=F32) / l
                outs.append(o)
                lse_acc = jnp.where(lane == h, m + jnp.log(l), lse_acc)
            o_ref[:, cs] = jnp.where(first, outs[0], outs[1])
        lse_ref[...] = lse_acc

    specs = ([pl.BlockSpec((ATTN_BLOCK, D), lambda r, n: (r * nb + n, 0))]
             + _window_specs(nb, d, 1, D) + _window_specs(nb, d, 2, D) + [BIAS_SPEC])
    row = lambda w: pl.BlockSpec((ATTN_BLOCK, w), lambda r, n: (r * nb + n, 0))
    return _pcall(
        body, name=name, grid=(d, nb), in_specs=specs,
        out_specs=[row(D), row(LANE)],
        out_shape=[jax.ShapeDtypeStruct((T, D), F32), jax.ShapeDtypeStruct((T, LANE), F32)],
        compiler_params=_params("parallel", "parallel"),
    )(*([qkv_g] * 7), _attn_bias_table(d, slopes))


def _stage(scr3, val):
    for c in range(val.shape[1] // LANE):
        scr3[c] = val[:, c * LANE:(c + 1) * LANE]


def _unstage(scr3):
    return jnp.concatenate([scr3[c] for c in range(scr3.shape[0])], axis=1)


def _gather_rows(scr3, r, n, d):
    return jnp.concatenate([scr3[c, pl.ds(r, n, stride=d), :] for c in range(scr3.shape[0])], axis=1)


def _scatter_rows(scr3, r, n, d, val):
    for c in range(scr3.shape[0]):
        scr3[c, pl.ds(r, n, stride=d), :] = val[:, c * LANE:(c + 1) * LANE]


def _attn_combine(os_, lses, dils):
    T = os_[0].shape[0]
    tm = min(256, T)
    ng = len(os_)
    n_scr = sum(1 for d in dils if d > 1)

    def body(*refs):
        in_o = refs[:ng]
        in_l = refs[ng:2 * ng]
        o32_ref, ob_ref, ot_ref, lt_ref = refs[2 * ng:2 * ng + 4]
        scr = refs[2 * ng + 4:]
        o_chunk, l_refs, si = [], [], 0
        for g, d in enumerate(dils):
            if d == 1:
                o_chunk.append(lambda hp, g=g: in_o[g][:, hp * LANE:(hp + 1) * LANE])
                l_refs.append(in_l[g])
                continue
            so, sl = scr[2 * si], scr[2 * si + 1]
            si += 1
            for r in range(d):
                _scatter_rows(so, r, tm // d, d, in_o[g][r])
                sl[pl.ds(r, tm // d, stride=d), :] = in_l[g][r]
            o_chunk.append(lambda hp, so=so: so[hp])
            l_refs.append(sl)
        ls = [r[...] for r in l_refs]
        m = ls[0]
        for l in ls[1:]:
            m = jnp.maximum(m, l)
        tot = jnp.exp(ls[0] - m)
        for l in ls[1:]:
            tot = tot + jnp.exp(l - m)
        lt = m + jnp.log(tot)
        lt_ref[...] = lt
        ws = [jnp.exp(l - lt) for l in ls]
        lane = lax.broadcasted_iota(jnp.int32, (tm, LANE), 1)
        first = lane < HEAD_DIM
        for hp in range(N_HEADS // 2):
            cs = slice(hp * LANE, (hp + 1) * LANE)
            acc = jnp.zeros((tm, LANE), F32)
            for g in range(ng):
                wt = jnp.where(first, _lane_col(ws[g], 2 * hp), _lane_col(ws[g], 2 * hp + 1))
                acc = acc + wt * o_chunk[g](hp)
            o32_ref[:, cs] = acc
            ob_ref[:, cs] = acc.astype(BF16)
        ot_ref[...] = o32_ref[...].T.astype(BF16)

    row = pl.BlockSpec((tm, D), lambda t: (t, 0))
    st = pl.BlockSpec((tm, LANE), lambda t: (t, 0))

    def sub_spec(d, w):
        return pl.BlockSpec((tm, w), lambda t: (t, 0)) if d == 1 else pl.BlockSpec((d, tm // d, w), lambda t: (0, t, 0))

    def sub_view(a, d):
        return a if d == 1 else a.reshape(d, T // d, a.shape[1])

    return _pcall(
        body, name="attn_combine", grid=(T // tm,),
        in_specs=[sub_spec(d, D) for d in dils] + [sub_spec(d, LANE) for d in dils],
        out_specs=[row, row, pl.BlockSpec((D, tm), lambda t: (0, t)), st],
        out_shape=[jax.ShapeDtypeStruct((T, D), F32), jax.ShapeDtypeStruct((T, D), BF16),
                   jax.ShapeDtypeStruct((D, T), BF16), jax.ShapeDtypeStruct((T, LANE), F32)],
        scratch_shapes=[pltpu.VMEM(s, F32) for _ in range(n_scr) for s in ((D // LANE, tm, LANE), (tm, LANE))],
        compiler_params=_params("parallel"),
    )(*[sub_view(a, d) for a, d in zip(os_, dils)], *[sub_view(a, d) for a, d in zip(lses, dils)])


def _attn_bwd_prep(dmb, w_out, o32, lse_tot, dils):
    T = dmb.shape[0]
    tm = min(512, T)
    ng = len(dils)

    def body(a_ref, w_ref, o_ref, l_ref, *rest):
        do_refs, st_refs = rest[:ng], rest[ng:2 * ng]
        do_scr, st_scr = rest[2 * ng:]
        do = lax.dot_general(a_ref[...], w_ref[...], (((1,), (1,)), ((), ())), preferred_element_type=F32)
        _stage(do_scr, do)
        prod = do * o_ref[...]
        lane = lax.broadcasted_iota(jnp.int32, (tm, LANE), 1)
        first = lane < HEAD_DIM
        st = jnp.where(lane < N_HEADS, l_ref[...], 0.0)
        for hp in range(N_HEADS // 2):
            pr = prod[:, hp * LANE:(hp + 1) * LANE]
            d0 = jnp.sum(jnp.where(first, pr, 0.0), axis=1, keepdims=True)
            d1 = jnp.sum(jnp.where(first, 0.0, pr), axis=1, keepdims=True)
            st = jnp.where(lane == N_HEADS + 2 * hp, d0, st)
            st = jnp.where(lane == N_HEADS + 2 * hp + 1, d1, st)
        st_scr[...] = st
        for g, d in enumerate(dils):
            if d == 1:
                do_refs[g][...] = do.astype(BF16)
                st_refs[g][...] = st
                continue
            for r in range(d):
                do_refs[g][r] = _gather_rows(do_scr, r, tm // d, d).astype(BF16)
                st_refs[g][r] = st_scr[pl.ds(r, tm // d, stride=d), :]

    row = pl.BlockSpec((tm, D), lambda t: (t, 0))
    stb = pl.BlockSpec((tm, LANE), lambda t: (t, 0))

    def sub_spec(d, w):
        return pl.BlockSpec((tm, w), lambda t: (t, 0)) if d == 1 else pl.BlockSpec((d, tm // d, w), lambda t: (0, t, 0))

    def sub_shape(d, w, dt):
        return jax.ShapeDtypeStruct((T, w) if d == 1 else (d, T // d, w), dt)

    outs = _pcall(
        body, name="attn_bwd_prep", grid=(T // tm,),
        in_specs=[row, pl.BlockSpec((D, D), lambda t: (0, 0)), row, stb],
        out_specs=[sub_spec(d, D) for d in dils] + [sub_spec(d, LANE) for d in dils],
        out_shape=[sub_shape(d, D, BF16) for d in dils] + [sub_shape(d, LANE, F32) for d in dils],
        scratch_shapes=[pltpu.VMEM((D // LANE, tm, LANE), F32), pltpu.VMEM((tm, LANE), F32)],
        compiler_params=_params("parallel"),
    )(dmb, w_out, o32, lse_tot)
    return ([o.reshape(T, D) for o in outs[:ng]], [o.reshape(T, LANE) for o in outs[ng:]])


def _attn_bwd(qkv_g, do_g, st_g, d, slopes, name):
    T = qkv_g.shape[0]
    L = T // d
    nb = L // ATTN_BLOCK
    scale = HEAD_DIM ** -0.5
    nt = (((1,), (1,)), ((), ()))

    def body(qp_ref, qc_ref, qn_ref, kp_ref, kc_ref, kn_ref, vp_ref, vc_ref, vn_ref,
             dp_ref, dc_ref, dn_ref, sp_ref, sc_ref, sn_ref, b_ref, o_ref):
        in_range = _attn_in_range(pl.program_id(1), L)
        lane = lax.broadcasted_iota(jnp.int32, (ATTN_BLOCK, LANE), 1)
        first = lane < HEAD_DIM
        head_mask = [jnp.where(first, 1.0, 0.0).astype(BF16), jnp.where(first, 0.0, 1.0).astype(BF16)]
        scaled_mask = [jnp.where(first, scale, 0.0).astype(BF16), jnp.where(first, 0.0, scale).astype(BF16)]
        stc = sc_ref[...]
        stw_t = jnp.concatenate([sp_ref[...], stc, sn_ref[...]], axis=0).T
        for hp in range(N_HEADS // 2):
            cs = slice(hp * LANE, (hp + 1) * LANE)
            cat = lambda a, b, c: jnp.concatenate([a[:, cs], b[:, cs], c[:, cs]], axis=0)
            q2, k2, v2, do2 = qc_ref[:, cs], kc_ref[:, cs], vc_ref[:, cs], dc_ref[:, cs]
            qw, kw, vw, dow = cat(qp_ref, qc_ref, qn_ref), cat(kp_ref, kc_ref, kn_ref), cat(vp_ref, vc_ref, vn_ref), cat(dp_ref, dc_ref, dn_ref)
            dqs, dks, dvs = [], [], []
            for hh in range(2):
                h = 2 * hp + hh
                pick = lambda t, hh=hh: t * head_mask[hh]
                pick_scaled = lambda t, hh=hh: t * scaled_mask[hh]
                bias = b_ref[h]
                s = lax.dot_general(pick_scaled(q2), kw, nt, preferred_element_type=F32)
                s = jnp.where(in_range, s + bias, MASK_VALUE)
                p = jnp.exp(s - _lane_col(stc, h))
                dp = lax.dot_general(pick(do2), vw, nt, preferred_element_type=F32)
                ds = p * (dp - _lane_col(stc, N_HEADS + h))
                dqs.append(jnp.dot(ds.astype(BF16), kw, preferred_element_type=F32) * scale)
                st_ = lax.dot_general(pick_scaled(k2), qw, nt, preferred_element_type=F32)
                st_ = jnp.where(in_range, st_ + bias, MASK_VALUE)
                pt = jnp.exp(st_ - stw_t[h:h + 1, :])
                dvs.append(jnp.dot(pt.astype(BF16), dow, preferred_element_type=F32))
                dpt = lax.dot_general(pick(v2), dow, nt, preferred_element_type=F32)
                dst = pt * (dpt - stw_t[N_HEADS + h:N_HEADS + h + 1, :])
                dks.append(jnp.dot(dst.astype(BF16), qw, preferred_element_type=F32) * scale)
            o_ref[:, hp * LANE:(hp + 1) * LANE] = jnp.where(first, dqs[0], dqs[1]).astype(BF16)
            o_ref[:, D + hp * LANE:D + (hp + 1) * LANE] = jnp.where(first, dks[0], dks[1]).astype(BF16)
            o_ref[:, 2 * D + hp * LANE:2 * D + (hp + 1) * LANE] = jnp.where(first, dvs[0], dvs[1]).astype(BF16)

    specs = (_window_specs(nb, d, 0, D) + _window_specs(nb, d, 1, D) + _window_specs(nb, d, 2, D)
             + _window_specs(nb, d, 0, D) + _window_specs(nb, d, 0, LANE) + [BIAS_SPEC])
    return _pcall(
        body, name=name, grid=(d, nb), in_specs=specs,
        out_specs=pl.BlockSpec((ATTN_BLOCK, 3 * D), lambda r, n: (r * nb + n, 0)),
        out_shape=jax.ShapeDtypeStruct((T, 3 * D), BF16),
        compiler_params=_params("parallel", "parallel"),
    )(*([qkv_g] * 9), *([do_g] * 3), *([st_g] * 3), _attn_bias_table(d, slopes))


QKV_TILE = QKV_SHARD // 3


def _qkv_tile_block(nn):
    dev = nn // 3
    return 4 * (dev % 2) + dev // 2, nn % 3


def _attn_qkv_group(xb, wqkv, gi, d, name, deps=()):
    T = xb.shape[0]
    tq = min(1024, T)
    nsub = tq // d
    tn = 768
    ntile = 3 * D // tn

    def body(x_ref, w_ref, o_ref, *scr):
        p = jnp.dot(x_ref[...], w_ref[...], preferred_element_type=F32)
        if d == 1:
            o_ref[...] = p.astype(BF16)
        else:
            _stage(scr[0], p)
            for r in range(d):
                o_ref[r] = _gather_rows(scr[0], r, nsub, d).astype(BF16)

    if d == 1:
        out_spec = pl.BlockSpec((tq, tn), lambda n, t: (t, n))
        out_shape = jax.ShapeDtypeStruct((T, 3 * D), BF16)
    else:
        out_spec = pl.BlockSpec((d, nsub, tn), lambda n, t: (0, t, n))
        out_shape = jax.ShapeDtypeStruct((d, T // d, 3 * D), BF16)
    out = _pcall(
        body, deps=deps, name=name, grid=(ntile, T // tq),
        in_specs=[pl.BlockSpec((tq, D), lambda n, t: (t, 0)),
                  pl.BlockSpec((D, tn), lambda n, t: (0, ntile * gi + n))],
        out_specs=out_spec, out_shape=out_shape,
        scratch_shapes=[] if d == 1 else [pltpu.VMEM((tn // LANE, tq, LANE), F32)],
        compiler_params=_params("parallel", "parallel"),
    )(xb, wqkv)
    return out.reshape(T, 3 * D)


def _attn_dx_group(dqkv_g, wqkv, gi, d, name, deps=()):
    T = dqkv_g.shape[0]
    tq = min(512, T)
    nsub = tq // d

    def body(a_ref, w_ref, o_ref, *stage):
        a = a_ref[...]
        if d > 1:
            a = a.reshape(tq, 3 * D)
        p = lax.dot_general(a, w_ref[...], (((1,), (1,)), ((), ())), preferred_element_type=F32)
        if d == 1:
            o_ref[...] = p
        else:
            for r in range(d):
                _scatter_rows(stage[0], r, nsub, d, p[r * nsub:(r + 1) * nsub, :])
            o_ref[...] = _unstage(stage[0])

    if d == 1:
        a_spec = pl.BlockSpec((tq, 3 * D), lambda t: (t, 0))
        a = dqkv_g
    else:
        a_spec = pl.BlockSpec((d, nsub, 3 * D), lambda t: (0, t, 0))
        a = dqkv_g.reshape(d, T // d, 3 * D)
    return _pcall(
        body, deps=deps, name=name, grid=(T // tq,),
        in_specs=[a_spec, pl.BlockSpec((D, 3 * D), lambda t: (0, gi))],
        out_specs=pl.BlockSpec((tq, D), lambda t: (t, 0)),
        out_shape=jax.ShapeDtypeStruct((T, D), F32),
        scratch_shapes=[] if d == 1 else [pltpu.VMEM((D // LANE, tq, LANE), F32)],
        compiler_params=_params("parallel"),
    )(a, wqkv)


def _attn_dw_group(xt, dqkv_g, gi, d, prev, name):
    T = dqkv_g.shape[0]
    L = T // d
    bm = 512
    ntile = 3 * D // QKV_TILE

    def body(*refs):
        a_ref, b_ref = refs[0], refs[1]
        o_ref, cat = refs[-2], refs[-1]
        if d == 1:
            a = a_ref[...]
        else:
            @pl.when(pl.program_id(1) == 0)
            def _():
                for r in range(d):
                    cat[:, r * L:(r + 1) * L] = a_ref[r]
            a = cat[...]
        o_ref[...] = jnp.dot(a, b_ref[...], preferred_element_type=F32).astype(BF16)

    def out_map(i, n):
        slot, sub = _qkv_tile_block(ntile * gi + n)
        return slot, i, sub

    a_spec = (pl.BlockSpec((bm, T), lambda i, n: (i, 0)) if d == 1
              else pl.BlockSpec((d, bm, L), lambda i, n: (0, i, 0)))
    in_specs = [a_spec, pl.BlockSpec((T, QKV_TILE), lambda i, n: (0, n))]
    args = [xt, dqkv_g]
    aliases = {}
    if prev is not None:
        in_specs.append(ANY_SPEC)
        args.append(prev)
        aliases = {2: 0}
    return _pcall(
        body, name=name, grid=(D // bm, ntile), in_specs=in_specs,
        out_specs=pl.BlockSpec((None, bm, QKV_TILE), out_map),
        out_shape=jax.ShapeDtypeStruct((N_DEV, D, QKV_SHARD), BF16),
        scratch_shapes=[pltpu.VMEM((bm, T), BF16)],
        input_output_aliases=aliases,
        compiler_params=_params("parallel", "arbitrary"),
    )(*args)


def _transpose_sub(x, d, name):
    T = x.shape[0]
    tm = LANE * d

    def body(x_ref, o_ref, scr):
        for c in range(D // LANE):
            scr[...] = x_ref[:, c * LANE:(c + 1) * LANE]
            for r in range(d):
                o_ref[r, c * LANE:(c + 1) * LANE, :] = scr[pl.ds(r, LANE, stride=d), :].T.astype(BF16)

    return _pcall(
        body, name=name, grid=(T // tm,),
        in_specs=[pl.BlockSpec((tm, D), lambda t: (t, 0))],
        out_specs=pl.BlockSpec((d, D, LANE), lambda t: (0, 0, t)),
        out_shape=jax.ShapeDtypeStruct((d, D, T // d), BF16),
        scratch_shapes=[pltpu.VMEM((tm, LANE), F32)],
        compiler_params=_params("parallel"),
    )(x)


HBM_SPEC = pl.BlockSpec(memory_space=pltpu.HBM)
SEM_SPEC = pl.BlockSpec(memory_space=pltpu.SEMAPHORE)
ANY_SPEC = pl.BlockSpec(memory_space=pl.ANY)
DATAFLOW = pltpu.SideEffectType.DATAFLOW_SIDE_EFFECTING


def _me_and_peers():
    x, y, c = lax.axis_index("x"), lax.axis_index("y"), lax.axis_index("c")
    return (x, y, c), [(x, y, 1 - c), (1 - x, y, c), (x, 1 - y, c), (1 - x, 1 - y, c)]


def _slot(px, py, pc):
    return 4 * pc + 2 * px + py


def _split_start(srcs, lands, after, start_copies, n_sem, name):
    n = len(srcs)
    n_after = len(after)

    def body(*refs):
        src_refs, land_refs = refs[:n], refs[n:2 * n]
        send_sems, recv_sems = refs[2 * n + n_after], refs[2 * n + n_after + 1]
        token = refs[-1]
        start_copies(src_refs, land_refs, send_sems, recv_sems)
        token[...] = jnp.zeros_like(token)

    outs = _pcall(
        body, name=name,
        in_specs=[HBM_SPEC] * (2 * n) + [ANY_SPEC] * n_after,
        out_shape=(pltpu.SemaphoreType.DMA(n_sem), pltpu.SemaphoreType.DMA(n_sem),
                   *[pltpu.HBM(a.shape, a.dtype) for a in srcs], *[pltpu.HBM(a.shape, a.dtype) for a in lands],
                   jax.ShapeDtypeStruct((8, LANE), F32)),
        out_specs=(SEM_SPEC, SEM_SPEC, *[HBM_SPEC] * (2 * n), pl.BlockSpec(memory_space=pltpu.VMEM)),
        input_output_aliases={i: 2 + i for i in range(2 * n)},
        compiler_params=pltpu.CompilerParams(has_side_effects=DATAFLOW),
    )(*[pltpu.with_memory_space_constraint(a, pltpu.HBM) for a in srcs],
      *[pltpu.with_memory_space_constraint(a, pltpu.HBM) for a in lands], *after)
    return outs[0], outs[1], list(outs[2:2 + n]), list(outs[2 + n:2 + 2 * n]), outs[-1]


def _split_wait(handle, after, wait_copies, name):
    send_sems, recv_sems, srcs, lands, _ = handle
    n = len(srcs)

    def body(*refs):
        src_refs, land_refs = refs[:n], refs[n:2 * n]
        wait_copies(src_refs, land_refs, refs[2 * n], refs[2 * n + 1])

    outs = _pcall(
        body, name=name,
        in_specs=[HBM_SPEC] * (2 * n) + [SEM_SPEC, SEM_SPEC] + [ANY_SPEC] * len(after),
        out_shape=tuple(pltpu.HBM(a.shape, a.dtype) for a in srcs + lands),
        out_specs=tuple([HBM_SPEC] * (2 * n)),
        input_output_aliases={i: i for i in range(2 * n)},
        compiler_params=pltpu.CompilerParams(has_side_effects=DATAFLOW),
    )(*srcs, *lands, send_sems, recv_sems, *after)
    return list(outs[:n]), list(outs[n:])


def _ag_copies(src_refs, land_refs, send_sems, recv_sems, received):
    me, peers = _me_and_peers()
    cps = []
    for i in range(len(src_refs)):
        for k, to in enumerate(peers):
            cps.append(pltpu.make_async_remote_copy(
                src_ref=src_refs[i], dst_ref=land_refs[i].at[_slot(*(to if received else me))],
                send_sem=send_sems.at[4 * i + k], recv_sem=recv_sems.at[4 * i + k], device_id=to,
                device_id_type=MESH))
    return cps


def _ag_start(shards, after, name):
    lands = [lax.empty((N_DEV,) + a.shape, a.dtype) for a in shards]

    def start(src_refs, land_refs, send_sems, recv_sems):
        for cp in _ag_copies(src_refs, land_refs, send_sems, recv_sems, False):
            cp.start()

    return _split_start(shards, lands, after, start, (4 * len(shards),), name)


def _ag_finish(handle, after, name):
    def wait(src_refs, land_refs, send_sems, recv_sems):
        for cp in _ag_copies(src_refs, land_refs, send_sems, recv_sems, True):
            cp.wait_send()
            cp.wait_recv()

    shards, lands = _split_wait(handle, after, wait, name + "_wait")
    n = len(shards)

    def body(*refs):
        src_refs, out_refs = refs[:n], refs[2 * n:3 * n]
        send_sems, recv_sems, local_sems = refs[3 * n:3 * n + 3]
        bounce = refs[3 * n + 3:]
        me, peers = _me_and_peers()
        loads = [pltpu.make_async_copy(src_refs[i], bounce[i], local_sems.at[i]) for i in range(n)]
        mine = [pltpu.make_async_copy(bounce[i], out_refs[i].at[_slot(*me)], local_sems.at[i]) for i in range(n)]
        for cp in loads:
            cp.start()
        cps = []
        for i in range(n):
            for j, chip in enumerate(peers[1:]):
                blk = out_refs[i].at[_slot(*chip)]
                cps.append(pltpu.make_async_remote_copy(
                    src_ref=blk, dst_ref=blk, send_sem=send_sems.at[i, j], recv_sem=recv_sems.at[i, j],
                    device_id=peers[0], device_id_type=MESH))
        for cp in cps:
            cp.start()
        for ld, st in zip(loads, mine):
            ld.wait()
            st.start()
        for cp in cps:
            cp.wait()
        for cp in mine:
            cp.wait()

    outs = _pcall(
        body, name=name + "_pass",
        in_specs=[ANY_SPEC] * (2 * n), out_specs=[ANY_SPEC] * n,
        out_shape=[jax.ShapeDtypeStruct(a.shape, a.dtype) for a in lands],
        input_output_aliases={n + i: i for i in range(n)},
        scratch_shapes=[pltpu.SemaphoreType.DMA((n, 3)), pltpu.SemaphoreType.DMA((n, 3)),
                        pltpu.SemaphoreType.DMA((n,))] + [pltpu.VMEM(a.shape, a.dtype) for a in shards],
        compiler_params=pltpu.CompilerParams(vmem_limit_bytes=VMEM_LIMIT),
    )(*shards, *lands)
    return list(outs)


def _small_all_gather(v, name):
    R, C = v.shape

    def body(x_ref, out_ref, sum_ref, send_sems, recv_sems, local_sem):
        x, y, c = lax.axis_index("x"), lax.axis_index("y"), lax.axis_index("c")
        me, sibling = (x, y, c), (x, y, 1 - c)
        chips = [(1 - x, y), (x, 1 - y), (1 - x, 1 - y)]

        def rows(px, py, pc):
            return out_ref.at[4 * pc + 2 * px + py]

        def copy(k, block, to, src=None):
            return pltpu.make_async_remote_copy(
                src_ref=rows(*block) if src is None else src, dst_ref=rows(*block),
                send_sem=send_sems.at[k], recv_sem=recv_sems.at[k],
                device_id=to, device_id_type=MESH)

        mine = pltpu.make_async_copy(x_ref, rows(*me), local_sem)
        mine.start()
        first = [copy(0, me, sibling, src=x_ref)]
        first += [copy(1 + j, me, (*chip, c), src=x_ref) for j, chip in enumerate(chips)]
        for cp in first:
            cp.start()
        passed = [copy(4 + j, (*chip, c), sibling) for j, chip in enumerate(chips)]
        for j, chip in enumerate(chips):
            copy(1 + j, (*chip, c), me).wait_recv()
            passed[j].start()
        copy(0, sibling, me).wait_recv()
        for j, chip in enumerate(chips):
            copy(4 + j, (*chip, 1 - c), me).wait_recv()
        for cp in first + passed:
            cp.wait_send()
        mine.wait()
        acc = out_ref[0]
        for s in range(1, N_DEV):
            acc = acc + out_ref[s]
        sum_ref[...] = acc

    vm = pl.BlockSpec(memory_space=pltpu.VMEM)
    return _pcall(
        body, name=name, in_specs=[vm], out_specs=[vm, vm],
        out_shape=[jax.ShapeDtypeStruct((N_DEV, R, C), v.dtype), jax.ShapeDtypeStruct((R, C), v.dtype)],
        scratch_shapes=[pltpu.SemaphoreType.DMA((7,)), pltpu.SemaphoreType.DMA((7,)), pltpu.SemaphoreType.DMA],
    )(v)


def _rs_sibling(arrs, name):
    n = len(arrs)

    def body(*refs):
        ins, outs = refs[:n], refs[n:2 * n]
        send_sems, recv_sems = refs[2 * n:]
        x, y, c = lax.axis_index("x"), lax.axis_index("y"), lax.axis_index("c")
        cps = [pltpu.make_async_remote_copy(
            src_ref=ins[i].at[pl.ds(4 * (1 - c), 4)], dst_ref=outs[i],
            send_sem=send_sems.at[i], recv_sem=recv_sems.at[i],
            device_id=(x, y, 1 - c), device_id_type=MESH) for i in range(n)]
        for cp in cps:
            cp.start()
        for cp in cps:
            cp.wait()

    hbm = pl.BlockSpec(memory_space=pl.ANY)
    return _pcall(
        body, name=name, in_specs=[hbm] * n, out_specs=[hbm] * n,
        out_shape=[jax.ShapeDtypeStruct((4,) + a.shape[1:], a.dtype) for a in arrs],
        scratch_shapes=[pltpu.SemaphoreType.DMA((n,)), pltpu.SemaphoreType.DMA((n,))],
    )(*arrs)


def _rs_copies(src_refs, land_refs, send_sems, recv_sems):
    _, peers = _me_and_peers()
    cps = []
    for i in range(len(src_refs)):
        for j, (px, py, pc) in enumerate(peers[1:]):
            cps.append(pltpu.make_async_remote_copy(
                src_ref=src_refs[i].at[2 * px + py], dst_ref=land_refs[i].at[j],
                send_sem=send_sems.at[3 * i + j], recv_sem=recv_sems.at[3 * i + j],
                device_id=(px, py, pc), device_id_type=MESH))
    return cps


def _rs_start(chipsums, after, name):
    lands = [lax.empty((3,) + a.shape[1:], a.dtype) for a in chipsums]

    def start(src_refs, land_refs, send_sems, recv_sems):
        for cp in _rs_copies(src_refs, land_refs, send_sems, recv_sems):
            cp.start()

    return _split_start(chipsums, lands, after, start, (3 * len(chipsums),), name)


def _rs_wait(handle, after, name):
    def wait(src_refs, land_refs, send_sems, recv_sems):
        for cp in _rs_copies(src_refs, land_refs, send_sems, recv_sems):
            cp.wait_send()
            cp.wait_recv()

    return _split_wait(handle, after, wait, name)


def _row_tile(R, C, itemsize=4, budget=4 * 1024 * 1024):
    best = None
    for t in range(16, R + 1, 16):
        if R % t == 0 and t * C * itemsize <= budget:
            best = t
    return best if best is not None else R


def _add_half(arr, recv, c_idx, name):
    _, R, C = arr.shape
    tr = _row_tile(R, C)

    def body(c_ref, a_ref, r_ref, o_ref):
        o_ref[...] = (a_ref[...].astype(F32) + r_ref[...].astype(F32)).astype(o_ref.dtype)

    gs = pltpu.PrefetchScalarGridSpec(
        num_scalar_prefetch=1, grid=(4, R // tr),
        in_specs=[pl.BlockSpec((None, tr, C), lambda q, i, c_ref: (4 * c_ref[0] + q, i, 0)),
                  pl.BlockSpec((None, tr, C), lambda q, i, c_ref: (q, i, 0))],
        out_specs=pl.BlockSpec((None, tr, C), lambda q, i, c_ref: (q, i, 0)))
    return _pcall(body, name=name, grid_spec=gs, out_shape=jax.ShapeDtypeStruct((4, R, C), arr.dtype),
                  compiler_params=_params("parallel", "parallel"))(c_idx, arr, recv)


def _sum_chips(chipsum, recv, q_idx, name, transposed=False):
    _, R, C = chipsum.shape
    tr = min(512, R) if transposed else _row_tile(R, C)

    def body(q_ref, a_ref, r_ref, o_ref):
        acc = a_ref[...].astype(F32)
        for j in range(3):
            acc = acc + r_ref[j].astype(F32)
        o_ref[...] = acc.T if transposed else acc

    gs = pltpu.PrefetchScalarGridSpec(
        num_scalar_prefetch=1, grid=(R // tr,),
        in_specs=[pl.BlockSpec((None, tr, C), lambda i, q_ref: (q_ref[0], i, 0)),
                  pl.BlockSpec((3, tr, C), lambda i, q_ref: (0, i, 0))],
        out_specs=(pl.BlockSpec((C, tr), lambda i, q_ref: (0, i)) if transposed
                   else pl.BlockSpec((tr, C), lambda i, q_ref: (i, 0))))
    return _pcall(body, name=name, grid_spec=gs,
                  out_shape=jax.ShapeDtypeStruct((C, R) if transposed else (R, C), F32),
                  compiler_params=_params("parallel"))(q_idx, chipsum, recv)


def _adamw(w, g, m, v, name):
    shape = w.shape
    C = shape[-1]
    R = int(np.prod(shape[:-1]))
    tr = _row_tile(R, C, budget=1024 * 1024)

    def body(w_ref, g_ref, m_ref, v_ref, d_ref, nm_ref, nv_ref):
        gv = g_ref[...]
        mv = ADAM_B1 * m_ref[...] + (1.0 - ADAM_B1) * gv
        vv = ADAM_B2 * v_ref[...] + (1.0 - ADAM_B2) * jnp.square(gv)
        m_hat = mv / (1.0 - ADAM_B1 ** ADAM_STEP)
        v_hat = vv / (1.0 - ADAM_B2 ** ADAM_STEP)
        d_ref[...] = -ADAM_LR * (m_hat / (jnp.sqrt(v_hat) + ADAM_EPS) + ADAM_WD * w_ref[...])
        nm_ref[...] = mv
        nv_ref[...] = vv

    blk = pl.BlockSpec((tr, C), lambda i: (i, 0))
    shp = jax.ShapeDtypeStruct((R, C), F32)
    outs = _pcall(body, name=name, grid=(R // tr,), in_specs=[blk] * 4, out_specs=[blk] * 3,
                  out_shape=[shp] * 3, compiler_params=_params("parallel"))(
        w.reshape(R, C), g.reshape(R, C), m.reshape(R, C), v.reshape(R, C))
    return tuple(o.reshape(shape) for o in outs)


def _pad_cols(w, width):
    return jnp.pad(w, ((0, 0), (0, width - w.shape[1])))


def _pad_rows(w, height):
    return jnp.pad(w, ((0, height - w.shape[0]), (0, 0)))


def _slot_to_device_order(a):
    s = a.shape
    return a.reshape((2, 4) + s[1:]).swapaxes(0, 1).reshape(s)


def _device_to_slot_order(a):
    s = a.shape
    return a.reshape((4, 2) + s[1:]).swapaxes(0, 1).reshape(s)


def kernel(x, ffn1_w_gate, ffn1_w_up, ffn1_w_down, ffn2_w_gate, ffn2_w_up, ffn2_w_down, ln_gain, ln_bias, pool_w_in, pool_w_group, pool_scale, pool_w_out, attn_w_qkv, attn_w_out, loss_target, m_ffn1_w_gate, m_ffn1_w_up, m_ffn1_w_down, m_ffn2_w_gate, m_ffn2_w_up, m_ffn2_w_down, m_ln_gain, m_ln_bias, m_pool_w_in, m_pool_w_group, m_pool_scale, m_pool_w_out, m_attn_w_qkv, m_attn_w_out, v_ffn1_w_gate, v_ffn1_w_up, v_ffn1_w_down, v_ffn2_w_gate, v_ffn2_w_up, v_ffn2_w_down, v_ln_gain, v_ln_bias, v_pool_w_in, v_pool_w_group, v_pool_scale, v_pool_w_out, v_attn_w_qkv, v_attn_w_out):
    T = x.shape[1]
    fs = ffn1_w_gate.shape[2]
    fp = _round_up(fs, LANE)
    rs = D // N_DEV
    x0 = x[0]
    tgt = loss_target[0]
    slopes = _alibi_slopes()
    c_idx = lax.axis_index("c").astype(jnp.int32).reshape(1)
    q_idx = (2 * lax.axis_index("x") + lax.axis_index("y")).astype(jnp.int32).reshape(1)

    gates = (ffn1_w_gate, ffn2_w_gate)
    ups = (ffn1_w_up, ffn2_w_up)
    downs = (ffn1_w_down, ffn2_w_down)
    ffns = [(i, k) for i in range(DEPTH) for k in range(2)]
    def padded_cols(w, i):
        return _pad_rows(jnp.swapaxes(w, 1, 2)[i], fp).T

    wgu_sh = [jnp.concatenate([padded_cols(gates[k], i), padded_cols(ups[k], i)], axis=1).astype(BF16)
              for i, k in ffns]
    wd_sh = [_pad_rows(downs[k][i], fp).astype(BF16) for i, k in ffns]
    sq_sh = jnp.concatenate([wd_sh[0], pool_w_in[0].astype(BF16), pool_w_out[0].astype(BF16),
                             pool_w_group[0].reshape(rs // 4, D).astype(BF16)], axis=0)
    qkv_sh = attn_w_qkv[0].astype(BF16)
    aout_sh = attn_w_out[0].astype(BF16)
    ln_sh = jnp.concatenate([ln_gain.reshape(DEPTH * 3, rs), ln_bias.reshape(DEPTH * 3, rs),
                             jnp.zeros((4, rs), F32)], axis=0)

    h0 = _ag_start([wgu_sh[0], ln_sh], (), "ag0")
    x0b, x0t = _transpose_cast(x0, "x_cast", deps=(h0[4],))
    wgu0, ln_all = _ag_finish(h0, (x0b,), "ag0")
    h1 = _ag_start([sq_sh], (wgu0,), "ag1")
    ln_all = _slot_to_device_order(ln_all).transpose(1, 0, 2).reshape(16, D)
    gain = lambda i, s: ln_all[3 * i + s][None]
    bias = lambda i, s: ln_all[DEPTH * 3 + 3 * i + s][None]

    def ffn_fwd(xf, xb, wgu, wd, f, i, s, dep_up=(), dep_down=()):
        g, u, act = _ffn_up(xb, wgu, fp, f"ffn_up{f}", deps=dep_up)
        wd = wd(act) if callable(wd) else wd
        y, yb, yt, xh, rstd = _mm_ln(act, wd, xf, gain(i, s), bias(i, s), MACARON, f"ffn_down_ln{f}",
                                     deps=dep_down() if callable(dep_down) else dep_down)
        return (y, yb, yt), dict(g=g, u=u, act=act, xh=xh, rstd=rstd, wgu=wgu, wd=wd)

    pool_w = {}

    def wd0_after(act):
        (sq_all,) = _ag_finish(h1, (act,), "ag1")
        pool_w["h2"] = _ag_start([wgu_sh[1], wd_sh[1]], (sq_all,), "ag2")
        pool_w["pin"] = _slot_to_device_order(sq_all[:, fp:fp + rs, :]).reshape(D, D)
        pool_w["pout"] = _slot_to_device_order(sq_all[:, fp + rs:fp + 2 * rs, :]).reshape(D, D)
        grp = _slot_to_device_order(sq_all[:, fp + 2 * rs:, :])
        pool_w["grp"] = grp.reshape(N_DEV, N_POOL_GROUPS, rs // 4, POOL_GROUP_DIM).transpose(1, 0, 2, 3).reshape(
            N_POOL_GROUPS, POOL_GROUP_DIM, POOL_GROUP_DIM)
        return sq_all[:, :fp, :].reshape(N_DEV * fp, D)

    (a1, a1b, a1t), s_f0 = ffn_fwd(x0, x0b, wgu0, wd0_after, 0, 0, 0, dep_up=(h1[4],),
                                   dep_down=lambda: (pool_w["h2"][4],))
    h2 = pool_w["h2"]
    w_pin, w_pout, w_grp = pool_w["pin"], pool_w["pout"], pool_w["grp"]
    tm = min(512, T)
    row_spec = pl.BlockSpec((tm, D), lambda i, j, k: (i, 0))
    full_w = pl.BlockSpec((D, D), lambda i, j, k: (0, 0))
    u_pool = _mm(a1b, w_pin, grid=(T // tm, 1, 1), a_spec=row_spec, b_spec=full_w,
                 out_shape=jax.ShapeDtypeStruct((T, D), F32), out_spec=row_spec, name="pool_in")
    mixedb, mixedt = _pool_window(u_pool, False, "pool_window")
    y_pool, ysb, yst = _pool_group(mixedb, w_grp, pool_scale)
    a2, a2b, a2t, xh_p, rstd_p = _mm_ln(ysb, w_pout, a1, gain(0, 1), bias(0, 1), 1.0, "pool_out_ln")
    wgu1, wd1 = _ag_finish(h2, (a2,), "ag2")
    h3 = _ag_start([wgu_sh[2], wd_sh[2]], (wgu1,), "ag3")
    (a3, a3b, a3t), s_f1 = ffn_fwd(a2, a2b, wgu1, wd1.reshape(N_DEV * fp, D), 1, 0, 2, dep_up=(h3[4],))
    wgu2, wd2 = _ag_finish(h3, (a3,), "ag3")
    h4 = _ag_start([qkv_sh, aout_sh], (wgu2,), "ag4")
    (b1, b1b, b1t), s_f2 = ffn_fwd(a3, a3b, wgu2, wd2.reshape(N_DEV * fp, D), 2, 1, 0, dep_up=(h4[4],))
    wqkv_all, aout_all = _ag_finish(h4, (b1,), "ag4")
    h5 = _ag_start([wgu_sh[3], wd_sh[3]], (wqkv_all,), "ag5")
    w_aout = _slot_to_device_order(aout_all).reshape(D, D)
    dils = [d for _, d in DIL_CONFIGS]
    wqkv_nat = _slots_to_columns(wqkv_all, "attn_wqkv_cols")
    qkv_gs, o_gs, lse_gs = [], [], []
    for gi, d in enumerate(dils):
        qkv_g = _attn_qkv_group(b1b, wqkv_nat, gi, d, f"attn_qkv{gi}", deps=(h5[4],) if gi == 0 else ())
        o_g, lse_g = _attn_fwd(qkv_g, d, slopes[gi], f"attn_fwd{gi}")
        qkv_gs.append(qkv_g)
        o_gs.append(o_g)
        lse_gs.append(lse_g)
    o32, ob, ot, lse_tot = _attn_combine(o_gs, lse_gs, dils)
    b2, b2b, b2t, xh_a, rstd_a = _mm_ln(ob, w_aout, b1, gain(1, 1), bias(1, 1), 1.0, "attn_out_ln")
    wgu3, wd3 = _ag_finish(h5, (b2,), "ag5")
    (b3, _, _), s_f3 = ffn_fwd(b2, b2b, wgu3, wd3.reshape(N_DEV * fp, D), 3, 1, 2)

    dy, loss_tile = _loss_head(b3, tgt)
    loss = lax.psum(loss_tile[0, 0], AXES)

    bm = min(512, D)
    dgains, dbiases = {}, {}
    rs_pending = []
    gsums = {}

    def rs_finish(after):
        h, tag = rs_pending.pop()
        chips, lands = _rs_wait(h, after, f"rs_{tag}_wait")
        gsums[tag] = [_sum_chips(a, r, q_idx, f"rs_sum_{tag}{i}", transposed=tag.startswith("f"))
                      for i, (a, r) in enumerate(zip(chips, lands))]

    def rs_stage(bufs, tag):
        if rs_pending:
            rs_finish((bufs[-1],))
        recv = _rs_sibling(bufs, f"rs_sib_{tag}")
        chips = [_add_half(a, r, c_idx, f"rs_add_{tag}{i}") for i, (a, r) in enumerate(zip(bufs, recv))]
        h = _rs_start(chips, (), f"rs_{tag}")
        rs_pending.append((h, tag))
        return h[4]

    def ffn_bwd(dys, f, i, s, st, xt):
        dxres, dhb, dht, dg, db = _ln_bwd(dys, st["xh"], st["rstd"], gain(i, s), MACARON, f"ffn_ln_bwd{f}")
        dgains[(i, s)], dbiases[(i, s)] = dg, db
        dgu = _ffn_bwd_act(dhb, st["wd"], st["g"], st["u"], fp, f"ffn_bwd_act{f}")
        g_dt = _ffn_dwd(dht, st["act"], fp, f"ffn_dwd{f}")
        g_gu = _mm(xt, dgu, grid=(D // bm, N_DEV, 1),
                   a_spec=pl.BlockSpec((bm, T), lambda r, j, k: (r, 0)),
                   b_spec=pl.BlockSpec((T, 2 * fp), lambda r, j, k: (0, j)),
                   out_shape=jax.ShapeDtypeStruct((N_DEV, D, 2 * fp), BF16),
                   out_spec=pl.BlockSpec((None, bm, 2 * fp), lambda r, j, k: (j, r, 0)), name=f"ffn_dwgu{f}")
        token = rs_stage([g_dt, g_gu], f"f{f}")
        return [_ffn_dx(dgu, st["wgu"], dxres, fp, f"ffn_dx{f}", deps=(token,))]

    def dw_square(at, bmat, name):
        return _mm(at, bmat, grid=(D // bm, 1, 1),
                   a_spec=pl.BlockSpec((bm, T), lambda r, j, k: (r, 0)),
                   b_spec=pl.BlockSpec((T, D), lambda r, j, k: (0, 0)),
                   out_shape=jax.ShapeDtypeStruct((D, D), BF16),
                   out_spec=pl.BlockSpec((bm, D), lambda r, j, k: (r, 0)), name=name)

    def dx_square(a, w, name, deps=()):
        return _mm(a, w, grid=(T // tm, 1, 1), nt=True, a_spec=row_spec, b_spec=full_w,
                   out_shape=jax.ShapeDtypeStruct((T, D), F32), out_spec=row_spec, name=name, deps=deps)

    to_slots = lambda g2d: _device_to_slot_order(g2d.reshape(N_DEV, rs, D))

    d_b2 = ffn_bwd([dy], 3, 1, 2, s_f3, b2t)
    dxres, dmb, dmt, dg, db = _ln_bwd(d_b2, xh_a, rstd_a, gain(1, 1), 1.0, "attn_ln_bwd")
    dgains[(1, 1)], dbiases[(1, 1)] = dg, db
    g_aout = dw_square(ot, dmb, "attn_dwout")
    dobs, statss = _attn_bwd_prep(dmb, w_aout, o32, lse_tot, dils)
    ntile = 3 * D // QKV_TILE
    g_qkv = None
    dqkv_gs = []
    for gi, d in enumerate(dils):
        dqkv_g = _attn_bwd(qkv_gs[gi], dobs[gi], statss[gi], d, slopes[gi], f"attn_bwd{gi}")
        dqkv_gs.append(dqkv_g)
        xt = b1t if d == 1 else _transpose_sub(b1, d, f"attn_xt{gi}")
        g_qkv = _attn_dw_group(xt, dqkv_g, gi, d, g_qkv, f"attn_dwqkv{gi}")
    token = rs_stage([to_slots(g_aout), g_qkv], "attn")
    dx_attn = [_attn_dx_group(dqkv_gs[gi], wqkv_nat, gi, d, f"attn_dx{gi}", deps=(token,) if gi == 0 else ())
               for gi, d in enumerate(dils)]
    d_a3 = ffn_bwd([dxres] + dx_attn, 2, 1, 0, s_f2, a3t)
    d_a2 = ffn_bwd(d_a3, 1, 0, 2, s_f1, a2t)
    dxres, dmb, dmt, dg, db = _ln_bwd(d_a2, xh_p, rstd_p, gain(0, 1), 1.0, "pool_ln_bwd")
    dgains[(0, 1)], dbiases[(0, 1)] = dg, db
    g_pout = dw_square(yst, dmb, "pool_dwout")
    dyb, dscale = _pool_bwd_out(dmb, w_pout, y_pool, pool_scale)
    gd = POOL_GROUP_DIM
    g_grp = _mm(mixedt, dyb, grid=(N_POOL_GROUPS, 1, 1),
                a_spec=pl.BlockSpec((gd, T), lambda g, j, k: (g, 0)),
                b_spec=pl.BlockSpec((T, gd), lambda g, j, k: (0, g)),
                out_shape=jax.ShapeDtypeStruct((N_POOL_GROUPS, gd, gd), BF16),
                out_spec=pl.BlockSpec((None, gd, gd), lambda g, j, k: (g, 0, 0)), name="pool_dwgroup")
    tg = min(1024, T)
    dmixed = _mm(dyb, w_grp, grid=(N_POOL_GROUPS, T // tg, 1), nt=True,
                 a_spec=pl.BlockSpec((tg, gd), lambda g, t, k: (t, g)),
                 b_spec=pl.BlockSpec((None, gd, gd), lambda g, t, k: (g, 0, 0)),
                 out_shape=jax.ShapeDtypeStruct((T, D), F32),
                 out_spec=pl.BlockSpec((tg, gd), lambda g, t, k: (t, g)), name="pool_dmixed")
    dub, _ = _pool_window(dmixed, True, "pool_window_bwd")
    g_pin = dw_square(a1t, dub, "pool_dwin")
    g_grp_slots = _device_to_slot_order(
        g_grp.reshape(N_POOL_GROUPS, N_DEV, rs // 4, gd).transpose(1, 0, 2, 3).reshape(N_DEV, rs // 4, D))
    token = rs_stage([to_slots(g_pout), g_grp_slots, to_slots(g_pin)], "pool")
    dx_pool = dx_square(dub, w_pin, "pool_dx", deps=(token,))
    d_x0 = ffn_bwd([dxres, dx_pool], 0, 0, 0, s_f0, x0t)
    grad_x = d_x0[0]
    gw_aout, gw_qkv = gsums["attn"]
    gw_pout, gw_grp, gw_pin = gsums["pool"]

    small = jnp.concatenate([dgains[(i, s)] for i in range(DEPTH) for s in range(3)]
                            + [dbiases[(i, s)] for i in range(DEPTH) for s in range(3)]
                            + [dscale, jnp.zeros((3, D), F32)], axis=0)
    _, small_sum = _small_all_gather(small, "ag_small_grads")
    dev = 4 * lax.axis_index("x") + 2 * lax.axis_index("y") + lax.axis_index("c")
    mine = lax.dynamic_slice_in_dim(small_sum, dev * rs, rs, axis=1)
    ffn_grad = {
        "gate": lambda k: jnp.stack([gsums[f"f{2 * i + k}"][1][:fs] for i in range(DEPTH)]),
        "up": lambda k: jnp.stack([gsums[f"f{2 * i + k}"][1][fp:fp + fs] for i in range(DEPTH)]),
        "down": lambda k: jnp.stack([gsums[f"f{2 * i + k}"][0][:fs] for i in range(DEPTH)]),
    }
    grads = {
        "ffn2_w_gate": ffn_grad["gate"](1),
        "ffn2_w_up": ffn_grad["up"](1),
        "ffn2_w_down": ffn_grad["down"](1),
        "ln_gain": mine[0:DEPTH * 3].reshape(DEPTH, 3, rs),
        "ln_bias": mine[DEPTH * 3:2 * DEPTH * 3].reshape(DEPTH, 3, rs),
        "pool_w_in": gw_pin[None],
        "pool_w_group": gw_grp[:rs // 4].reshape(N_POOL_GROUPS, rs // 4, gd)[None],
        "pool_scale": small_sum[2 * DEPTH * 3][None],
        "pool_w_out": gw_pout[None],
        "attn_w_qkv": gw_qkv[None],
        "attn_w_out": gw_aout[None],
    }
    weights = dict(ffn1_w_gate=ffn1_w_gate, ffn1_w_up=ffn1_w_up, ffn1_w_down=ffn1_w_down,
                   ffn2_w_gate=ffn2_w_gate, ffn2_w_up=ffn2_w_up, ffn2_w_down=ffn2_w_down,
                   ln_gain=ln_gain, ln_bias=ln_bias, pool_w_in=pool_w_in, pool_w_group=pool_w_group,
                   pool_scale=pool_scale, pool_w_out=pool_w_out, attn_w_qkv=attn_w_qkv, attn_w_out=attn_w_out)
    ms = dict(ffn1_w_gate=m_ffn1_w_gate, ffn1_w_up=m_ffn1_w_up, ffn1_w_down=m_ffn1_w_down,
              ffn2_w_gate=m_ffn2_w_gate, ffn2_w_up=m_ffn2_w_up, ffn2_w_down=m_ffn2_w_down,
              ln_gain=m_ln_gain, ln_bias=m_ln_bias, pool_w_in=m_pool_w_in, pool_w_group=m_pool_w_group,
              pool_scale=m_pool_scale, pool_w_out=m_pool_w_out, attn_w_qkv=m_attn_w_qkv, attn_w_out=m_attn_w_out)
    vs = dict(ffn1_w_gate=v_ffn1_w_gate, ffn1_w_up=v_ffn1_w_up, ffn1_w_down=v_ffn1_w_down,
              ffn2_w_gate=v_ffn2_w_gate, ffn2_w_up=v_ffn2_w_up, ffn2_w_down=v_ffn2_w_down,
              ln_gain=v_ln_gain, ln_bias=v_ln_bias, pool_w_in=v_pool_w_in, pool_w_group=v_pool_w_group,
              pool_scale=v_pool_scale, pool_w_out=v_pool_w_out, attn_w_qkv=v_attn_w_qkv, attn_w_out=v_attn_w_out)
    names = list(weights)
    col_sharded = ("ffn1_w_gate", "ffn1_w_up", "ffn2_w_gate", "ffn2_w_up")
    deltas, new_m, new_v = {}, {}, {}

    def update(nme):
        if nme in col_sharded:
            outs = _adamw(jnp.swapaxes(weights[nme], 1, 2), grads[nme], jnp.swapaxes(ms[nme], 1, 2),
                          jnp.swapaxes(vs[nme], 1, 2), f"adamw_{nme}")
            deltas[nme], new_m[nme], new_v[nme] = (jnp.swapaxes(o, 1, 2) for o in outs)
            grads[nme] = jnp.swapaxes(grads[nme], 1, 2)
        else:
            deltas[nme], new_m[nme], new_v[nme] = _adamw(weights[nme], grads[nme], ms[nme], vs[nme], f"adamw_{nme}")

    late = ("ffn1_w_gate", "ffn1_w_up", "ffn1_w_down")
    for nme in names:
        if nme not in late:
            update(nme)
    rs_finish((grad_x,) + tuple(deltas[nme] for nme in names if nme not in late))
    grad_x = grad_x[None]
    for part, nme in zip(("gate", "up", "down"), late):
        grads[nme] = ffn_grad[part](0)
        update(nme)
    return (loss, grad_x, *[grads[k] for k in names], *[deltas[k] for k in names],
            *[new_m[k] for k in names], *[new_v[k] for k in names])
```
